```python
import math
import jax
import jax.numpy as jnp
from jax import lax
import numpy as np

D_MODEL = 1024
BATCH = 16
SEQ = 2048
DEPTH = 2

CHUNK = 64
N_A_LAYERS = DEPTH // 2
N_B_LAYERS = DEPTH - N_A_LAYERS
EPS = 1e-6

RET_HEADS = 4
RET_DK = D_MODEL // RET_HEADS
RET_DV = 2 * RET_DK
RET_QK = RET_HEADS * RET_DK
RET_V = RET_HEADS * RET_DV
RET_IN = 2 * RET_QK + 2 * RET_V
ROPE_BASE = 10000.0

DIFF_HEAD_DIM = 64
DIFF_HEADS = D_MODEL // (2 * DIFF_HEAD_DIM)
DIFF_QK = DIFF_HEADS * 2 * DIFF_HEAD_DIM
DIFF_V = DIFF_HEADS * 2 * DIFF_HEAD_DIM
Q_BLOCK = 128

N_GROUPS = 4
EXPERTS_PER_GROUP = 4
N_EXPERTS = N_GROUPS * EXPERTS_PER_GROUP
TOP_K_IN_GROUP = 2
EXPERT_HIDDEN = D_MODEL
ROUTE_BLOCK = 128

kernel_name = 'yoco_retention_diffattn_hmoe_adaln'


def _rmsnorm(x, g):
    xf = x.astype(jnp.float32)
    y = xf * lax.rsqrt(jnp.mean(xf * xf, axis=-1, keepdims=True) + EPS)
    return (y * g.astype(jnp.float32)).astype(x.dtype)


def _modulate(h, shift, scale):
    return h * (1 + scale[:, None, :]) + shift[:, None, :]


def _rotary(x, pos):
    half = x.shape[-1] // 2
    inv_freq = 1.0 / (ROPE_BASE ** (jnp.arange(half, dtype=jnp.float32) / half))
    ang = pos[:, None] * inv_freq[None, :]
    cos = jnp.cos(ang)[None, :, None, :].astype(x.dtype)
    sin = jnp.sin(ang)[None, :, None, :].astype(x.dtype)
    x1, x2 = x[..., :half], x[..., half:]
    return jnp.concatenate([x1 * cos - x2 * sin, x2 * cos + x1 * sin], axis=-1)


def _retention(h, w_in, gn, w_out):
    bsz, seq, _ = h.shape
    n_chunks = seq // CHUNK
    f32 = jnp.float32
    proj = h @ w_in
    q, k, v, g = jnp.split(proj, [RET_QK, 2 * RET_QK, 2 * RET_QK + RET_V], axis=-1)
    pos = jnp.arange(seq, dtype=f32)
    q = _rotary(q.reshape(bsz, seq, RET_HEADS, RET_DK), pos).astype(f32)
    k = _rotary(k.reshape(bsz, seq, RET_HEADS, RET_DK), pos).astype(f32) * (RET_DK ** -0.5)
    v = v.reshape(bsz, seq, RET_HEADS, RET_DV).astype(f32)
    log_gamma = jnp.log1p(-jnp.exp2(-5.0 - jnp.arange(RET_HEADS, dtype=f32)))
    n = jnp.arange(CHUNK, dtype=f32)
    rel = n[:, None] - n[None, :]
    intra = jnp.where(rel >= 0, jnp.exp(jnp.maximum(rel, 0.0)[None] * log_gamma[:, None, None]), 0.0)
    q_decay = jnp.exp((n + 1.0)[None, :] * log_gamma[:, None])
    k_decay = jnp.exp((CHUNK - 1.0 - n)[None, :] * log_gamma[:, None])
    chunk_decay = jnp.exp(CHUNK * log_gamma)

    def to_chunks(t):
        return t.reshape(bsz, n_chunks, CHUNK, RET_HEADS, t.shape[-1]).transpose(1, 0, 3, 2, 4)

    def step(state, inp):
        qi, ki, vi = inp
        scores = jnp.einsum('bhnd,bhmd->bhnm', qi, ki) * intra
        out = (jnp.einsum('bhnm,bhmv->bhnv', scores, vi)
               + jnp.einsum('bhnd,bhdv->bhnv', qi * q_decay[:, :, None], state))
        state = state * chunk_decay[:, None, None] + jnp.einsum(
            'bhmd,bhmv->bhdv', ki * k_decay[:, :, None], vi)
        return state, out

    state0 = jnp.zeros((bsz, RET_HEADS, RET_DK, RET_DV), f32)
    _, o = lax.scan(step, state0, (to_chunks(q), to_chunks(k), to_chunks(v)))
    o = o.transpose(1, 0, 3, 2, 4).reshape(bsz, seq, RET_HEADS, RET_DV)
    o = o * lax.rsqrt(jnp.mean(o * o, axis=-1, keepdims=True) + EPS)
    o = o.reshape(bsz, seq, RET_V) * gn.astype(f32)
    y = jax.nn.silu(g) * o.astype(h.dtype)
    return y @ w_out


def _shared_kv(x, c_act, ada_w, ada_b, norm_g, w_kv, k_norm):
    bsz, seq, _ = x.shape
    shift, scale = jnp.split(c_act @ ada_w + ada_b, 2, axis=-1)
    h = _modulate(_rmsnorm(x, norm_g), shift, scale)
    kv = h @ w_kv
    k = _rmsnorm(kv[..., :DIFF_QK].reshape(bsz, seq, DIFF_HEADS, 2, DIFF_HEAD_DIM), k_norm)
    v = kv[..., DIFF_QK:].reshape(bsz, seq, DIFF_HEADS, 2 * DIFF_HEAD_DIM)
    return k, v


def _diff_attention(h, k, v, w_q, q_norm, lam, subln, w_out, lambda_init):
    bsz, seq, _ = h.shape
    f32 = jnp.float32
    q = _rmsnorm((h @ w_q).reshape(bsz, seq, DIFF_HEADS, 2, DIFF_HEAD_DIM), q_norm) * (DIFF_HEAD_DIM ** -0.5)
    lam_f = lam.astype(f32)
    lam_full = (jnp.exp(jnp.sum(lam_f[0] * lam_f[1])) - jnp.exp(jnp.sum(lam_f[2] * lam_f[3]))
                + lambda_init)
    n_qb = seq // Q_BLOCK
    qb = q.reshape(bsz, n_qb, Q_BLOCK, DIFF_HEADS, 2, DIFF_HEAD_DIM).transpose(1, 0, 2, 3, 4, 5)
    kpos = jnp.arange(seq)

    def block(args):
        qi, bi = args
        qpos = bi * Q_BLOCK + jnp.arange(Q_BLOCK)
        visible_end = (qpos // CHUNK + 1) * CHUNK
        mask = kpos[None, :] < visible_end[:, None]
        s = jnp.einsum('bqhtd,bkhtd->bhtqk', qi, k).astype(f32)
        p = jax.nn.softmax(jnp.where(mask, s, -jnp.inf), axis=-1)
        a = p[:, :, 0] - lam_full * p[:, :, 1]
        return jnp.einsum('bhqk,bkhv->bqhv', a.astype(v.dtype), v)

    o = lax.map(block, (qb, jnp.arange(n_qb)))
    o = o.transpose(1, 0, 2, 3, 4).reshape(bsz, seq, DIFF_HEADS, 2 * DIFF_HEAD_DIM)
    o = _rmsnorm(o, subln) * (1.0 - lambda_init)
    return o.reshape(bsz, seq, DIFF_V) @ w_out


def _hier_moe(h, w_group, b_group, w_expert, b_expert, w1, w3, w2):
    bsz, seq, dm = h.shape
    n_tok = bsz * seq
    f32 = jnp.float32
    t = h.reshape(n_tok, dm)
    g_logits = (t @ w_group + b_group).astype(f32)
    g_idx = jnp.argmax(g_logits, axis=-1)
    g_gate = jnp.take_along_axis(jax.nn.softmax(g_logits, axis=-1), g_idx[:, None], axis=-1)
    e_logits = (t @ w_expert + b_expert).astype(f32).reshape(n_tok, N_GROUPS, EXPERTS_PER_GROUP)
    e_sel = jnp.take_along_axis(e_logits, g_idx[:, None, None], axis=1)[:, 0]
    top_w, top_i = lax.top_k(jax.nn.softmax(e_sel, axis=-1), TOP_K_IN_GROUP)
    top_w = top_w / jnp.sum(top_w, axis=-1, keepdims=True) * g_gate
    expert_id = g_idx[:, None] * EXPERTS_PER_GROUP + top_i
    m = n_tok * TOP_K_IN_GROUP
    flat_e = expert_id.reshape(m)
    flat_w = top_w.reshape(m)
    flat_tok = jnp.repeat(jnp.arange(n_tok), TOP_K_IN_GROUP)
    order = jnp.argsort(flat_e)
    sorted_e = flat_e[order]
    counts = jnp.bincount(flat_e, length=N_EXPERTS)
    starts = jnp.cumsum(counts) - counts
    padded = (counts + ROUTE_BLOCK - 1) // ROUTE_BLOCK * ROUTE_BLOCK
    padded_ends = jnp.cumsum(padded)
    padded_starts = padded_ends - padded
    dest = padded_starts[sorted_e] + (jnp.arange(m) - starts[sorted_e])
    n_blocks = -(-m // ROUTE_BLOCK) + N_EXPERTS
    rows = n_blocks * ROUTE_BLOCK
    buf_tok = jnp.zeros((rows,), jnp.int32).at[dest].set(flat_tok[order])
    buf_w = jnp.zeros((rows,), f32).at[dest].set(flat_w[order])
    block_e = jnp.minimum(jnp.searchsorted(padded_ends, jnp.arange(n_blocks) * ROUTE_BLOCK, side='right'),
                          N_EXPERTS - 1)
    xb = t[buf_tok].reshape(n_blocks, ROUTE_BLOCK, dm)

    def expert_block(args):
        xi, e = args
        hid = jax.nn.silu(xi @ w1[e]) * (xi @ w3[e])
        return hid @ w2[e]

    yb = lax.map(expert_block, (xb, block_e)).reshape(rows, dm)
    out = jnp.zeros((n_tok, dm), yb.dtype).at[buf_tok].add(yb * buf_w[:, None].astype(yb.dtype))
    return out.reshape(bsz, seq, dm).astype(h.dtype)


def setup_inputs(seed: int = 0) -> dict:
    key = jax.random.key(seed)
    ks = jax.random.split(key, 32)
    f32 = jnp.float32
    D = D_MODEL

    def nrm(k, shape, fan_in):
        return jax.random.normal(k, shape, f32) * (fan_in ** -0.5)

    def gain(k, shape):
        return 1.0 + 0.02 * jax.random.normal(k, shape, f32)

    def small(k, shape):
        return 0.01 * jax.random.normal(k, shape, f32)

    return {
        'x': jax.random.normal(ks[0], (BATCH, SEQ, D), f32),
        'c': jax.random.normal(ks[1], (BATCH, D), f32),
        'ada_w': nrm(ks[2], (DEPTH, D, 6 * D), D),
        'ada_b': small(ks[3], (DEPTH, 6 * D)),
        'norm_mix': gain(ks[4], (DEPTH, D)),
        'norm_ffn': gain(ks[5], (DEPTH, D)),
        'ret_w_in': nrm(ks[6], (N_A_LAYERS, D, RET_IN), D),
        'ret_gn': gain(ks[7], (N_A_LAYERS, RET_V)),
        'ret_w_out': nrm(ks[8], (N_A_LAYERS, RET_V, D), RET_V),
        'kv_ada_w': nrm(ks[9], (D, 2 * D), D),
        'kv_ada_b': small(ks[10], (2 * D,)),
        'kv_norm': gain(ks[11], (D,)),
        'kv_w': nrm(ks[12], (D, DIFF_QK + DIFF_V), D),
        'k_norm': gain(ks[13], (DIFF_HEAD_DIM,)),
        'diff_w_q': nrm(ks[14], (N_B_LAYERS, D, DIFF_QK), D),
        'q_norm': gain(ks[15], (N_B_LAYERS, DIFF_HEAD_DIM)),
        'diff_lam': 0.1 * jax.random.normal(ks[16], (N_B_LAYERS, 4, DIFF_HEAD_DIM), f32),
        'diff_subln': gain(ks[17], (N_B_LAYERS, 2 * DIFF_HEAD_DIM)),
        'diff_w_out': nrm(ks[18], (N_B_LAYERS, DIFF_V, D), DIFF_V),
        'moe_w_group': nrm(ks[19], (DEPTH, D, N_GROUPS), D),
        'moe_b_group': small(ks[20], (DEPTH, N_GROUPS)),
        'moe_w_expert': nrm(ks[21], (DEPTH, D, N_EXPERTS), D),
        'moe_b_expert': small(ks[22], (DEPTH, N_EXPERTS)),
        'moe_w1': nrm(ks[23], (DEPTH, N_EXPERTS, D, EXPERT_HIDDEN), D),
        'moe_w3': nrm(ks[24], (DEPTH, N_EXPERTS, D, EXPERT_HIDDEN), D),
        'moe_w2': nrm(ks[25], (DEPTH, N_EXPERTS, EXPERT_HIDDEN, D), EXPERT_HIDDEN),
    }


def reference(x, c, ada_w, ada_b, norm_mix, norm_ffn, ret_w_in, ret_gn, ret_w_out,
              kv_ada_w, kv_ada_b, kv_norm, kv_w, k_norm, diff_w_q, q_norm, diff_lam,
              diff_subln, diff_w_out, moe_w_group, moe_b_group, moe_w_expert, moe_b_expert,
              moe_w1, moe_w3, moe_w2):
    c_act = jax.nn.silu(c)
    k_shared = None
    v_shared = None
    for layer in range(DEPTH):
        sh1, sc1, g1, sh2, sc2, g2 = jnp.split(c_act @ ada_w[layer] + ada_b[layer], 6, axis=-1)
        h = _modulate(_rmsnorm(x, norm_mix[layer]), sh1, sc1)
        if layer < N_A_LAYERS:
            a = layer
            y = _retention(h, ret_w_in[a], ret_gn[a], ret_w_out[a])
        else:
            b = layer - N_A_LAYERS
            if b == 0:
                k_shared, v_shared = _shared_kv(x, c_act, kv_ada_w, kv_ada_b, kv_norm, kv_w, k_norm)
            lambda_init = 0.8 - 0.6 * math.exp(-0.3 * layer)
            y = _diff_attention(h, k_shared, v_shared, diff_w_q[b], q_norm[b], diff_lam[b],
                                diff_subln[b], diff_w_out[b], lambda_init)
        x = x + g1[:, None, :] * y
        h = _modulate(_rmsnorm(x, norm_ffn[layer]), sh2, sc2)
        x = x + g2[:, None, :] * _hier_moe(h, moe_w_group[layer], moe_b_group[layer],
                                           moe_w_expert[layer], moe_b_expert[layer],
                                           moe_w1[layer], moe_w3[layer], moe_w2[layer])
    return x
```

```python
import functools
import math

import jax
import jax.numpy as jnp
from jax import lax
from jax.experimental import pallas as pl
from jax.experimental.pallas import tpu as pltpu

F32 = jnp.float32
BF16 = jnp.bfloat16

D_MODEL = 1024
EPS = 1e-6

RET_HEADS = 4
RET_DK = D_MODEL // RET_HEADS
RET_DV = 2 * RET_DK
RET_QK = RET_HEADS * RET_DK
RET_V = RET_HEADS * RET_DV
RET_IN = 2 * RET_QK + 2 * RET_V
ROPE_BASE = 10000.0
RET_CHUNK = 256

DIFF_HEAD_DIM = 64
DIFF_HEADS = D_MODEL // (2 * DIFF_HEAD_DIM)
MASK_CHUNK = 64
ATT_TQ = 256
ATT_TK = 256

N_GROUPS = 4
EXPERTS_PER_GROUP = 4
N_EXPERTS = N_GROUPS * EXPERTS_PER_GROUP
TOP_K = 2
MOE_BM = 256
ROUTER_LANES = 128

LANE = 128
SUBLANE = 8
VMEM_LIMIT = 56 * 1024 * 1024


def _dot(a, b):
    return jnp.dot(a, b, preferred_element_type=F32)


def _dot_nt(a, b):
    return lax.dot_general(a, b, (((1,), (1,)), ((), ())), preferred_element_type=F32)


def _dot_tn(a, b):
    return lax.dot_general(a, b, (((0,), (0,)), ((), ())), preferred_element_type=F32)


def _silu(x):
    return x * (1.0 / (1.0 + jnp.exp(-x)))


def _rms(x):
    return x * lax.rsqrt(jnp.mean(x * x, axis=-1, keepdims=True) + EPS)


def _params(*sem):
    return pltpu.CompilerParams(dimension_semantics=sem, vmem_limit_bytes=VMEM_LIMIT)


def _ada_kernel(c_ref, w_ref, b_ref, o_ref):
    ca = _silu(c_ref[...])
    o_ref[0] = jnp.dot(ca, w_ref[0], preferred_element_type=F32,
                       precision=lax.Precision.HIGHEST) + b_ref[0]


def _ada(c, w, b):
    n_l, d, n = w.shape
    bsz = c.shape[0]
    tn = 1024
    return pl.pallas_call(
        _ada_kernel,
        out_shape=jax.ShapeDtypeStruct((n_l, bsz, n), F32),
        grid=(n_l, n // tn),
        in_specs=[
            pl.BlockSpec((bsz, d), lambda l, j: (0, 0)),
            pl.BlockSpec((1, d, tn), lambda l, j: (l, 0, j)),
            pl.BlockSpec((1, 1, tn), lambda l, j: (l, 0, j)),
        ],
        out_specs=pl.BlockSpec((1, bsz, tn), lambda l, j: (l, 0, j)),
        compiler_params=_params("arbitrary", "arbitrary"),
        name="ada",
    )(c, w, b.reshape(n_l, 1, n))


def _ffn_prenorm_logits(xn, gain, shift, scale, wr_cat_ref, wr_hi_ref, br):
    h = _rms(xn) * gain
    h = h * (1.0 + scale) + shift
    h_hi = h.astype(BF16)
    h_lo = (h - h_hi.astype(F32)).astype(BF16)
    t = _dot(h_hi, wr_cat_ref[...])
    u = _dot(h_lo, wr_hi_ref[...])
    logits = t[:, :ROUTER_LANES] + t[:, ROUTER_LANES:] + u + br
    return h, logits


def _ret_in_kernel(x_ref, ada_ref, gain_ref, cos_ref, sin_ref, w_ref, o_ref):
    x = x_ref[0]
    shift = ada_ref[0, 0:1, :]
    scale = ada_ref[0, 1:2, :]
    h = (_rms(x) * gain_ref[...]) * (1.0 + scale) + shift
    hb = h.astype(BF16)
    cos = cos_ref[...]
    sin = sin_ref[...]
    half = RET_DK // 2
    for j in range(RET_IN // D_MODEL):
        p = _dot(hb, w_ref[:, j * D_MODEL:(j + 1) * D_MODEL])
        if j < 2:
            post = 1.0 if j == 0 else RET_DK ** -0.5
            for hd in range(RET_HEADS):
                lo = hd * RET_DK
                x1 = p[:, lo:lo + half]
                x2 = p[:, lo + half:lo + RET_DK]
                o_ref[0, :, j * D_MODEL + lo:j * D_MODEL + lo + half] = (
                    (x1 * cos - x2 * sin) * post).astype(BF16)
                o_ref[0, :, j * D_MODEL + lo + half:j * D_MODEL + lo + RET_DK] = (
                    (x2 * cos + x1 * sin) * post).astype(BF16)
        else:
            o_ref[0, :, j * D_MODEL:(j + 1) * D_MODEL] = p.astype(BF16)


def _ret_in(x, ada, gain, cos, sin, w_in):
    bsz, seq, d = x.shape
    tm = 512
    return pl.pallas_call(
        _ret_in_kernel,
        out_shape=jax.ShapeDtypeStruct((bsz, seq, RET_IN), BF16),
        grid=(bsz, seq // tm),
        in_specs=[
            pl.BlockSpec((1, tm, d), lambda b, i: (b, i, 0)),
            pl.BlockSpec((1, 6, d), lambda b, i: (b, 0, 0)),
            pl.BlockSpec((1, d), lambda b, i: (0, 0)),
            pl.BlockSpec((tm, RET_DK // 2), lambda b, i: (i, 0)),
            pl.BlockSpec((tm, RET_DK // 2), lambda b, i: (i, 0)),
            pl.BlockSpec((d, RET_IN), lambda b, i: (0, 0), pipeline_mode=pl.Buffered(1)),
        ],
        out_specs=pl.BlockSpec((1, tm, RET_IN), lambda b, i: (b, i, 0)),
        compiler_params=_params("arbitrary", "arbitrary"),
        name="ret_in",
    )(x, ada, gain, cos, sin, w_in)


def _ret_core_kernel(q_ref, k_ref, v_ref, g_ref, x_ref, ada_ref, intra_ref, qd_ref, kd_ref, cd_ref,
                     gn_ref, wo_ref, ngain_ref, wr_cat_ref, wr_hi_ref, br_ref,
                     xo_ref, h_ref, lg_ref, state, ybuf):
    @pl.when(pl.program_id(1) == 0)
    def _():
        state[...] = jnp.zeros_like(state)

    for hd in range(RET_HEADS):
        q = q_ref[0, :, hd * RET_DK:(hd + 1) * RET_DK]
        k = k_ref[0, :, hd * RET_DK:(hd + 1) * RET_DK]
        v = v_ref[0, :, hd * RET_DV:(hd + 1) * RET_DV]
        st = state[hd]
        s = _dot_nt(q, k) * intra_ref[hd]
        o = _dot(s.astype(BF16), v) + qd_ref[hd] * _dot(q, st.astype(BF16))
        kdec = (k.astype(F32) * kd_ref[hd]).astype(BF16)
        state[hd] = st * cd_ref[hd] + _dot_tn(kdec, v)
        o = _rms(o) * gn_ref[:, hd * RET_DV:(hd + 1) * RET_DV]
        g = g_ref[0, :, hd * RET_DV:(hd + 1) * RET_DV].astype(F32)
        ybuf[:, hd * RET_DV:(hd + 1) * RET_DV] = (_silu(g) * o).astype(BF16)

    y = _dot(ybuf[...], wo_ref[...])
    gate1 = ada_ref[0, 2:3, :]
    xn = x_ref[0] + gate1 * y
    xo_ref[0] = xn
    h, logits = _ffn_prenorm_logits(xn, ngain_ref[...], ada_ref[0, 3:4, :], ada_ref[0, 4:5, :],
                                    wr_cat_ref, wr_hi_ref, br_ref[...])
    h_ref[0] = h
    lg_ref[0] = logits


def _ret_core(proj, x, ada, gn, w_out, ngain, wr_cat, wr_hi, br):
    bsz, seq, d = x.shape
    c = RET_CHUNK
    f32 = F32
    log_gamma = jnp.log1p(-jnp.exp2(-5.0 - jnp.arange(RET_HEADS, dtype=f32)))
    n = jnp.arange(c, dtype=f32)
    rel = n[:, None] - n[None, :]
    intra = jnp.where(rel >= 0, jnp.exp(jnp.maximum(rel, 0.0)[None] * log_gamma[:, None, None]), 0.0)
    qd = jnp.exp((n + 1.0)[None, :] * log_gamma[:, None])[:, :, None]
    kd = jnp.exp((c - 1.0 - n)[None, :] * log_gamma[:, None])[:, :, None]
    cd = jnp.broadcast_to(jnp.exp(c * log_gamma)[:, None, None], (RET_HEADS, 1, RET_DV))
    const = lambda *shape: pl.BlockSpec(shape, lambda b, i: (0,) * len(shape))
    tok = lambda w, j: pl.BlockSpec((1, c, w), lambda b, i: (b, i, j))
    return pl.pallas_call(
        _ret_core_kernel,
        out_shape=(jax.ShapeDtypeStruct((bsz, seq, d), f32),
                   jax.ShapeDtypeStruct((bsz, seq, d), f32),
                   jax.ShapeDtypeStruct((bsz, seq, ROUTER_LANES), f32)),
        grid=(bsz, seq // c),
        in_specs=[
            tok(RET_QK, 0), tok(RET_QK, 1), tok(RET_V, 1), tok(RET_V, 2),
            tok(d, 0),
            pl.BlockSpec((1, 6, d), lambda b, i: (b, 0, 0)),
            const(RET_HEADS, c, c), const(RET_HEADS, c, 1), const(RET_HEADS, c, 1),
            const(RET_HEADS, 1, RET_DV),
            const(1, RET_V), const(RET_V, d), const(1, d),
            const(d, 2 * ROUTER_LANES), const(d, ROUTER_LANES), const(1, ROUTER_LANES),
        ],
        out_specs=(tok(d, 0), tok(d, 0), tok(ROUTER_LANES, 0)),
        scratch_shapes=[pltpu.VMEM((RET_HEADS, RET_DK, RET_DV), f32),
                        pltpu.VMEM((c, RET_V), BF16)],
        compiler_params=_params("arbitrary", "arbitrary"),
        name="ret_core",
    )(proj, proj, proj, proj, x, ada, intra, qd, kd, cd, gn, w_out, ngain, wr_cat, wr_hi, br)


def _moe_kernel(be_ref, nv_ref, tokc_ref, tokn_ref, oidx_ref, w_ref, h_hbm, w1_ref, w3_ref, w2_ref,
                y_hbm, xbuf, obuf, gsem, ssem):
    j = pl.program_id(0)
    nv = nv_ref[j]
    nv_next = nv_ref[j + 1]
    slot = j % 2

    def gather_rows(tok_ref, s, n):
        def body(r, carry):
            tok = tok_ref[0, 0, r]
            pltpu.make_async_copy(h_hbm.at[pl.ds(tok, 1), :], xbuf.at[s, pl.ds(r, 1), :],
                                  gsem.at[s]).start()
            return carry
        lax.fori_loop(0, n, body, 0)

    def scatter_rows(s, n):
        def body(r, carry):
            dst = oidx_ref[0, 0, r]
            pltpu.make_async_copy(obuf.at[s, pl.ds(r, 1), :], y_hbm.at[pl.ds(dst, 1), :],
                                  ssem.at[s]).start()
            return carry
        lax.fori_loop(0, n, body, 0)

    def wait_rows(copy, n):
        n_tiled = pl.multiple_of((n // SUBLANE) * SUBLANE, SUBLANE)

        @pl.when(n_tiled > 0)
        def _():
            copy(0, n_tiled).wait()

        def body(r, carry):
            copy(r, 1).wait()
            return carry
        lax.fori_loop(n_tiled, n, body, 0)

    def wait_gather(s, n):
        wait_rows(lambda r, k: pltpu.make_async_copy(
            h_hbm.at[pl.ds(r, k), :], xbuf.at[s, pl.ds(r, k), :], gsem.at[s]), n)

    def wait_scatter(s, n):
        wait_rows(lambda r, k: pltpu.make_async_copy(
            obuf.at[s, pl.ds(r, k), :], y_hbm.at[pl.ds(r, k), :], ssem.at[s]), n)

    @pl.when(j == 0)
    def _():
        xbuf[...] = jnp.zeros_like(xbuf)

        @pl.when(nv > 0)
        def _():
            gather_rows(tokc_ref, 0, nv)

    @pl.when(nv_next > 0)
    def _():
        gather_rows(tokn_ref, 1 - slot, nv_next)

    @pl.when(nv > 0)
    def _():
        wait_gather(slot, nv)
        x = xbuf[slot].astype(BF16)
        a = _dot(x, w1_ref[0])
        b = _dot(x, w3_ref[0])
        hid = (_silu(a) * b).astype(BF16)
        obuf[slot] = _dot(hid, w2_ref[0]) * w_ref[...]
        scatter_rows(slot, nv)

        @pl.when(j >= 1)
        def _():
            wait_scatter(1 - slot, nv_ref[jnp.maximum(j - 1, 0)])

        @pl.when(nv_next == 0)
        def _():
            wait_scatter(slot, nv)


def _moe(h, logits, w1, w3, w2):
    n_tok, d = h.shape
    bm = MOE_BM
    m = n_tok * TOP_K
    n_blocks = m // bm + N_EXPERTS
    rows = n_blocks * bm

    g_logits = logits[:, :N_GROUPS]
    g_idx = jnp.argmax(g_logits, axis=-1)
    g_gate = jnp.take_along_axis(jax.nn.softmax(g_logits, axis=-1), g_idx[:, None], axis=-1)
    e_logits = logits[:, N_GROUPS:N_GROUPS + N_EXPERTS].reshape(n_tok, N_GROUPS, EXPERTS_PER_GROUP)
    e_sel = jnp.take_along_axis(e_logits, g_idx[:, None, None], axis=1)[:, 0]
    top_w, top_i = lax.top_k(jax.nn.softmax(e_sel, axis=-1), TOP_K)
    top_w = top_w / jnp.sum(top_w, axis=-1, keepdims=True) * g_gate
    expert_id = (g_idx[:, None] * EXPERTS_PER_GROUP + top_i).astype(jnp.int32)

    flat_e = expert_id.reshape(m)
    flat_w = top_w.reshape(m)
    a = jnp.arange(m, dtype=jnp.int32)
    onehot = (flat_e[:, None] == jnp.arange(N_EXPERTS, dtype=jnp.int32)[None, :]).astype(jnp.int32)
    csum = jnp.cumsum(onehot, axis=0)
    rank = jnp.take_along_axis(csum, flat_e[:, None], axis=1)[:, 0] - 1
    counts = csum[-1]
    padded = (counts + bm - 1) // bm * bm
    padded_ends = jnp.cumsum(padded)
    padded_starts = padded_ends - padded
    dest = padded_starts[flat_e] + rank
    buf_tok = jnp.zeros((rows,), jnp.int32).at[dest].set(a // TOP_K)
    buf_oidx = jnp.zeros((rows,), jnp.int32).at[dest].set((a % TOP_K) * n_tok + a // TOP_K)
    buf_w = jnp.zeros((rows,), F32).at[dest].set(flat_w)
    block_start = jnp.arange(n_blocks + 1, dtype=jnp.int32) * bm
    block_e = jnp.minimum(jnp.searchsorted(padded_ends, block_start, side='right'),
                          N_EXPERTS - 1).astype(jnp.int32)
    n_valid = jnp.clip(padded_starts[block_e] + counts[block_e] - block_start, 0, bm)
    n_valid = jnp.where(block_start < padded_ends[-1], n_valid, 0).astype(jnp.int32)

    tok3 = buf_tok.reshape(n_blocks, 1, bm)
    oidx3 = buf_oidx.reshape(n_blocks, 1, bm)
    smem_blk = lambda f: pl.BlockSpec((1, 1, bm), f, memory_space=pltpu.SMEM)
    wspec = lambda: pl.BlockSpec((1, d, d), lambda j, be, na: (be[j], 0, 0))
    return pl.pallas_call(
        _moe_kernel,
        out_shape=jax.ShapeDtypeStruct((TOP_K * n_tok, d), F32),
        grid_spec=pltpu.PrefetchScalarGridSpec(
            num_scalar_prefetch=2,
            grid=(n_blocks,),
            in_specs=[
                smem_blk(lambda j, be, na: (j, 0, 0)),
                smem_blk(lambda j, be, na: (jnp.minimum(j + 1, n_blocks - 1), 0, 0)),
                smem_blk(lambda j, be, na: (j, 0, 0)),
                pl.BlockSpec((bm, 1), lambda j, be, na: (j, 0)),
                pl.BlockSpec(memory_space=pl.ANY),
                wspec(), wspec(), wspec(),
            ],
            out_specs=pl.BlockSpec(memory_space=pl.ANY),
            scratch_shapes=[pltpu.VMEM((2, bm, d), F32), pltpu.VMEM((2, bm, d), F32),
                            pltpu.SemaphoreType.DMA((2,)), pltpu.SemaphoreType.DMA((2,))],
        ),
        compiler_params=_params("arbitrary"),
        name="moe",
    )(block_e, n_valid, tok3, tok3, oidx3, buf_w.reshape(rows, 1), h, w1, w3, w2)


def _group_norm64(t, ind_ref, indt_ref):
    ss = _dot((t * t).astype(BF16), ind_ref[...])
    r = lax.rsqrt(ss * (1.0 / DIFF_HEAD_DIM) + EPS)
    r_hi = r.astype(BF16)
    r_lo = (r - r_hi.astype(F32)).astype(BF16)
    rb = _dot(jnp.concatenate([r_hi, r_lo], axis=1), indt_ref[...])
    return t * rb


def _kvq_kernel(x_ref, y_ref, ada0_ref, ada1_ref, kvada_ref, kvgain_ref, qgain_ref, knorm_ref, qnorm_ref,
                wkv_ref, wq_ref, ind_ref, indt_ref, xo_ref, q_ref, k_ref, v_ref):
    gate2 = ada0_ref[0, 5:6, :]
    xn = x_ref[0] + gate2 * (y_ref[0] + y_ref[1])
    xo_ref[0] = xn
    r = _rms(xn)
    hk = ((r * kvgain_ref[...]) * (1.0 + kvada_ref[0, 1:2, :]) + kvada_ref[0, 0:1, :]).astype(BF16)
    hq = ((r * qgain_ref[...]) * (1.0 + ada1_ref[0, 1:2, :]) + ada1_ref[0, 0:1, :]).astype(BF16)
    kk = _dot(hk, wkv_ref[:, :D_MODEL])
    k_ref[0] = (_group_norm64(kk, ind_ref, indt_ref) * knorm_ref[...]).astype(BF16)
    v_ref[0] = _dot(hk, wkv_ref[:, D_MODEL:]).astype(BF16)
    qq = _dot(hq, wq_ref[...])
    q_ref[0] = (_group_norm64(qq, ind_ref, indt_ref) * (qnorm_ref[...] * DIFF_HEAD_DIM ** -0.5)).astype(BF16)


def _kvq(x, y, ada0, ada1, kvada, kv_gain, q_gain, k_norm, q_norm, w_kv, w_q):
    bsz, seq, d = x.shape
    tm = 512
    nt = seq // tm
    lane_group = jnp.arange(d, dtype=jnp.int32) // DIFF_HEAD_DIM
    ind = (lane_group[:, None] == jnp.arange(LANE, dtype=jnp.int32)[None, :]).astype(BF16)
    indt = jnp.concatenate([ind.T, ind.T], axis=0)
    reps = d // DIFF_HEAD_DIM
    const = lambda *shape: pl.BlockSpec(shape, lambda b, i: (0,) * len(shape))
    tok = lambda w, dt=None: pl.BlockSpec((1, tm, w), lambda b, i: (b, i, 0))
    ada_spec = lambda rows: pl.BlockSpec((1, rows, d), lambda b, i: (b, 0, 0))
    return pl.pallas_call(
        _kvq_kernel,
        out_shape=(jax.ShapeDtypeStruct((bsz, seq, d), F32),
                   jax.ShapeDtypeStruct((bsz, seq, d), BF16),
                   jax.ShapeDtypeStruct((bsz, seq, d), BF16),
                   jax.ShapeDtypeStruct((bsz, seq, d), BF16)),
        grid=(bsz, nt),
        in_specs=[
            tok(d),
            pl.BlockSpec((TOP_K, tm, d), lambda b, i: (0, b * nt + i, 0)),
            ada_spec(6), ada_spec(6), ada_spec(2),
            const(1, d), const(1, d), const(1, d), const(1, d),
            const(d, 2 * d), const(d, d), const(d, LANE), const(2 * LANE, d),
        ],
        out_specs=(tok(d), tok(d), tok(d), tok(d)),
        compiler_params=_params("arbitrary", "arbitrary"),
        name="kvq",
    )(x, y.reshape(TOP_K, -1, d), ada0, ada1, kvada, kv_gain, q_gain,
      jnp.tile(k_norm, reps)[None, :], jnp.tile(q_norm, reps)[None, :], w_kv, w_q, ind, indt)


def _attn_kernel(lambda_init, q_ref, k_ref, v_ref, x_ref, ada_ref, lam_ref, subln_ref, wo_ref, ngain_ref,
                 wr_cat_ref, wr_hi_ref, br_ref, xo_ref, h_ref, lg_ref, obuf):
    i = pl.program_id(1)
    tq, tk = ATT_TQ, ATT_TK
    lam = lam_ref[...]
    lam_full = (jnp.exp(jnp.sum(lam[0:1] * lam[1:2], axis=-1, keepdims=True))
                - jnp.exp(jnp.sum(lam[2:3] * lam[3:4], axis=-1, keepdims=True)) + lambda_init)
    lane = lax.broadcasted_iota(jnp.int32, (1, 2 * DIFF_HEAD_DIM), 1)
    keep1 = jnp.where(lane < DIFF_HEAD_DIM, 1.0, 0.0).astype(BF16)
    keep2 = jnp.where(lane >= DIFF_HEAD_DIM, 1.0, 0.0).astype(BF16)
    row = lax.broadcasted_iota(jnp.int32, (tq, tk), 0)
    col = lax.broadcasted_iota(jnp.int32, (tq, tk), 1)
    diag_visible = col < (row // MASK_CHUNK + 1) * MASK_CHUNK
    neg_inf = jnp.float32(-jnp.inf)

    def online(carry, s, vb):
        m, l, acc = carry
        m_new = jnp.maximum(m, jnp.max(s, axis=-1, keepdims=True))
        alpha = jnp.exp(m - m_new)
        p = jnp.exp(s - m_new)
        l = alpha * l + jnp.sum(p, axis=-1, keepdims=True)
        acc = alpha * acc + _dot(p.astype(BF16), vb)
        return m_new, l, acc

    for hd in range(DIFF_HEADS):
        c0 = hd * 2 * DIFF_HEAD_DIM
        c1 = c0 + 2 * DIFF_HEAD_DIM
        qh = q_ref[0, :, c0:c1]
        q1 = qh * keep1
        q2 = qh * keep2

        def kv_step(j, carry, hd=hd, c0=c0, c1=c1, q1=q1, q2=q2):
            start = pl.multiple_of(j * tk, tk)
            kb = k_ref[0, pl.ds(start, tk), c0:c1]
            vb = v_ref[0, pl.ds(start, tk), c0:c1]
            return (online(carry[0], _dot_nt(q1, kb), vb), online(carry[1], _dot_nt(q2, kb), vb))

        init = (jnp.full((tq, 1), neg_inf, F32), jnp.zeros((tq, 1), F32), jnp.zeros((tq, 2 * DIFF_HEAD_DIM), F32))
        carry = lax.fori_loop(0, i, kv_step, (init, init))
        start = pl.multiple_of(i * tk, tk)
        kb = k_ref[0, pl.ds(start, tk), c0:c1]
        vb = v_ref[0, pl.ds(start, tk), c0:c1]
        s1 = jnp.where(diag_visible, _dot_nt(q1, kb), neg_inf)
        s2 = jnp.where(diag_visible, _dot_nt(q2, kb), neg_inf)
        _, l1, a1 = online(carry[0], s1, vb)
        _, l2, a2 = online(carry[1], s2, vb)
        o = a1 * (1.0 / l1) - lam_full * (a2 * (1.0 / l2))
        o = _rms(o) * subln_ref[...] * (1.0 - lambda_init)
        obuf[:, c0:c1] = o.astype(BF16)

    y = _dot(obuf[...], wo_ref[...])
    xn = x_ref[0] + ada_ref[0, 2:3, :] * y
    xo_ref[0] = xn
    h, logits = _ffn_prenorm_logits(xn, ngain_ref[...], ada_ref[0, 3:4, :], ada_ref[0, 4:5, :],
                                    wr_cat_ref, wr_hi_ref, br_ref[...])
    h_ref[0] = h
    lg_ref[0] = logits


def _attn(q, k, v, x, ada, lam, subln, w_out, ngain, wr_cat, wr_hi, br, lambda_init):
    bsz, seq, d = x.shape
    tq = ATT_TQ
    const = lambda *shape: pl.BlockSpec(shape, lambda b, i: (0,) * len(shape))
    tok = lambda w: pl.BlockSpec((1, tq, w), lambda b, i: (b, i, 0))
    full = pl.BlockSpec((1, seq, d), lambda b, i: (b, 0, 0))
    return pl.pallas_call(
        functools.partial(_attn_kernel, lambda_init),
        out_shape=(jax.ShapeDtypeStruct((bsz, seq, d), F32),
                   jax.ShapeDtypeStruct((bsz, seq, d), F32),
                   jax.ShapeDtypeStruct((bsz, seq, ROUTER_LANES), F32)),
        grid=(bsz, seq // tq),
        in_specs=[
            tok(d), full, full, tok(d),
            pl.BlockSpec((1, 6, d), lambda b, i: (b, 0, 0)),
            const(4, DIFF_HEAD_DIM), const(1, 2 * DIFF_HEAD_DIM), const(d, d), const(1, d),
            const(d, 2 * ROUTER_LANES), const(d, ROUTER_LANES), const(1, ROUTER_LANES),
        ],
        out_specs=(tok(d), tok(d), tok(ROUTER_LANES)),
        scratch_shapes=[pltpu.VMEM((tq, d), BF16)],
        compiler_params=_params("arbitrary", "arbitrary"),
        name="attn",
    )(q, k, v, x, ada, lam, subln, w_out, ngain, wr_cat, wr_hi, br)


def _combine_kernel(x_ref, y_ref, ada_ref, o_ref):
    o_ref[0] = x_ref[0] + ada_ref[0, 5:6, :] * (y_ref[0] + y_ref[1])


def _combine(x, y, ada):
    bsz, seq, d = x.shape
    tm = 512
    nt = seq // tm
    return pl.pallas_call(
        _combine_kernel,
        out_shape=jax.ShapeDtypeStruct((bsz, seq, d), F32),
        grid=(bsz, nt),
        in_specs=[
            pl.BlockSpec((1, tm, d), lambda b, i: (b, i, 0)),
            pl.BlockSpec((TOP_K, tm, d), lambda b, i: (0, b * nt + i, 0)),
            pl.BlockSpec((1, 6, d), lambda b, i: (b, 0, 0)),
        ],
        out_specs=pl.BlockSpec((1, tm, d), lambda b, i: (b, i, 0)),
        compiler_params=_params("arbitrary", "arbitrary"),
        name="combine",
    )(x, y.reshape(TOP_K, -1, d), ada)


def _router_weights(w_group, b_group, w_expert, b_expert):
    d = w_group.shape[0]
    pad = ROUTER_LANES - N_GROUPS - N_EXPERTS
    w = jnp.concatenate([w_group, w_expert, jnp.zeros((d, pad), F32)], axis=1)
    b = jnp.concatenate([b_group, b_expert, jnp.zeros((pad,), F32)])[None, :]
    w_hi = w.astype(BF16)
    w_lo = (w - w_hi.astype(F32)).astype(BF16)
    return jnp.concatenate([w_hi, w_lo], axis=1), w_hi, b


def kernel(x, c, ada_w, ada_b, norm_mix, norm_ffn, ret_w_in, ret_gn, ret_w_out, kv_ada_w, kv_ada_b, kv_norm, kv_w, k_norm, diff_w_q, q_norm, diff_lam, diff_subln, diff_w_out, moe_w_group, moe_b_group, moe_w_expert, moe_b_expert, moe_w1, moe_w3, moe_w2):
    bsz, seq, d = x.shape
    n_tok = bsz * seq

    ada = _ada(c, ada_w, ada_b).reshape(2, bsz, 6, d)
    kvada = _ada(c, kv_ada_w[None], kv_ada_b[None]).reshape(bsz, 2, d)

    half = RET_DK // 2
    inv_freq = 1.0 / (ROPE_BASE ** (jnp.arange(half, dtype=F32) / half))
    ang = jnp.arange(seq, dtype=F32)[:, None] * inv_freq[None, :]
    cos, sin = jnp.cos(ang), jnp.sin(ang)

    routers = [_router_weights(moe_w_group[l], moe_b_group[l], moe_w_expert[l], moe_b_expert[l])
               for l in range(2)]

    proj = _ret_in(x, ada[0], norm_mix[0][None, :], cos, sin, ret_w_in[0].astype(BF16))
    x1, h, logits = _ret_core(proj, x, ada[0], ret_gn[0][None, :], ret_w_out[0].astype(BF16),
                              norm_ffn[0][None, :], *routers[0])
    y = _moe(h.reshape(n_tok, d), logits.reshape(n_tok, ROUTER_LANES),
             moe_w1[0].astype(BF16), moe_w3[0].astype(BF16), moe_w2[0].astype(BF16))

    x2, q, k, v = _kvq(x1, y, ada[0], ada[1], kvada, kv_norm[None, :], norm_mix[1][None, :], k_norm, q_norm[0],
                       kv_w.astype(BF16), diff_w_q[0].astype(BF16))
    lambda_init = 0.8 - 0.6 * math.exp(-0.3 * 1)
    x3, h, logits = _attn(q, k, v, x2, ada[1], diff_lam[0], diff_subln[0][None, :], diff_w_out[0].astype(BF16),
                          norm_ffn[1][None, :], *routers[1], lambda_init)
    y = _moe(h.reshape(n_tok, d), logits.reshape(n_tok, ROUTER_LANES),
             moe_w1[1].astype(BF16), moe_w3[1].astype(BF16), moe_w2[1].astype(BF16))
    return _combine(x3, y, ada[1])
```

```python
import functools
import math

import jax
import jax.numpy as jnp
from jax import lax
from jax.experimental import pallas as pl
from jax.experimental.pallas import tpu as pltpu

F32 = jnp.float32
BF16 = jnp.bfloat16

D_MODEL = 1024
EPS = 1e-6

RET_HEADS = 4
RET_DK = D_MODEL // RET_HEADS
RET_DV = 2 * RET_DK
RET_QK = RET_HEADS * RET_DK
RET_V = RET_HEADS * RET_DV
RET_IN = 2 * RET_QK + 2 * RET_V
ROPE_BASE = 10000.0
RET_CHUNK = 256

DIFF_HEAD_DIM = 64
DIFF_HEADS = D_MODEL // (2 * DIFF_HEAD_DIM)
MASK_CHUNK = 64
ATT_TQ = 256
ATT_TK = 256

N_GROUPS = 4
EXPERTS_PER_GROUP = 4
N_EXPERTS = N_GROUPS * EXPERTS_PER_GROUP
TOP_K = 2
MOE_BM = 256
ROUTER_ROWS = 32
ROUTE_ROWS = 8
ROW_TILE = 512

LANE = 128
SUBLANE = 8
VMEM_LIMIT = 56 * 1024 * 1024


def _dot(a, b):
    return jnp.dot(a, b, preferred_element_type=F32)


def _dot_nt(a, b):
    return lax.dot_general(a, b, (((1,), (1,)), ((), ())), preferred_element_type=F32)


def _dot_tn(a, b):
    return lax.dot_general(a, b, (((0,), (0,)), ((), ())), preferred_element_type=F32)


def _silu(x):
    return x * (1.0 / (1.0 + jnp.exp(-x)))


def _rms(x):
    return x * lax.rsqrt(jnp.mean(x * x, axis=-1, keepdims=True) + EPS)


def _params(*sem):
    return pltpu.CompilerParams(dimension_semantics=sem, vmem_limit_bytes=VMEM_LIMIT)


def _ada_kernel(c_ref, w_ref, b_ref, o_ref):
    ca = _silu(c_ref[...])
    o_ref[0] = jnp.dot(ca, w_ref[0], preferred_element_type=F32,
                       precision=lax.Precision.HIGHEST) + b_ref[0]


def _ada(c, w, b):
    n_l, d, n = w.shape
    bsz = c.shape[0]
    tn = 1024
    return pl.pallas_call(
        _ada_kernel,
        out_shape=jax.ShapeDtypeStruct((n_l, bsz, n), F32),
        grid=(n_l, n // tn),
        in_specs=[
            pl.BlockSpec((bsz, d), lambda l, j: (0, 0)),
            pl.BlockSpec((1, d, tn), lambda l, j: (l, 0, j)),
            pl.BlockSpec((1, 1, tn), lambda l, j: (l, 0, j)),
        ],
        out_specs=pl.BlockSpec((1, bsz, tn), lambda l, j: (l, 0, j)),
        compiler_params=_params("arbitrary", "arbitrary"),
        name="ada",
    )(c, w, b.reshape(n_l, 1, n))


def _ffn_prenorm_route(first_step, xn, gain, shift, scale, wr_hi_ref, wr_lo_ref, br_ref, tri_ref,
                       route_ref, cnt_ref, carry):
    c = xn.shape[0]
    h = _rms(xn) * gain
    h = h * (1.0 + scale) + shift
    h_hi = h.astype(BF16)
    h_lo = (h - h_hi.astype(F32)).astype(BF16)
    w_hi = wr_hi_ref[...]
    lt = _dot_nt(w_hi, h_hi) + _dot_nt(wr_lo_ref[...], h_hi) + _dot_nt(w_hi, h_lo) + br_ref[...]
    neg_inf = jnp.float32(-jnp.inf)

    grow = lax.broadcasted_iota(jnp.int32, (SUBLANE, c), 0).astype(F32)
    g = jnp.where(grow < N_GROUPS, lt[N_EXPERTS:N_EXPERTS + SUBLANE], neg_inf)
    gmax = jnp.max(g, axis=0, keepdims=True)
    gidx = jnp.min(jnp.where(g == gmax, grow, float(SUBLANE)), axis=0, keepdims=True)
    gate = 1.0 / jnp.sum(jnp.exp(g - gmax), axis=0, keepdims=True)

    erow_i = lax.broadcasted_iota(jnp.int32, (N_EXPERTS, c), 0)
    erow = erow_i.astype(F32)
    egroup = (erow_i // EXPERTS_PER_GROUP).astype(F32)
    el = jnp.where(egroup == gidx, lt[0:N_EXPERTS], neg_inf)
    m1 = jnp.max(el, axis=0, keepdims=True)
    i1 = jnp.min(jnp.where(el == m1, erow, float(N_EXPERTS)), axis=0, keepdims=True)
    el2 = jnp.where(erow == i1, neg_inf, el)
    m2 = jnp.max(el2, axis=0, keepdims=True)
    i2 = jnp.min(jnp.where(el2 == m2, erow, float(N_EXPERTS)), axis=0, keepdims=True)
    t = jnp.exp(m2 - m1)
    den = 1.0 / (1.0 + t)
    w1 = gate * den
    w2 = gate * t * den

    @pl.when(first_step)
    def _():
        carry[...] = jnp.zeros_like(carry)

    oh1 = erow == i1
    oh2 = erow == i2
    oh = jnp.where(jnp.logical_or(oh1, oh2), 1.0, 0.0)
    before = carry[...] + _dot(oh.astype(BF16), tri_ref[...])
    rank1 = jnp.sum(jnp.where(oh1, before, 0.0), axis=0, keepdims=True)
    rank2 = jnp.sum(jnp.where(oh2, before, 0.0), axis=0, keepdims=True)
    total = carry[...] + jnp.sum(oh, axis=1, keepdims=True)
    carry[...] = total
    cnt_ref[...] = jnp.broadcast_to(total, cnt_ref.shape)

    rrow = lax.broadcasted_iota(jnp.int32, (ROUTE_ROWS, c), 0)
    rec = jnp.zeros((ROUTE_ROWS, c), F32)
    for idx, val in enumerate((i1, i2, rank1, rank2, w1, w2)):
        rec = jnp.where(rrow == idx, val, rec)
    route_ref[0, 0] = rec
    return h


def _route_specs(d, c):
    const = lambda *shape: pl.BlockSpec(shape, lambda b, i: (0,) * len(shape))
    return [const(ROUTER_ROWS, d), const(ROUTER_ROWS, d), const(ROUTER_ROWS, 1), const(c, c)]


def _route_out(bsz, seq, c):
    shapes = (jax.ShapeDtypeStruct((bsz, seq // c, ROUTE_ROWS, c), F32),
              jax.ShapeDtypeStruct((N_EXPERTS, LANE), F32))
    specs = (pl.BlockSpec((1, 1, ROUTE_ROWS, c), lambda b, i: (b, i, 0, 0)),
             pl.BlockSpec((N_EXPERTS, LANE), lambda b, i: (0, 0)))
    return shapes, specs


def _earlier_token_matrix(c):
    t = jnp.arange(c, dtype=jnp.int32)
    return (t[:, None] < t[None, :]).astype(BF16)


def _ret_in_kernel(x_ref, ada_ref, gain_ref, cos_ref, sin_ref, w_ref, o_ref):
    x = x_ref[0]
    shift = ada_ref[0, 0:1, :]
    scale = ada_ref[0, 1:2, :]
    h = (_rms(x) * gain_ref[...]) * (1.0 + scale) + shift
    hb = h.astype(BF16)
    cos = cos_ref[...]
    sin = sin_ref[...]
    half = RET_DK // 2
    for j in range(RET_IN // D_MODEL):
        p = _dot(hb, w_ref[:, j * D_MODEL:(j + 1) * D_MODEL])
        if j < 2:
            post = 1.0 if j == 0 else RET_DK ** -0.5
            for hd in range(RET_HEADS):
                lo = hd * RET_DK
                x1 = p[:, lo:lo + half]
                x2 = p[:, lo + half:lo + RET_DK]
                o_ref[0, :, j * D_MODEL + lo:j * D_MODEL + lo + half] = (
                    (x1 * cos - x2 * sin) * post).astype(BF16)
                o_ref[0, :, j * D_MODEL + lo + half:j * D_MODEL + lo + RET_DK] = (
                    (x2 * cos + x1 * sin) * post).astype(BF16)
        else:
            o_ref[0, :, j * D_MODEL:(j + 1) * D_MODEL] = p.astype(BF16)


def _ret_in(x, ada, gain, cos, sin, w_in):
    bsz, seq, d = x.shape
    tm = ROW_TILE
    return pl.pallas_call(
        _ret_in_kernel,
        out_shape=jax.ShapeDtypeStruct((bsz, seq, RET_IN), BF16),
        grid=(bsz, seq // tm),
        in_specs=[
            pl.BlockSpec((1, tm, d), lambda b, i: (b, i, 0)),
            pl.BlockSpec((1, 6, d), lambda b, i: (b, 0, 0)),
            pl.BlockSpec((1, d), lambda b, i: (0, 0)),
            pl.BlockSpec((tm, RET_DK // 2), lambda b, i: (i, 0)),
            pl.BlockSpec((tm, RET_DK // 2), lambda b, i: (i, 0)),
            pl.BlockSpec((d, RET_IN), lambda b, i: (0, 0), pipeline_mode=pl.Buffered(1)),
        ],
        out_specs=pl.BlockSpec((1, tm, RET_IN), lambda b, i: (b, i, 0)),
        compiler_params=_params("arbitrary", "arbitrary"),
        name="ret_in",
    )(x, ada, gain, cos, sin, w_in)


def _ret_core_kernel(q_ref, k_ref, v_ref, g_ref, x_ref, ada_ref, intra_ref, qd_ref, kd_ref, cd_ref,
                     gn_ref, wo_ref, ngain_ref, wr_hi_ref, wr_lo_ref, br_ref, tri_ref,
                     xo_ref, h_ref, route_ref, cnt_ref, state, ybuf, carry):
    @pl.when(pl.program_id(1) == 0)
    def _():
        state[...] = jnp.zeros_like(state)

    for hd in range(RET_HEADS):
        q = q_ref[0, :, hd * RET_DK:(hd + 1) * RET_DK]
        k = k_ref[0, :, hd * RET_DK:(hd + 1) * RET_DK]
        v = v_ref[0, :, hd * RET_DV:(hd + 1) * RET_DV]
        st = state[hd]
        s = _dot_nt(q, k) * intra_ref[hd]
        o = _dot(s.astype(BF16), v) + qd_ref[hd] * _dot(q, st.astype(BF16))
        kdec = (k.astype(F32) * kd_ref[hd]).astype(BF16)
        state[hd] = st * cd_ref[hd] + _dot_tn(kdec, v)
        o = _rms(o) * gn_ref[:, hd * RET_DV:(hd + 1) * RET_DV]
        g = g_ref[0, :, hd * RET_DV:(hd + 1) * RET_DV].astype(F32)
        ybuf[:, hd * RET_DV:(hd + 1) * RET_DV] = (_silu(g) * o).astype(BF16)

    y = _dot(ybuf[...], wo_ref[...])
    xn = x_ref[0] + ada_ref[0, 2:3, :] * y
    xo_ref[0] = xn
    first = jnp.logical_and(pl.program_id(0) == 0, pl.program_id(1) == 0)
    h_ref[0] = _ffn_prenorm_route(first, xn, ngain_ref[...], ada_ref[0, 3:4, :], ada_ref[0, 4:5, :],
                                  wr_hi_ref, wr_lo_ref, br_ref, tri_ref, route_ref, cnt_ref, carry)


def _ret_core(proj, x, ada, gn, w_out, ngain, router):
    bsz, seq, d = x.shape
    c = RET_CHUNK
    f32 = F32
    log_gamma = jnp.log1p(-jnp.exp2(-5.0 - jnp.arange(RET_HEADS, dtype=f32)))
    n = jnp.arange(c, dtype=f32)
    rel = n[:, None] - n[None, :]
    intra = jnp.where(rel >= 0, jnp.exp(jnp.maximum(rel, 0.0)[None] * log_gamma[:, None, None]), 0.0)
    qd = jnp.exp((n + 1.0)[None, :] * log_gamma[:, None])[:, :, None]
    kd = jnp.exp((c - 1.0 - n)[None, :] * log_gamma[:, None])[:, :, None]
    cd = jnp.broadcast_to(jnp.exp(c * log_gamma)[:, None, None], (RET_HEADS, 1, RET_DV))
    const = lambda *shape: pl.BlockSpec(shape, lambda b, i: (0,) * len(shape))
    tok = lambda w, j: pl.BlockSpec((1, c, w), lambda b, i: (b, i, j))
    route_shapes, route_specs = _route_out(bsz, seq, c)
    return pl.pallas_call(
        _ret_core_kernel,
        out_shape=(jax.ShapeDtypeStruct((bsz, seq, d), f32),
                   jax.ShapeDtypeStruct((bsz, seq, d), f32)) + route_shapes,
        grid=(bsz, seq // c),
        in_specs=[
            tok(RET_QK, 0), tok(RET_QK, 1), tok(RET_V, 1), tok(RET_V, 2),
            tok(d, 0),
            pl.BlockSpec((1, 6, d), lambda b, i: (b, 0, 0)),
            const(RET_HEADS, c, c), const(RET_HEADS, c, 1), const(RET_HEADS, c, 1),
            const(RET_HEADS, 1, RET_DV),
            const(1, RET_V), const(RET_V, d), const(1, d),
        ] + _route_specs(d, c),
        out_specs=(tok(d, 0), tok(d, 0)) + route_specs,
        scratch_shapes=[pltpu.VMEM((RET_HEADS, RET_DK, RET_DV), f32),
                        pltpu.VMEM((c, RET_V), BF16),
                        pltpu.VMEM((N_EXPERTS, 1), f32)],
        compiler_params=_params("arbitrary", "arbitrary"),
        name="ret_core",
    )(proj, proj, proj, proj, x, ada, intra, qd, kd, cd, gn, w_out, ngain, *router, _earlier_token_matrix(c))


def _dispatch_kernel(d1_ref, d2_ref, h_ref, xs_hbm, sem):
    tm = h_ref.shape[0]

    def body(t, carry):
        for d_ref in (d1_ref, d2_ref):
            pltpu.make_async_copy(h_ref.at[pl.ds(t, 1), :], xs_hbm.at[pl.ds(d_ref[0, 0, t], 1), :], sem).start()
        return carry
    lax.fori_loop(0, tm, body, 0, unroll=8)
    for _ in range(TOP_K):
        pltpu.make_async_copy(h_ref, xs_hbm.at[pl.ds(0, tm), :], sem).wait()


def _dispatch(h, dest1, dest2):
    n_tok, d = h.shape
    tm = ROW_TILE
    nt = n_tok // tm
    idx = lambda: pl.BlockSpec((1, 1, tm), lambda i: (i, 0, 0), memory_space=pltpu.SMEM)
    return pl.pallas_call(
        _dispatch_kernel,
        out_shape=jax.ShapeDtypeStruct((TOP_K * n_tok, d), F32),
        grid=(nt,),
        in_specs=[idx(), idx(), pl.BlockSpec((tm, d), lambda i: (i, 0))],
        out_specs=pl.BlockSpec(memory_space=pl.ANY),
        scratch_shapes=[pltpu.SemaphoreType.DMA(())],
        compiler_params=_params("arbitrary"),
        name="dispatch",
    )(dest1.reshape(nt, 1, tm), dest2.reshape(nt, 1, tm), h)


def _moe_kernel(blk_ref, exp_ref, nitem_ref, start_ref, cnt_ref, x_ref, w1_ref, w3_ref, w2_ref, o_ref):
    w = pl.program_id(0)
    bm = x_ref.shape[0]

    @pl.when(w < nitem_ref[0])
    def _():
        e = exp_ref[w]
        blk = blk_ref[w]
        x = x_ref[...].astype(BF16)
        a = _dot(x, w1_ref[0])
        b = _dot(x, w3_ref[0])
        y = _dot((_silu(a) * b).astype(BF16), w2_ref[0])
        row = blk * bm + lax.broadcasted_iota(jnp.int32, (bm, 1), 0)
        mine = jnp.logical_and(row >= start_ref[e], row < start_ref[e] + cnt_ref[e])
        first_visit = jnp.logical_or(w == 0, blk_ref[jnp.maximum(w - 1, 0)] != blk)

        @pl.when(first_visit)
        def _():
            o_ref[...] = jnp.where(mine, y, 0.0)

        @pl.when(jnp.logical_not(first_visit))
        def _():
            o_ref[...] = jnp.where(mine, y, o_ref[...])


def _moe(xs, counts, w1, w3, w2):
    m, d = xs.shape
    bm = MOE_BM
    n_blocks = m // bm
    n_items = n_blocks + N_EXPERTS - 1
    ends = jnp.cumsum(counts)
    starts = ends - counts
    first_blk = starts // bm
    last_blk = jnp.maximum(ends - 1, starts) // bm
    visits = jnp.where(counts > 0, last_blk - first_blk + 1, 0)
    item_end = jnp.cumsum(visits)
    item_start = item_end - visits
    n_item = item_end[-1]
    w = jnp.minimum(jnp.arange(n_items, dtype=jnp.int32), n_item - 1)
    item_e = jnp.searchsorted(item_end, w, side='right').astype(jnp.int32)
    item_blk = (first_blk[item_e] + w - item_start[item_e]).astype(jnp.int32)
    wspec = lambda: pl.BlockSpec((1, d, d), lambda i, blk, e, *_: (e[i], 0, 0))
    return pl.pallas_call(
        _moe_kernel,
        out_shape=jax.ShapeDtypeStruct((m, d), F32),
        grid_spec=pltpu.PrefetchScalarGridSpec(
            num_scalar_prefetch=5,
            grid=(n_items,),
            in_specs=[pl.BlockSpec((bm, d), lambda i, blk, *_: (blk[i], 0)), wspec(), wspec(), wspec()],
            out_specs=pl.BlockSpec((bm, d), lambda i, blk, *_: (blk[i], 0)),
        ),
        compiler_params=_params("arbitrary"),
        name="moe",
    )(item_blk, item_e, n_item.astype(jnp.int32).reshape(1), starts.astype(jnp.int32), counts.astype(jnp.int32),
      xs, w1, w3, w2)


class _RowGather:
    def __init__(self, ys_hbm, buf, sem, tm):
        self.ys, self.buf, self.sem, self.tm = ys_hbm, buf, sem, tm

    def start(self, d_refs, slot):
        def body(t, carry):
            for k, d_ref in enumerate(d_refs):
                pltpu.make_async_copy(self.ys.at[pl.ds(d_ref[0, 0, t], 1), :],
                                      self.buf.at[slot, k, pl.ds(t, 1), :], self.sem.at[slot]).start()
            return carry
        lax.fori_loop(0, self.tm, body, 0, unroll=8)

    def wait(self, slot):
        for k in range(TOP_K):
            pltpu.make_async_copy(self.ys.at[pl.ds(0, self.tm), :], self.buf.at[slot, k], self.sem.at[slot]).wait()

    def combined(self, step, n_steps, cur_refs, nxt_refs, w1_ref, w2_ref):
        slot = step % 2

        @pl.when(step == 0)
        def _():
            self.start(cur_refs, 0)

        @pl.when(step + 1 < n_steps)
        def _():
            self.start(nxt_refs, 1 - slot)

        self.wait(slot)
        return w1_ref[...] * self.buf[slot, 0] + w2_ref[...] * self.buf[slot, 1]


def _gather_specs(tm, nt_total, step_of):
    cur = lambda: pl.BlockSpec((1, 1, tm), lambda *g: (step_of(*g), 0, 0), memory_space=pltpu.SMEM)
    nxt = lambda: pl.BlockSpec((1, 1, tm), lambda *g: (jnp.minimum(step_of(*g) + 1, nt_total - 1), 0, 0),
                               memory_space=pltpu.SMEM)
    wcol = lambda: pl.BlockSpec((tm, 1), lambda *g: (step_of(*g), 0))
    return [cur(), cur(), nxt(), nxt(), wcol(), wcol()]


def _gather_scratch(tm, d):
    return [pltpu.VMEM((2, TOP_K, tm, d), F32), pltpu.SemaphoreType.DMA((2,))]


def _group_norm64(t, ind_ref, indt_ref):
    ss = _dot((t * t).astype(BF16), ind_ref[...])
    r = lax.rsqrt(ss * (1.0 / DIFF_HEAD_DIM) + EPS)
    r_hi = r.astype(BF16)
    r_lo = (r - r_hi.astype(F32)).astype(BF16)
    rb = _dot(jnp.concatenate([r_hi, r_lo], axis=1), indt_ref[...])
    return t * rb


def _kvq_kernel(d1c, d2c, d1n, d2n, w1_ref, w2_ref, ys_hbm, x_ref, ada0_ref, ada1_ref, kvada_ref, kvgain_ref,
                qgain_ref, knorm_ref, qnorm_ref, wk_ref, wvt_ref, wq_ref, ind_ref, indt_ref,
                xo_ref, q_ref, k_ref, vt_ref, gbuf, gsem):
    tm = x_ref.shape[1]
    step = pl.program_id(0) * pl.num_programs(1) + pl.program_id(1)
    n_steps = pl.num_programs(0) * pl.num_programs(1)
    moe = _RowGather(ys_hbm, gbuf, gsem, tm).combined(step, n_steps, (d1c, d2c), (d1n, d2n), w1_ref, w2_ref)
    xn = x_ref[0] + ada0_ref[0, 5:6, :] * moe
    xo_ref[0] = xn
    r = _rms(xn)
    hk = ((r * kvgain_ref[...]) * (1.0 + kvada_ref[0, 1:2, :]) + kvada_ref[0, 0:1, :]).astype(BF16)
    hq = ((r * qgain_ref[...]) * (1.0 + ada1_ref[0, 1:2, :]) + ada1_ref[0, 0:1, :]).astype(BF16)
    kk = _dot(hk, wk_ref[...])
    k_ref[0] = (_group_norm64(kk, ind_ref, indt_ref) * knorm_ref[...]).astype(BF16)
    vt = _dot_nt(wvt_ref[...], hk).astype(BF16)
    for j in range(tm // ATT_TK):
        vt_ref[0, j] = vt[:, j * ATT_TK:(j + 1) * ATT_TK]
    qq = _dot(hq, wq_ref[...])
    q_ref[0] = (_group_norm64(qq, ind_ref, indt_ref) * qnorm_ref[...]).astype(BF16)


def _kvq(x, ys, dest1, dest2, w1, w2, ada0, ada1, kvada, kv_gain, q_gain, k_norm, q_norm, w_k, w_vt, w_q):
    bsz, seq, d = x.shape
    tm = ROW_TILE
    nt = seq // tm
    nt_total = bsz * nt
    lane_group = jnp.arange(d, dtype=jnp.int32) // DIFF_HEAD_DIM
    ind = (lane_group[:, None] == jnp.arange(LANE, dtype=jnp.int32)[None, :]).astype(BF16)
    indt = jnp.concatenate([ind.T, ind.T], axis=0)
    reps = d // DIFF_HEAD_DIM
    q_scale = jnp.tile(q_norm, reps)[None, :] * (DIFF_HEAD_DIM ** -0.5 * math.log2(math.e))
    const = lambda *shape: pl.BlockSpec(shape, lambda b, i: (0,) * len(shape))
    tok = lambda w: pl.BlockSpec((1, tm, w), lambda b, i: (b, i, 0))
    ada_spec = lambda rows: pl.BlockSpec((1, rows, d), lambda b, i: (b, 0, 0))
    col = lambda a: a.reshape(-1, 1)
    idx = lambda a: a.reshape(nt_total, 1, tm)
    return pl.pallas_call(
        _kvq_kernel,
        out_shape=(jax.ShapeDtypeStruct((bsz, seq, d), F32),
                   jax.ShapeDtypeStruct((bsz, seq, d), BF16),
                   jax.ShapeDtypeStruct((bsz, seq, d), BF16),
                   jax.ShapeDtypeStruct((bsz, seq // ATT_TK, d, ATT_TK), BF16)),
        grid=(bsz, nt),
        in_specs=_gather_specs(tm, nt_total, lambda b, i: b * nt + i) + [
            pl.BlockSpec(memory_space=pl.ANY),
            tok(d),
            ada_spec(6), ada_spec(6), ada_spec(2),
            const(1, d), const(1, d), const(1, d), const(1, d),
            const(d, d), const(d, d), const(d, d), const(d, LANE), const(2 * LANE, d),
        ],
        out_specs=(tok(d), tok(d), tok(d),
                   pl.BlockSpec((1, tm // ATT_TK, d, ATT_TK), lambda b, i: (b, i, 0, 0))),
        scratch_shapes=_gather_scratch(tm, d),
        compiler_params=_params("arbitrary", "arbitrary"),
        name="kvq",
    )(idx(dest1), idx(dest2), idx(dest1), idx(dest2), col(w1), col(w2), ys, x, ada0, ada1, kvada,
      kv_gain, q_gain, jnp.tile(k_norm, reps)[None, :], q_scale, w_k, w_vt, w_q, ind, indt)


def _attn_kernel(lambda_init, q_ref, k_ref, vt_ref, x_ref, ada_ref, lam_ref, subln_ref, wo_ref, ngain_ref,
                 wr_hi_ref, wr_lo_ref, br_ref, tri_ref, xo_ref, h_ref, route_ref, cnt_ref, obuf, carry):
    i = pl.program_id(1)
    tq, tk = ATT_TQ, ATT_TK
    hd2 = 2 * DIFF_HEAD_DIM
    lam = lam_ref[...]
    lam_full = (jnp.exp(jnp.sum(lam[0:1] * lam[1:2], axis=-1, keepdims=True))
                - jnp.exp(jnp.sum(lam[2:3] * lam[3:4], axis=-1, keepdims=True)) + lambda_init)
    lane = lax.broadcasted_iota(jnp.int32, (1, hd2), 1)
    keep1 = jnp.where(lane < DIFF_HEAD_DIM, 1.0, 0.0).astype(BF16)
    keep2 = jnp.where(lane >= DIFF_HEAD_DIM, 1.0, 0.0).astype(BF16)
    key = lax.broadcasted_iota(jnp.int32, (tk, tq), 0)
    qry = lax.broadcasted_iota(jnp.int32, (tk, tq), 1)
    diag_visible = key < (qry // MASK_CHUNK + 1) * MASK_CHUNK
    neg_inf = jnp.float32(-jnp.inf)

    def online(carry_t, st, vtb):
        m, l, acc = carry_t
        m_new = jnp.maximum(m, jnp.max(st, axis=0, keepdims=True))
        alpha = jnp.exp2(m - m_new)
        p = jnp.exp2(st - m_new)
        l = alpha * l + jnp.sum(p, axis=0, keepdims=True)
        acc = alpha * acc + _dot(vtb, p.astype(BF16))
        return m_new, l, acc

    for hd in range(DIFF_HEADS):
        c0 = hd * hd2
        c1 = c0 + hd2
        qh = q_ref[0, :, c0:c1]
        q1 = qh * keep1
        q2 = qh * keep2

        def kv_step(j, carry_t, c0=c0, c1=c1, q1=q1, q2=q2):
            kb = k_ref[0, pl.ds(pl.multiple_of(j * tk, tk), tk), c0:c1]
            vtb = vt_ref[0, j, c0:c1, :]
            return (online(carry_t[0], _dot_nt(kb, q1), vtb), online(carry_t[1], _dot_nt(kb, q2), vtb))

        init = (jnp.full((1, tq), neg_inf, F32), jnp.zeros((1, tq), F32), jnp.zeros((hd2, tq), F32))
        carry_t = lax.fori_loop(0, i, kv_step, (init, init))
        kb = k_ref[0, pl.ds(pl.multiple_of(i * tk, tk), tk), c0:c1]
        vtb = vt_ref[0, i, c0:c1, :]
        _, l1, a1 = online(carry_t[0], jnp.where(diag_visible, _dot_nt(kb, q1), neg_inf), vtb)
        _, l2, a2 = online(carry_t[1], jnp.where(diag_visible, _dot_nt(kb, q2), neg_inf), vtb)
        o = a1 * (1.0 / l1) - lam_full * (a2 * (1.0 / l2))
        o = o * lax.rsqrt(jnp.mean(o * o, axis=0, keepdims=True) + EPS)
        obuf[c0:c1, :] = (o * (subln_ref[...] * (1.0 - lambda_init))).astype(BF16)

    y = _dot_tn(obuf[...], wo_ref[...])
    xn = x_ref[0] + ada_ref[0, 2:3, :] * y
    xo_ref[0] = xn
    first = jnp.logical_and(pl.program_id(0) == 0, i == 0)
    h_ref[0] = _ffn_prenorm_route(first, xn, ngain_ref[...], ada_ref[0, 3:4, :], ada_ref[0, 4:5, :],
                                  wr_hi_ref, wr_lo_ref, br_ref, tri_ref, route_ref, cnt_ref, carry)


def _attn(q, k, vt, x, ada, lam, subln, w_out, ngain, router, lambda_init):
    bsz, seq, d = x.shape
    tq = ATT_TQ
    const = lambda *shape: pl.BlockSpec(shape, lambda b, i: (0,) * len(shape))
    tok = lambda w: pl.BlockSpec((1, tq, w), lambda b, i: (b, i, 0))
    route_shapes, route_specs = _route_out(bsz, seq, tq)
    return pl.pallas_call(
        functools.partial(_attn_kernel, lambda_init),
        out_shape=(jax.ShapeDtypeStruct((bsz, seq, d), F32),
                   jax.ShapeDtypeStruct((bsz, seq, d), F32)) + route_shapes,
        grid=(bsz, seq // tq),
        in_specs=[
            tok(d),
            pl.BlockSpec((1, seq, d), lambda b, i: (b, 0, 0)),
            pl.BlockSpec((1, seq // ATT_TK, d, ATT_TK), lambda b, i: (b, 0, 0, 0)),
            tok(d),
            pl.BlockSpec((1, 6, d), lambda b, i: (b, 0, 0)),
            const(4, DIFF_HEAD_DIM), const(2 * DIFF_HEAD_DIM, 1), const(d, d), const(1, d),
        ] + _route_specs(d, tq),
        out_specs=(tok(d), tok(d)) + route_specs,
        scratch_shapes=[pltpu.VMEM((d, tq), BF16), pltpu.VMEM((N_EXPERTS, 1), F32)],
        compiler_params=_params("arbitrary", "arbitrary"),
        name="attn",
    )(q, k, vt, x, ada, lam, subln, w_out, ngain, *router, _earlier_token_matrix(tq))


def _combine_kernel(d1c, d2c, d1n, d2n, w1_ref, w2_ref, ys_hbm, x_ref, ada_ref, o_ref, gbuf, gsem):
    tm = x_ref.shape[1]
    step = pl.program_id(0) * pl.num_programs(1) + pl.program_id(1)
    n_steps = pl.num_programs(0) * pl.num_programs(1)
    moe = _RowGather(ys_hbm, gbuf, gsem, tm).combined(step, n_steps, (d1c, d2c), (d1n, d2n), w1_ref, w2_ref)
    o_ref[0] = x_ref[0] + ada_ref[0, 5:6, :] * moe


def _combine(x, ys, dest1, dest2, w1, w2, ada):
    bsz, seq, d = x.shape
    tm = ROW_TILE
    nt = seq // tm
    nt_total = bsz * nt
    col = lambda a: a.reshape(-1, 1)
    idx = lambda a: a.reshape(nt_total, 1, tm)
    return pl.pallas_call(
        _combine_kernel,
        out_shape=jax.ShapeDtypeStruct((bsz, seq, d), F32),
        grid=(bsz, nt),
        in_specs=_gather_specs(tm, nt_total, lambda b, i: b * nt + i) + [
            pl.BlockSpec(memory_space=pl.ANY),
            pl.BlockSpec((1, tm, d), lambda b, i: (b, i, 0)),
            pl.BlockSpec((1, 6, d), lambda b, i: (b, 0, 0)),
        ],
        out_specs=pl.BlockSpec((1, tm, d), lambda b, i: (b, i, 0)),
        scratch_shapes=_gather_scratch(tm, d),
        compiler_params=_params("arbitrary", "arbitrary"),
        name="combine",
    )(idx(dest1), idx(dest2), idx(dest1), idx(dest2), col(w1), col(w2), ys, x, ada)


def _router_weights(w_group, b_group, w_expert, b_expert):
    d = w_group.shape[0]
    pad = ROUTER_ROWS - N_EXPERTS - N_GROUPS
    w = jnp.concatenate([w_expert.T, w_group.T, jnp.zeros((pad, d), F32)], axis=0)
    b = jnp.concatenate([b_expert, b_group, jnp.zeros((pad,), F32)])[:, None]
    w_hi = w.astype(BF16)
    w_lo = (w - w_hi.astype(F32)).astype(BF16)
    return w_hi, w_lo, b


def _routing_tables(route, cnt):
    rec = lambda r: route[:, :, r, :].reshape(-1)
    counts = cnt[:, 0].astype(jnp.int32)
    starts = jnp.cumsum(counts) - counts
    e1 = rec(0).astype(jnp.int32)
    e2 = rec(1).astype(jnp.int32)
    dest1 = starts[e1] + rec(2).astype(jnp.int32)
    dest2 = starts[e2] + rec(3).astype(jnp.int32)
    return dest1, dest2, rec(4), rec(5), counts


def kernel(x, c, ada_w, ada_b, norm_mix, norm_ffn, ret_w_in, ret_gn, ret_w_out, kv_ada_w, kv_ada_b, kv_norm, kv_w, k_norm, diff_w_q, q_norm, diff_lam, diff_subln, diff_w_out, moe_w_group, moe_b_group, moe_w_expert, moe_b_expert, moe_w1, moe_w3, moe_w2):
    bsz, seq, d = x.shape
    n_tok = bsz * seq

    ada = _ada(c, ada_w, ada_b).reshape(2, bsz, 6, d)
    kvada = _ada(c, kv_ada_w[None], kv_ada_b[None]).reshape(bsz, 2, d)

    half = RET_DK // 2
    inv_freq = 1.0 / (ROPE_BASE ** (jnp.arange(half, dtype=F32) / half))
    ang = jnp.arange(seq, dtype=F32)[:, None] * inv_freq[None, :]
    cos, sin = jnp.cos(ang), jnp.sin(ang)

    routers = [_router_weights(moe_w_group[l], moe_b_group[l], moe_w_expert[l], moe_b_expert[l])
               for l in range(2)]

    def moe_layer(h, route, cnt, layer):
        dest1, dest2, w1, w2, counts = _routing_tables(route, cnt)
        xs = _dispatch(h.reshape(n_tok, d), dest1, dest2)
        ys = _moe(xs, counts, moe_w1[layer].astype(BF16), moe_w3[layer].astype(BF16), moe_w2[layer].astype(BF16))
        return ys, dest1, dest2, w1, w2

    proj = _ret_in(x, ada[0], norm_mix[0][None, :], cos, sin, ret_w_in[0].astype(BF16))
    x1, h, route, cnt = _ret_core(proj, x, ada[0], ret_gn[0][None, :], ret_w_out[0].astype(BF16),
                                  norm_ffn[0][None, :], routers[0])
    moe0 = moe_layer(h, route, cnt, 0)

    x2, q, k, vt = _kvq(x1, *moe0, ada[0], ada[1], kvada, kv_norm[None, :], norm_mix[1][None, :], k_norm, q_norm[0],
                        kv_w[:, :d].astype(BF16), kv_w[:, d:].T.astype(BF16), diff_w_q[0].astype(BF16))
    lambda_init = 0.8 - 0.6 * math.exp(-0.3 * 1)
    x3, h, route, cnt = _attn(q, k, vt, x2, ada[1], diff_lam[0], diff_subln[0][:, None], diff_w_out[0].astype(BF16),
                              norm_ffn[1][None, :], routers[1], lambda_init)
    moe1 = moe_layer(h, route, cnt, 1)
    return _combine(x3, *moe1, ada[1])
```

```python
import functools
import math

import jax
import jax.numpy as jnp
from jax import lax
from jax.experimental import pallas as pl
from jax.experimental.pallas import tpu as pltpu

F32 = jnp.float32
BF16 = jnp.bfloat16

D_MODEL = 1024
EPS = 1e-6

RET_HEADS = 4
RET_DK = D_MODEL // RET_HEADS
RET_DV = 2 * RET_DK
RET_QK = RET_HEADS * RET_DK
RET_V = RET_HEADS * RET_DV
RET_IN = 2 * RET_QK + 2 * RET_V
ROPE_BASE = 10000.0
RET_CHUNK = 256

DIFF_HEAD_DIM = 64
DIFF_HEADS = D_MODEL // (2 * DIFF_HEAD_DIM)
MASK_CHUNK = 64
ATT_TQ = 256
ATT_TK = 256

N_GROUPS = 4
EXPERTS_PER_GROUP = 4
N_EXPERTS = N_GROUPS * EXPERTS_PER_GROUP
TOP_K = 2
MOE_BM = 256
ROUTER_ROWS = 32
ROUTE_ROWS = 8
ROW_TILE = 512

LANE = 128
SUBLANE = 8
VMEM_LIMIT = 56 * 1024 * 1024


def _dot(a, b):
    return jnp.dot(a, b, preferred_element_type=F32)


def _dot_nt(a, b):
    return lax.dot_general(a, b, (((1,), (1,)), ((), ())), preferred_element_type=F32)


def _dot_tn(a, b):
    return lax.dot_general(a, b, (((0,), (0,)), ((), ())), preferred_element_type=F32)


def _silu(x):
    return x * (1.0 / (1.0 + jnp.exp(-x)))


def _rms(x):
    return x * lax.rsqrt(jnp.mean(x * x, axis=-1, keepdims=True) + EPS)


def _params(*sem):
    return pltpu.CompilerParams(dimension_semantics=sem, vmem_limit_bytes=VMEM_LIMIT)


def _ada_kernel(c_ref, w_ref, b_ref, o_ref):
    ca = _silu(c_ref[...])
    o_ref[0] = jnp.dot(ca, w_ref[0], preferred_element_type=F32,
                       precision=lax.Precision.HIGHEST) + b_ref[0]


def _ada(c, w, b):
    n_l, d, n = w.shape
    bsz = c.shape[0]
    tn = 1024
    return pl.pallas_call(
        _ada_kernel,
        out_shape=jax.ShapeDtypeStruct((n_l, bsz, n), F32),
        grid=(n_l, n // tn),
        in_specs=[
            pl.BlockSpec((bsz, d), lambda l, j: (0, 0)),
            pl.BlockSpec((1, d, tn), lambda l, j: (l, 0, j)),
            pl.BlockSpec((1, 1, tn), lambda l, j: (l, 0, j)),
        ],
        out_specs=pl.BlockSpec((1, bsz, tn), lambda l, j: (l, 0, j)),
        compiler_params=_params("arbitrary", "arbitrary"),
        name="ada",
    )(c, w, b.reshape(n_l, 1, n))


def _ffn_prenorm_route(first_step, xn, gain, shift, scale, wr_hi_ref, wr_lo_ref, br_ref, tri_ref,
                       route_ref, cnt_ref, carry):
    c = xn.shape[0]
    h = _rms(xn) * gain
    h = h * (1.0 + scale) + shift
    h_hi = h.astype(BF16)
    h_lo = (h - h_hi.astype(F32)).astype(BF16)
    w_hi = wr_hi_ref[...]
    lt = _dot_nt(w_hi, h_hi) + _dot_nt(wr_lo_ref[...], h_hi) + _dot_nt(w_hi, h_lo) + br_ref[...]
    neg_inf = jnp.float32(-jnp.inf)

    grow = lax.broadcasted_iota(jnp.int32, (SUBLANE, c), 0).astype(F32)
    g = jnp.where(grow < N_GROUPS, lt[N_EXPERTS:N_EXPERTS + SUBLANE], neg_inf)
    gmax = jnp.max(g, axis=0, keepdims=True)
    gidx = jnp.min(jnp.where(g == gmax, grow, float(SUBLANE)), axis=0, keepdims=True)
    gate = 1.0 / jnp.sum(jnp.exp(g - gmax), axis=0, keepdims=True)

    erow_i = lax.broadcasted_iota(jnp.int32, (N_EXPERTS, c), 0)
    erow = erow_i.astype(F32)
    egroup = (erow_i // EXPERTS_PER_GROUP).astype(F32)
    el = jnp.where(egroup == gidx, lt[0:N_EXPERTS], neg_inf)
    m1 = jnp.max(el, axis=0, keepdims=True)
    i1 = jnp.min(jnp.where(el == m1, erow, float(N_EXPERTS)), axis=0, keepdims=True)
    el2 = jnp.where(erow == i1, neg_inf, el)
    m2 = jnp.max(el2, axis=0, keepdims=True)
    i2 = jnp.min(jnp.where(el2 == m2, erow, float(N_EXPERTS)), axis=0, keepdims=True)
    t = jnp.exp(m2 - m1)
    den = 1.0 / (1.0 + t)
    w1 = gate * den
    w2 = gate * t * den

    @pl.when(first_step)
    def _():
        carry[...] = jnp.zeros_like(carry)

    oh1 = erow == i1
    oh2 = erow == i2
    oh = jnp.where(jnp.logical_or(oh1, oh2), 1.0, 0.0)
    before = carry[...] + _dot(oh.astype(BF16), tri_ref[...])
    rank1 = jnp.sum(jnp.where(oh1, before, 0.0), axis=0, keepdims=True)
    rank2 = jnp.sum(jnp.where(oh2, before, 0.0), axis=0, keepdims=True)
    total = carry[...] + jnp.sum(oh, axis=1, keepdims=True)
    carry[...] = total
    cnt_ref[...] = jnp.broadcast_to(total, cnt_ref.shape)

    rrow = lax.broadcasted_iota(jnp.int32, (ROUTE_ROWS, c), 0)
    rec = jnp.zeros((ROUTE_ROWS, c), F32)
    for idx, val in enumerate((i1, i2, rank1, rank2, w1, w2)):
        rec = jnp.where(rrow == idx, val, rec)
    route_ref[0, 0] = rec
    return h


def _route_specs(d, c):
    const = lambda *shape: pl.BlockSpec(shape, lambda b, i: (0,) * len(shape))
    return [const(ROUTER_ROWS, d), const(ROUTER_ROWS, d), const(ROUTER_ROWS, 1), const(c, c)]


def _route_out(bsz, seq, c):
    shapes = (jax.ShapeDtypeStruct((bsz, seq // c, ROUTE_ROWS, c), F32),
              jax.ShapeDtypeStruct((N_EXPERTS, LANE), F32))
    specs = (pl.BlockSpec((1, 1, ROUTE_ROWS, c), lambda b, i: (b, i, 0, 0)),
             pl.BlockSpec((N_EXPERTS, LANE), lambda b, i: (0, 0)))
    return shapes, specs


def _earlier_token_matrix(c):
    t = jnp.arange(c, dtype=jnp.int32)
    return (t[:, None] < t[None, :]).astype(BF16)


def _ret_in_kernel(x_ref, ada_ref, gain_ref, cos_ref, sin_ref, w_ref, o_ref):
    x = x_ref[0]
    shift = ada_ref[0, 0:1, :]
    scale = ada_ref[0, 1:2, :]
    h = (_rms(x) * gain_ref[...]) * (1.0 + scale) + shift
    hb = h.astype(BF16)
    cos = cos_ref[...]
    sin = sin_ref[...]
    half = RET_DK // 2
    for j in range(RET_IN // D_MODEL):
        p = _dot(hb, w_ref[:, j * D_MODEL:(j + 1) * D_MODEL])
        if j < 2:
            post = 1.0 if j == 0 else RET_DK ** -0.5
            for hd in range(RET_HEADS):
                lo = hd * RET_DK
                x1 = p[:, lo:lo + half]
                x2 = p[:, lo + half:lo + RET_DK]
                o_ref[0, :, j * D_MODEL + lo:j * D_MODEL + lo + half] = (
                    (x1 * cos - x2 * sin) * post).astype(BF16)
                o_ref[0, :, j * D_MODEL + lo + half:j * D_MODEL + lo + RET_DK] = (
                    (x2 * cos + x1 * sin) * post).astype(BF16)
        else:
            o_ref[0, :, j * D_MODEL:(j + 1) * D_MODEL] = p.astype(BF16)


def _ret_in(x, ada, gain, cos, sin, w_in):
    bsz, seq, d = x.shape
    tm = ROW_TILE
    return pl.pallas_call(
        _ret_in_kernel,
        out_shape=jax.ShapeDtypeStruct((bsz, seq, RET_IN), BF16),
        grid=(bsz, seq // tm),
        in_specs=[
            pl.BlockSpec((1, tm, d), lambda b, i: (b, i, 0)),
            pl.BlockSpec((1, 6, d), lambda b, i: (b, 0, 0)),
            pl.BlockSpec((1, d), lambda b, i: (0, 0)),
            pl.BlockSpec((tm, RET_DK // 2), lambda b, i: (i, 0)),
            pl.BlockSpec((tm, RET_DK // 2), lambda b, i: (i, 0)),
            pl.BlockSpec((d, RET_IN), lambda b, i: (0, 0), pipeline_mode=pl.Buffered(1)),
        ],
        out_specs=pl.BlockSpec((1, tm, RET_IN), lambda b, i: (b, i, 0)),
        compiler_params=_params("arbitrary", "arbitrary"),
        name="ret_in",
    )(x, ada, gain, cos, sin, w_in)


def _ret_core_kernel(q_ref, k_ref, v_ref, g_ref, x_ref, ada_ref, intra_ref, qd_ref, kd_ref, cd_ref,
                     gn_ref, wo_ref, ngain_ref, wr_hi_ref, wr_lo_ref, br_ref, tri_ref,
                     xo_ref, h_ref, route_ref, cnt_ref, state, ybuf, carry):
    @pl.when(pl.program_id(1) == 0)
    def _():
        state[...] = jnp.zeros_like(state)

    for hd in range(RET_HEADS):
        q = q_ref[0, :, hd * RET_DK:(hd + 1) * RET_DK]
        k = k_ref[0, :, hd * RET_DK:(hd + 1) * RET_DK]
        v = v_ref[0, :, hd * RET_DV:(hd + 1) * RET_DV]
        st = state[hd]
        s = _dot_nt(q, k) * intra_ref[hd]
        o = _dot(s.astype(BF16), v) + qd_ref[hd] * _dot(q, st.astype(BF16))
        kdec = (k.astype(F32) * kd_ref[hd]).astype(BF16)
        state[hd] = st * cd_ref[hd] + _dot_tn(kdec, v)
        o = _rms(o) * gn_ref[:, hd * RET_DV:(hd + 1) * RET_DV]
        g = g_ref[0, :, hd * RET_DV:(hd + 1) * RET_DV].astype(F32)
        ybuf[:, hd * RET_DV:(hd + 1) * RET_DV] = (_silu(g) * o).astype(BF16)

    y = _dot(ybuf[...], wo_ref[...])
    xn = x_ref[0] + ada_ref[0, 2:3, :] * y
    xo_ref[0] = xn
    first = jnp.logical_and(pl.program_id(0) == 0, pl.program_id(1) == 0)
    h_ref[0] = _ffn_prenorm_route(first, xn, ngain_ref[...], ada_ref[0, 3:4, :], ada_ref[0, 4:5, :],
                                  wr_hi_ref, wr_lo_ref, br_ref, tri_ref, route_ref, cnt_ref, carry)


def _ret_core(proj, x, ada, gn, w_out, ngain, router):
    bsz, seq, d = x.shape
    c = RET_CHUNK
    f32 = F32
    log_gamma = jnp.log1p(-jnp.exp2(-5.0 - jnp.arange(RET_HEADS, dtype=f32)))
    n = jnp.arange(c, dtype=f32)
    rel = n[:, None] - n[None, :]
    intra = jnp.where(rel >= 0, jnp.exp(jnp.maximum(rel, 0.0)[None] * log_gamma[:, None, None]), 0.0)
    qd = jnp.exp((n + 1.0)[None, :] * log_gamma[:, None])[:, :, None]
    kd = jnp.exp((c - 1.0 - n)[None, :] * log_gamma[:, None])[:, :, None]
    cd = jnp.broadcast_to(jnp.exp(c * log_gamma)[:, None, None], (RET_HEADS, 1, RET_DV))
    const = lambda *shape: pl.BlockSpec(shape, lambda b, i: (0,) * len(shape))
    tok = lambda w, j: pl.BlockSpec((1, c, w), lambda b, i: (b, i, j))
    route_shapes, route_specs = _route_out(bsz, seq, c)
    return pl.pallas_call(
        _ret_core_kernel,
        out_shape=(jax.ShapeDtypeStruct((bsz, seq, d), f32),
                   jax.ShapeDtypeStruct((bsz, seq, d), f32)) + route_shapes,
        grid=(bsz, seq // c),
        in_specs=[
            tok(RET_QK, 0), tok(RET_QK, 1), tok(RET_V, 1), tok(RET_V, 2),
            tok(d, 0),
            pl.BlockSpec((1, 6, d), lambda b, i: (b, 0, 0)),
            const(RET_HEADS, c, c), const(RET_HEADS, c, 1), const(RET_HEADS, c, 1),
            const(RET_HEADS, 1, RET_DV),
            const(1, RET_V), const(RET_V, d), const(1, d),
        ] + _route_specs(d, c),
        out_specs=(tok(d, 0), tok(d, 0)) + route_specs,
        scratch_shapes=[pltpu.VMEM((RET_HEADS, RET_DK, RET_DV), f32),
                        pltpu.VMEM((c, RET_V), BF16),
                        pltpu.VMEM((N_EXPERTS, 1), f32)],
        compiler_params=_params("arbitrary", "arbitrary"),
        name="ret_core",
    )(proj, proj, proj, proj, x, ada, intra, qd, kd, cd, gn, w_out, ngain, *router, _earlier_token_matrix(c))


def _dispatch_kernel(d1_ref, d2_ref, h_ref, xs_hbm, sem):
    tm = h_ref.shape[0]

    def body(t, carry):
        for d_ref in (d1_ref, d2_ref):
            pltpu.make_async_copy(h_ref.at[pl.ds(t, 1), :], xs_hbm.at[pl.ds(d_ref[0, 0, t], 1), :], sem).start()
        return carry
    lax.fori_loop(0, tm, body, 0, unroll=8)
    for _ in range(TOP_K):
        pltpu.make_async_copy(h_ref, xs_hbm.at[pl.ds(0, tm), :], sem).wait()


def _dispatch(h, dest1, dest2):
    n_tok, d = h.shape
    tm = ROW_TILE
    nt = n_tok // tm
    idx = lambda: pl.BlockSpec((1, 1, tm), lambda i: (i, 0, 0), memory_space=pltpu.SMEM)
    return pl.pallas_call(
        _dispatch_kernel,
        out_shape=jax.ShapeDtypeStruct((TOP_K * n_tok, d), F32),
        grid=(nt,),
        in_specs=[idx(), idx(), pl.BlockSpec((tm, d), lambda i: (i, 0))],
        out_specs=pl.BlockSpec(memory_space=pl.ANY),
        scratch_shapes=[pltpu.SemaphoreType.DMA(())],
        compiler_params=_params("arbitrary"),
        name="dispatch",
    )(dest1.reshape(nt, 1, tm), dest2.reshape(nt, 1, tm), h)


def _moe_kernel(blk_ref, exp_ref, nitem_ref, start_ref, cnt_ref, x_ref, w1_ref, w3_ref, w2_ref, o_ref):
    w = pl.program_id(0)
    bm = x_ref.shape[0]

    @pl.when(w < nitem_ref[0])
    def _():
        e = exp_ref[w]
        blk = blk_ref[w]
        x = x_ref[...].astype(BF16)
        a = _dot(x, w1_ref[0])
        b = _dot(x, w3_ref[0])
        y = _dot((_silu(a) * b).astype(BF16), w2_ref[0])
        row = blk * bm + lax.broadcasted_iota(jnp.int32, (bm, 1), 0)
        mine = jnp.logical_and(row >= start_ref[e], row < start_ref[e] + cnt_ref[e])
        first_visit = jnp.logical_or(w == 0, blk_ref[jnp.maximum(w - 1, 0)] != blk)

        @pl.when(first_visit)
        def _():
            o_ref[...] = jnp.where(mine, y, 0.0)

        @pl.when(jnp.logical_not(first_visit))
        def _():
            o_ref[...] = jnp.where(mine, y, o_ref[...])


def _moe(xs, counts, w1, w3, w2):
    m, d = xs.shape
    bm = MOE_BM
    n_blocks = m // bm
    n_items = n_blocks + N_EXPERTS - 1
    ends = jnp.cumsum(counts)
    starts = ends - counts
    first_blk = starts // bm
    last_blk = jnp.maximum(ends - 1, starts) // bm
    visits = jnp.where(counts > 0, last_blk - first_blk + 1, 0)
    item_end = jnp.cumsum(visits)
    item_start = item_end - visits
    n_item = item_end[-1]
    w = jnp.minimum(jnp.arange(n_items, dtype=jnp.int32), n_item - 1)
    item_e = jnp.sum(w[:, None] >= item_end[None, :], axis=1).astype(jnp.int32)
    item_blk = (first_blk[item_e] + w - item_start[item_e]).astype(jnp.int32)
    wspec = lambda: pl.BlockSpec((1, d, d), lambda i, blk, e, *_: (e[i], 0, 0))
    return pl.pallas_call(
        _moe_kernel,
        out_shape=jax.ShapeDtypeStruct((m, d), F32),
        grid_spec=pltpu.PrefetchScalarGridSpec(
            num_scalar_prefetch=5,
            grid=(n_items,),
            in_specs=[pl.BlockSpec((bm, d), lambda i, blk, *_: (blk[i], 0)), wspec(), wspec(), wspec()],
            out_specs=pl.BlockSpec((bm, d), lambda i, blk, *_: (blk[i], 0)),
        ),
        compiler_params=_params("arbitrary"),
        name="moe",
    )(item_blk, item_e, n_item.astype(jnp.int32).reshape(1), starts.astype(jnp.int32), counts.astype(jnp.int32),
      xs, w1, w3, w2)


class _RowGather:
    def __init__(self, ys_hbm, buf, sem, tm):
        self.ys, self.buf, self.sem, self.tm = ys_hbm, buf, sem, tm

    def start(self, d_refs, slot):
        def body(t, carry):
            for k, d_ref in enumerate(d_refs):
                pltpu.make_async_copy(self.ys.at[pl.ds(d_ref[0, 0, t], 1), :],
                                      self.buf.at[slot, k, pl.ds(t, 1), :], self.sem.at[slot]).start()
            return carry
        lax.fori_loop(0, self.tm, body, 0, unroll=8)

    def wait(self, slot):
        for k in range(TOP_K):
            pltpu.make_async_copy(self.ys.at[pl.ds(0, self.tm), :], self.buf.at[slot, k], self.sem.at[slot]).wait()

    def combined(self, step, n_steps, cur_refs, nxt_refs, w1_ref, w2_ref):
        slot = step % 2

        @pl.when(step == 0)
        def _():
            self.start(cur_refs, 0)

        @pl.when(step + 1 < n_steps)
        def _():
            self.start(nxt_refs, 1 - slot)

        self.wait(slot)
        return w1_ref[...] * self.buf[slot, 0] + w2_ref[...] * self.buf[slot, 1]


def _gather_specs(tm, nt_total, step_of):
    cur = lambda: pl.BlockSpec((1, 1, tm), lambda *g: (step_of(*g), 0, 0), memory_space=pltpu.SMEM)
    nxt = lambda: pl.BlockSpec((1, 1, tm), lambda *g: (jnp.minimum(step_of(*g) + 1, nt_total - 1), 0, 0),
                               memory_space=pltpu.SMEM)
    wcol = lambda: pl.BlockSpec((tm, 1), lambda *g: (step_of(*g), 0))
    return [cur(), cur(), nxt(), nxt(), wcol(), wcol()]


def _gather_scratch(tm, d):
    return [pltpu.VMEM((2, TOP_K, tm, d), F32), pltpu.SemaphoreType.DMA((2,))]


def _group_norm64(t, ind_ref, indt_ref):
    ss = _dot((t * t).astype(BF16), ind_ref[...])
    r = lax.rsqrt(ss * (1.0 / DIFF_HEAD_DIM) + EPS)
    r_hi = r.astype(BF16)
    r_lo = (r - r_hi.astype(F32)).astype(BF16)
    rb = _dot(jnp.concatenate([r_hi, r_lo], axis=1), indt_ref[...])
    return t * rb


def _kvq_kernel(d1c, d2c, d1n, d2n, w1_ref, w2_ref, ys_hbm, x_ref, ada0_ref, ada1_ref, kvada_ref, kvgain_ref,
                qgain_ref, knorm_ref, qnorm_ref, wk_ref, wvt_ref, wq_ref, ind_ref, indt_ref,
                xo_ref, q_ref, k_ref, vt_ref, gbuf, gsem):
    tm = x_ref.shape[1]
    step = pl.program_id(0) * pl.num_programs(1) + pl.program_id(1)
    n_steps = pl.num_programs(0) * pl.num_programs(1)
    moe = _RowGather(ys_hbm, gbuf, gsem, tm).combined(step, n_steps, (d1c, d2c), (d1n, d2n), w1_ref, w2_ref)
    xn = x_ref[0] + ada0_ref[0, 5:6, :] * moe
    xo_ref[0] = xn
    r = _rms(xn)
    hk = ((r * kvgain_ref[...]) * (1.0 + kvada_ref[0, 1:2, :]) + kvada_ref[0, 0:1, :]).astype(BF16)
    hq = ((r * qgain_ref[...]) * (1.0 + ada1_ref[0, 1:2, :]) + ada1_ref[0, 0:1, :]).astype(BF16)
    kk = _dot(hk, wk_ref[...])
    k_ref[0] = (_group_norm64(kk, ind_ref, indt_ref) * knorm_ref[...]).astype(BF16)
    vt = _dot_nt(wvt_ref[...], hk).astype(BF16)
    for j in range(tm // ATT_TK):
        vt_ref[0, j] = vt[:, j * ATT_TK:(j + 1) * ATT_TK]
    qq = _dot(hq, wq_ref[...])
    q_ref[0] = (_group_norm64(qq, ind_ref, indt_ref) * qnorm_ref[...]).astype(BF16)


def _kvq(x, ys, dest1, dest2, w1, w2, ada0, ada1, kvada, kv_gain, q_gain, k_norm, q_norm, w_k, w_vt, w_q):
    bsz, seq, d = x.shape
    tm = ROW_TILE
    nt = seq // tm
    nt_total = bsz * nt
    lane_group = jnp.arange(d, dtype=jnp.int32) // DIFF_HEAD_DIM
    ind = (lane_group[:, None] == jnp.arange(LANE, dtype=jnp.int32)[None, :]).astype(BF16)
    indt = jnp.concatenate([ind.T, ind.T], axis=0)
    reps = d // DIFF_HEAD_DIM
    q_scale = jnp.tile(q_norm, reps)[None, :] * (DIFF_HEAD_DIM ** -0.5 * math.log2(math.e))
    const = lambda *shape: pl.BlockSpec(shape, lambda b, i: (0,) * len(shape))
    tok = lambda w: pl.BlockSpec((1, tm, w), lambda b, i: (b, i, 0))
    ada_spec = lambda rows: pl.BlockSpec((1, rows, d), lambda b, i: (b, 0, 0))
    col = lambda a: a.reshape(-1, 1)
    idx = lambda a: a.reshape(nt_total, 1, tm)
    return pl.pallas_call(
        _kvq_kernel,
        out_shape=(jax.ShapeDtypeStruct((bsz, seq, d), F32),
                   jax.ShapeDtypeStruct((bsz, seq, d), BF16),
                   jax.ShapeDtypeStruct((bsz, seq, d), BF16),
                   jax.ShapeDtypeStruct((bsz, seq // ATT_TK, d, ATT_TK), BF16)),
        grid=(bsz, nt),
        in_specs=_gather_specs(tm, nt_total, lambda b, i: b * nt + i) + [
            pl.BlockSpec(memory_space=pl.ANY),
            tok(d),
            ada_spec(6), ada_spec(6), ada_spec(2),
            const(1, d), const(1, d), const(1, d), const(1, d),
            const(d, d), const(d, d), const(d, d), const(d, LANE), const(2 * LANE, d),
        ],
        out_specs=(tok(d), tok(d), tok(d),
                   pl.BlockSpec((1, tm // ATT_TK, d, ATT_TK), lambda b, i: (b, i, 0, 0))),
        scratch_shapes=_gather_scratch(tm, d),
        compiler_params=_params("arbitrary", "arbitrary"),
        name="kvq",
    )(idx(dest1), idx(dest2), idx(dest1), idx(dest2), col(w1), col(w2), ys, x, ada0, ada1, kvada,
      kv_gain, q_gain, jnp.tile(k_norm, reps)[None, :], q_scale, w_k, w_vt, w_q, ind, indt)


def _attn_kernel(lambda_init, q_ref, k_ref, vt_ref, x_ref, ada_ref, lam_ref, subln_ref, wo_ref, ngain_ref,
                 wr_hi_ref, wr_lo_ref, br_ref, tri_ref, xo_ref, h_ref, route_ref, cnt_ref, obuf, carry):
    i = pl.program_id(1)
    tq, tk = ATT_TQ, ATT_TK
    hd2 = 2 * DIFF_HEAD_DIM
    lam = lam_ref[...]
    lam_full = (jnp.exp(jnp.sum(lam[0:1] * lam[1:2], axis=-1, keepdims=True))
                - jnp.exp(jnp.sum(lam[2:3] * lam[3:4], axis=-1, keepdims=True)) + lambda_init)
    lane = lax.broadcasted_iota(jnp.int32, (1, hd2), 1)
    keep1 = jnp.where(lane < DIFF_HEAD_DIM, 1.0, 0.0).astype(BF16)
    keep2 = jnp.where(lane >= DIFF_HEAD_DIM, 1.0, 0.0).astype(BF16)
    key = lax.broadcasted_iota(jnp.int32, (tk, tq), 0)
    qry = lax.broadcasted_iota(jnp.int32, (tk, tq), 1)
    diag_visible = key < (qry // MASK_CHUNK + 1) * MASK_CHUNK
    neg_inf = jnp.float32(-jnp.inf)

    def softmax_step(stats, st):
        m, l = stats
        m_new = jnp.maximum(m, jnp.max(st, axis=0, keepdims=True))
        alpha = jnp.exp2(m - m_new)
        p = jnp.exp2(st - m_new)
        return (m_new, alpha * l + jnp.sum(p, axis=0, keepdims=True)), alpha, p.astype(BF16)

    for hd in range(DIFF_HEADS):
        c0 = hd * hd2
        c1 = c0 + hd2
        qh = q_ref[0, :, c0:c1]
        q1 = qh * keep1
        q2 = qh * keep2

        def scores(j, c0=c0, c1=c1, q1=q1, q2=q2):
            kb = k_ref[0, pl.ds(pl.multiple_of(j * tk, tk), tk), c0:c1]
            return _dot_nt(kb, q1), _dot_nt(kb, q2)

        def values(j, c0=c0, c1=c1):
            return vt_ref[0, j, c0:c1, :]

        def kv_step(j, carry_t, scores=scores, values=values):
            s, p, stats, acc = carry_t
            vtb = values(jnp.maximum(j - 1, 0))
            pv = (_dot(vtb, p[0]), _dot(vtb, p[1]))
            s_next = scores(j + 1)
            stats1, alpha1, p1 = softmax_step(stats[0], s[0])
            stats2, alpha2, p2 = softmax_step(stats[1], s[1])
            return (s_next, (p1, p2), (stats1, stats2),
                    (alpha1 * (acc[0] + pv[0]), alpha2 * (acc[1] + pv[1])))

        stat0 = (jnp.full((1, tq), neg_inf, F32), jnp.zeros((1, tq), F32))
        p0 = jnp.zeros((tk, tq), BF16)
        acc0 = jnp.zeros((hd2, tq), F32)
        s, p, stats, acc = lax.fori_loop(0, i, kv_step, (scores(0), (p0, p0), (stat0, stat0), (acc0, acc0)))
        vtb = values(jnp.maximum(i - 1, 0))
        (_, l1), alpha1, p1 = softmax_step(stats[0], jnp.where(diag_visible, s[0], neg_inf))
        (_, l2), alpha2, p2 = softmax_step(stats[1], jnp.where(diag_visible, s[1], neg_inf))
        vtd = values(i)
        a1 = alpha1 * (acc[0] + _dot(vtb, p[0])) + _dot(vtd, p1)
        a2 = alpha2 * (acc[1] + _dot(vtb, p[1])) + _dot(vtd, p2)
        o = a1 * (1.0 / l1) - lam_full * (a2 * (1.0 / l2))
        o = o * lax.rsqrt(jnp.mean(o * o, axis=0, keepdims=True) + EPS)
        obuf[c0:c1, :] = (o * (subln_ref[...] * (1.0 - lambda_init))).astype(BF16)

    y = _dot_tn(obuf[...], wo_ref[...])
    xn = x_ref[0] + ada_ref[0, 2:3, :] * y
    xo_ref[0] = xn
    first = jnp.logical_and(pl.program_id(0) == 0, i == 0)
    h_ref[0] = _ffn_prenorm_route(first, xn, ngain_ref[...], ada_ref[0, 3:4, :], ada_ref[0, 4:5, :],
                                  wr_hi_ref, wr_lo_ref, br_ref, tri_ref, route_ref, cnt_ref, carry)


def _attn(q, k, vt, x, ada, lam, subln, w_out, ngain, router, lambda_init):
    bsz, seq, d = x.shape
    tq = ATT_TQ
    const = lambda *shape: pl.BlockSpec(shape, lambda b, i: (0,) * len(shape))
    tok = lambda w: pl.BlockSpec((1, tq, w), lambda b, i: (b, i, 0))
    route_shapes, route_specs = _route_out(bsz, seq, tq)
    return pl.pallas_call(
        functools.partial(_attn_kernel, lambda_init),
        out_shape=(jax.ShapeDtypeStruct((bsz, seq, d), F32),
                   jax.ShapeDtypeStruct((bsz, seq, d), F32)) + route_shapes,
        grid=(bsz, seq // tq),
        in_specs=[
            tok(d),
            pl.BlockSpec((1, seq, d), lambda b, i: (b, 0, 0)),
            pl.BlockSpec((1, seq // ATT_TK, d, ATT_TK), lambda b, i: (b, 0, 0, 0)),
            tok(d),
            pl.BlockSpec((1, 6, d), lambda b, i: (b, 0, 0)),
            const(4, DIFF_HEAD_DIM), const(2 * DIFF_HEAD_DIM, 1), const(d, d), const(1, d),
        ] + _route_specs(d, tq),
        out_specs=(tok(d), tok(d)) + route_specs,
        scratch_shapes=[pltpu.VMEM((d, tq), BF16), pltpu.VMEM((N_EXPERTS, 1), F32)],
        compiler_params=_params("arbitrary", "arbitrary"),
        name="attn",
    )(q, k, vt, x, ada, lam, subln, w_out, ngain, *router, _earlier_token_matrix(tq))


def _combine_kernel(d1c, d2c, d1n, d2n, w1_ref, w2_ref, ys_hbm, x_ref, ada_ref, o_ref, gbuf, gsem):
    tm = x_ref.shape[1]
    step = pl.program_id(0) * pl.num_programs(1) + pl.program_id(1)
    n_steps = pl.num_programs(0) * pl.num_programs(1)
    moe = _RowGather(ys_hbm, gbuf, gsem, tm).combined(step, n_steps, (d1c, d2c), (d1n, d2n), w1_ref, w2_ref)
    o_ref[0] = x_ref[0] + ada_ref[0, 5:6, :] * moe


def _combine(x, ys, dest1, dest2, w1, w2, ada):
    bsz, seq, d = x.shape
    tm = ROW_TILE
    nt = seq // tm
    nt_total = bsz * nt
    col = lambda a: a.reshape(-1, 1)
    idx = lambda a: a.reshape(nt_total, 1, tm)
    return pl.pallas_call(
        _combine_kernel,
        out_shape=jax.ShapeDtypeStruct((bsz, seq, d), F32),
        grid=(bsz, nt),
        in_specs=_gather_specs(tm, nt_total, lambda b, i: b * nt + i) + [
            pl.BlockSpec(memory_space=pl.ANY),
            pl.BlockSpec((1, tm, d), lambda b, i: (b, i, 0)),
            pl.BlockSpec((1, 6, d), lambda b, i: (b, 0, 0)),
        ],
        out_specs=pl.BlockSpec((1, tm, d), lambda b, i: (b, i, 0)),
        scratch_shapes=_gather_scratch(tm, d),
        compiler_params=_params("arbitrary", "arbitrary"),
        name="combine",
    )(idx(dest1), idx(dest2), idx(dest1), idx(dest2), col(w1), col(w2), ys, x, ada)


def _router_weights(w_group, b_group, w_expert, b_expert):
    d = w_group.shape[0]
    pad = ROUTER_ROWS - N_EXPERTS - N_GROUPS
    w = jnp.concatenate([w_expert.T, w_group.T, jnp.zeros((pad, d), F32)], axis=0)
    b = jnp.concatenate([b_expert, b_group, jnp.zeros((pad,), F32)])[:, None]
    w_hi = w.astype(BF16)
    w_lo = (w - w_hi.astype(F32)).astype(BF16)
    return w_hi, w_lo, b


def _routing_tables(route, cnt):
    rec = lambda r: route[:, :, r, :].reshape(-1)
    counts = cnt[:, 0].astype(jnp.int32)
    starts = jnp.cumsum(counts) - counts
    e1 = rec(0).astype(jnp.int32)
    e2 = rec(1).astype(jnp.int32)
    dest1 = starts[e1] + rec(2).astype(jnp.int32)
    dest2 = starts[e2] + rec(3).astype(jnp.int32)
    return dest1, dest2, rec(4), rec(5), counts


def kernel(x, c, ada_w, ada_b, norm_mix, norm_ffn, ret_w_in, ret_gn, ret_w_out, kv_ada_w, kv_ada_b, kv_norm, kv_w, k_norm, diff_w_q, q_norm, diff_lam, diff_subln, diff_w_out, moe_w_group, moe_b_group, moe_w_expert, moe_b_expert, moe_w1, moe_w3, moe_w2):
    bsz, seq, d = x.shape
    n_tok = bsz * seq

    ada = _ada(c, ada_w, ada_b).reshape(2, bsz, 6, d)
    kvada = _ada(c, kv_ada_w[None], kv_ada_b[None]).reshape(bsz, 2, d)

    half = RET_DK // 2
    inv_freq = 1.0 / (ROPE_BASE ** (jnp.arange(half, dtype=F32) / half))
    ang = jnp.arange(seq, dtype=F32)[:, None] * inv_freq[None, :]
    cos, sin = jnp.cos(ang), jnp.sin(ang)

    routers = [_router_weights(moe_w_group[l], moe_b_group[l], moe_w_expert[l], moe_b_expert[l])
               for l in range(2)]

    def moe_layer(h, route, cnt, layer):
        dest1, dest2, w1, w2, counts = _routing_tables(route, cnt)
        xs = _dispatch(h.reshape(n_tok, d), dest1, dest2)
        ys = _moe(xs, counts, moe_w1[layer].astype(BF16), moe_w3[layer].astype(BF16), moe_w2[layer].astype(BF16))
        return ys, dest1, dest2, w1, w2

    proj = _ret_in(x, ada[0], norm_mix[0][None, :], cos, sin, ret_w_in[0].astype(BF16))
    x1, h, route, cnt = _ret_core(proj, x, ada[0], ret_gn[0][None, :], ret_w_out[0].astype(BF16),
                                  norm_ffn[0][None, :], routers[0])
    moe0 = moe_layer(h, route, cnt, 0)

    x2, q, k, vt = _kvq(x1, *moe0, ada[0], ada[1], kvada, kv_norm[None, :], norm_mix[1][None, :], k_norm, q_norm[0],
                        kv_w[:, :d].astype(BF16), kv_w[:, d:].T.astype(BF16), diff_w_q[0].astype(BF16))
    lambda_init = 0.8 - 0.6 * math.exp(-0.3 * 1)
    x3, h, route, cnt = _attn(q, k, vt, x2, ada[1], diff_lam[0], diff_subln[0][:, None], diff_w_out[0].astype(BF16),
                              norm_ffn[1][None, :], routers[1], lambda_init)
    moe1 = moe_layer(h, route, cnt, 1)
    return _combine(x3, *moe1, ada[1])
```

```python
import functools
import math

import jax
import jax.numpy as jnp
from jax import lax
from jax.experimental import pallas as pl
from jax.experimental.pallas import tpu as pltpu

F32 = jnp.float32
BF16 = jnp.bfloat16

D_MODEL = 1024
EPS = 1e-6

RET_HEADS = 4
RET_DK = D_MODEL // RET_HEADS
RET_DV = 2 * RET_DK
RET_QK = RET_HEADS * RET_DK
RET_V = RET_HEADS * RET_DV
RET_IN = 2 * RET_QK + 2 * RET_V
ROPE_BASE = 10000.0
RET_CHUNK = 256

DIFF_HEAD_DIM = 64
DIFF_HEADS = D_MODEL // (2 * DIFF_HEAD_DIM)
MASK_CHUNK = 64
ATT_TQ = 256
ATT_TK = 256

N_GROUPS = 4
EXPERTS_PER_GROUP = 4
N_EXPERTS = N_GROUPS * EXPERTS_PER_GROUP
TOP_K = 2
MOE_BM = 256
ROUTER_ROWS = 32
ROUTE_ROWS = 8
ROW_TILE = 512

LANE = 128
SUBLANE = 8
VMEM_LIMIT = 56 * 1024 * 1024


def _dot(a, b):
    return jnp.dot(a, b, preferred_element_type=F32)


def _dot_nt(a, b):
    return lax.dot_general(a, b, (((1,), (1,)), ((), ())), preferred_element_type=F32)


def _dot_tn(a, b):
    return lax.dot_general(a, b, (((0,), (0,)), ((), ())), preferred_element_type=F32)


def _silu(x):
    return x * (1.0 / (1.0 + jnp.exp(-x)))


def _rms(x):
    return x * lax.rsqrt(jnp.mean(x * x, axis=-1, keepdims=True) + EPS)


def _params(*sem):
    return pltpu.CompilerParams(dimension_semantics=sem, vmem_limit_bytes=VMEM_LIMIT)


def _ada_kernel(c_ref, w_ref, b_ref, o_ref):
    ca = _silu(c_ref[...])
    o_ref[0] = jnp.dot(ca, w_ref[0], preferred_element_type=F32,
                       precision=lax.Precision.HIGHEST) + b_ref[0]


def _ada(c, w, b):
    n_l, d, n = w.shape
    bsz = c.shape[0]
    tn = 1024
    return pl.pallas_call(
        _ada_kernel,
        out_shape=jax.ShapeDtypeStruct((n_l, bsz, n), F32),
        grid=(n_l, n // tn),
        in_specs=[
            pl.BlockSpec((bsz, d), lambda l, j: (0, 0)),
            pl.BlockSpec((1, d, tn), lambda l, j: (l, 0, j)),
            pl.BlockSpec((1, 1, tn), lambda l, j: (l, 0, j)),
        ],
        out_specs=pl.BlockSpec((1, bsz, tn), lambda l, j: (l, 0, j)),
        compiler_params=_params("arbitrary", "arbitrary"),
        name="ada",
    )(c, w, b.reshape(n_l, 1, n))


def _ffn_prenorm_route(first_step, xn, gain, shift, scale, wr_hi_ref, wr_lo_ref, br_ref, tri_ref,
                       route_ref, cnt_ref, carry):
    c = xn.shape[0]
    h = _rms(xn) * gain
    h = h * (1.0 + scale) + shift
    h_hi = h.astype(BF16)
    h_lo = (h - h_hi.astype(F32)).astype(BF16)
    w_hi = wr_hi_ref[...]
    lt = _dot_nt(w_hi, h_hi) + _dot_nt(wr_lo_ref[...], h_hi) + _dot_nt(w_hi, h_lo) + br_ref[...]
    neg_inf = jnp.float32(-jnp.inf)

    grow = lax.broadcasted_iota(jnp.int32, (SUBLANE, c), 0).astype(F32)
    g = jnp.where(grow < N_GROUPS, lt[N_EXPERTS:N_EXPERTS + SUBLANE], neg_inf)
    gmax = jnp.max(g, axis=0, keepdims=True)
    gidx = jnp.min(jnp.where(g == gmax, grow, float(SUBLANE)), axis=0, keepdims=True)
    gate = 1.0 / jnp.sum(jnp.exp(g - gmax), axis=0, keepdims=True)

    erow_i = lax.broadcasted_iota(jnp.int32, (N_EXPERTS, c), 0)
    erow = erow_i.astype(F32)
    egroup = (erow_i // EXPERTS_PER_GROUP).astype(F32)
    el = jnp.where(egroup == gidx, lt[0:N_EXPERTS], neg_inf)
    m1 = jnp.max(el, axis=0, keepdims=True)
    i1 = jnp.min(jnp.where(el == m1, erow, float(N_EXPERTS)), axis=0, keepdims=True)
    el2 = jnp.where(erow == i1, neg_inf, el)
    m2 = jnp.max(el2, axis=0, keepdims=True)
    i2 = jnp.min(jnp.where(el2 == m2, erow, float(N_EXPERTS)), axis=0, keepdims=True)
    t = jnp.exp(m2 - m1)
    den = 1.0 / (1.0 + t)
    w1 = gate * den
    w2 = gate * t * den

    @pl.when(first_step)
    def _():
        carry[...] = jnp.zeros_like(carry)

    oh1 = erow == i1
    oh2 = erow == i2
    oh = jnp.where(jnp.logical_or(oh1, oh2), 1.0, 0.0)
    before = carry[...] + _dot(oh.astype(BF16), tri_ref[...])
    rank1 = jnp.sum(jnp.where(oh1, before, 0.0), axis=0, keepdims=True)
    rank2 = jnp.sum(jnp.where(oh2, before, 0.0), axis=0, keepdims=True)
    total = carry[...] + jnp.sum(oh, axis=1, keepdims=True)
    carry[...] = total
    cnt_ref[...] = jnp.broadcast_to(total, cnt_ref.shape)

    rrow = lax.broadcasted_iota(jnp.int32, (ROUTE_ROWS, c), 0)
    rec = jnp.zeros((ROUTE_ROWS, c), F32)
    for idx, val in enumerate((i1, i2, rank1, rank2, w1, w2)):
        rec = jnp.where(rrow == idx, val, rec)
    route_ref[0, 0] = rec
    return h


def _route_specs(d, c):
    const = lambda *shape: pl.BlockSpec(shape, lambda b, i: (0,) * len(shape))
    return [const(ROUTER_ROWS, d), const(ROUTER_ROWS, d), const(ROUTER_ROWS, 1), const(c, c)]


def _route_out(bsz, seq, c):
    shapes = (jax.ShapeDtypeStruct((bsz, seq // c, ROUTE_ROWS, c), F32),
              jax.ShapeDtypeStruct((N_EXPERTS, LANE), F32))
    specs = (pl.BlockSpec((1, 1, ROUTE_ROWS, c), lambda b, i: (b, i, 0, 0)),
             pl.BlockSpec((N_EXPERTS, LANE), lambda b, i: (0, 0)))
    return shapes, specs


def _earlier_token_matrix(c):
    t = jnp.arange(c, dtype=jnp.int32)
    return (t[:, None] < t[None, :]).astype(BF16)


def _ret_in_kernel(x_ref, ada_ref, gain_ref, cos_ref, sin_ref, w_ref, o_ref):
    x = x_ref[0]
    shift = ada_ref[0, 0:1, :]
    scale = ada_ref[0, 1:2, :]
    h = (_rms(x) * gain_ref[...]) * (1.0 + scale) + shift
    hb = h.astype(BF16)
    cos = cos_ref[...]
    sin = sin_ref[...]
    half = RET_DK // 2
    for j in range(RET_IN // D_MODEL):
        p = _dot(hb, w_ref[:, j * D_MODEL:(j + 1) * D_MODEL])
        if j < 2:
            post = 1.0 if j == 0 else RET_DK ** -0.5
            for hd in range(RET_HEADS):
                lo = hd * RET_DK
                x1 = p[:, lo:lo + half]
                x2 = p[:, lo + half:lo + RET_DK]
                o_ref[0, :, j * D_MODEL + lo:j * D_MODEL + lo + half] = (
                    (x1 * cos - x2 * sin) * post).astype(BF16)
                o_ref[0, :, j * D_MODEL + lo + half:j * D_MODEL + lo + RET_DK] = (
                    (x2 * cos + x1 * sin) * post).astype(BF16)
        else:
            o_ref[0, :, j * D_MODEL:(j + 1) * D_MODEL] = p.astype(BF16)


def _ret_in(x, ada, gain, cos, sin, w_in):
    bsz, seq, d = x.shape
    tm = ROW_TILE
    return pl.pallas_call(
        _ret_in_kernel,
        out_shape=jax.ShapeDtypeStruct((bsz, seq, RET_IN), BF16),
        grid=(bsz, seq // tm),
        in_specs=[
            pl.BlockSpec((1, tm, d), lambda b, i: (b, i, 0)),
            pl.BlockSpec((1, 6, d), lambda b, i: (b, 0, 0)),
            pl.BlockSpec((1, d), lambda b, i: (0, 0)),
            pl.BlockSpec((tm, RET_DK // 2), lambda b, i: (i, 0)),
            pl.BlockSpec((tm, RET_DK // 2), lambda b, i: (i, 0)),
            pl.BlockSpec((d, RET_IN), lambda b, i: (0, 0), pipeline_mode=pl.Buffered(1)),
        ],
        out_specs=pl.BlockSpec((1, tm, RET_IN), lambda b, i: (b, i, 0)),
        compiler_params=_params("arbitrary", "arbitrary"),
        name="ret_in",
    )(x, ada, gain, cos, sin, w_in)


def _ret_core_kernel(q_ref, k_ref, v_ref, g_ref, x_ref, ada_ref, intra_ref, qd_ref, kd_ref, cd_ref,
                     gn_ref, wo_ref, ngain_ref, wr_hi_ref, wr_lo_ref, br_ref, tri_ref,
                     xo_ref, h_ref, route_ref, cnt_ref, state, ybuf, carry):
    @pl.when(pl.program_id(1) == 0)
    def _():
        state[...] = jnp.zeros_like(state)

    for hd in range(RET_HEADS):
        q = q_ref[0, :, hd * RET_DK:(hd + 1) * RET_DK]
        k = k_ref[0, :, hd * RET_DK:(hd + 1) * RET_DK]
        v = v_ref[0, :, hd * RET_DV:(hd + 1) * RET_DV]
        st = state[hd]
        s = _dot_nt(q, k) * intra_ref[hd]
        o = _dot(s.astype(BF16), v) + qd_ref[hd] * _dot(q, st.astype(BF16))
        kdec = (k.astype(F32) * kd_ref[hd]).astype(BF16)
        state[hd] = st * cd_ref[hd] + _dot_tn(kdec, v)
        o = _rms(o) * gn_ref[:, hd * RET_DV:(hd + 1) * RET_DV]
        g = g_ref[0, :, hd * RET_DV:(hd + 1) * RET_DV].astype(F32)
        ybuf[:, hd * RET_DV:(hd + 1) * RET_DV] = (_silu(g) * o).astype(BF16)

    y = _dot(ybuf[...], wo_ref[...])
    xn = x_ref[0] + ada_ref[0, 2:3, :] * y
    xo_ref[0] = xn
    first = jnp.logical_and(pl.program_id(0) == 0, pl.program_id(1) == 0)
    h_ref[0] = _ffn_prenorm_route(first, xn, ngain_ref[...], ada_ref[0, 3:4, :], ada_ref[0, 4:5, :],
                                  wr_hi_ref, wr_lo_ref, br_ref, tri_ref, route_ref, cnt_ref, carry)


def _ret_core(proj, x, ada, gn, w_out, ngain, router):
    bsz, seq, d = x.shape
    c = RET_CHUNK
    f32 = F32
    log_gamma = jnp.log1p(-jnp.exp2(-5.0 - jnp.arange(RET_HEADS, dtype=f32)))
    n = jnp.arange(c, dtype=f32)
    rel = n[:, None] - n[None, :]
    intra = jnp.where(rel >= 0, jnp.exp(jnp.maximum(rel, 0.0)[None] * log_gamma[:, None, None]), 0.0)
    qd = jnp.exp((n + 1.0)[None, :] * log_gamma[:, None])[:, :, None]
    kd = jnp.exp((c - 1.0 - n)[None, :] * log_gamma[:, None])[:, :, None]
    cd = jnp.broadcast_to(jnp.exp(c * log_gamma)[:, None, None], (RET_HEADS, 1, RET_DV))
    const = lambda *shape: pl.BlockSpec(shape, lambda b, i: (0,) * len(shape))
    tok = lambda w, j: pl.BlockSpec((1, c, w), lambda b, i: (b, i, j))
    route_shapes, route_specs = _route_out(bsz, seq, c)
    return pl.pallas_call(
        _ret_core_kernel,
        out_shape=(jax.ShapeDtypeStruct((bsz, seq, d), f32),
                   jax.ShapeDtypeStruct((bsz, seq, d), f32)) + route_shapes,
        grid=(bsz, seq // c),
        in_specs=[
            tok(RET_QK, 0), tok(RET_QK, 1), tok(RET_V, 1), tok(RET_V, 2),
            tok(d, 0),
            pl.BlockSpec((1, 6, d), lambda b, i: (b, 0, 0)),
            const(RET_HEADS, c, c), const(RET_HEADS, c, 1), const(RET_HEADS, c, 1),
            const(RET_HEADS, 1, RET_DV),
            const(1, RET_V), const(RET_V, d), const(1, d),
        ] + _route_specs(d, c),
        out_specs=(tok(d, 0), tok(d, 0)) + route_specs,
        scratch_shapes=[pltpu.VMEM((RET_HEADS, RET_DK, RET_DV), f32),
                        pltpu.VMEM((c, RET_V), BF16),
                        pltpu.VMEM((N_EXPERTS, 1), f32)],
        compiler_params=_params("arbitrary", "arbitrary"),
        name="ret_core",
    )(proj, proj, proj, proj, x, ada, intra, qd, kd, cd, gn, w_out, ngain, *router, _earlier_token_matrix(c))


def _dispatch_kernel(d1_ref, d2_ref, h_ref, xs_hbm, sem):
    tm = h_ref.shape[0]

    def body(t, carry):
        for d_ref in (d1_ref, d2_ref):
            pltpu.make_async_copy(h_ref.at[pl.ds(t, 1), :], xs_hbm.at[pl.ds(d_ref[0, 0, t], 1), :], sem).start()
        return carry
    lax.fori_loop(0, tm, body, 0, unroll=8)
    for _ in range(TOP_K):
        pltpu.make_async_copy(h_ref, xs_hbm.at[pl.ds(0, tm), :], sem).wait()


def _dispatch(h, dest1, dest2):
    n_tok, d = h.shape
    tm = ROW_TILE
    nt = n_tok // tm
    idx = lambda: pl.BlockSpec((1, 1, tm), lambda i: (i, 0, 0), memory_space=pltpu.SMEM)
    return pl.pallas_call(
        _dispatch_kernel,
        out_shape=jax.ShapeDtypeStruct((TOP_K * n_tok, d), F32),
        grid=(nt,),
        in_specs=[idx(), idx(), pl.BlockSpec((tm, d), lambda i: (i, 0))],
        out_specs=pl.BlockSpec(memory_space=pl.ANY),
        scratch_shapes=[pltpu.SemaphoreType.DMA(())],
        compiler_params=_params("arbitrary"),
        name="dispatch",
    )(dest1.reshape(nt, 1, tm), dest2.reshape(nt, 1, tm), h)


def _moe_kernel(blk_ref, exp_ref, nitem_ref, start_ref, cnt_ref, x_ref, w1_ref, w3_ref, w2_ref, o_ref):
    w = pl.program_id(0)
    bm = x_ref.shape[0]

    @pl.when(w < nitem_ref[0])
    def _():
        e = exp_ref[w]
        blk = blk_ref[w]
        x = x_ref[...].astype(BF16)
        a = _dot(x, w1_ref[0])
        b = _dot(x, w3_ref[0])
        y = _dot((_silu(a) * b).astype(BF16), w2_ref[0])
        row = blk * bm + lax.broadcasted_iota(jnp.int32, (bm, 1), 0)
        mine = jnp.logical_and(row >= start_ref[e], row < start_ref[e] + cnt_ref[e])
        first_visit = jnp.logical_or(w == 0, blk_ref[jnp.maximum(w - 1, 0)] != blk)

        @pl.when(first_visit)
        def _():
            o_ref[...] = jnp.where(mine, y, 0.0)

        @pl.when(jnp.logical_not(first_visit))
        def _():
            o_ref[...] = jnp.where(mine, y, o_ref[...])


def _moe(xs, counts, w1, w3, w2):
    m, d = xs.shape
    bm = MOE_BM
    n_blocks = m // bm
    n_items = n_blocks + N_EXPERTS - 1
    ends = jnp.cumsum(counts)
    starts = ends - counts
    first_blk = starts // bm
    last_blk = jnp.maximum(ends - 1, starts) // bm
    visits = jnp.where(counts > 0, last_blk - first_blk + 1, 0)
    item_end = jnp.cumsum(visits)
    item_start = item_end - visits
    n_item = item_end[-1]
    w = jnp.minimum(jnp.arange(n_items, dtype=jnp.int32), n_item - 1)
    item_e = jnp.sum(w[:, None] >= item_end[None, :], axis=1).astype(jnp.int32)
    item_blk = (first_blk[item_e] + w - item_start[item_e]).astype(jnp.int32)
    wspec = lambda: pl.BlockSpec((1, d, d), lambda i, blk, e, *_: (e[i], 0, 0))
    return pl.pallas_call(
        _moe_kernel,
        out_shape=jax.ShapeDtypeStruct((m, d), F32),
        grid_spec=pltpu.PrefetchScalarGridSpec(
            num_scalar_prefetch=5,
            grid=(n_items,),
            in_specs=[pl.BlockSpec((bm, d), lambda i, blk, *_: (blk[i], 0)), wspec(), wspec(), wspec()],
            out_specs=pl.BlockSpec((bm, d), lambda i, blk, *_: (blk[i], 0)),
        ),
        compiler_params=_params("arbitrary"),
        name="moe",
    )(item_blk, item_e, n_item.astype(jnp.int32).reshape(1), starts.astype(jnp.int32), counts.astype(jnp.int32),
      xs, w1, w3, w2)


class _RowGather:
    def __init__(self, ys_hbm, buf, sem, tm):
        self.ys, self.buf, self.sem, self.tm = ys_hbm, buf, sem, tm

    def start(self, d_refs, slot):
        def body(t, carry):
            for k, d_ref in enumerate(d_refs):
                pltpu.make_async_copy(self.ys.at[pl.ds(d_ref[0, 0, t], 1), :],
                                      self.buf.at[slot, k, pl.ds(t, 1), :], self.sem.at[slot]).start()
            return carry
        lax.fori_loop(0, self.tm, body, 0, unroll=8)

    def wait(self, slot):
        for k in range(TOP_K):
            pltpu.make_async_copy(self.ys.at[pl.ds(0, self.tm), :], self.buf.at[slot, k], self.sem.at[slot]).wait()

    def combined(self, step, n_steps, cur_refs, nxt_refs, w1_ref, w2_ref):
        slot = step % 2

        @pl.when(step == 0)
        def _():
            self.start(cur_refs, 0)

        @pl.when(step + 1 < n_steps)
        def _():
            self.start(nxt_refs, 1 - slot)

        self.wait(slot)
        return w1_ref[...] * self.buf[slot, 0] + w2_ref[...] * self.buf[slot, 1]


def _gather_specs(tm, nt_total, step_of):
    cur = lambda: pl.BlockSpec((1, 1, tm), lambda *g: (step_of(*g), 0, 0), memory_space=pltpu.SMEM)
    nxt = lambda: pl.BlockSpec((1, 1, tm), lambda *g: (jnp.minimum(step_of(*g) + 1, nt_total - 1), 0, 0),
                               memory_space=pltpu.SMEM)
    wcol = lambda: pl.BlockSpec((tm, 1), lambda *g: (step_of(*g), 0))
    return [cur(), cur(), nxt(), nxt(), wcol(), wcol()]


def _gather_scratch(tm, d):
    return [pltpu.VMEM((2, TOP_K, tm, d), F32), pltpu.SemaphoreType.DMA((2,))]


def _group_norm64(t, ind_ref, indt_ref):
    ss = _dot((t * t).astype(BF16), ind_ref[...])
    r = lax.rsqrt(ss * (1.0 / DIFF_HEAD_DIM) + EPS)
    r_hi = r.astype(BF16)
    r_lo = (r - r_hi.astype(F32)).astype(BF16)
    rb = _dot(jnp.concatenate([r_hi, r_lo], axis=1), indt_ref[...])
    return t * rb


def _kvq_kernel(d1c, d2c, d1n, d2n, w1_ref, w2_ref, ys_hbm, x_ref, ada0_ref, ada1_ref, kvada_ref, kvgain_ref,
                qgain_ref, knorm_ref, qnorm_ref, wk_ref, wvt_ref, wq_ref, ind_ref, indt_ref,
                xo_ref, q_ref, k_ref, vt_ref, gbuf, gsem):
    tm = x_ref.shape[1]
    step = pl.program_id(0) * pl.num_programs(1) + pl.program_id(1)
    n_steps = pl.num_programs(0) * pl.num_programs(1)
    moe = _RowGather(ys_hbm, gbuf, gsem, tm).combined(step, n_steps, (d1c, d2c), (d1n, d2n), w1_ref, w2_ref)
    xn = x_ref[0] + ada0_ref[0, 5:6, :] * moe
    xo_ref[0] = xn
    r = _rms(xn)
    hk = ((r * kvgain_ref[...]) * (1.0 + kvada_ref[0, 1:2, :]) + kvada_ref[0, 0:1, :]).astype(BF16)
    hq = ((r * qgain_ref[...]) * (1.0 + ada1_ref[0, 1:2, :]) + ada1_ref[0, 0:1, :]).astype(BF16)
    hd2 = 2 * DIFF_HEAD_DIM
    kk = _dot(hk, wk_ref[...])
    kn = (_group_norm64(kk, ind_ref, indt_ref) * knorm_ref[...]).astype(BF16)
    qq = _dot(hq, wq_ref[...])
    qn = (_group_norm64(qq, ind_ref, indt_ref) * qnorm_ref[...]).astype(BF16)
    for hd in range(DIFF_HEADS):
        k_ref[0, hd] = kn[:, hd * hd2:(hd + 1) * hd2]
        q_ref[0, hd] = qn[:, hd * hd2:(hd + 1) * hd2]
    vt = _dot_nt(wvt_ref[...], hk).astype(BF16)
    for j in range(tm // ATT_TK):
        vt_ref[0, j] = vt[:, j * ATT_TK:(j + 1) * ATT_TK].reshape(DIFF_HEADS, hd2, ATT_TK)


def _kvq(x, ys, dest1, dest2, w1, w2, ada0, ada1, kvada, kv_gain, q_gain, k_norm, q_norm, w_k, w_vt, w_q):
    bsz, seq, d = x.shape
    tm = ROW_TILE
    nt = seq // tm
    nt_total = bsz * nt
    lane_group = jnp.arange(d, dtype=jnp.int32) // DIFF_HEAD_DIM
    ind = (lane_group[:, None] == jnp.arange(LANE, dtype=jnp.int32)[None, :]).astype(BF16)
    indt = jnp.concatenate([ind.T, ind.T], axis=0)
    reps = d // DIFF_HEAD_DIM
    q_scale = jnp.tile(q_norm, reps)[None, :] * (DIFF_HEAD_DIM ** -0.5 * math.log2(math.e))
    const = lambda *shape: pl.BlockSpec(shape, lambda b, i: (0,) * len(shape))
    tok = lambda w: pl.BlockSpec((1, tm, w), lambda b, i: (b, i, 0))
    hd2 = 2 * DIFF_HEAD_DIM
    heads = pl.BlockSpec((1, DIFF_HEADS, tm, hd2), lambda b, i: (b, 0, i, 0))
    ada_spec = lambda rows: pl.BlockSpec((1, rows, d), lambda b, i: (b, 0, 0))
    col = lambda a: a.reshape(-1, 1)
    idx = lambda a: a.reshape(nt_total, 1, tm)
    return pl.pallas_call(
        _kvq_kernel,
        out_shape=(jax.ShapeDtypeStruct((bsz, seq, d), F32),
                   jax.ShapeDtypeStruct((bsz, DIFF_HEADS, seq, hd2), BF16),
                   jax.ShapeDtypeStruct((bsz, DIFF_HEADS, seq, hd2), BF16),
                   jax.ShapeDtypeStruct((bsz, seq // ATT_TK, DIFF_HEADS, hd2, ATT_TK), BF16)),
        grid=(bsz, nt),
        in_specs=_gather_specs(tm, nt_total, lambda b, i: b * nt + i) + [
            pl.BlockSpec(memory_space=pl.ANY),
            tok(d),
            ada_spec(6), ada_spec(6), ada_spec(2),
            const(1, d), const(1, d), const(1, d), const(1, d),
            const(d, d), const(d, d), const(d, d), const(d, LANE), const(2 * LANE, d),
        ],
        out_specs=(tok(d), heads, heads,
                   pl.BlockSpec((1, tm // ATT_TK, DIFF_HEADS, hd2, ATT_TK), lambda b, i: (b, i, 0, 0, 0))),
        scratch_shapes=_gather_scratch(tm, d),
        compiler_params=_params("arbitrary", "arbitrary"),
        name="kvq",
    )(idx(dest1), idx(dest2), idx(dest1), idx(dest2), col(w1), col(w2), ys, x, ada0, ada1, kvada,
      kv_gain, q_gain, jnp.tile(k_norm, reps)[None, :], q_scale, w_k, w_vt, w_q, ind, indt)


def _attn_kernel(lambda_init, q_ref, k_ref, vt_ref, x_ref, ada_ref, lam_ref, subln_ref, wo_ref, ngain_ref,
                 wr_hi_ref, wr_lo_ref, br_ref, tri_ref, xo_ref, h_ref, route_ref, cnt_ref,
                 s_a, s_b, p_a, p_b, acc, obuf, carry):
    i = pl.program_id(1)
    tq, tk = ATT_TQ, ATT_TK
    hd2 = 2 * DIFF_HEAD_DIM
    n_kblk = k_ref.shape[2] // tk
    lam = lam_ref[...]
    lam_full = (jnp.exp(jnp.sum(lam[0:1] * lam[1:2], axis=-1, keepdims=True))
                - jnp.exp(jnp.sum(lam[2:3] * lam[3:4], axis=-1, keepdims=True)) + lambda_init)
    lane = lax.broadcasted_iota(jnp.int32, (1, hd2), 1)
    keep = (jnp.where(lane < DIFF_HEAD_DIM, 1.0, 0.0).astype(BF16),
            jnp.where(lane >= DIFF_HEAD_DIM, 1.0, 0.0).astype(BF16))
    key = lax.broadcasted_iota(jnp.int32, (tk, tq), 0)
    qry = lax.broadcasted_iota(jnp.int32, (1, tq), 1)
    key_limit = ((i * tq + qry) // MASK_CHUNK + 1) * MASK_CHUNK
    neg_inf = jnp.float32(-jnp.inf)
    n_pairs = (i + 2) // 2

    def head(hd, _):
        qh = q_ref[0, hd]
        qs = (qh * keep[0], qh * keep[1])

        def produce_scores(jb, s_out):
            kb = k_ref[0, hd, pl.ds(pl.multiple_of(jnp.minimum(jb, n_kblk - 1) * tk, tk), tk), :]
            visible = key < key_limit - jb * tk
            maxima = []
            for t in range(2):
                st = jnp.where(visible, _dot_nt(kb, qs[t]), neg_inf)
                s_out[t] = st
                maxima.append(jnp.max(st, axis=0, keepdims=True))
            return tuple(maxima)

        def step(j, s_in, s_out, p_in, p_out, state):
            stats, blk_max = state
            vtb = vt_ref[0, jnp.clip(j - 1, 0, n_kblk - 1), hd]
            next_max = produce_scores(j + 1, s_out)
            new_stats = []
            for t in range(2):
                m, l = stats[t]
                m_new = jnp.maximum(m, blk_max[t])
                alpha = jnp.exp2(m - m_new)
                p = jnp.exp2(s_in[t] - m_new)
                p_out[t] = p.astype(BF16)
                new_stats.append((m_new, alpha * l + jnp.sum(p, axis=0, keepdims=True)))
                acc[t] = alpha * (acc[t] + _dot(vtb, p_in[t]))
            return tuple(new_stats), next_max

        p_b[...] = jnp.zeros_like(p_b)
        acc[...] = jnp.zeros_like(acc)
        stat0 = (jnp.full((1, tq), neg_inf, F32), jnp.zeros((1, tq), F32))
        state = ((stat0, stat0), produce_scores(0, s_a))

        def pair(jj, state):
            state = step(2 * jj, s_a, s_b, p_b, p_a, state)
            return step(2 * jj + 1, s_b, s_a, p_a, p_b, state)

        (stats, _) = lax.fori_loop(0, n_pairs, pair, state)
        vtb = vt_ref[0, jnp.minimum(2 * n_pairs - 1, n_kblk - 1), hd]
        a1 = acc[0] + _dot(vtb, p_b[0])
        a2 = acc[1] + _dot(vtb, p_b[1])
        o = a1 * (1.0 / stats[0][1]) - lam_full * (a2 * (1.0 / stats[1][1]))
        o = o * lax.rsqrt(jnp.mean(o * o, axis=0, keepdims=True) + EPS)
        obuf[hd] = (o * (subln_ref[...] * (1.0 - lambda_init))).astype(BF16)
        return 0

    lax.fori_loop(0, DIFF_HEADS, head, 0)

    y = _dot_tn(obuf[...].reshape(DIFF_HEADS * hd2, tq), wo_ref[...])
    xn = x_ref[0] + ada_ref[0, 2:3, :] * y
    xo_ref[0] = xn
    first = jnp.logical_and(pl.program_id(0) == 0, i == 0)
    h_ref[0] = _ffn_prenorm_route(first, xn, ngain_ref[...], ada_ref[0, 3:4, :], ada_ref[0, 4:5, :],
                                  wr_hi_ref, wr_lo_ref, br_ref, tri_ref, route_ref, cnt_ref, carry)


def _attn(q, k, vt, x, ada, lam, subln, w_out, ngain, router, lambda_init):
    bsz, seq, d = x.shape
    tq, tk = ATT_TQ, ATT_TK
    hd2 = 2 * DIFF_HEAD_DIM
    const = lambda *shape: pl.BlockSpec(shape, lambda b, i: (0,) * len(shape))
    tok = lambda w: pl.BlockSpec((1, tq, w), lambda b, i: (b, i, 0))
    route_shapes, route_specs = _route_out(bsz, seq, tq)
    return pl.pallas_call(
        functools.partial(_attn_kernel, lambda_init),
        out_shape=(jax.ShapeDtypeStruct((bsz, seq, d), F32),
                   jax.ShapeDtypeStruct((bsz, seq, d), F32)) + route_shapes,
        grid=(bsz, seq // tq),
        in_specs=[
            pl.BlockSpec((1, DIFF_HEADS, tq, hd2), lambda b, i: (b, 0, i, 0)),
            pl.BlockSpec((1, DIFF_HEADS, seq, hd2), lambda b, i: (b, 0, 0, 0)),
            pl.BlockSpec((1, seq // tk, DIFF_HEADS, hd2, tk), lambda b, i: (b, 0, 0, 0, 0)),
            tok(d),
            pl.BlockSpec((1, 6, d), lambda b, i: (b, 0, 0)),
            const(4, DIFF_HEAD_DIM), const(hd2, 1), const(d, d), const(1, d),
        ] + _route_specs(d, tq),
        out_specs=(tok(d), tok(d)) + route_specs,
        scratch_shapes=[pltpu.VMEM((2, tk, tq), F32), pltpu.VMEM((2, tk, tq), F32),
                        pltpu.VMEM((2, tk, tq), BF16), pltpu.VMEM((2, tk, tq), BF16),
                        pltpu.VMEM((2, hd2, tq), F32),
                        pltpu.VMEM((DIFF_HEADS, hd2, tq), BF16), pltpu.VMEM((N_EXPERTS, 1), F32)],
        compiler_params=_params("arbitrary", "arbitrary"),
        name="attn",
    )(q, k, vt, x, ada, lam, subln, w_out, ngain, *router, _earlier_token_matrix(tq))


def _combine_kernel(d1c, d2c, d1n, d2n, w1_ref, w2_ref, ys_hbm, x_ref, ada_ref, o_ref, gbuf, gsem):
    tm = x_ref.shape[1]
    step = pl.program_id(0) * pl.num_programs(1) + pl.program_id(1)
    n_steps = pl.num_programs(0) * pl.num_programs(1)
    moe = _RowGather(ys_hbm, gbuf, gsem, tm).combined(step, n_steps, (d1c, d2c), (d1n, d2n), w1_ref, w2_ref)
    o_ref[0] = x_ref[0] + ada_ref[0, 5:6, :] * moe


def _combine(x, ys, dest1, dest2, w1, w2, ada):
    bsz, seq, d = x.shape
    tm = ROW_TILE
    nt = seq // tm
    nt_total = bsz * nt
    col = lambda a: a.reshape(-1, 1)
    idx = lambda a: a.reshape(nt_total, 1, tm)
    return pl.pallas_call(
        _combine_kernel,
        out_shape=jax.ShapeDtypeStruct((bsz, seq, d), F32),
        grid=(bsz, nt),
        in_specs=_gather_specs(tm, nt_total, lambda b, i: b * nt + i) + [
            pl.BlockSpec(memory_space=pl.ANY),
            pl.BlockSpec((1, tm, d), lambda b, i: (b, i, 0)),
            pl.BlockSpec((1, 6, d), lambda b, i: (b, 0, 0)),
        ],
        out_specs=pl.BlockSpec((1, tm, d), lambda b, i: (b, i, 0)),
        scratch_shapes=_gather_scratch(tm, d),
        compiler_params=_params("arbitrary", "arbitrary"),
        name="combine",
    )(idx(dest1), idx(dest2), idx(dest1), idx(dest2), col(w1), col(w2), ys, x, ada)


def _router_weights(w_group, b_group, w_expert, b_expert):
    d = w_group.shape[0]
    pad = ROUTER_ROWS - N_EXPERTS - N_GROUPS
    w = jnp.concatenate([w_expert.T, w_group.T, jnp.zeros((pad, d), F32)], axis=0)
    b = jnp.concatenate([b_expert, b_group, jnp.zeros((pad,), F32)])[:, None]
    w_hi = w.astype(BF16)
    w_lo = (w - w_hi.astype(F32)).astype(BF16)
    return w_hi, w_lo, b


def _routing_tables(route, cnt):
    rec = lambda r: route[:, :, r, :].reshape(-1)
    counts = cnt[:, 0].astype(jnp.int32)
    starts = jnp.cumsum(counts) - counts
    e1 = rec(0).astype(jnp.int32)
    e2 = rec(1).astype(jnp.int32)
    dest1 = starts[e1] + rec(2).astype(jnp.int32)
    dest2 = starts[e2] + rec(3).astype(jnp.int32)
    return dest1, dest2, rec(4), rec(5), counts


def kernel(x, c, ada_w, ada_b, norm_mix, norm_ffn, ret_w_in, ret_gn, ret_w_out, kv_ada_w, kv_ada_b, kv_norm, kv_w, k_norm, diff_w_q, q_norm, diff_lam, diff_subln, diff_w_out, moe_w_group, moe_b_group, moe_w_expert, moe_b_expert, moe_w1, moe_w3, moe_w2):
    bsz, seq, d = x.shape
    n_tok = bsz * seq

    ada = _ada(c, ada_w, ada_b).reshape(2, bsz, 6, d)
    kvada = _ada(c, kv_ada_w[None], kv_ada_b[None]).reshape(bsz, 2, d)

    half = RET_DK // 2
    inv_freq = 1.0 / (ROPE_BASE ** (jnp.arange(half, dtype=F32) / half))
    ang = jnp.arange(seq, dtype=F32)[:, None] * inv_freq[None, :]
    cos, sin = jnp.cos(ang), jnp.sin(ang)

    routers = [_router_weights(moe_w_group[l], moe_b_group[l], moe_w_expert[l], moe_b_expert[l])
               for l in range(2)]

    def moe_layer(h, route, cnt, layer):
        dest1, dest2, w1, w2, counts = _routing_tables(route, cnt)
        xs = _dispatch(h.reshape(n_tok, d), dest1, dest2)
        ys = _moe(xs, counts, moe_w1[layer].astype(BF16), moe_w3[layer].astype(BF16), moe_w2[layer].astype(BF16))
        return ys, dest1, dest2, w1, w2

    proj = _ret_in(x, ada[0], norm_mix[0][None, :], cos, sin, ret_w_in[0].astype(BF16))
    x1, h, route, cnt = _ret_core(proj, x, ada[0], ret_gn[0][None, :], ret_w_out[0].astype(BF16),
                                  norm_ffn[0][None, :], routers[0])
    moe0 = moe_layer(h, route, cnt, 0)

    x2, q, k, vt = _kvq(x1, *moe0, ada[0], ada[1], kvada, kv_norm[None, :], norm_mix[1][None, :], k_norm, q_norm[0],
                        kv_w[:, :d].astype(BF16), kv_w[:, d:].T.astype(BF16), diff_w_q[0].astype(BF16))
    lambda_init = 0.8 - 0.6 * math.exp(-0.3 * 1)
    x3, h, route, cnt = _attn(q, k, vt, x2, ada[1], diff_lam[0], diff_subln[0][:, None], diff_w_out[0].astype(BF16),
                              norm_ffn[1][None, :], routers[1], lambda_init)
    moe1 = moe_layer(h, route, cnt, 1)
    return _combine(x3, *moe1, ada[1])
```

```python
import functools
import math

import jax
import jax.numpy as jnp
from jax import lax
from jax.experimental import pallas as pl
from jax.experimental.pallas import tpu as pltpu

F32 = jnp.float32
BF16 = jnp.bfloat16

D_MODEL = 1024
EPS = 1e-6

RET_HEADS = 4
RET_DK = D_MODEL // RET_HEADS
RET_DV = 2 * RET_DK
RET_QK = RET_HEADS * RET_DK
RET_V = RET_HEADS * RET_DV
RET_IN = 2 * RET_QK + 2 * RET_V
ROPE_BASE = 10000.0
RET_CHUNK = 256

DIFF_HEAD_DIM = 64
DIFF_HEADS = D_MODEL // (2 * DIFF_HEAD_DIM)
MASK_CHUNK = 64
ATT_TQ = 256
ATT_TK = 256

N_GROUPS = 4
EXPERTS_PER_GROUP = 4
N_EXPERTS = N_GROUPS * EXPERTS_PER_GROUP
TOP_K = 2
MOE_BM = 256
ROUTER_ROWS = 32
ROUTE_ROWS = 8
ROW_TILE = 512

LANE = 128
SUBLANE = 8
VMEM_LIMIT = 56 * 1024 * 1024


def _dot(a, b):
    return jnp.dot(a, b, preferred_element_type=F32)


def _dot_nt(a, b):
    return lax.dot_general(a, b, (((1,), (1,)), ((), ())), preferred_element_type=F32)


def _dot_tn(a, b):
    return lax.dot_general(a, b, (((0,), (0,)), ((), ())), preferred_element_type=F32)


def _silu(x):
    return x * (1.0 / (1.0 + jnp.exp(-x)))


def _rms(x):
    return x * lax.rsqrt(jnp.mean(x * x, axis=-1, keepdims=True) + EPS)


def _params(*sem):
    return pltpu.CompilerParams(dimension_semantics=sem, vmem_limit_bytes=VMEM_LIMIT)


ROW_TILES = D_MODEL // LANE


def _load_row_tiles(ref, lead, rows):
    return jnp.concatenate(
        [ref[lead + (pl.ds(c, rows, stride=ROW_TILES), slice(None))] for c in range(ROW_TILES)], axis=1)


def _store_row_tiles(ref, lead, val):
    rows = val.shape[0]
    for c in range(ROW_TILES):
        ref[lead + (pl.ds(c, rows, stride=ROW_TILES), slice(None))] = val[:, c * LANE:(c + 1) * LANE]


def _row_tile(ref, lead, r):
    return ref.at[lead + (pl.ds(pl.multiple_of(r * ROW_TILES, ROW_TILES), ROW_TILES), slice(None))]


def _ada_kernel(c_ref, w_ref, b_ref, o_ref):
    ca = _silu(c_ref[...])
    o_ref[0] = jnp.dot(ca, w_ref[0], preferred_element_type=F32,
                       precision=lax.Precision.HIGHEST) + b_ref[0]


def _ada(c, w, b):
    n_l, d, n = w.shape
    bsz = c.shape[0]
    tn = 1024
    return pl.pallas_call(
        _ada_kernel,
        out_shape=jax.ShapeDtypeStruct((n_l, bsz, n), F32),
        grid=(n_l, n // tn),
        in_specs=[
            pl.BlockSpec((bsz, d), lambda l, j: (0, 0)),
            pl.BlockSpec((1, d, tn), lambda l, j: (l, 0, j)),
            pl.BlockSpec((1, 1, tn), lambda l, j: (l, 0, j)),
        ],
        out_specs=pl.BlockSpec((1, bsz, tn), lambda l, j: (l, 0, j)),
        compiler_params=_params("arbitrary", "arbitrary"),
        name="ada",
    )(c, w, b.reshape(n_l, 1, n))


def _ffn_prenorm_route(first_step, xn, gain, shift, scale, wr_hi_ref, wr_lo_ref, br_ref, tri_ref,
                       route_ref, cnt_ref, carry):
    c = xn.shape[0]
    h = _rms(xn) * gain
    h = h * (1.0 + scale) + shift
    h_hi = h.astype(BF16)
    h_lo = (h - h_hi.astype(F32)).astype(BF16)
    w_hi = wr_hi_ref[...]
    lt = _dot_nt(w_hi, h_hi) + _dot_nt(wr_lo_ref[...], h_hi) + _dot_nt(w_hi, h_lo) + br_ref[...]
    neg_inf = jnp.float32(-jnp.inf)

    grow = lax.broadcasted_iota(jnp.int32, (SUBLANE, c), 0).astype(F32)
    g = jnp.where(grow < N_GROUPS, lt[N_EXPERTS:N_EXPERTS + SUBLANE], neg_inf)
    gmax = jnp.max(g, axis=0, keepdims=True)
    gidx = jnp.min(jnp.where(g == gmax, grow, float(SUBLANE)), axis=0, keepdims=True)
    gate = 1.0 / jnp.sum(jnp.exp(g - gmax), axis=0, keepdims=True)

    erow_i = lax.broadcasted_iota(jnp.int32, (N_EXPERTS, c), 0)
    erow = erow_i.astype(F32)
    egroup = (erow_i // EXPERTS_PER_GROUP).astype(F32)
    el = jnp.where(egroup == gidx, lt[0:N_EXPERTS], neg_inf)
    m1 = jnp.max(el, axis=0, keepdims=True)
    i1 = jnp.min(jnp.where(el == m1, erow, float(N_EXPERTS)), axis=0, keepdims=True)
    el2 = jnp.where(erow == i1, neg_inf, el)
    m2 = jnp.max(el2, axis=0, keepdims=True)
    i2 = jnp.min(jnp.where(el2 == m2, erow, float(N_EXPERTS)), axis=0, keepdims=True)
    t = jnp.exp(m2 - m1)
    den = 1.0 / (1.0 + t)
    w1 = gate * den
    w2 = gate * t * den

    @pl.when(first_step)
    def _():
        carry[...] = jnp.zeros_like(carry)

    oh1 = erow == i1
    oh2 = erow == i2
    oh = jnp.where(jnp.logical_or(oh1, oh2), 1.0, 0.0)
    before = carry[...] + _dot(oh.astype(BF16), tri_ref[...])
    rank1 = jnp.sum(jnp.where(oh1, before, 0.0), axis=0, keepdims=True)
    rank2 = jnp.sum(jnp.where(oh2, before, 0.0), axis=0, keepdims=True)
    total = carry[...] + jnp.sum(oh, axis=1, keepdims=True)
    carry[...] = total
    cnt_ref[...] = jnp.broadcast_to(total, cnt_ref.shape)

    rrow = lax.broadcasted_iota(jnp.int32, (ROUTE_ROWS, c), 0)
    rec = jnp.zeros((ROUTE_ROWS, c), F32)
    for idx, val in enumerate((i1, i2, rank1, rank2, w1, w2)):
        rec = jnp.where(rrow == idx, val, rec)
    route_ref[0, 0] = rec
    return h


def _route_specs(d, c):
    const = lambda *shape: pl.BlockSpec(shape, lambda b, i: (0,) * len(shape))
    return [const(ROUTER_ROWS, d), const(ROUTER_ROWS, d), const(ROUTER_ROWS, 1), const(c, c)]


def _route_out(bsz, seq, c):
    shapes = (jax.ShapeDtypeStruct((bsz, seq // c, ROUTE_ROWS, c), F32),
              jax.ShapeDtypeStruct((N_EXPERTS, LANE), F32))
    specs = (pl.BlockSpec((1, 1, ROUTE_ROWS, c), lambda b, i: (b, i, 0, 0)),
             pl.BlockSpec((N_EXPERTS, LANE), lambda b, i: (0, 0)))
    return shapes, specs


def _earlier_token_matrix(c):
    t = jnp.arange(c, dtype=jnp.int32)
    return (t[:, None] < t[None, :]).astype(BF16)


def _ret_in_kernel(x_ref, ada_ref, gain_ref, cos_ref, sin_ref, w_ref, o_ref):
    x = x_ref[0]
    shift = ada_ref[0, 0:1, :]
    scale = ada_ref[0, 1:2, :]
    h = (_rms(x) * gain_ref[...]) * (1.0 + scale) + shift
    hb = h.astype(BF16)
    cos = cos_ref[...]
    sin = sin_ref[...]
    half = RET_DK // 2
    for j in range(RET_IN // D_MODEL):
        p = _dot(hb, w_ref[:, j * D_MODEL:(j + 1) * D_MODEL])
        if j < 2:
            post = 1.0 if j == 0 else RET_DK ** -0.5
            for hd in range(RET_HEADS):
                lo = hd * RET_DK
                x1 = p[:, lo:lo + half]
                x2 = p[:, lo + half:lo + RET_DK]
                o_ref[0, :, j * D_MODEL + lo:j * D_MODEL + lo + half] = (
                    (x1 * cos - x2 * sin) * post).astype(BF16)
                o_ref[0, :, j * D_MODEL + lo + half:j * D_MODEL + lo + RET_DK] = (
                    (x2 * cos + x1 * sin) * post).astype(BF16)
        else:
            o_ref[0, :, j * D_MODEL:(j + 1) * D_MODEL] = p.astype(BF16)


def _ret_in(x, ada, gain, cos, sin, w_in):
    bsz, seq, d = x.shape
    tm = ROW_TILE
    return pl.pallas_call(
        _ret_in_kernel,
        out_shape=jax.ShapeDtypeStruct((bsz, seq, RET_IN), BF16),
        grid=(bsz, seq // tm),
        in_specs=[
            pl.BlockSpec((1, tm, d), lambda b, i: (b, i, 0)),
            pl.BlockSpec((1, 6, d), lambda b, i: (b, 0, 0)),
            pl.BlockSpec((1, d), lambda b, i: (0, 0)),
            pl.BlockSpec((tm, RET_DK // 2), lambda b, i: (i, 0)),
            pl.BlockSpec((tm, RET_DK // 2), lambda b, i: (i, 0)),
            pl.BlockSpec((d, RET_IN), lambda b, i: (0, 0), pipeline_mode=pl.Buffered(1)),
        ],
        out_specs=pl.BlockSpec((1, tm, RET_IN), lambda b, i: (b, i, 0)),
        compiler_params=_params("arbitrary", "arbitrary"),
        name="ret_in",
    )(x, ada, gain, cos, sin, w_in)


def _ret_core_kernel(q_ref, k_ref, v_ref, g_ref, x_ref, ada_ref, intra_ref, qd_ref, kd_ref, cd_ref,
                     gn_ref, wo_ref, ngain_ref, wr_hi_ref, wr_lo_ref, br_ref, tri_ref,
                     xo_ref, h_ref, route_ref, cnt_ref, state, ybuf, carry):
    @pl.when(pl.program_id(1) == 0)
    def _():
        state[...] = jnp.zeros_like(state)

    for hd in range(RET_HEADS):
        q = q_ref[0, :, hd * RET_DK:(hd + 1) * RET_DK]
        k = k_ref[0, :, hd * RET_DK:(hd + 1) * RET_DK]
        v = v_ref[0, :, hd * RET_DV:(hd + 1) * RET_DV]
        st = state[hd]
        s = _dot_nt(q, k) * intra_ref[hd]
        o = _dot(s.astype(BF16), v) + qd_ref[hd] * _dot(q, st.astype(BF16))
        kdec = (k.astype(F32) * kd_ref[hd]).astype(BF16)
        state[hd] = st * cd_ref[hd] + _dot_tn(kdec, v)
        o = _rms(o) * gn_ref[:, hd * RET_DV:(hd + 1) * RET_DV]
        g = g_ref[0, :, hd * RET_DV:(hd + 1) * RET_DV].astype(F32)
        ybuf[:, hd * RET_DV:(hd + 1) * RET_DV] = (_silu(g) * o).astype(BF16)

    y = _dot(ybuf[...], wo_ref[...])
    xn = x_ref[0] + ada_ref[0, 2:3, :] * y
    xo_ref[0] = xn
    first = jnp.logical_and(pl.program_id(0) == 0, pl.program_id(1) == 0)
    h = _ffn_prenorm_route(first, xn, ngain_ref[...], ada_ref[0, 3:4, :], ada_ref[0, 4:5, :],
                           wr_hi_ref, wr_lo_ref, br_ref, tri_ref, route_ref, cnt_ref, carry)
    _store_row_tiles(h_ref, (0,), h)


def _ret_core(proj, x, ada, gn, w_out, ngain, router):
    bsz, seq, d = x.shape
    c = RET_CHUNK
    f32 = F32
    log_gamma = jnp.log1p(-jnp.exp2(-5.0 - jnp.arange(RET_HEADS, dtype=f32)))
    n = jnp.arange(c, dtype=f32)
    rel = n[:, None] - n[None, :]
    intra = jnp.where(rel >= 0, jnp.exp(jnp.maximum(rel, 0.0)[None] * log_gamma[:, None, None]), 0.0)
    qd = jnp.exp((n + 1.0)[None, :] * log_gamma[:, None])[:, :, None]
    kd = jnp.exp((c - 1.0 - n)[None, :] * log_gamma[:, None])[:, :, None]
    cd = jnp.broadcast_to(jnp.exp(c * log_gamma)[:, None, None], (RET_HEADS, 1, RET_DV))
    const = lambda *shape: pl.BlockSpec(shape, lambda b, i: (0,) * len(shape))
    tok = lambda w, j: pl.BlockSpec((1, c, w), lambda b, i: (b, i, j))
    route_shapes, route_specs = _route_out(bsz, seq, c)
    return pl.pallas_call(
        _ret_core_kernel,
        out_shape=(jax.ShapeDtypeStruct((bsz, seq, d), f32),
                   jax.ShapeDtypeStruct((bsz, seq * ROW_TILES, LANE), f32)) + route_shapes,
        grid=(bsz, seq // c),
        in_specs=[
            tok(RET_QK, 0), tok(RET_QK, 1), tok(RET_V, 1), tok(RET_V, 2),
            tok(d, 0),
            pl.BlockSpec((1, 6, d), lambda b, i: (b, 0, 0)),
            const(RET_HEADS, c, c), const(RET_HEADS, c, 1), const(RET_HEADS, c, 1),
            const(RET_HEADS, 1, RET_DV),
            const(1, RET_V), const(RET_V, d), const(1, d),
        ] + _route_specs(d, c),
        out_specs=(tok(d, 0), pl.BlockSpec((1, c * ROW_TILES, LANE), lambda b, i: (b, i, 0))) + route_specs,
        scratch_shapes=[pltpu.VMEM((RET_HEADS, RET_DK, RET_DV), f32),
                        pltpu.VMEM((c, RET_V), BF16),
                        pltpu.VMEM((N_EXPERTS, 1), f32)],
        compiler_params=_params("arbitrary", "arbitrary"),
        name="ret_core",
    )(proj, proj, proj, proj, x, ada, intra, qd, kd, cd, gn, w_out, ngain, *router, _earlier_token_matrix(c))


def _dispatch_kernel(d1_ref, d2_ref, h_ref, xs_hbm, sem):
    tm = h_ref.shape[0] // ROW_TILES

    def body(t, carry):
        for d_ref in (d1_ref, d2_ref):
            pltpu.make_async_copy(_row_tile(h_ref, (), t), _row_tile(xs_hbm, (), d_ref[0, 0, t]), sem).start()
        return carry
    lax.fori_loop(0, tm, body, 0, unroll=8)
    for _ in range(TOP_K):
        pltpu.make_async_copy(h_ref, xs_hbm.at[pl.ds(0, tm * ROW_TILES), :], sem).wait()


def _dispatch(h, dest1, dest2):
    n_tok = h.shape[0] // ROW_TILES
    tm = ROW_TILE
    nt = n_tok // tm
    idx = lambda: pl.BlockSpec((1, 1, tm), lambda i: (i, 0, 0), memory_space=pltpu.SMEM)
    return pl.pallas_call(
        _dispatch_kernel,
        out_shape=jax.ShapeDtypeStruct((TOP_K * n_tok * ROW_TILES, LANE), F32),
        grid=(nt,),
        in_specs=[idx(), idx(), pl.BlockSpec((tm * ROW_TILES, LANE), lambda i: (i, 0))],
        out_specs=pl.BlockSpec(memory_space=pl.ANY),
        scratch_shapes=[pltpu.SemaphoreType.DMA(())],
        compiler_params=_params("arbitrary"),
        name="dispatch",
    )(dest1.reshape(nt, 1, tm), dest2.reshape(nt, 1, tm), h)


def _moe_kernel(blk_ref, exp_ref, nitem_ref, start_ref, cnt_ref, x_ref, w1_ref, w3_ref, w2_ref, o_ref,
                w1b, w3b, w2b):
    w = pl.program_id(0)
    bm = x_ref.shape[0] // ROW_TILES

    @pl.when(w < nitem_ref[0])
    def _():
        e = exp_ref[w]
        blk = blk_ref[w]
        prev = jnp.maximum(w - 1, 0)

        @pl.when(jnp.logical_or(w == 0, exp_ref[prev] != e))
        def _():
            w1b[...] = w1_ref[0, 0].astype(BF16)
            w3b[...] = w3_ref[0, 0].astype(BF16)
            w2b[...] = w2_ref[0, 0].astype(BF16)

        x = _load_row_tiles(x_ref, (), bm).astype(BF16)
        a = _dot(x, w1b[...])
        b = _dot(x, w3b[...])
        y = _dot((_silu(a) * b).astype(BF16), w2b[...])
        row = blk * bm + lax.broadcasted_iota(jnp.int32, (bm, 1), 0)
        mine = jnp.logical_and(row >= start_ref[e], row < start_ref[e] + cnt_ref[e])
        first_visit = jnp.logical_or(w == 0, blk_ref[prev] != blk)

        @pl.when(first_visit)
        def _():
            _store_row_tiles(o_ref, (), jnp.where(mine, y, 0.0))

        @pl.when(jnp.logical_not(first_visit))
        def _():
            _store_row_tiles(o_ref, (), jnp.where(mine, y, _load_row_tiles(o_ref, (), bm)))


def _moe(xs, counts, w1, w3, w2, layer):
    m = xs.shape[0] // ROW_TILES
    d = w1.shape[-1]
    bm = MOE_BM
    n_blocks = m // bm
    n_items = n_blocks + N_EXPERTS - 1
    ends = jnp.cumsum(counts)
    starts = ends - counts
    first_blk = starts // bm
    last_blk = jnp.maximum(ends - 1, starts) // bm
    visits = jnp.where(counts > 0, last_blk - first_blk + 1, 0)
    item_end = jnp.cumsum(visits)
    item_start = item_end - visits
    n_item = item_end[-1]
    w = jnp.minimum(jnp.arange(n_items, dtype=jnp.int32), n_item - 1)
    item_e = jnp.sum(w[:, None] >= item_end[None, :], axis=1).astype(jnp.int32)
    item_blk = (first_blk[item_e] + w - item_start[item_e]).astype(jnp.int32)
    wspec = lambda: pl.BlockSpec((1, 1, d, d), lambda i, blk, e, *_: (layer, e[i], 0, 0))
    rows = lambda: pl.BlockSpec((bm * ROW_TILES, LANE), lambda i, blk, *_: (blk[i], 0))
    return pl.pallas_call(
        _moe_kernel,
        out_shape=jax.ShapeDtypeStruct((m * ROW_TILES, LANE), F32),
        grid_spec=pltpu.PrefetchScalarGridSpec(
            num_scalar_prefetch=5,
            grid=(n_items,),
            in_specs=[rows(), wspec(), wspec(), wspec()],
            out_specs=rows(),
            scratch_shapes=[pltpu.VMEM((d, d), BF16)] * 3,
        ),
        compiler_params=_params("arbitrary"),
        name="moe",
    )(item_blk, item_e, n_item.astype(jnp.int32).reshape(1), starts.astype(jnp.int32), counts.astype(jnp.int32),
      xs, w1, w3, w2)


class _RowGather:
    def __init__(self, ys_hbm, buf, sem, tm):
        self.ys, self.buf, self.sem, self.tm = ys_hbm, buf, sem, tm

    def start(self, d_refs, slot):
        def body(t, carry):
            for k, d_ref in enumerate(d_refs):
                pltpu.make_async_copy(_row_tile(self.ys, (), d_ref[0, 0, t]), _row_tile(self.buf, (slot, k), t),
                                      self.sem.at[slot]).start()
            return carry
        lax.fori_loop(0, self.tm, body, 0, unroll=8)

    def wait(self, slot):
        for k in range(TOP_K):
            pltpu.make_async_copy(self.ys.at[pl.ds(0, self.tm * ROW_TILES), :], self.buf.at[slot, k],
                                  self.sem.at[slot]).wait()

    def combined(self, step, n_steps, cur_refs, nxt_refs, w1_ref, w2_ref):
        slot = step % 2

        @pl.when(step == 0)
        def _():
            self.start(cur_refs, 0)

        @pl.when(step + 1 < n_steps)
        def _():
            self.start(nxt_refs, 1 - slot)

        self.wait(slot)
        return (w1_ref[...] * _load_row_tiles(self.buf, (slot, 0), self.tm)
                + w2_ref[...] * _load_row_tiles(self.buf, (slot, 1), self.tm))


def _gather_specs(tm, nt_total, step_of):
    cur = lambda: pl.BlockSpec((1, 1, tm), lambda *g: (step_of(*g), 0, 0), memory_space=pltpu.SMEM)
    nxt = lambda: pl.BlockSpec((1, 1, tm), lambda *g: (jnp.minimum(step_of(*g) + 1, nt_total - 1), 0, 0),
                               memory_space=pltpu.SMEM)
    wcol = lambda: pl.BlockSpec((tm, 1), lambda *g: (step_of(*g), 0))
    return [cur(), cur(), nxt(), nxt(), wcol(), wcol()]


def _gather_scratch(tm):
    return [pltpu.VMEM((2, TOP_K, tm * ROW_TILES, LANE), F32), pltpu.SemaphoreType.DMA((2,))]


def _group_norm64(t, ind_ref, indt_ref):
    ss = _dot((t * t).astype(BF16), ind_ref[...])
    r = lax.rsqrt(ss * (1.0 / DIFF_HEAD_DIM) + EPS)
    r_hi = r.astype(BF16)
    r_lo = (r - r_hi.astype(F32)).astype(BF16)
    rb = _dot(jnp.concatenate([r_hi, r_lo], axis=1), indt_ref[...])
    return t * rb


def _kvq_kernel(d1c, d2c, d1n, d2n, w1_ref, w2_ref, ys_hbm, x_ref, ada0_ref, ada1_ref, kvada_ref, kvgain_ref,
                qgain_ref, knorm_ref, qnorm_ref, wk_ref, wvt_ref, wq_ref, ind_ref, indt_ref,
                xo_ref, q_ref, k_ref, vt_ref, gbuf, gsem):
    tm = x_ref.shape[1]
    step = pl.program_id(0) * pl.num_programs(1) + pl.program_id(1)
    n_steps = pl.num_programs(0) * pl.num_programs(1)
    moe = _RowGather(ys_hbm, gbuf, gsem, tm).combined(step, n_steps, (d1c, d2c), (d1n, d2n), w1_ref, w2_ref)
    xn = x_ref[0] + ada0_ref[0, 5:6, :] * moe
    xo_ref[0] = xn
    r = _rms(xn)
    hk = ((r * kvgain_ref[...]) * (1.0 + kvada_ref[0, 1:2, :]) + kvada_ref[0, 0:1, :]).astype(BF16)
    hq = ((r * qgain_ref[...]) * (1.0 + ada1_ref[0, 1:2, :]) + ada1_ref[0, 0:1, :]).astype(BF16)
    hd2 = 2 * DIFF_HEAD_DIM
    kk = _dot(hk, wk_ref[...])
    kn = (_group_norm64(kk, ind_ref, indt_ref) * knorm_ref[...]).astype(BF16)
    qq = _dot(hq, wq_ref[...])
    qn = (_group_norm64(qq, ind_ref, indt_ref) * qnorm_ref[...]).astype(BF16)
    for hd in range(DIFF_HEADS):
        k_ref[0, hd] = kn[:, hd * hd2:(hd + 1) * hd2]
        q_ref[0, hd] = qn[:, hd * hd2:(hd + 1) * hd2]
    vt = _dot_nt(wvt_ref[...], hk).astype(BF16)
    for j in range(tm // ATT_TK):
        vt_ref[0, j] = vt[:, j * ATT_TK:(j + 1) * ATT_TK].reshape(DIFF_HEADS, hd2, ATT_TK)


def _kvq(x, ys, dest1, dest2, w1, w2, ada0, ada1, kvada, kv_gain, q_gain, k_norm, q_norm, w_k, w_vt, w_q):
    bsz, seq, d = x.shape
    tm = ROW_TILE
    nt = seq // tm
    nt_total = bsz * nt
    lane_group = jnp.arange(d, dtype=jnp.int32) // DIFF_HEAD_DIM
    ind = (lane_group[:, None] == jnp.arange(LANE, dtype=jnp.int32)[None, :]).astype(BF16)
    indt = jnp.concatenate([ind.T, ind.T], axis=0)
    reps = d // DIFF_HEAD_DIM
    q_scale = jnp.tile(q_norm, reps)[None, :] * (DIFF_HEAD_DIM ** -0.5 * math.log2(math.e))
    const = lambda *shape: pl.BlockSpec(shape, lambda b, i: (0,) * len(shape))
    tok = lambda w: pl.BlockSpec((1, tm, w), lambda b, i: (b, i, 0))
    hd2 = 2 * DIFF_HEAD_DIM
    heads = pl.BlockSpec((1, DIFF_HEADS, tm, hd2), lambda b, i: (b, 0, i, 0))
    ada_spec = lambda rows: pl.BlockSpec((1, rows, d), lambda b, i: (b, 0, 0))
    col = lambda a: a.reshape(-1, 1)
    idx = lambda a: a.reshape(nt_total, 1, tm)
    return pl.pallas_call(
        _kvq_kernel,
        out_shape=(jax.ShapeDtypeStruct((bsz, seq, d), F32),
                   jax.ShapeDtypeStruct((bsz, DIFF_HEADS, seq, hd2), BF16),
                   jax.ShapeDtypeStruct((bsz, DIFF_HEADS, seq, hd2), BF16),
                   jax.ShapeDtypeStruct((bsz, seq // ATT_TK, DIFF_HEADS, hd2, ATT_TK), BF16)),
        grid=(bsz, nt),
        in_specs=_gather_specs(tm, nt_total, lambda b, i: b * nt + i) + [
            pl.BlockSpec(memory_space=pl.ANY),
            tok(d),
            ada_spec(6), ada_spec(6), ada_spec(2),
            const(1, d), const(1, d), const(1, d), const(1, d),
            const(d, d), const(d, d), const(d, d), const(d, LANE), const(2 * LANE, d),
        ],
        out_specs=(tok(d), heads, heads,
                   pl.BlockSpec((1, tm // ATT_TK, DIFF_HEADS, hd2, ATT_TK), lambda b, i: (b, i, 0, 0, 0))),
        scratch_shapes=_gather_scratch(tm),
        compiler_params=_params("arbitrary", "arbitrary"),
        name="kvq",
    )(idx(dest1), idx(dest2), idx(dest1), idx(dest2), col(w1), col(w2), ys, x, ada0, ada1, kvada,
      kv_gain, q_gain, jnp.tile(k_norm, reps)[None, :], q_scale, w_k, w_vt, w_q, ind, indt)


def _attn_kernel(lambda_init, q_ref, k_ref, vt_ref, x_ref, ada_ref, lam_ref, subln_ref, wo_ref, ngain_ref,
                 wr_hi_ref, wr_lo_ref, br_ref, tri_ref, xo_ref, h_ref, route_ref, cnt_ref,
                 s_a, s_b, p_a, p_b, acc, obuf, carry):
    i = pl.program_id(1)
    tq, tk = ATT_TQ, ATT_TK
    hd2 = 2 * DIFF_HEAD_DIM
    n_kblk = k_ref.shape[2] // tk
    lam = lam_ref[...]
    lam_full = (jnp.exp(jnp.sum(lam[0:1] * lam[1:2], axis=-1, keepdims=True))
                - jnp.exp(jnp.sum(lam[2:3] * lam[3:4], axis=-1, keepdims=True)) + lambda_init)
    lane = lax.broadcasted_iota(jnp.int32, (1, hd2), 1)
    keep = (jnp.where(lane < DIFF_HEAD_DIM, 1.0, 0.0).astype(BF16),
            jnp.where(lane >= DIFF_HEAD_DIM, 1.0, 0.0).astype(BF16))
    key = lax.broadcasted_iota(jnp.int32, (tk, tq), 0)
    qry = lax.broadcasted_iota(jnp.int32, (1, tq), 1)
    key_limit = ((i * tq + qry) // MASK_CHUNK + 1) * MASK_CHUNK
    neg_inf = jnp.float32(-jnp.inf)
    n_pairs = (i + 2) // 2

    def head(hd, _):
        qh = q_ref[0, hd]
        qs = (qh * keep[0], qh * keep[1])

        def produce_scores(jb, s_out):
            kb = k_ref[0, hd, pl.ds(pl.multiple_of(jnp.minimum(jb, n_kblk - 1) * tk, tk), tk), :]
            visible = key < key_limit - jb * tk
            maxima = []
            for t in range(2):
                st = jnp.where(visible, _dot_nt(kb, qs[t]), neg_inf)
                s_out[t] = st
                maxima.append(jnp.max(st, axis=0, keepdims=True))
            return tuple(maxima)

        def step(j, s_in, s_out, p_in, p_out, state):
            stats, blk_max = state
            vtb = vt_ref[0, jnp.clip(j - 1, 0, n_kblk - 1), hd]
            next_max = produce_scores(j + 1, s_out)
            new_stats = []
            for t in range(2):
                m, l = stats[t]
                m_new = jnp.maximum(m, blk_max[t])
                alpha = jnp.exp2(m - m_new)
                p = jnp.exp2(s_in[t] - m_new)
                p_out[t] = p.astype(BF16)
                new_stats.append((m_new, alpha * l + jnp.sum(p, axis=0, keepdims=True)))
                acc[t] = alpha * (acc[t] + _dot(vtb, p_in[t]))
            return tuple(new_stats), next_max

        p_b[...] = jnp.zeros_like(p_b)
        acc[...] = jnp.zeros_like(acc)
        stat0 = (jnp.full((1, tq), neg_inf, F32), jnp.zeros((1, tq), F32))
        state = ((stat0, stat0), produce_scores(0, s_a))

        def pair(jj, state):
            state = step(2 * jj, s_a, s_b, p_b, p_a, state)
            return step(2 * jj + 1, s_b, s_a, p_a, p_b, state)

        (stats, _) = lax.fori_loop(0, n_pairs, pair, state)
        vtb = vt_ref[0, jnp.minimum(2 * n_pairs - 1, n_kblk - 1), hd]
        a1 = acc[0] + _dot(vtb, p_b[0])
        a2 = acc[1] + _dot(vtb, p_b[1])
        o = a1 * (1.0 / stats[0][1]) - lam_full * (a2 * (1.0 / stats[1][1]))
        o = o * lax.rsqrt(jnp.mean(o * o, axis=0, keepdims=True) + EPS)
        obuf[hd] = (o * (subln_ref[...] * (1.0 - lambda_init))).astype(BF16)
        return 0

    lax.fori_loop(0, DIFF_HEADS, head, 0)

    y = _dot_tn(obuf[...].reshape(DIFF_HEADS * hd2, tq), wo_ref[...])
    xn = x_ref[0] + ada_ref[0, 2:3, :] * y
    xo_ref[0] = xn
    first = jnp.logical_and(pl.program_id(0) == 0, i == 0)
    h = _ffn_prenorm_route(first, xn, ngain_ref[...], ada_ref[0, 3:4, :], ada_ref[0, 4:5, :],
                           wr_hi_ref, wr_lo_ref, br_ref, tri_ref, route_ref, cnt_ref, carry)
    _store_row_tiles(h_ref, (0,), h)


def _attn(q, k, vt, x, ada, lam, subln, w_out, ngain, router, lambda_init):
    bsz, seq, d = x.shape
    tq, tk = ATT_TQ, ATT_TK
    hd2 = 2 * DIFF_HEAD_DIM
    const = lambda *shape: pl.BlockSpec(shape, lambda b, i: (0,) * len(shape))
    tok = lambda w: pl.BlockSpec((1, tq, w), lambda b, i: (b, i, 0))
    route_shapes, route_specs = _route_out(bsz, seq, tq)
    return pl.pallas_call(
        functools.partial(_attn_kernel, lambda_init),
        out_shape=(jax.ShapeDtypeStruct((bsz, seq, d), F32),
                   jax.ShapeDtypeStruct((bsz, seq * ROW_TILES, LANE), F32)) + route_shapes,
        grid=(bsz, seq // tq),
        in_specs=[
            pl.BlockSpec((1, DIFF_HEADS, tq, hd2), lambda b, i: (b, 0, i, 0)),
            pl.BlockSpec((1, DIFF_HEADS, seq, hd2), lambda b, i: (b, 0, 0, 0)),
            pl.BlockSpec((1, seq // tk, DIFF_HEADS, hd2, tk), lambda b, i: (b, 0, 0, 0, 0)),
            tok(d),
            pl.BlockSpec((1, 6, d), lambda b, i: (b, 0, 0)),
            const(4, DIFF_HEAD_DIM), const(hd2, 1), const(d, d), const(1, d),
        ] + _route_specs(d, tq),
        out_specs=(tok(d), pl.BlockSpec((1, tq * ROW_TILES, LANE), lambda b, i: (b, i, 0))) + route_specs,
        scratch_shapes=[pltpu.VMEM((2, tk, tq), F32), pltpu.VMEM((2, tk, tq), F32),
                        pltpu.VMEM((2, tk, tq), BF16), pltpu.VMEM((2, tk, tq), BF16),
                        pltpu.VMEM((2, hd2, tq), F32),
                        pltpu.VMEM((DIFF_HEADS, hd2, tq), BF16), pltpu.VMEM((N_EXPERTS, 1), F32)],
        compiler_params=_params("arbitrary", "arbitrary"),
        name="attn",
    )(q, k, vt, x, ada, lam, subln, w_out, ngain, *router, _earlier_token_matrix(tq))


def _combine_kernel(d1c, d2c, d1n, d2n, w1_ref, w2_ref, ys_hbm, x_ref, ada_ref, o_ref, gbuf, gsem):
    tm = x_ref.shape[1]
    step = pl.program_id(0) * pl.num_programs(1) + pl.program_id(1)
    n_steps = pl.num_programs(0) * pl.num_programs(1)
    moe = _RowGather(ys_hbm, gbuf, gsem, tm).combined(step, n_steps, (d1c, d2c), (d1n, d2n), w1_ref, w2_ref)
    o_ref[0] = x_ref[0] + ada_ref[0, 5:6, :] * moe


def _combine(x, ys, dest1, dest2, w1, w2, ada):
    bsz, seq, d = x.shape
    tm = ROW_TILE
    nt = seq // tm
    nt_total = bsz * nt
    col = lambda a: a.reshape(-1, 1)
    idx = lambda a: a.reshape(nt_total, 1, tm)
    return pl.pallas_call(
        _combine_kernel,
        out_shape=jax.ShapeDtypeStruct((bsz, seq, d), F32),
        grid=(bsz, nt),
        in_specs=_gather_specs(tm, nt_total, lambda b, i: b * nt + i) + [
            pl.BlockSpec(memory_space=pl.ANY),
            pl.BlockSpec((1, tm, d), lambda b, i: (b, i, 0)),
            pl.BlockSpec((1, 6, d), lambda b, i: (b, 0, 0)),
        ],
        out_specs=pl.BlockSpec((1, tm, d), lambda b, i: (b, i, 0)),
        scratch_shapes=_gather_scratch(tm),
        compiler_params=_params("arbitrary", "arbitrary"),
        name="combine",
    )(idx(dest1), idx(dest2), idx(dest1), idx(dest2), col(w1), col(w2), ys, x, ada)


def _router_weights(w_group, b_group, w_expert, b_expert):
    d = w_group.shape[0]
    pad = ROUTER_ROWS - N_EXPERTS - N_GROUPS
    w = jnp.concatenate([w_expert.T, w_group.T, jnp.zeros((pad, d), F32)], axis=0)
    b = jnp.concatenate([b_expert, b_group, jnp.zeros((pad,), F32)])[:, None]
    w_hi = w.astype(BF16)
    w_lo = (w - w_hi.astype(F32)).astype(BF16)
    return w_hi, w_lo, b


def _routing_tables(route, cnt):
    rec = lambda r: route[:, :, r, :].reshape(-1)
    counts = cnt[:, 0].astype(jnp.int32)
    starts = jnp.cumsum(counts) - counts
    e1 = rec(0).astype(jnp.int32)
    e2 = rec(1).astype(jnp.int32)
    dest1 = starts[e1] + rec(2).astype(jnp.int32)
    dest2 = starts[e2] + rec(3).astype(jnp.int32)
    return dest1, dest2, rec(4), rec(5), counts


def kernel(x, c, ada_w, ada_b, norm_mix, norm_ffn, ret_w_in, ret_gn, ret_w_out, kv_ada_w, kv_ada_b, kv_norm, kv_w, k_norm, diff_w_q, q_norm, diff_lam, diff_subln, diff_w_out, moe_w_group, moe_b_group, moe_w_expert, moe_b_expert, moe_w1, moe_w3, moe_w2):
    bsz, seq, d = x.shape
    n_tok = bsz * seq

    ada = _ada(c, ada_w, ada_b).reshape(2, bsz, 6, d)
    kvada = _ada(c, kv_ada_w[None], kv_ada_b[None]).reshape(bsz, 2, d)

    half = RET_DK // 2
    inv_freq = 1.0 / (ROPE_BASE ** (jnp.arange(half, dtype=F32) / half))
    ang = jnp.arange(seq, dtype=F32)[:, None] * inv_freq[None, :]
    cos, sin = jnp.cos(ang), jnp.sin(ang)

    routers = [_router_weights(moe_w_group[l], moe_b_group[l], moe_w_expert[l], moe_b_expert[l])
               for l in range(2)]

    def moe_layer(h, route, cnt, layer):
        dest1, dest2, w1, w2, counts = _routing_tables(route, cnt)
        xs = _dispatch(h.reshape(n_tok * ROW_TILES, LANE), dest1, dest2)
        ys = _moe(xs, counts, moe_w1, moe_w3, moe_w2, layer)
        return ys, dest1, dest2, w1, w2

    proj = _ret_in(x, ada[0], norm_mix[0][None, :], cos, sin, ret_w_in[0].astype(BF16))
    x1, h, route, cnt = _ret_core(proj, x, ada[0], ret_gn[0][None, :], ret_w_out[0].astype(BF16),
                                  norm_ffn[0][None, :], routers[0])
    moe0 = moe_layer(h, route, cnt, 0)

    x2, q, k, vt = _kvq(x1, *moe0, ada[0], ada[1], kvada, kv_norm[None, :], norm_mix[1][None, :], k_norm, q_norm[0],
                        kv_w[:, :d].astype(BF16), kv_w[:, d:].T.astype(BF16), diff_w_q[0].astype(BF16))
    lambda_init = 0.8 - 0.6 * math.exp(-0.3 * 1)
    x3, h, route, cnt = _attn(q, k, vt, x2, ada[1], diff_lam[0], diff_subln[0][:, None], diff_w_out[0].astype(BF16),
                              norm_ffn[1][None, :], routers[1], lambda_init)
    moe1 = moe_layer(h, route, cnt, 1)
    return _combine(x3, *moe1, ada[1])
```

```python
import functools
import math

import jax
import jax.numpy as jnp
from jax import lax
from jax.experimental import pallas as pl
from jax.experimental.pallas import tpu as pltpu

F32 = jnp.float32
BF16 = jnp.bfloat16

D_MODEL = 1024
EPS = 1e-6

RET_HEADS = 4
RET_DK = D_MODEL // RET_HEADS
RET_DV = 2 * RET_DK
RET_QK = RET_HEADS * RET_DK
RET_V = RET_HEADS * RET_DV
RET_IN = 2 * RET_QK + 2 * RET_V
ROPE_BASE = 10000.0
RET_CHUNK = 256

DIFF_HEAD_DIM = 64
DIFF_HEADS = D_MODEL // (2 * DIFF_HEAD_DIM)
MASK_CHUNK = 64
ATT_TQ = 256
ATT_TK = 256
ATT_HEADS = 1

N_GROUPS = 4
EXPERTS_PER_GROUP = 4
N_EXPERTS = N_GROUPS * EXPERTS_PER_GROUP
TOP_K = 2
MOE_BM = 256
ROUTER_ROWS = 32
ROUTE_ROWS = 8
ROW_TILE = 512
ROUTE_TILE = 2048
ROUTE_SEG = 256

LANE = 128
SUBLANE = 8
VMEM_LIMIT = 56 * 1024 * 1024


def _dot(a, b):
    return jnp.dot(a, b, preferred_element_type=F32)


def _dot_nt(a, b):
    return lax.dot_general(a, b, (((1,), (1,)), ((), ())), preferred_element_type=F32)


def _dot_tn(a, b):
    return lax.dot_general(a, b, (((0,), (0,)), ((), ())), preferred_element_type=F32)


def _silu(x):
    return x * (1.0 / (1.0 + jnp.exp(-x)))


def _rms(x):
    return x * lax.rsqrt(jnp.mean(x * x, axis=-1, keepdims=True) + EPS)


def _params(*sem, flags=None):
    return pltpu.CompilerParams(dimension_semantics=sem, vmem_limit_bytes=VMEM_LIMIT, flags=flags)


ROW_TILES = D_MODEL // LANE


def _load_row_tiles(ref, lead, rows):
    return jnp.concatenate(
        [ref[lead + (pl.ds(c, rows, stride=ROW_TILES), slice(None))] for c in range(ROW_TILES)], axis=1)


def _store_row_tiles(ref, lead, val):
    rows = val.shape[0]
    for c in range(ROW_TILES):
        ref[lead + (pl.ds(c, rows, stride=ROW_TILES), slice(None))] = val[:, c * LANE:(c + 1) * LANE]


def _row_tile(ref, lead, r):
    return ref.at[lead + (pl.ds(pl.multiple_of(r * ROW_TILES, ROW_TILES), ROW_TILES), slice(None))]


def _ada_kernel(c_ref, w_ref, b_ref, o_ref):
    ca = _silu(c_ref[...])
    o_ref[0] = jnp.dot(ca, w_ref[0], preferred_element_type=F32,
                       precision=lax.Precision.HIGHEST) + b_ref[0]


def _ada(c, w, b):
    n_l, d, n = w.shape
    bsz = c.shape[0]
    tn = 1024
    return pl.pallas_call(
        _ada_kernel,
        out_shape=jax.ShapeDtypeStruct((n_l, bsz, n), F32),
        grid=(n_l, n // tn),
        in_specs=[
            pl.BlockSpec((bsz, d), lambda l, j: (0, 0)),
            pl.BlockSpec((1, d, tn), lambda l, j: (l, 0, j)),
            pl.BlockSpec((1, 1, tn), lambda l, j: (l, 0, j)),
        ],
        out_specs=pl.BlockSpec((1, bsz, tn), lambda l, j: (l, 0, j)),
        compiler_params=_params("arbitrary", "arbitrary"),
        name="ada",
    )(c, w, b.reshape(n_l, 1, n))


def _ffn_prenorm_logits(xn, gain, shift, scale, wr_hi_ref, wr_lo_ref, br_ref, lt_ref):
    h = _rms(xn) * gain
    h = h * (1.0 + scale) + shift
    h_hi = h.astype(BF16)
    h_lo = (h - h_hi.astype(F32)).astype(BF16)
    w_hi = wr_hi_ref[...]
    lt_ref[...] = _dot_nt(w_hi, h_hi) + _dot_nt(wr_lo_ref[...], h_hi) + _dot_nt(w_hi, h_lo) + br_ref[...]
    return h


def _route_kernel(lt_ref, tri_ref, route_ref, cnt_ref, carry):
    lt = lt_ref[...]
    c = lt.shape[1]
    seg = tri_ref.shape[0]
    neg_inf = jnp.float32(-jnp.inf)

    grow = lax.broadcasted_iota(jnp.int32, (SUBLANE, c), 0).astype(F32)
    g = jnp.where(grow < N_GROUPS, lt[N_EXPERTS:N_EXPERTS + SUBLANE], neg_inf)
    gmax = jnp.max(g, axis=0, keepdims=True)
    gidx = jnp.min(jnp.where(g == gmax, grow, float(SUBLANE)), axis=0, keepdims=True)
    gate = 1.0 / jnp.sum(jnp.exp(g - gmax), axis=0, keepdims=True)

    erow_i = lax.broadcasted_iota(jnp.int32, (N_EXPERTS, c), 0)
    erow = erow_i.astype(F32)
    egroup = (erow_i // EXPERTS_PER_GROUP).astype(F32)
    el = jnp.where(egroup == gidx, lt[0:N_EXPERTS], neg_inf)
    m1 = jnp.max(el, axis=0, keepdims=True)
    i1 = jnp.min(jnp.where(el == m1, erow, float(N_EXPERTS)), axis=0, keepdims=True)
    el2 = jnp.where(erow == i1, neg_inf, el)
    m2 = jnp.max(el2, axis=0, keepdims=True)
    i2 = jnp.min(jnp.where(el2 == m2, erow, float(N_EXPERTS)), axis=0, keepdims=True)
    t = jnp.exp(m2 - m1)
    den = 1.0 / (1.0 + t)
    w1 = gate * den
    w2 = gate * t * den

    @pl.when(pl.program_id(0) == 0)
    def _():
        carry[...] = jnp.zeros_like(carry)

    oh1 = erow == i1
    oh2 = erow == i2
    oh = jnp.where(jnp.logical_or(oh1, oh2), 1.0, 0.0)
    offset = carry[...]
    before = []
    for s in range(c // seg):
        oh_s = oh[:, s * seg:(s + 1) * seg]
        before.append(offset + _dot(oh_s.astype(BF16), tri_ref[...]))
        offset = offset + jnp.sum(oh_s, axis=1, keepdims=True)
    before = jnp.concatenate(before, axis=1)
    rank1 = jnp.sum(jnp.where(oh1, before, 0.0), axis=0, keepdims=True)
    rank2 = jnp.sum(jnp.where(oh2, before, 0.0), axis=0, keepdims=True)
    carry[...] = offset
    cnt_ref[...] = jnp.broadcast_to(offset, cnt_ref.shape)

    rrow = lax.broadcasted_iota(jnp.int32, (ROUTE_ROWS, c), 0)
    rec = jnp.zeros((ROUTE_ROWS, c), F32)
    for idx, val in enumerate((i1, i2, rank1, rank2, w1, w2)):
        rec = jnp.where(rrow == idx, val, rec)
    route_ref[...] = rec


def _route(lt):
    n_tok = lt.shape[1]
    tile = min(ROUTE_TILE, n_tok)
    t = jnp.arange(ROUTE_SEG, dtype=jnp.int32)
    earlier = (t[:, None] < t[None, :]).astype(BF16)
    return pl.pallas_call(
        _route_kernel,
        out_shape=(jax.ShapeDtypeStruct((ROUTE_ROWS, n_tok), F32),
                   jax.ShapeDtypeStruct((N_EXPERTS, LANE), F32)),
        grid=(n_tok // tile,),
        in_specs=[pl.BlockSpec((ROUTER_ROWS, tile), lambda i: (0, i)),
                  pl.BlockSpec((ROUTE_SEG, ROUTE_SEG), lambda i: (0, 0))],
        out_specs=(pl.BlockSpec((ROUTE_ROWS, tile), lambda i: (0, i)),
                   pl.BlockSpec((N_EXPERTS, LANE), lambda i: (0, 0))),
        scratch_shapes=[pltpu.VMEM((N_EXPERTS, 1), F32)],
        compiler_params=_params("arbitrary"),
        name="route",
    )(lt, earlier)


def _logit_specs(d):
    const = lambda *shape: pl.BlockSpec(shape, lambda b, i: (0,) * len(shape))
    return [const(ROUTER_ROWS, d), const(ROUTER_ROWS, d), const(ROUTER_ROWS, 1)]


def _ret_in_kernel(x_ref, ada_ref, gain_ref, cos_ref, sin_ref, w_ref, o_ref):
    x = x_ref[0]
    shift = ada_ref[0, 0:1, :]
    scale = ada_ref[0, 1:2, :]
    h = (_rms(x) * gain_ref[...]) * (1.0 + scale) + shift
    hb = h.astype(BF16)
    cos = cos_ref[...]
    sin = sin_ref[...]
    half = RET_DK // 2
    for j in range(RET_IN // D_MODEL):
        p = _dot(hb, w_ref[:, j * D_MODEL:(j + 1) * D_MODEL])
        if j < 2:
            post = 1.0 if j == 0 else RET_DK ** -0.5
            for hd in range(RET_HEADS):
                lo = hd * RET_DK
                x1 = p[:, lo:lo + half]
                x2 = p[:, lo + half:lo + RET_DK]
                o_ref[0, :, j * D_MODEL + lo:j * D_MODEL + lo + half] = (
                    (x1 * cos - x2 * sin) * post).astype(BF16)
                o_ref[0, :, j * D_MODEL + lo + half:j * D_MODEL + lo + RET_DK] = (
                    (x2 * cos + x1 * sin) * post).astype(BF16)
        else:
            o_ref[0, :, j * D_MODEL:(j + 1) * D_MODEL] = p.astype(BF16)


def _ret_in(x, ada, gain, cos, sin, w_in):
    bsz, seq, d = x.shape
    tm = ROW_TILE
    return pl.pallas_call(
        _ret_in_kernel,
        out_shape=jax.ShapeDtypeStruct((bsz, seq, RET_IN), BF16),
        grid=(bsz, seq // tm),
        in_specs=[
            pl.BlockSpec((1, tm, d), lambda b, i: (b, i, 0)),
            pl.BlockSpec((1, 6, d), lambda b, i: (b, 0, 0)),
            pl.BlockSpec((1, d), lambda b, i: (0, 0)),
            pl.BlockSpec((tm, RET_DK // 2), lambda b, i: (i, 0)),
            pl.BlockSpec((tm, RET_DK // 2), lambda b, i: (i, 0)),
            pl.BlockSpec((d, RET_IN), lambda b, i: (0, 0), pipeline_mode=pl.Buffered(1)),
        ],
        out_specs=pl.BlockSpec((1, tm, RET_IN), lambda b, i: (b, i, 0)),
        compiler_params=_params("arbitrary", "arbitrary"),
        name="ret_in",
    )(x, ada, gain, cos, sin, w_in)


def _ret_core_kernel(q_ref, k_ref, v_ref, g_ref, x_ref, ada_ref, intra_ref, qd_ref, kd_ref, cd_ref,
                     gn_ref, wo_ref, ngain_ref, wr_hi_ref, wr_lo_ref, br_ref,
                     xo_ref, h_ref, lt_ref, state, ybuf):
    @pl.when(pl.program_id(1) == 0)
    def _():
        state[...] = jnp.zeros_like(state)

    for hd in range(RET_HEADS):
        q = q_ref[0, :, hd * RET_DK:(hd + 1) * RET_DK]
        k = k_ref[0, :, hd * RET_DK:(hd + 1) * RET_DK]
        v = v_ref[0, :, hd * RET_DV:(hd + 1) * RET_DV]
        st = state[hd]
        s = _dot_nt(q, k) * intra_ref[hd]
        o = _dot(s.astype(BF16), v) + qd_ref[hd] * _dot(q, st.astype(BF16))
        kdec = (k.astype(F32) * kd_ref[hd]).astype(BF16)
        state[hd] = st * cd_ref[hd] + _dot_tn(kdec, v)
        o = _rms(o) * gn_ref[:, hd * RET_DV:(hd + 1) * RET_DV]
        g = g_ref[0, :, hd * RET_DV:(hd + 1) * RET_DV].astype(F32)
        ybuf[:, hd * RET_DV:(hd + 1) * RET_DV] = (_silu(g) * o).astype(BF16)

    y = _dot(ybuf[...], wo_ref[...])
    xn = x_ref[0] + ada_ref[0, 2:3, :] * y
    xo_ref[0] = xn
    h = _ffn_prenorm_logits(xn, ngain_ref[...], ada_ref[0, 3:4, :], ada_ref[0, 4:5, :],
                            wr_hi_ref, wr_lo_ref, br_ref, lt_ref)
    _store_row_tiles(h_ref, (0,), h)


def _ret_core(proj, x, ada, gn, w_out, ngain, router):
    bsz, seq, d = x.shape
    c = RET_CHUNK
    f32 = F32
    log_gamma = jnp.log1p(-jnp.exp2(-5.0 - jnp.arange(RET_HEADS, dtype=f32)))
    n = jnp.arange(c, dtype=f32)
    rel = n[:, None] - n[None, :]
    intra = jnp.where(rel >= 0, jnp.exp(jnp.maximum(rel, 0.0)[None] * log_gamma[:, None, None]), 0.0)
    qd = jnp.exp((n + 1.0)[None, :] * log_gamma[:, None])[:, :, None]
    kd = jnp.exp((c - 1.0 - n)[None, :] * log_gamma[:, None])[:, :, None]
    cd = jnp.broadcast_to(jnp.exp(c * log_gamma)[:, None, None], (RET_HEADS, 1, RET_DV))
    const = lambda *shape: pl.BlockSpec(shape, lambda b, i: (0,) * len(shape))
    tok = lambda w, j: pl.BlockSpec((1, c, w), lambda b, i: (b, i, j))
    nc = seq // c
    return pl.pallas_call(
        _ret_core_kernel,
        out_shape=(jax.ShapeDtypeStruct((bsz, seq, d), f32),
                   jax.ShapeDtypeStruct((bsz, seq * ROW_TILES, LANE), f32),
                   jax.ShapeDtypeStruct((ROUTER_ROWS, bsz * seq), f32)),
        grid=(bsz, nc),
        in_specs=[
            tok(RET_QK, 0), tok(RET_QK, 1), tok(RET_V, 1), tok(RET_V, 2),
            tok(d, 0),
            pl.BlockSpec((1, 6, d), lambda b, i: (b, 0, 0)),
            const(RET_HEADS, c, c), const(RET_HEADS, c, 1), const(RET_HEADS, c, 1),
            const(RET_HEADS, 1, RET_DV),
            const(1, RET_V), const(RET_V, d), const(1, d),
        ] + _logit_specs(d),
        out_specs=(tok(d, 0), pl.BlockSpec((1, c * ROW_TILES, LANE), lambda b, i: (b, i, 0)),
                   pl.BlockSpec((ROUTER_ROWS, c), lambda b, i: (0, b * nc + i))),
        scratch_shapes=[pltpu.VMEM((RET_HEADS, RET_DK, RET_DV), f32),
                        pltpu.VMEM((c, RET_V), BF16)],
        compiler_params=_params("arbitrary", "arbitrary"),
        name="ret_core",
    )(proj, proj, proj, proj, x, ada, intra, qd, kd, cd, gn, w_out, ngain, *router)


def _dispatch_kernel(d1_ref, d2_ref, h_ref, xs_hbm, sem):
    tm = h_ref.shape[0] // ROW_TILES

    def body(t, carry):
        for d_ref in (d1_ref, d2_ref):
            pltpu.make_async_copy(_row_tile(h_ref, (), t), _row_tile(xs_hbm, (), d_ref[0, 0, t]), sem).start()
        return carry
    lax.fori_loop(0, tm, body, 0, unroll=8)
    for _ in range(TOP_K):
        pltpu.make_async_copy(h_ref, xs_hbm.at[pl.ds(0, tm * ROW_TILES), :], sem).wait()


def _dispatch(h, dest1, dest2):
    n_tok = h.shape[0] // ROW_TILES
    tm = ROW_TILE
    nt = n_tok // tm
    idx = lambda: pl.BlockSpec((1, 1, tm), lambda i: (i, 0, 0), memory_space=pltpu.SMEM)
    return pl.pallas_call(
        _dispatch_kernel,
        out_shape=jax.ShapeDtypeStruct((TOP_K * n_tok * ROW_TILES, LANE), F32),
        grid=(nt,),
        in_specs=[idx(), idx(), pl.BlockSpec((tm * ROW_TILES, LANE), lambda i: (i, 0))],
        out_specs=pl.BlockSpec(memory_space=pl.ANY),
        scratch_shapes=[pltpu.SemaphoreType.DMA(())],
        compiler_params=_params("arbitrary"),
        name="dispatch",
    )(dest1.reshape(nt, 1, tm), dest2.reshape(nt, 1, tm), h)


def _moe_kernel(blk_ref, exp_ref, nitem_ref, start_ref, cnt_ref, x_ref, w1_ref, w3_ref, w2_ref, o_ref,
                w1b, w3b, w2b):
    w = pl.program_id(0)
    bm = x_ref.shape[0] // ROW_TILES

    @pl.when(w < nitem_ref[0])
    def _():
        e = exp_ref[w]
        blk = blk_ref[w]
        prev = jnp.maximum(w - 1, 0)

        @pl.when(jnp.logical_or(w == 0, exp_ref[prev] != e))
        def _():
            w1b[...] = w1_ref[0, 0].astype(BF16)
            w3b[...] = w3_ref[0, 0].astype(BF16)
            w2b[...] = w2_ref[0, 0].astype(BF16)

        x = _load_row_tiles(x_ref, (), bm).astype(BF16)
        a = _dot(x, w1b[...])
        b = _dot(x, w3b[...])
        y = _dot((_silu(a) * b).astype(BF16), w2b[...])
        row = blk * bm + lax.broadcasted_iota(jnp.int32, (bm, 1), 0)
        mine = jnp.logical_and(row >= start_ref[e], row < start_ref[e] + cnt_ref[e])
        first_visit = jnp.logical_or(w == 0, blk_ref[prev] != blk)

        @pl.when(first_visit)
        def _():
            _store_row_tiles(o_ref, (), jnp.where(mine, y, 0.0))

        @pl.when(jnp.logical_not(first_visit))
        def _():
            _store_row_tiles(o_ref, (), jnp.where(mine, y, _load_row_tiles(o_ref, (), bm)))


def _moe(xs, counts, w1, w3, w2, layer):
    m = xs.shape[0] // ROW_TILES
    d = w1.shape[-1]
    bm = MOE_BM
    n_blocks = m // bm
    n_items = n_blocks + N_EXPERTS - 1
    ends = jnp.cumsum(counts)
    starts = ends - counts
    first_blk = starts // bm
    last_blk = jnp.maximum(ends - 1, starts) // bm
    visits = jnp.where(counts > 0, last_blk - first_blk + 1, 0)
    item_end = jnp.cumsum(visits)
    item_start = item_end - visits
    n_item = item_end[-1]
    w = jnp.minimum(jnp.arange(n_items, dtype=jnp.int32), n_item - 1)
    item_e = jnp.sum(w[:, None] >= item_end[None, :], axis=1).astype(jnp.int32)
    item_blk = (first_blk[item_e] + w - item_start[item_e]).astype(jnp.int32)
    wspec = lambda: pl.BlockSpec((1, 1, d, d), lambda i, blk, e, *_: (layer, e[i], 0, 0))
    rows = lambda: pl.BlockSpec((bm * ROW_TILES, LANE), lambda i, blk, *_: (blk[i], 0))
    return pl.pallas_call(
        _moe_kernel,
        out_shape=jax.ShapeDtypeStruct((m * ROW_TILES, LANE), F32),
        grid_spec=pltpu.PrefetchScalarGridSpec(
            num_scalar_prefetch=5,
            grid=(n_items,),
            in_specs=[rows(), wspec(), wspec(), wspec()],
            out_specs=rows(),
            scratch_shapes=[pltpu.VMEM((d, d), BF16)] * 3,
        ),
        compiler_params=_params("arbitrary"),
        name="moe",
    )(item_blk, item_e, n_item.astype(jnp.int32).reshape(1), starts.astype(jnp.int32), counts.astype(jnp.int32),
      xs, w1, w3, w2)


class _RowGather:
    def __init__(self, ys_hbm, buf, sem, tm):
        self.ys, self.buf, self.sem, self.tm = ys_hbm, buf, sem, tm

    def start(self, d_refs, slot):
        def body(t, carry):
            for k, d_ref in enumerate(d_refs):
                pltpu.make_async_copy(_row_tile(self.ys, (), d_ref[0, 0, t]), _row_tile(self.buf, (slot, k), t),
                                      self.sem.at[slot]).start()
            return carry
        lax.fori_loop(0, self.tm, body, 0, unroll=8)

    def wait(self, slot):
        for k in range(TOP_K):
            pltpu.make_async_copy(self.ys.at[pl.ds(0, self.tm * ROW_TILES), :], self.buf.at[slot, k],
                                  self.sem.at[slot]).wait()

    def combined(self, step, n_steps, cur_refs, nxt_refs, w1_ref, w2_ref):
        slot = step % 2

        @pl.when(step == 0)
        def _():
            self.start(cur_refs, 0)

        @pl.when(step + 1 < n_steps)
        def _():
            self.start(nxt_refs, 1 - slot)

        self.wait(slot)
        return (w1_ref[...] * _load_row_tiles(self.buf, (slot, 0), self.tm)
                + w2_ref[...] * _load_row_tiles(self.buf, (slot, 1), self.tm))


def _gather_specs(tm, nt_total, step_of):
    cur = lambda: pl.BlockSpec((1, 1, tm), lambda *g: (step_of(*g), 0, 0), memory_space=pltpu.SMEM)
    nxt = lambda: pl.BlockSpec((1, 1, tm), lambda *g: (jnp.minimum(step_of(*g) + 1, nt_total - 1), 0, 0),
                               memory_space=pltpu.SMEM)
    wcol = lambda: pl.BlockSpec((tm, 1), lambda *g: (step_of(*g), 0))
    return [cur(), cur(), nxt(), nxt(), wcol(), wcol()]


def _gather_scratch(tm):
    return [pltpu.VMEM((2, TOP_K, tm * ROW_TILES, LANE), F32), pltpu.SemaphoreType.DMA((2,))]


def _group_norm64(t, ind_ref, indt_ref):
    ss = _dot((t * t).astype(BF16), ind_ref[...])
    r = lax.rsqrt(ss * (1.0 / DIFF_HEAD_DIM) + EPS)
    r_hi = r.astype(BF16)
    r_lo = (r - r_hi.astype(F32)).astype(BF16)
    rb = _dot(jnp.concatenate([r_hi, r_lo], axis=1), indt_ref[...])
    return t * rb


def _kvq_kernel(d1c, d2c, d1n, d2n, w1_ref, w2_ref, ys_hbm, x_ref, ada0_ref, ada1_ref, kvada_ref, kvgain_ref,
                qgain_ref, knorm_ref, qnorm_ref, wk_ref, wvt_ref, wq_ref, ind_ref, indt_ref,
                xo_ref, q_ref, k_ref, vt_ref, gbuf, gsem):
    tm = x_ref.shape[1]
    step = pl.program_id(0) * pl.num_programs(1) + pl.program_id(1)
    n_steps = pl.num_programs(0) * pl.num_programs(1)
    moe = _RowGather(ys_hbm, gbuf, gsem, tm).combined(step, n_steps, (d1c, d2c), (d1n, d2n), w1_ref, w2_ref)
    xn = x_ref[0] + ada0_ref[0, 5:6, :] * moe
    xo_ref[0] = xn
    r = _rms(xn)
    hk = ((r * kvgain_ref[...]) * (1.0 + kvada_ref[0, 1:2, :]) + kvada_ref[0, 0:1, :]).astype(BF16)
    hq = ((r * qgain_ref[...]) * (1.0 + ada1_ref[0, 1:2, :]) + ada1_ref[0, 0:1, :]).astype(BF16)
    hd2 = 2 * DIFF_HEAD_DIM
    kk = _dot(hk, wk_ref[...])
    kn = (_group_norm64(kk, ind_ref, indt_ref) * knorm_ref[...]).astype(BF16)
    qq = _dot(hq, wq_ref[...])
    qn = (_group_norm64(qq, ind_ref, indt_ref) * qnorm_ref[...]).astype(BF16)
    for hd in range(DIFF_HEADS):
        k_ref[0, hd] = kn[:, hd * hd2:(hd + 1) * hd2]
        q_ref[0, hd] = qn[:, hd * hd2:(hd + 1) * hd2]
    vt = _dot_nt(wvt_ref[...], hk).astype(BF16)
    for j in range(tm // ATT_TK):
        vt_ref[0, j] = vt[:, j * ATT_TK:(j + 1) * ATT_TK].reshape(DIFF_HEADS, hd2, ATT_TK)


def _kvq(x, ys, dest1, dest2, w1, w2, ada0, ada1, kvada, kv_gain, q_gain, k_norm, q_norm, w_k, w_vt, w_q):
    bsz, seq, d = x.shape
    tm = ROW_TILE
    nt = seq // tm
    nt_total = bsz * nt
    lane_group = jnp.arange(d, dtype=jnp.int32) // DIFF_HEAD_DIM
    ind = (lane_group[:, None] == jnp.arange(LANE, dtype=jnp.int32)[None, :]).astype(BF16)
    indt = jnp.concatenate([ind.T, ind.T], axis=0)
    reps = d // DIFF_HEAD_DIM
    q_scale = jnp.tile(q_norm, reps)[None, :] * (DIFF_HEAD_DIM ** -0.5 * math.log2(math.e))
    const = lambda *shape: pl.BlockSpec(shape, lambda b, i: (0,) * len(shape))
    tok = lambda w: pl.BlockSpec((1, tm, w), lambda b, i: (b, i, 0))
    hd2 = 2 * DIFF_HEAD_DIM
    heads = pl.BlockSpec((1, DIFF_HEADS, tm, hd2), lambda b, i: (b, 0, i, 0))
    ada_spec = lambda rows: pl.BlockSpec((1, rows, d), lambda b, i: (b, 0, 0))
    col = lambda a: a.reshape(-1, 1)
    idx = lambda a: a.reshape(nt_total, 1, tm)
    return pl.pallas_call(
        _kvq_kernel,
        out_shape=(jax.ShapeDtypeStruct((bsz, seq, d), F32),
                   jax.ShapeDtypeStruct((bsz, DIFF_HEADS, seq, hd2), BF16),
                   jax.ShapeDtypeStruct((bsz, DIFF_HEADS, seq, hd2), BF16),
                   jax.ShapeDtypeStruct((bsz, seq // ATT_TK, DIFF_HEADS, hd2, ATT_TK), BF16)),
        grid=(bsz, nt),
        in_specs=_gather_specs(tm, nt_total, lambda b, i: b * nt + i) + [
            pl.BlockSpec(memory_space=pl.ANY),
            tok(d),
            ada_spec(6), ada_spec(6), ada_spec(2),
            const(1, d), const(1, d), const(1, d), const(1, d),
            const(d, d), const(d, d), const(d, d), const(d, LANE), const(2 * LANE, d),
        ],
        out_specs=(tok(d), heads, heads,
                   pl.BlockSpec((1, tm // ATT_TK, DIFF_HEADS, hd2, ATT_TK), lambda b, i: (b, i, 0, 0, 0))),
        scratch_shapes=_gather_scratch(tm),
        compiler_params=_params("arbitrary", "arbitrary"),
        name="kvq",
    )(idx(dest1), idx(dest2), idx(dest1), idx(dest2), col(w1), col(w2), ys, x, ada0, ada1, kvada,
      kv_gain, q_gain, jnp.tile(k_norm, reps)[None, :], q_scale, w_k, w_vt, w_q, ind, indt)


def _attn_kernel(lambda_init, q_ref, k_ref, vt_ref, x_ref, ada_ref, lam_ref, subln_ref, wo_ref, ngain_ref,
                 wr_hi_ref, wr_lo_ref, br_ref, xo_ref, h_ref, lt_ref,
                 s_a, s_b, p_a, p_b, acc, obuf):
    i = pl.program_id(1)
    tq, tk = ATT_TQ, ATT_TK
    hd2 = 2 * DIFF_HEAD_DIM
    n_kblk = k_ref.shape[2] // tk
    lam = lam_ref[...]
    lam_full = (jnp.exp(jnp.sum(lam[0:1] * lam[1:2], axis=-1, keepdims=True))
                - jnp.exp(jnp.sum(lam[2:3] * lam[3:4], axis=-1, keepdims=True)) + lambda_init)
    lane = lax.broadcasted_iota(jnp.int32, (1, hd2), 1)
    keep = (jnp.where(lane < DIFF_HEAD_DIM, 1.0, 0.0).astype(BF16),
            jnp.where(lane >= DIFF_HEAD_DIM, 1.0, 0.0).astype(BF16))
    key = lax.broadcasted_iota(jnp.int32, (tk, tq), 0)
    qry = lax.broadcasted_iota(jnp.int32, (1, tq), 1)
    key_limit = ((i * tq + qry) // MASK_CHUNK + 1) * MASK_CHUNK
    neg_inf = jnp.float32(-jnp.inf)
    n_pairs = (i + 2) // 2
    n_plain = jnp.maximum((i - 1) // 2, 0)

    streams = [(u, t) for u in range(ATT_HEADS) for t in range(2)]

    def head_group(hg, _):
        hds = [hg * ATT_HEADS + u for u in range(ATT_HEADS)]
        qs = []
        for hd in hds:
            qh = q_ref[0, hd]
            qs.append((qh * keep[0], qh * keep[1]))

        def produce_scores(jb, s_out, masked):
            start = pl.multiple_of(jnp.minimum(jb, n_kblk - 1) * tk, tk)
            visible = key < key_limit - jb * tk
            maxima = []
            for n, (u, t) in enumerate(streams):
                st = _dot_nt(k_ref[0, hds[u], pl.ds(start, tk), :], qs[u][t])
                if masked:
                    st = jnp.where(visible, st, neg_inf)
                s_out[n] = st
                maxima.append(jnp.max(st, axis=0, keepdims=True))
            return tuple(maxima)

        def step(j, s_in, s_out, p_in, p_out, state, masked):
            stats, blk_max = state
            jv = jnp.clip(j - 1, 0, n_kblk - 1)
            next_max = produce_scores(j + 1, s_out, masked)
            new_stats = []
            for n, (u, t) in enumerate(streams):
                m, l = stats[n]
                m_new = jnp.maximum(m, blk_max[n])
                alpha = jnp.exp2(m - m_new)
                p = jnp.exp2(s_in[n] - m_new)
                p_out[n] = p.astype(BF16)
                new_stats.append((m_new, alpha * l + jnp.sum(p, axis=0, keepdims=True)))
                acc[n] = alpha * (acc[n] + _dot(vt_ref[0, jv, hds[u]], p_in[n]))
            return tuple(new_stats), next_max

        p_b[...] = jnp.zeros_like(p_b)
        acc[...] = jnp.zeros_like(acc)
        stat0 = (jnp.full((1, tq), neg_inf, F32), jnp.zeros((1, tq), F32))
        state = ((stat0,) * len(streams), produce_scores(0, s_a, True))

        def pair(masked, jj, state):
            state = step(2 * jj, s_a, s_b, p_b, p_a, state, masked)
            return step(2 * jj + 1, s_b, s_a, p_a, p_b, state, masked)

        state = lax.fori_loop(0, n_plain, functools.partial(pair, False), state)
        (stats, _) = lax.fori_loop(n_plain, n_pairs, functools.partial(pair, True), state)
        jv = jnp.minimum(2 * n_pairs - 1, n_kblk - 1)
        for u, hd in enumerate(hds):
            vtb = vt_ref[0, jv, hd]
            a1 = acc[2 * u] + _dot(vtb, p_b[2 * u])
            a2 = acc[2 * u + 1] + _dot(vtb, p_b[2 * u + 1])
            o = a1 * (1.0 / stats[2 * u][1]) - lam_full * (a2 * (1.0 / stats[2 * u + 1][1]))
            o = o * lax.rsqrt(jnp.mean(o * o, axis=0, keepdims=True) + EPS)
            obuf[hd] = (o * (subln_ref[...] * (1.0 - lambda_init))).astype(BF16)
        return 0

    lax.fori_loop(0, DIFF_HEADS // ATT_HEADS, head_group, 0)

    y = _dot_tn(obuf[...].reshape(DIFF_HEADS * hd2, tq), wo_ref[...])
    xn = x_ref[0] + ada_ref[0, 2:3, :] * y
    xo_ref[0] = xn
    h = _ffn_prenorm_logits(xn, ngain_ref[...], ada_ref[0, 3:4, :], ada_ref[0, 4:5, :],
                            wr_hi_ref, wr_lo_ref, br_ref, lt_ref)
    _store_row_tiles(h_ref, (0,), h)


def _attn(q, k, vt, x, ada, lam, subln, w_out, ngain, router, lambda_init):
    bsz, seq, d = x.shape
    tq, tk = ATT_TQ, ATT_TK
    hd2 = 2 * DIFF_HEAD_DIM
    const = lambda *shape: pl.BlockSpec(shape, lambda b, i: (0,) * len(shape))
    tok = lambda w: pl.BlockSpec((1, tq, w), lambda b, i: (b, i, 0))
    nq = seq // tq
    ns = 2 * ATT_HEADS
    return pl.pallas_call(
        functools.partial(_attn_kernel, lambda_init),
        out_shape=(jax.ShapeDtypeStruct((bsz, seq, d), F32),
                   jax.ShapeDtypeStruct((bsz, seq * ROW_TILES, LANE), F32),
                   jax.ShapeDtypeStruct((ROUTER_ROWS, bsz * seq), F32)),
        grid=(bsz, nq),
        in_specs=[
            pl.BlockSpec((1, DIFF_HEADS, tq, hd2), lambda b, i: (b, 0, i, 0)),
            pl.BlockSpec((1, DIFF_HEADS, seq, hd2), lambda b, i: (b, 0, 0, 0)),
            pl.BlockSpec((1, seq // tk, DIFF_HEADS, hd2, tk), lambda b, i: (b, 0, 0, 0, 0)),
            tok(d),
            pl.BlockSpec((1, 6, d), lambda b, i: (b, 0, 0)),
            const(4, DIFF_HEAD_DIM), const(hd2, 1), const(d, d), const(1, d),
        ] + _logit_specs(d),
        out_specs=(tok(d), pl.BlockSpec((1, tq * ROW_TILES, LANE), lambda b, i: (b, i, 0)),
                   pl.BlockSpec((ROUTER_ROWS, tq), lambda b, i: (0, b * nq + i))),
        scratch_shapes=[pltpu.VMEM((ns, tk, tq), F32), pltpu.VMEM((ns, tk, tq), F32),
                        pltpu.VMEM((ns, tk, tq), BF16), pltpu.VMEM((ns, tk, tq), BF16),
                        pltpu.VMEM((ns, hd2, tq), F32),
                        pltpu.VMEM((DIFF_HEADS, hd2, tq), BF16)],
        compiler_params=_params("arbitrary", "arbitrary"),
        name="attn",
    )(q, k, vt, x, ada, lam, subln, w_out, ngain, *router)


def _combine_kernel(d1c, d2c, d1n, d2n, w1_ref, w2_ref, ys_hbm, x_ref, ada_ref, o_ref, gbuf, gsem):
    tm = x_ref.shape[1]
    step = pl.program_id(0) * pl.num_programs(1) + pl.program_id(1)
    n_steps = pl.num_programs(0) * pl.num_programs(1)
    moe = _RowGather(ys_hbm, gbuf, gsem, tm).combined(step, n_steps, (d1c, d2c), (d1n, d2n), w1_ref, w2_ref)
    o_ref[0] = x_ref[0] + ada_ref[0, 5:6, :] * moe


def _combine(x, ys, dest1, dest2, w1, w2, ada):
    bsz, seq, d = x.shape
    tm = ROW_TILE
    nt = seq // tm
    nt_total = bsz * nt
    col = lambda a: a.reshape(-1, 1)
    idx = lambda a: a.reshape(nt_total, 1, tm)
    return pl.pallas_call(
        _combine_kernel,
        out_shape=jax.ShapeDtypeStruct((bsz, seq, d), F32),
        grid=(bsz, nt),
        in_specs=_gather_specs(tm, nt_total, lambda b, i: b * nt + i) + [
            pl.BlockSpec(memory_space=pl.ANY),
            pl.BlockSpec((1, tm, d), lambda b, i: (b, i, 0)),
            pl.BlockSpec((1, 6, d), lambda b, i: (b, 0, 0)),
        ],
        out_specs=pl.BlockSpec((1, tm, d), lambda b, i: (b, i, 0)),
        scratch_shapes=_gather_scratch(tm),
        compiler_params=_params("arbitrary", "arbitrary"),
        name="combine",
    )(idx(dest1), idx(dest2), idx(dest1), idx(dest2), col(w1), col(w2), ys, x, ada)


def _router_weights(w_group, b_group, w_expert, b_expert):
    d = w_group.shape[0]
    pad = ROUTER_ROWS - N_EXPERTS - N_GROUPS
    w = jnp.concatenate([w_expert.T, w_group.T, jnp.zeros((pad, d), F32)], axis=0)
    b = jnp.concatenate([b_expert, b_group, jnp.zeros((pad,), F32)])[:, None]
    w_hi = w.astype(BF16)
    w_lo = (w - w_hi.astype(F32)).astype(BF16)
    return w_hi, w_lo, b


def _routing_tables(route, cnt):
    rec = lambda r: route[r]
    counts = cnt[:, 0].astype(jnp.int32)
    starts = jnp.cumsum(counts) - counts
    e1 = rec(0).astype(jnp.int32)
    e2 = rec(1).astype(jnp.int32)
    dest1 = starts[e1] + rec(2).astype(jnp.int32)
    dest2 = starts[e2] + rec(3).astype(jnp.int32)
    return dest1, dest2, rec(4), rec(5), counts


def kernel(x, c, ada_w, ada_b, norm_mix, norm_ffn, ret_w_in, ret_gn, ret_w_out, kv_ada_w, kv_ada_b, kv_norm, kv_w, k_norm, diff_w_q, q_norm, diff_lam, diff_subln, diff_w_out, moe_w_group, moe_b_group, moe_w_expert, moe_b_expert, moe_w1, moe_w3, moe_w2):
    bsz, seq, d = x.shape
    n_tok = bsz * seq

    ada = _ada(c, ada_w, ada_b).reshape(2, bsz, 6, d)
    kvada = _ada(c, kv_ada_w[None], kv_ada_b[None]).reshape(bsz, 2, d)

    half = RET_DK // 2
    inv_freq = 1.0 / (ROPE_BASE ** (jnp.arange(half, dtype=F32) / half))
    ang = jnp.arange(seq, dtype=F32)[:, None] * inv_freq[None, :]
    cos, sin = jnp.cos(ang), jnp.sin(ang)

    routers = [_router_weights(moe_w_group[l], moe_b_group[l], moe_w_expert[l], moe_b_expert[l])
               for l in range(2)]

    def moe_layer(h, logits_t, layer):
        dest1, dest2, w1, w2, counts = _routing_tables(*_route(logits_t))
        xs = _dispatch(h.reshape(n_tok * ROW_TILES, LANE), dest1, dest2)
        ys = _moe(xs, counts, moe_w1, moe_w3, moe_w2, layer)
        return ys, dest1, dest2, w1, w2

    proj = _ret_in(x, ada[0], norm_mix[0][None, :], cos, sin, ret_w_in[0].astype(BF16))
    x1, h, logits_t = _ret_core(proj, x, ada[0], ret_gn[0][None, :], ret_w_out[0].astype(BF16),
                                norm_ffn[0][None, :], routers[0])
    moe0 = moe_layer(h, logits_t, 0)

    x2, q, k, vt = _kvq(x1, *moe0, ada[0], ada[1], kvada, kv_norm[None, :], norm_mix[1][None, :], k_norm, q_norm[0],
                        kv_w[:, :d].astype(BF16), kv_w[:, d:].T.astype(BF16), diff_w_q[0].astype(BF16))
    lambda_init = 0.8 - 0.6 * math.exp(-0.3 * 1)
    x3, h, logits_t = _attn(q, k, vt, x2, ada[1], diff_lam[0], diff_subln[0][:, None], diff_w_out[0].astype(BF16),
                            norm_ffn[1][None, :], routers[1], lambda_init)
    moe1 = moe_layer(h, logits_t, 1)
    return _combine(x3, *moe1, ada[1])
```

```python
import functools
import math

import jax
import jax.numpy as jnp
from jax import lax
from jax.experimental import pallas as pl
from jax.experimental.pallas import tpu as pltpu

F32 = jnp.float32
BF16 = jnp.bfloat16

D_MODEL = 1024
EPS = 1e-6

RET_HEADS = 4
RET_DK = D_MODEL // RET_HEADS
RET_DV = 2 * RET_DK
RET_QK = RET_HEADS * RET_DK
RET_V = RET_HEADS * RET_DV
RET_IN = 2 * RET_QK + 2 * RET_V
ROPE_BASE = 10000.0
RET_CHUNK = 256

DIFF_HEAD_DIM = 64
DIFF_HEADS = D_MODEL // (2 * DIFF_HEAD_DIM)
MASK_CHUNK = 64
ATT_TQ = 256
ATT_TK = 256
ATT_HEADS = 1

N_GROUPS = 4
EXPERTS_PER_GROUP = 4
N_EXPERTS = N_GROUPS * EXPERTS_PER_GROUP
TOP_K = 2
TOP_K_SHIFT = 1
MOE_BM = 256
ROUTER_ROWS = 32
ROUTE_ROWS = 8
ROW_TILE = 512
ROUTE_TILE = 2048
ROUTE_SEG = 256

LANE = 128
SUBLANE = 8
VMEM_LIMIT = 56 * 1024 * 1024


def _dot(a, b):
    return jnp.dot(a, b, preferred_element_type=F32)


def _dot_nt(a, b):
    return lax.dot_general(a, b, (((1,), (1,)), ((), ())), preferred_element_type=F32)


def _dot_tn(a, b):
    return lax.dot_general(a, b, (((0,), (0,)), ((), ())), preferred_element_type=F32)


def _silu(x):
    return x * (1.0 / (1.0 + jnp.exp(-x)))


def _rms(x):
    return x * lax.rsqrt(jnp.mean(x * x, axis=-1, keepdims=True) + EPS)


def _params(*sem, flags=None):
    return pltpu.CompilerParams(dimension_semantics=sem, vmem_limit_bytes=VMEM_LIMIT, flags=flags)


ROW_TILES = D_MODEL // LANE


def _load_row_tiles(ref, lead, rows):
    return jnp.concatenate(
        [ref[lead + (pl.ds(c, rows, stride=ROW_TILES), slice(None))] for c in range(ROW_TILES)], axis=1)


def _store_row_tiles(ref, lead, val):
    rows = val.shape[0]
    for c in range(ROW_TILES):
        ref[lead + (pl.ds(c, rows, stride=ROW_TILES), slice(None))] = val[:, c * LANE:(c + 1) * LANE]


def _row_tile(ref, lead, r):
    return ref.at[lead + (pl.ds(pl.multiple_of(r * ROW_TILES, ROW_TILES), ROW_TILES), slice(None))]


def _ada_kernel(c_ref, w_ref, b_ref, o_ref):
    ca = _silu(c_ref[...])
    o_ref[0] = jnp.dot(ca, w_ref[0], preferred_element_type=F32,
                       precision=lax.Precision.HIGHEST) + b_ref[0]


def _ada(c, w, b):
    n_l, d, n = w.shape
    bsz = c.shape[0]
    tn = 1024
    return pl.pallas_call(
        _ada_kernel,
        out_shape=jax.ShapeDtypeStruct((n_l, bsz, n), F32),
        grid=(n_l, n // tn),
        in_specs=[
            pl.BlockSpec((bsz, d), lambda l, j: (0, 0)),
            pl.BlockSpec((1, d, tn), lambda l, j: (l, 0, j)),
            pl.BlockSpec((1, 1, tn), lambda l, j: (l, 0, j)),
        ],
        out_specs=pl.BlockSpec((1, bsz, tn), lambda l, j: (l, 0, j)),
        compiler_params=_params("arbitrary", "arbitrary"),
        name="ada",
    )(c, w, b.reshape(n_l, 1, n))


def _ffn_prenorm_logits(xn, gain, shift, scale, wr_hi_ref, wr_lo_ref, br_ref, lt_ref):
    h = _rms(xn) * gain
    h = h * (1.0 + scale) + shift
    h_hi = h.astype(BF16)
    h_lo = (h - h_hi.astype(F32)).astype(BF16)
    w_hi = wr_hi_ref[...]
    lt_ref[...] = _dot_nt(w_hi, h_hi) + _dot_nt(wr_lo_ref[...], h_hi) + _dot_nt(w_hi, h_lo) + br_ref[...]
    return h


def _route_kernel(lt_ref, tri_ref, route_ref, cnt_ref, carry):
    lt = lt_ref[...]
    c = lt.shape[1]
    seg = tri_ref.shape[0]
    neg_inf = jnp.float32(-jnp.inf)

    grow = lax.broadcasted_iota(jnp.int32, (SUBLANE, c), 0).astype(F32)
    g = jnp.where(grow < N_GROUPS, lt[N_EXPERTS:N_EXPERTS + SUBLANE], neg_inf)
    gmax = jnp.max(g, axis=0, keepdims=True)
    gidx = jnp.min(jnp.where(g == gmax, grow, float(SUBLANE)), axis=0, keepdims=True)
    gate = 1.0 / jnp.sum(jnp.exp(g - gmax), axis=0, keepdims=True)

    erow_i = lax.broadcasted_iota(jnp.int32, (N_EXPERTS, c), 0)
    erow = erow_i.astype(F32)
    egroup = (erow_i // EXPERTS_PER_GROUP).astype(F32)
    el = jnp.where(egroup == gidx, lt[0:N_EXPERTS], neg_inf)
    m1 = jnp.max(el, axis=0, keepdims=True)
    i1 = jnp.min(jnp.where(el == m1, erow, float(N_EXPERTS)), axis=0, keepdims=True)
    el2 = jnp.where(erow == i1, neg_inf, el)
    m2 = jnp.max(el2, axis=0, keepdims=True)
    i2 = jnp.min(jnp.where(el2 == m2, erow, float(N_EXPERTS)), axis=0, keepdims=True)
    t = jnp.exp(m2 - m1)
    den = 1.0 / (1.0 + t)
    w1 = gate * den
    w2 = gate * t * den

    @pl.when(pl.program_id(0) == 0)
    def _():
        carry[...] = jnp.zeros_like(carry)

    oh1 = erow == i1
    oh2 = erow == i2
    oh = jnp.where(jnp.logical_or(oh1, oh2), 1.0, 0.0)
    offset = carry[...]
    before = []
    for s in range(c // seg):
        oh_s = oh[:, s * seg:(s + 1) * seg]
        before.append(offset + _dot(oh_s.astype(BF16), tri_ref[...]))
        offset = offset + jnp.sum(oh_s, axis=1, keepdims=True)
    before = jnp.concatenate(before, axis=1)
    rank1 = jnp.sum(jnp.where(oh1, before, 0.0), axis=0, keepdims=True)
    rank2 = jnp.sum(jnp.where(oh2, before, 0.0), axis=0, keepdims=True)
    carry[...] = offset
    cnt_ref[...] = jnp.broadcast_to(offset, cnt_ref.shape)

    rrow = lax.broadcasted_iota(jnp.int32, (ROUTE_ROWS, c), 0)
    rec = jnp.zeros((ROUTE_ROWS, c), F32)
    for idx, val in enumerate((i1, i2, rank1, rank2, w1, w2)):
        rec = jnp.where(rrow == idx, val, rec)
    route_ref[...] = rec


def _route(lt):
    n_tok = lt.shape[1]
    tile = min(ROUTE_TILE, n_tok)
    t = jnp.arange(ROUTE_SEG, dtype=jnp.int32)
    earlier = (t[:, None] < t[None, :]).astype(BF16)
    return pl.pallas_call(
        _route_kernel,
        out_shape=(jax.ShapeDtypeStruct((ROUTE_ROWS, n_tok), F32),
                   jax.ShapeDtypeStruct((N_EXPERTS, LANE), F32)),
        grid=(n_tok // tile,),
        in_specs=[pl.BlockSpec((ROUTER_ROWS, tile), lambda i: (0, i)),
                  pl.BlockSpec((ROUTE_SEG, ROUTE_SEG), lambda i: (0, 0))],
        out_specs=(pl.BlockSpec((ROUTE_ROWS, tile), lambda i: (0, i)),
                   pl.BlockSpec((N_EXPERTS, LANE), lambda i: (0, 0))),
        scratch_shapes=[pltpu.VMEM((N_EXPERTS, 1), F32)],
        compiler_params=_params("arbitrary"),
        name="route",
    )(lt, earlier)


def _logit_specs(d):
    const = lambda *shape: pl.BlockSpec(shape, lambda b, i: (0,) * len(shape))
    return [const(ROUTER_ROWS, d), const(ROUTER_ROWS, d), const(ROUTER_ROWS, 1)]


def _ret_in_kernel(x_ref, ada_ref, gain_ref, cos_ref, sin_ref, w_ref, o_ref):
    x = x_ref[0]
    shift = ada_ref[0, 0:1, :]
    scale = ada_ref[0, 1:2, :]
    h = (_rms(x) * gain_ref[...]) * (1.0 + scale) + shift
    hb = h.astype(BF16)
    cos = cos_ref[...]
    sin = sin_ref[...]
    half = RET_DK // 2
    for j in range(RET_IN // D_MODEL):
        p = _dot(hb, w_ref[:, j * D_MODEL:(j + 1) * D_MODEL])
        if j < 2:
            post = 1.0 if j == 0 else RET_DK ** -0.5
            for hd in range(RET_HEADS):
                lo = hd * RET_DK
                x1 = p[:, lo:lo + half]
                x2 = p[:, lo + half:lo + RET_DK]
                o_ref[0, :, j * D_MODEL + lo:j * D_MODEL + lo + half] = (
                    (x1 * cos - x2 * sin) * post).astype(BF16)
                o_ref[0, :, j * D_MODEL + lo + half:j * D_MODEL + lo + RET_DK] = (
                    (x2 * cos + x1 * sin) * post).astype(BF16)
        else:
            o_ref[0, :, j * D_MODEL:(j + 1) * D_MODEL] = p.astype(BF16)


def _ret_in(x, ada, gain, cos, sin, w_in):
    bsz, seq, d = x.shape
    tm = ROW_TILE
    return pl.pallas_call(
        _ret_in_kernel,
        out_shape=jax.ShapeDtypeStruct((bsz, seq, RET_IN), BF16),
        grid=(bsz, seq // tm),
        in_specs=[
            pl.BlockSpec((1, tm, d), lambda b, i: (b, i, 0)),
            pl.BlockSpec((1, 6, d), lambda b, i: (b, 0, 0)),
            pl.BlockSpec((1, d), lambda b, i: (0, 0)),
            pl.BlockSpec((tm, RET_DK // 2), lambda b, i: (i, 0)),
            pl.BlockSpec((tm, RET_DK // 2), lambda b, i: (i, 0)),
            pl.BlockSpec((d, RET_IN), lambda b, i: (0, 0), pipeline_mode=pl.Buffered(1)),
        ],
        out_specs=pl.BlockSpec((1, tm, RET_IN), lambda b, i: (b, i, 0)),
        compiler_params=_params("arbitrary", "arbitrary"),
        name="ret_in",
    )(x, ada, gain, cos, sin, w_in)


def _ret_core_kernel(q_ref, k_ref, v_ref, g_ref, x_ref, ada_ref, intra_ref, qd_ref, kd_ref, cd_ref,
                     gn_ref, wo_ref, ngain_ref, wr_hi_ref, wr_lo_ref, br_ref,
                     xo_ref, h_ref, lt_ref, state, ybuf):
    @pl.when(pl.program_id(1) == 0)
    def _():
        state[...] = jnp.zeros_like(state)

    for hd in range(RET_HEADS):
        q = q_ref[0, :, hd * RET_DK:(hd + 1) * RET_DK]
        k = k_ref[0, :, hd * RET_DK:(hd + 1) * RET_DK]
        v = v_ref[0, :, hd * RET_DV:(hd + 1) * RET_DV]
        st = state[hd]
        s = _dot_nt(q, k) * intra_ref[hd]
        o = _dot(s.astype(BF16), v) + qd_ref[hd] * _dot(q, st.astype(BF16))
        kdec = (k.astype(F32) * kd_ref[hd]).astype(BF16)
        state[hd] = st * cd_ref[hd] + _dot_tn(kdec, v)
        o = _rms(o) * gn_ref[:, hd * RET_DV:(hd + 1) * RET_DV]
        g = g_ref[0, :, hd * RET_DV:(hd + 1) * RET_DV].astype(F32)
        ybuf[:, hd * RET_DV:(hd + 1) * RET_DV] = (_silu(g) * o).astype(BF16)

    y = _dot(ybuf[...], wo_ref[...])
    xn = x_ref[0] + ada_ref[0, 2:3, :] * y
    xo_ref[0] = xn
    h = _ffn_prenorm_logits(xn, ngain_ref[...], ada_ref[0, 3:4, :], ada_ref[0, 4:5, :],
                            wr_hi_ref, wr_lo_ref, br_ref, lt_ref)
    _store_row_tiles(h_ref, (0,), h)


def _ret_core(proj, x, ada, gn, w_out, ngain, router):
    bsz, seq, d = x.shape
    c = RET_CHUNK
    f32 = F32
    log_gamma = jnp.log1p(-jnp.exp2(-5.0 - jnp.arange(RET_HEADS, dtype=f32)))
    n = jnp.arange(c, dtype=f32)
    rel = n[:, None] - n[None, :]
    intra = jnp.where(rel >= 0, jnp.exp(jnp.maximum(rel, 0.0)[None] * log_gamma[:, None, None]), 0.0)
    qd = jnp.exp((n + 1.0)[None, :] * log_gamma[:, None])[:, :, None]
    kd = jnp.exp((c - 1.0 - n)[None, :] * log_gamma[:, None])[:, :, None]
    cd = jnp.broadcast_to(jnp.exp(c * log_gamma)[:, None, None], (RET_HEADS, 1, RET_DV))
    const = lambda *shape: pl.BlockSpec(shape, lambda b, i: (0,) * len(shape))
    tok = lambda w, j: pl.BlockSpec((1, c, w), lambda b, i: (b, i, j))
    nc = seq // c
    return pl.pallas_call(
        _ret_core_kernel,
        out_shape=(jax.ShapeDtypeStruct((bsz, seq, d), f32),
                   jax.ShapeDtypeStruct((bsz, seq * ROW_TILES, LANE), f32),
                   jax.ShapeDtypeStruct((ROUTER_ROWS, bsz * seq), f32)),
        grid=(bsz, nc),
        in_specs=[
            tok(RET_QK, 0), tok(RET_QK, 1), tok(RET_V, 1), tok(RET_V, 2),
            tok(d, 0),
            pl.BlockSpec((1, 6, d), lambda b, i: (b, 0, 0)),
            const(RET_HEADS, c, c), const(RET_HEADS, c, 1), const(RET_HEADS, c, 1),
            const(RET_HEADS, 1, RET_DV),
            const(1, RET_V), const(RET_V, d), const(1, d),
        ] + _logit_specs(d),
        out_specs=(tok(d, 0), pl.BlockSpec((1, c * ROW_TILES, LANE), lambda b, i: (b, i, 0)),
                   pl.BlockSpec((ROUTER_ROWS, c), lambda b, i: (0, b * nc + i))),
        scratch_shapes=[pltpu.VMEM((RET_HEADS, RET_DK, RET_DV), f32),
                        pltpu.VMEM((c, RET_V), BF16)],
        compiler_params=_params("arbitrary", "arbitrary"),
        name="ret_core",
    )(proj, proj, proj, proj, x, ada, intra, qd, kd, cd, gn, w_out, ngain, *router)


def _moe_kernel(blk_ref, exp_ref, nitem_ref, start_ref, cnt_ref, src_cur, src_nxt, h_hbm, w1_ref, w3_ref, w2_ref,
                y_hbm, xbuf, obuf, w1b, w3b, w2b, gsem, ssem):
    w = pl.program_id(0)
    bm = MOE_BM
    n_item = nitem_ref[0]
    n_tok = h_hbm.shape[0] // ROW_TILES
    n_blocks = TOP_K * n_tok // bm

    def gather_rows(src_ref, s):
        def body(r, carry):
            tok = src_ref[0, 0, r] >> TOP_K_SHIFT
            pltpu.make_async_copy(_row_tile(h_hbm, (), tok), _row_tile(xbuf, (s,), r), gsem.at[s]).start()
            return carry
        lax.fori_loop(0, bm, body, 0, unroll=8)

    def scatter_rows(s):
        def body(r, carry):
            a = src_cur[0, 0, r]
            dst = (a & (TOP_K - 1)) * n_tok + (a >> TOP_K_SHIFT)
            pltpu.make_async_copy(_row_tile(obuf, (s,), r), _row_tile(y_hbm, (), dst), ssem.at[s]).start()
            return carry
        lax.fori_loop(0, bm, body, 0, unroll=8)

    def wait_gather(s):
        pltpu.make_async_copy(h_hbm.at[pl.ds(0, bm * ROW_TILES), :], xbuf.at[s], gsem.at[s]).wait()

    def wait_scatter(s):
        pltpu.make_async_copy(obuf.at[s], y_hbm.at[pl.ds(0, bm * ROW_TILES), :], ssem.at[s]).wait()

    @pl.when(w < n_item)
    def _():
        e = exp_ref[w]
        blk = blk_ref[w]
        slot = blk % 2
        prev = jnp.maximum(w - 1, 0)
        first_visit = jnp.logical_or(w == 0, blk_ref[prev] != blk)
        last_visit = jnp.logical_or(w == n_item - 1, blk_ref[jnp.minimum(w + 1, pl.num_programs(0) - 1)] != blk)

        @pl.when(w == 0)
        def _():
            gather_rows(src_cur, 0)

        @pl.when(first_visit)
        def _():
            @pl.when(blk + 1 < n_blocks)
            def _():
                gather_rows(src_nxt, 1 - slot)
            wait_gather(slot)

        @pl.when(jnp.logical_or(w == 0, exp_ref[prev] != e))
        def _():
            w1b[...] = w1_ref[0, 0].astype(BF16)
            w3b[...] = w3_ref[0, 0].astype(BF16)
            w2b[...] = w2_ref[0, 0].astype(BF16)

        x = _load_row_tiles(xbuf, (slot,), bm).astype(BF16)
        a = _dot(x, w1b[...])
        b = _dot(x, w3b[...])
        y = _dot((_silu(a) * b).astype(BF16), w2b[...])
        row = blk * bm + lax.broadcasted_iota(jnp.int32, (bm, 1), 0)
        mine = jnp.logical_and(row >= start_ref[e], row < start_ref[e] + cnt_ref[e])

        @pl.when(first_visit)
        def _():
            _store_row_tiles(obuf, (slot,), jnp.where(mine, y, 0.0))

        @pl.when(jnp.logical_not(first_visit))
        def _():
            _store_row_tiles(obuf, (slot,), jnp.where(mine, y, _load_row_tiles(obuf, (slot,), bm)))

        @pl.when(last_visit)
        def _():
            scatter_rows(slot)

            @pl.when(blk >= 1)
            def _():
                wait_scatter(1 - slot)

            @pl.when(w == n_item - 1)
            def _():
                wait_scatter(slot)


def _moe(h, order, counts, w1, w3, w2, layer):
    m = order.shape[0]
    d = w1.shape[-1]
    bm = MOE_BM
    n_blocks = m // bm
    n_items = n_blocks + N_EXPERTS - 1
    ends = jnp.cumsum(counts)
    starts = ends - counts
    first_blk = starts // bm
    last_blk = jnp.maximum(ends - 1, starts) // bm
    visits = jnp.where(counts > 0, last_blk - first_blk + 1, 0)
    item_end = jnp.cumsum(visits)
    item_start = item_end - visits
    n_item = item_end[-1]
    w = jnp.minimum(jnp.arange(n_items, dtype=jnp.int32), n_item - 1)
    item_e = jnp.sum(w[:, None] >= item_end[None, :], axis=1).astype(jnp.int32)
    item_blk = (first_blk[item_e] + w - item_start[item_e]).astype(jnp.int32)
    wspec = lambda: pl.BlockSpec((1, 1, d, d), lambda i, blk, e, *_: (layer, e[i], 0, 0))
    idx = lambda f: pl.BlockSpec((1, 1, bm), f, memory_space=pltpu.SMEM)
    order3 = order.reshape(n_blocks, 1, bm)
    return pl.pallas_call(
        _moe_kernel,
        out_shape=jax.ShapeDtypeStruct((m * ROW_TILES, LANE), F32),
        grid_spec=pltpu.PrefetchScalarGridSpec(
            num_scalar_prefetch=5,
            grid=(n_items,),
            in_specs=[idx(lambda i, blk, *_: (blk[i], 0, 0)),
                      idx(lambda i, blk, *_: (jnp.minimum(blk[i] + 1, n_blocks - 1), 0, 0)),
                      pl.BlockSpec(memory_space=pl.ANY), wspec(), wspec(), wspec()],
            out_specs=pl.BlockSpec(memory_space=pl.ANY),
            scratch_shapes=[pltpu.VMEM((2, bm * ROW_TILES, LANE), F32), pltpu.VMEM((2, bm * ROW_TILES, LANE), F32),
                            pltpu.VMEM((d, d), BF16), pltpu.VMEM((d, d), BF16), pltpu.VMEM((d, d), BF16),
                            pltpu.SemaphoreType.DMA((2,)), pltpu.SemaphoreType.DMA((2,))],
        ),
        compiler_params=_params("arbitrary"),
        name="moe",
    )(item_blk, item_e, n_item.astype(jnp.int32).reshape(1), starts.astype(jnp.int32), counts.astype(jnp.int32),
      order3, order3, h, w1, w3, w2)


def _moe_combined(y_ref, w1_ref, w2_ref, tm):
    return w1_ref[...] * _load_row_tiles(y_ref, (0,), tm) + w2_ref[...] * _load_row_tiles(y_ref, (1,), tm)


def _combine_specs(tm, step_of):
    wcol = lambda: pl.BlockSpec((tm, 1), lambda *g: (step_of(*g), 0))
    return [wcol(), wcol(), pl.BlockSpec((TOP_K, tm * ROW_TILES, LANE), lambda *g: (0, step_of(*g), 0))]


def _group_norm64(t, ind_ref, indt_ref):
    ss = _dot((t * t).astype(BF16), ind_ref[...])
    r = lax.rsqrt(ss * (1.0 / DIFF_HEAD_DIM) + EPS)
    r_hi = r.astype(BF16)
    r_lo = (r - r_hi.astype(F32)).astype(BF16)
    rb = _dot(jnp.concatenate([r_hi, r_lo], axis=1), indt_ref[...])
    return t * rb


def _kvq_kernel(w1_ref, w2_ref, y_ref, x_ref, ada0_ref, ada1_ref, kvada_ref, kvgain_ref,
                qgain_ref, knorm_ref, qnorm_ref, wk_ref, wvt_ref, wq_ref, ind_ref, indt_ref,
                xo_ref, q_ref, k_ref, vt_ref):
    tm = x_ref.shape[1]
    xn = x_ref[0] + ada0_ref[0, 5:6, :] * _moe_combined(y_ref, w1_ref, w2_ref, tm)
    xo_ref[0] = xn
    r = _rms(xn)
    hk = ((r * kvgain_ref[...]) * (1.0 + kvada_ref[0, 1:2, :]) + kvada_ref[0, 0:1, :]).astype(BF16)
    hq = ((r * qgain_ref[...]) * (1.0 + ada1_ref[0, 1:2, :]) + ada1_ref[0, 0:1, :]).astype(BF16)
    hd2 = 2 * DIFF_HEAD_DIM
    kk = _dot(hk, wk_ref[...])
    kn = (_group_norm64(kk, ind_ref, indt_ref) * knorm_ref[...]).astype(BF16)
    qq = _dot(hq, wq_ref[...])
    qn = (_group_norm64(qq, ind_ref, indt_ref) * qnorm_ref[...]).astype(BF16)
    for hd in range(DIFF_HEADS):
        k_ref[0, hd] = kn[:, hd * hd2:(hd + 1) * hd2]
        q_ref[0, hd] = qn[:, hd * hd2:(hd + 1) * hd2]
    vt = _dot_nt(wvt_ref[...], hk).astype(BF16)
    for j in range(tm // ATT_TK):
        vt_ref[0, j] = vt[:, j * ATT_TK:(j + 1) * ATT_TK].reshape(DIFF_HEADS, hd2, ATT_TK)


def _kvq(x, y, w1, w2, ada0, ada1, kvada, kv_gain, q_gain, k_norm, q_norm, w_k, w_vt, w_q):
    bsz, seq, d = x.shape
    tm = ROW_TILE
    nt = seq // tm
    lane_group = jnp.arange(d, dtype=jnp.int32) // DIFF_HEAD_DIM
    ind = (lane_group[:, None] == jnp.arange(LANE, dtype=jnp.int32)[None, :]).astype(BF16)
    indt = jnp.concatenate([ind.T, ind.T], axis=0)
    reps = d // DIFF_HEAD_DIM
    q_scale = jnp.tile(q_norm, reps)[None, :] * (DIFF_HEAD_DIM ** -0.5 * math.log2(math.e))
    const = lambda *shape: pl.BlockSpec(shape, lambda b, i: (0,) * len(shape))
    tok = lambda w: pl.BlockSpec((1, tm, w), lambda b, i: (b, i, 0))
    hd2 = 2 * DIFF_HEAD_DIM
    heads = pl.BlockSpec((1, DIFF_HEADS, tm, hd2), lambda b, i: (b, 0, i, 0))
    ada_spec = lambda rows: pl.BlockSpec((1, rows, d), lambda b, i: (b, 0, 0))
    col = lambda a: a.reshape(-1, 1)
    return pl.pallas_call(
        _kvq_kernel,
        out_shape=(jax.ShapeDtypeStruct((bsz, seq, d), F32),
                   jax.ShapeDtypeStruct((bsz, DIFF_HEADS, seq, hd2), BF16),
                   jax.ShapeDtypeStruct((bsz, DIFF_HEADS, seq, hd2), BF16),
                   jax.ShapeDtypeStruct((bsz, seq // ATT_TK, DIFF_HEADS, hd2, ATT_TK), BF16)),
        grid=(bsz, nt),
        in_specs=_combine_specs(tm, lambda b, i: b * nt + i) + [
            tok(d),
            ada_spec(6), ada_spec(6), ada_spec(2),
            const(1, d), const(1, d), const(1, d), const(1, d),
            const(d, d), const(d, d), const(d, d), const(d, LANE), const(2 * LANE, d),
        ],
        out_specs=(tok(d), heads, heads,
                   pl.BlockSpec((1, tm // ATT_TK, DIFF_HEADS, hd2, ATT_TK), lambda b, i: (b, i, 0, 0, 0))),
        compiler_params=_params("arbitrary", "arbitrary"),
        name="kvq",
    )(col(w1), col(w2), y.reshape(TOP_K, -1, LANE), x, ada0, ada1, kvada,
      kv_gain, q_gain, jnp.tile(k_norm, reps)[None, :], q_scale, w_k, w_vt, w_q, ind, indt)


def _attn_kernel(lambda_init, q_ref, k_ref, vt_ref, x_ref, ada_ref, lam_ref, subln_ref, wo_ref, ngain_ref,
                 wr_hi_ref, wr_lo_ref, br_ref, xo_ref, h_ref, lt_ref,
                 s_a, s_b, p_a, p_b, acc, obuf):
    i = pl.program_id(1)
    tq, tk = ATT_TQ, ATT_TK
    hd2 = 2 * DIFF_HEAD_DIM
    n_kblk = k_ref.shape[2] // tk
    lam = lam_ref[...]
    lam_full = (jnp.exp(jnp.sum(lam[0:1] * lam[1:2], axis=-1, keepdims=True))
                - jnp.exp(jnp.sum(lam[2:3] * lam[3:4], axis=-1, keepdims=True)) + lambda_init)
    lane = lax.broadcasted_iota(jnp.int32, (1, hd2), 1)
    keep = (jnp.where(lane < DIFF_HEAD_DIM, 1.0, 0.0).astype(BF16),
            jnp.where(lane >= DIFF_HEAD_DIM, 1.0, 0.0).astype(BF16))
    key = lax.broadcasted_iota(jnp.int32, (tk, tq), 0)
    qry = lax.broadcasted_iota(jnp.int32, (1, tq), 1)
    key_limit = ((i * tq + qry) // MASK_CHUNK + 1) * MASK_CHUNK
    neg_inf = jnp.float32(-jnp.inf)
    n_pairs = (i + 2) // 2
    n_plain = jnp.maximum((i - 1) // 2, 0)

    streams = [(u, t) for u in range(ATT_HEADS) for t in range(2)]

    def head_group(hg, _):
        hds = [hg * ATT_HEADS + u for u in range(ATT_HEADS)]
        qs = []
        for hd in hds:
            qh = q_ref[0, hd]
            qs.append((qh * keep[0], qh * keep[1]))

        def produce_scores(jb, s_out, masked):
            start = pl.multiple_of(jnp.minimum(jb, n_kblk - 1) * tk, tk)
            visible = key < key_limit - jb * tk
            maxima = []
            for n, (u, t) in enumerate(streams):
                st = _dot_nt(k_ref[0, hds[u], pl.ds(start, tk), :], qs[u][t])
                if masked:
                    st = jnp.where(visible, st, neg_inf)
                s_out[n] = st
                maxima.append(jnp.max(st, axis=0, keepdims=True))
            return tuple(maxima)

        def step(j, s_in, s_out, p_in, p_out, state, masked):
            stats, blk_max = state
            jv = jnp.clip(j - 1, 0, n_kblk - 1)
            next_max = produce_scores(j + 1, s_out, masked)
            new_stats = []
            for n, (u, t) in enumerate(streams):
                m, l = stats[n]
                m_new = jnp.maximum(m, blk_max[n])
                alpha = jnp.exp2(m - m_new)
                p = jnp.exp2(s_in[n] - m_new)
                p_out[n] = p.astype(BF16)
                new_stats.append((m_new, alpha * l + jnp.sum(p, axis=0, keepdims=True)))
                acc[n] = alpha * (acc[n] + _dot(vt_ref[0, jv, hds[u]], p_in[n]))
            return tuple(new_stats), next_max

        p_b[...] = jnp.zeros_like(p_b)
        acc[...] = jnp.zeros_like(acc)
        stat0 = (jnp.full((1, tq), neg_inf, F32), jnp.zeros((1, tq), F32))
        state = ((stat0,) * len(streams), produce_scores(0, s_a, True))

        def pair(masked, jj, state):
            state = step(2 * jj, s_a, s_b, p_b, p_a, state, masked)
            return step(2 * jj + 1, s_b, s_a, p_a, p_b, state, masked)

        state = lax.fori_loop(0, n_plain, functools.partial(pair, False), state)
        (stats, _) = lax.fori_loop(n_plain, n_pairs, functools.partial(pair, True), state)
        jv = jnp.minimum(2 * n_pairs - 1, n_kblk - 1)
        for u, hd in enumerate(hds):
            vtb = vt_ref[0, jv, hd]
            a1 = acc[2 * u] + _dot(vtb, p_b[2 * u])
            a2 = acc[2 * u + 1] + _dot(vtb, p_b[2 * u + 1])
            o = a1 * (1.0 / stats[2 * u][1]) - lam_full * (a2 * (1.0 / stats[2 * u + 1][1]))
            o = o * lax.rsqrt(jnp.mean(o * o, axis=0, keepdims=True) + EPS)
            obuf[hd] = (o * (subln_ref[...] * (1.0 - lambda_init))).astype(BF16)
        return 0

    lax.fori_loop(0, DIFF_HEADS // ATT_HEADS, head_group, 0)

    y = _dot_tn(obuf[...].reshape(DIFF_HEADS * hd2, tq), wo_ref[...])
    xn = x_ref[0] + ada_ref[0, 2:3, :] * y
    xo_ref[0] = xn
    h = _ffn_prenorm_logits(xn, ngain_ref[...], ada_ref[0, 3:4, :], ada_ref[0, 4:5, :],
                            wr_hi_ref, wr_lo_ref, br_ref, lt_ref)
    _store_row_tiles(h_ref, (0,), h)


def _attn(q, k, vt, x, ada, lam, subln, w_out, ngain, router, lambda_init):
    bsz, seq, d = x.shape
    tq, tk = ATT_TQ, ATT_TK
    hd2 = 2 * DIFF_HEAD_DIM
    const = lambda *shape: pl.BlockSpec(shape, lambda b, i: (0,) * len(shape))
    tok = lambda w: pl.BlockSpec((1, tq, w), lambda b, i: (b, i, 0))
    nq = seq // tq
    ns = 2 * ATT_HEADS
    return pl.pallas_call(
        functools.partial(_attn_kernel, lambda_init),
        out_shape=(jax.ShapeDtypeStruct((bsz, seq, d), F32),
                   jax.ShapeDtypeStruct((bsz, seq * ROW_TILES, LANE), F32),
                   jax.ShapeDtypeStruct((ROUTER_ROWS, bsz * seq), F32)),
        grid=(bsz, nq),
        in_specs=[
            pl.BlockSpec((1, DIFF_HEADS, tq, hd2), lambda b, i: (b, 0, i, 0)),
            pl.BlockSpec((1, DIFF_HEADS, seq, hd2), lambda b, i: (b, 0, 0, 0)),
            pl.BlockSpec((1, seq // tk, DIFF_HEADS, hd2, tk), lambda b, i: (b, 0, 0, 0, 0)),
            tok(d),
            pl.BlockSpec((1, 6, d), lambda b, i: (b, 0, 0)),
            const(4, DIFF_HEAD_DIM), const(hd2, 1), const(d, d), const(1, d),
        ] + _logit_specs(d),
        out_specs=(tok(d), pl.BlockSpec((1, tq * ROW_TILES, LANE), lambda b, i: (b, i, 0)),
                   pl.BlockSpec((ROUTER_ROWS, tq), lambda b, i: (0, b * nq + i))),
        scratch_shapes=[pltpu.VMEM((ns, tk, tq), F32), pltpu.VMEM((ns, tk, tq), F32),
                        pltpu.VMEM((ns, tk, tq), BF16), pltpu.VMEM((ns, tk, tq), BF16),
                        pltpu.VMEM((ns, hd2, tq), F32),
                        pltpu.VMEM((DIFF_HEADS, hd2, tq), BF16)],
        compiler_params=_params("arbitrary", "arbitrary"),
        name="attn",
    )(q, k, vt, x, ada, lam, subln, w_out, ngain, *router)


def _combine_kernel(w1_ref, w2_ref, y_ref, x_ref, ada_ref, o_ref):
    tm = x_ref.shape[1]
    o_ref[0] = x_ref[0] + ada_ref[0, 5:6, :] * _moe_combined(y_ref, w1_ref, w2_ref, tm)


def _combine(x, y, w1, w2, ada):
    bsz, seq, d = x.shape
    tm = ROW_TILE
    nt = seq // tm
    col = lambda a: a.reshape(-1, 1)
    return pl.pallas_call(
        _combine_kernel,
        out_shape=jax.ShapeDtypeStruct((bsz, seq, d), F32),
        grid=(bsz, nt),
        in_specs=_combine_specs(tm, lambda b, i: b * nt + i) + [
            pl.BlockSpec((1, tm, d), lambda b, i: (b, i, 0)),
            pl.BlockSpec((1, 6, d), lambda b, i: (b, 0, 0)),
        ],
        out_specs=pl.BlockSpec((1, tm, d), lambda b, i: (b, i, 0)),
        compiler_params=_params("arbitrary", "arbitrary"),
        name="combine",
    )(col(w1), col(w2), y.reshape(TOP_K, -1, LANE), x, ada)


def _router_weights(w_group, b_group, w_expert, b_expert):
    d = w_group.shape[0]
    pad = ROUTER_ROWS - N_EXPERTS - N_GROUPS
    w = jnp.concatenate([w_expert.T, w_group.T, jnp.zeros((pad, d), F32)], axis=0)
    b = jnp.concatenate([b_expert, b_group, jnp.zeros((pad,), F32)])[:, None]
    w_hi = w.astype(BF16)
    w_lo = (w - w_hi.astype(F32)).astype(BF16)
    return w_hi, w_lo, b


def _routing_tables(route, cnt):
    rec = lambda r: route[r]
    counts = cnt[:, 0].astype(jnp.int32)
    starts = jnp.cumsum(counts) - counts
    dest = jnp.stack([starts[rec(0).astype(jnp.int32)] + rec(2).astype(jnp.int32),
                      starts[rec(1).astype(jnp.int32)] + rec(3).astype(jnp.int32)], axis=1).reshape(-1)
    _, order = lax.sort((dest, jnp.arange(dest.shape[0], dtype=jnp.int32)), num_keys=1)
    return order, rec(4), rec(5), counts


def kernel(x, c, ada_w, ada_b, norm_mix, norm_ffn, ret_w_in, ret_gn, ret_w_out, kv_ada_w, kv_ada_b, kv_norm, kv_w, k_norm, diff_w_q, q_norm, diff_lam, diff_subln, diff_w_out, moe_w_group, moe_b_group, moe_w_expert, moe_b_expert, moe_w1, moe_w3, moe_w2):
    bsz, seq, d = x.shape
    n_tok = bsz * seq

    ada = _ada(c, ada_w, ada_b).reshape(2, bsz, 6, d)
    kvada = _ada(c, kv_ada_w[None], kv_ada_b[None]).reshape(bsz, 2, d)

    half = RET_DK // 2
    inv_freq = 1.0 / (ROPE_BASE ** (jnp.arange(half, dtype=F32) / half))
    ang = jnp.arange(seq, dtype=F32)[:, None] * inv_freq[None, :]
    cos, sin = jnp.cos(ang), jnp.sin(ang)

    routers = [_router_weights(moe_w_group[l], moe_b_group[l], moe_w_expert[l], moe_b_expert[l])
               for l in range(2)]

    def moe_layer(h, logits_t, layer):
        order, w1, w2, counts = _routing_tables(*_route(logits_t))
        y = _moe(h.reshape(n_tok * ROW_TILES, LANE), order, counts, moe_w1, moe_w3, moe_w2, layer)
        return y, w1, w2

    proj = _ret_in(x, ada[0], norm_mix[0][None, :], cos, sin, ret_w_in[0].astype(BF16))
    x1, h, logits_t = _ret_core(proj, x, ada[0], ret_gn[0][None, :], ret_w_out[0].astype(BF16),
                                norm_ffn[0][None, :], routers[0])
    moe0 = moe_layer(h, logits_t, 0)

    x2, q, k, vt = _kvq(x1, *moe0, ada[0], ada[1], kvada, kv_norm[None, :], norm_mix[1][None, :], k_norm, q_norm[0],
                        kv_w[:, :d].astype(BF16), kv_w[:, d:].T.astype(BF16), diff_w_q[0].astype(BF16))
    lambda_init = 0.8 - 0.6 * math.exp(-0.3 * 1)
    x3, h, logits_t = _attn(q, k, vt, x2, ada[1], diff_lam[0], diff_subln[0][:, None], diff_w_out[0].astype(BF16),
                            norm_ffn[1][None, :], routers[1], lambda_init)
    moe1 = moe_layer(h, logits_t, 1)
    return _combine(x3, *moe1, ada[1])
```

```python
import functools
import math

import jax
import jax.numpy as jnp
from jax import lax
from jax.experimental import pallas as pl
from jax.experimental.pallas import tpu as pltpu

F32 = jnp.float32
BF16 = jnp.bfloat16

D_MODEL = 1024
EPS = 1e-6

RET_HEADS = 4
RET_DK = D_MODEL // RET_HEADS
RET_DV = 2 * RET_DK
RET_QK = RET_HEADS * RET_DK
RET_V = RET_HEADS * RET_DV
RET_IN = 2 * RET_QK + 2 * RET_V
ROPE_BASE = 10000.0
RET_CHUNK = 256

DIFF_HEAD_DIM = 64
DIFF_HEADS = D_MODEL // (2 * DIFF_HEAD_DIM)
MASK_CHUNK = 64
ATT_TQ = 256
ATT_TK = 256
ATT_HEADS = 1

N_GROUPS = 4
EXPERTS_PER_GROUP = 4
N_EXPERTS = N_GROUPS * EXPERTS_PER_GROUP
TOP_K = 2
TOP_K_SHIFT = 1
MOE_BM = 256
MOE_TN = 256
ROUTER_ROWS = 32
ROUTE_ROWS = 8
ROW_TILE = 512
ROUTE_TILE = 2048
ROUTE_SEG = 256

LANE = 128
SUBLANE = 8
VMEM_LIMIT = 56 * 1024 * 1024


def _dot(a, b):
    return jnp.dot(a, b, preferred_element_type=F32)


def _dot_nt(a, b):
    return lax.dot_general(a, b, (((1,), (1,)), ((), ())), preferred_element_type=F32)


def _dot_tn(a, b):
    return lax.dot_general(a, b, (((0,), (0,)), ((), ())), preferred_element_type=F32)


def _silu(x):
    return x * (1.0 / (1.0 + jnp.exp(-x)))


def _rms(x):
    return x * lax.rsqrt(jnp.mean(x * x, axis=-1, keepdims=True) + EPS)


def _params(*sem, flags=None):
    return pltpu.CompilerParams(dimension_semantics=sem, vmem_limit_bytes=VMEM_LIMIT, flags=flags)


ROW_TILES = D_MODEL // LANE


def _load_row_tiles(ref, lead, rows):
    return jnp.concatenate(
        [ref[lead + (pl.ds(c, rows, stride=ROW_TILES), slice(None))] for c in range(ROW_TILES)], axis=1)


def _store_row_tiles(ref, lead, val):
    rows = val.shape[0]
    for c in range(ROW_TILES):
        ref[lead + (pl.ds(c, rows, stride=ROW_TILES), slice(None))] = val[:, c * LANE:(c + 1) * LANE]


def _row_tile(ref, lead, r):
    return ref.at[lead + (pl.ds(pl.multiple_of(r * ROW_TILES, ROW_TILES), ROW_TILES), slice(None))]


def _ada_kernel(c_ref, w_ref, b_ref, o_ref):
    ca = _silu(c_ref[...])
    o_ref[0] = jnp.dot(ca, w_ref[0], preferred_element_type=F32,
                       precision=lax.Precision.HIGHEST) + b_ref[0]


def _ada(c, w, b):
    n_l, d, n = w.shape
    bsz = c.shape[0]
    tn = 1024
    return pl.pallas_call(
        _ada_kernel,
        out_shape=jax.ShapeDtypeStruct((n_l, bsz, n), F32),
        grid=(n_l, n // tn),
        in_specs=[
            pl.BlockSpec((bsz, d), lambda l, j: (0, 0)),
            pl.BlockSpec((1, d, tn), lambda l, j: (l, 0, j)),
            pl.BlockSpec((1, 1, tn), lambda l, j: (l, 0, j)),
        ],
        out_specs=pl.BlockSpec((1, bsz, tn), lambda l, j: (l, 0, j)),
        compiler_params=_params("arbitrary", "arbitrary"),
        name="ada",
    )(c, w, b.reshape(n_l, 1, n))


def _ffn_prenorm_logits(xn, gain, shift, scale, wr_hi_ref, wr_lo_ref, br_ref, lt_ref):
    h = _rms(xn) * gain
    h = h * (1.0 + scale) + shift
    h_hi = h.astype(BF16)
    h_lo = (h - h_hi.astype(F32)).astype(BF16)
    w_hi = wr_hi_ref[...]
    lt_ref[...] = _dot_nt(w_hi, h_hi) + _dot_nt(wr_lo_ref[...], h_hi) + _dot_nt(w_hi, h_lo) + br_ref[...]
    return h


def _route_kernel(lt_ref, tri_ref, route_ref, cnt_ref, carry):
    lt = lt_ref[...]
    c = lt.shape[1]
    seg = tri_ref.shape[0]
    neg_inf = jnp.float32(-jnp.inf)

    grow = lax.broadcasted_iota(jnp.int32, (SUBLANE, c), 0).astype(F32)
    g = jnp.where(grow < N_GROUPS, lt[N_EXPERTS:N_EXPERTS + SUBLANE], neg_inf)
    gmax = jnp.max(g, axis=0, keepdims=True)
    gidx = jnp.min(jnp.where(g == gmax, grow, float(SUBLANE)), axis=0, keepdims=True)
    gate = 1.0 / jnp.sum(jnp.exp(g - gmax), axis=0, keepdims=True)

    erow_i = lax.broadcasted_iota(jnp.int32, (N_EXPERTS, c), 0)
    erow = erow_i.astype(F32)
    egroup = (erow_i // EXPERTS_PER_GROUP).astype(F32)
    el = jnp.where(egroup == gidx, lt[0:N_EXPERTS], neg_inf)
    m1 = jnp.max(el, axis=0, keepdims=True)
    i1 = jnp.min(jnp.where(el == m1, erow, float(N_EXPERTS)), axis=0, keepdims=True)
    el2 = jnp.where(erow == i1, neg_inf, el)
    m2 = jnp.max(el2, axis=0, keepdims=True)
    i2 = jnp.min(jnp.where(el2 == m2, erow, float(N_EXPERTS)), axis=0, keepdims=True)
    t = jnp.exp(m2 - m1)
    den = 1.0 / (1.0 + t)
    w1 = gate * den
    w2 = gate * t * den

    @pl.when(pl.program_id(0) == 0)
    def _():
        carry[...] = jnp.zeros_like(carry)

    oh1 = erow == i1
    oh2 = erow == i2
    oh = jnp.where(jnp.logical_or(oh1, oh2), 1.0, 0.0)
    offset = carry[...]
    before = []
    for s in range(c // seg):
        oh_s = oh[:, s * seg:(s + 1) * seg]
        before.append(offset + _dot(oh_s.astype(BF16), tri_ref[...]))
        offset = offset + jnp.sum(oh_s, axis=1, keepdims=True)
    before = jnp.concatenate(before, axis=1)
    rank1 = jnp.sum(jnp.where(oh1, before, 0.0), axis=0, keepdims=True)
    rank2 = jnp.sum(jnp.where(oh2, before, 0.0), axis=0, keepdims=True)
    carry[...] = offset
    cnt_ref[...] = jnp.broadcast_to(offset, cnt_ref.shape)

    rrow = lax.broadcasted_iota(jnp.int32, (ROUTE_ROWS, c), 0)
    rec = jnp.zeros((ROUTE_ROWS, c), F32)
    for idx, val in enumerate((i1, i2, rank1, rank2, w1, w2)):
        rec = jnp.where(rrow == idx, val, rec)
    route_ref[...] = rec


def _route(lt):
    n_tok = lt.shape[1]
    tile = min(ROUTE_TILE, n_tok)
    t = jnp.arange(ROUTE_SEG, dtype=jnp.int32)
    earlier = (t[:, None] < t[None, :]).astype(BF16)
    return pl.pallas_call(
        _route_kernel,
        out_shape=(jax.ShapeDtypeStruct((ROUTE_ROWS, n_tok), F32),
                   jax.ShapeDtypeStruct((N_EXPERTS, LANE), F32)),
        grid=(n_tok // tile,),
        in_specs=[pl.BlockSpec((ROUTER_ROWS, tile), lambda i: (0, i)),
                  pl.BlockSpec((ROUTE_SEG, ROUTE_SEG), lambda i: (0, 0))],
        out_specs=(pl.BlockSpec((ROUTE_ROWS, tile), lambda i: (0, i)),
                   pl.BlockSpec((N_EXPERTS, LANE), lambda i: (0, 0))),
        scratch_shapes=[pltpu.VMEM((N_EXPERTS, 1), F32)],
        compiler_params=_params("arbitrary"),
        name="route",
    )(lt, earlier)


def _logit_specs(d):
    const = lambda *shape: pl.BlockSpec(shape, lambda b, i: (0,) * len(shape))
    return [const(ROUTER_ROWS, d), const(ROUTER_ROWS, d), const(ROUTER_ROWS, 1)]


def _ret_in_kernel(x_ref, ada_ref, gain_ref, cos_ref, sin_ref, w_ref, o_ref):
    x = x_ref[0]
    shift = ada_ref[0, 0:1, :]
    scale = ada_ref[0, 1:2, :]
    h = (_rms(x) * gain_ref[...]) * (1.0 + scale) + shift
    hb = h.astype(BF16)
    cos = cos_ref[...]
    sin = sin_ref[...]
    half = RET_DK // 2
    for j in range(RET_IN // D_MODEL):
        p = _dot(hb, w_ref[:, j * D_MODEL:(j + 1) * D_MODEL])
        if j < 2:
            post = 1.0 if j == 0 else RET_DK ** -0.5
            for hd in range(RET_HEADS):
                lo = hd * RET_DK
                x1 = p[:, lo:lo + half]
                x2 = p[:, lo + half:lo + RET_DK]
                o_ref[0, :, j * D_MODEL + lo:j * D_MODEL + lo + half] = (
                    (x1 * cos - x2 * sin) * post).astype(BF16)
                o_ref[0, :, j * D_MODEL + lo + half:j * D_MODEL + lo + RET_DK] = (
                    (x2 * cos + x1 * sin) * post).astype(BF16)
        else:
            o_ref[0, :, j * D_MODEL:(j + 1) * D_MODEL] = p.astype(BF16)


def _ret_in(x, ada, gain, cos, sin, w_in):
    bsz, seq, d = x.shape
    tm = ROW_TILE
    return pl.pallas_call(
        _ret_in_kernel,
        out_shape=jax.ShapeDtypeStruct((bsz, seq, RET_IN), BF16),
        grid=(bsz, seq // tm),
        in_specs=[
            pl.BlockSpec((1, tm, d), lambda b, i: (b, i, 0)),
            pl.BlockSpec((1, 6, d), lambda b, i: (b, 0, 0)),
            pl.BlockSpec((1, d), lambda b, i: (0, 0)),
            pl.BlockSpec((tm, RET_DK // 2), lambda b, i: (i, 0)),
            pl.BlockSpec((tm, RET_DK // 2), lambda b, i: (i, 0)),
            pl.BlockSpec((d, RET_IN), lambda b, i: (0, 0), pipeline_mode=pl.Buffered(1)),
        ],
        out_specs=pl.BlockSpec((1, tm, RET_IN), lambda b, i: (b, i, 0)),
        compiler_params=_params("arbitrary", "arbitrary"),
        name="ret_in",
    )(x, ada, gain, cos, sin, w_in)


def _ret_core_kernel(q_ref, k_ref, v_ref, g_ref, x_ref, ada_ref, intra_ref, qd_ref, kd_ref, cd_ref,
                     gn_ref, wo_ref, ngain_ref, wr_hi_ref, wr_lo_ref, br_ref,
                     xo_ref, h_ref, lt_ref, state, ybuf):
    @pl.when(pl.program_id(1) == 0)
    def _():
        state[...] = jnp.zeros_like(state)

    for hd in range(RET_HEADS):
        q = q_ref[0, :, hd * RET_DK:(hd + 1) * RET_DK]
        k = k_ref[0, :, hd * RET_DK:(hd + 1) * RET_DK]
        v = v_ref[0, :, hd * RET_DV:(hd + 1) * RET_DV]
        st = state[hd]
        s = _dot_nt(q, k) * intra_ref[hd]
        o = _dot(s.astype(BF16), v) + qd_ref[hd] * _dot(q, st.astype(BF16))
        kdec = (k.astype(F32) * kd_ref[hd]).astype(BF16)
        state[hd] = st * cd_ref[hd] + _dot_tn(kdec, v)
        o = _rms(o) * gn_ref[:, hd * RET_DV:(hd + 1) * RET_DV]
        g = g_ref[0, :, hd * RET_DV:(hd + 1) * RET_DV].astype(F32)
        ybuf[:, hd * RET_DV:(hd + 1) * RET_DV] = (_silu(g) * o).astype(BF16)

    y = _dot(ybuf[...], wo_ref[...])
    xn = x_ref[0] + ada_ref[0, 2:3, :] * y
    xo_ref[0] = xn
    h = _ffn_prenorm_logits(xn, ngain_ref[...], ada_ref[0, 3:4, :], ada_ref[0, 4:5, :],
                            wr_hi_ref, wr_lo_ref, br_ref, lt_ref)
    _store_row_tiles(h_ref, (0,), h)


def _ret_core(proj, x, ada, gn, w_out, ngain, router):
    bsz, seq, d = x.shape
    c = RET_CHUNK
    f32 = F32
    log_gamma = jnp.log1p(-jnp.exp2(-5.0 - jnp.arange(RET_HEADS, dtype=f32)))
    n = jnp.arange(c, dtype=f32)
    rel = n[:, None] - n[None, :]
    intra = jnp.where(rel >= 0, jnp.exp(jnp.maximum(rel, 0.0)[None] * log_gamma[:, None, None]), 0.0)
    qd = jnp.exp((n + 1.0)[None, :] * log_gamma[:, None])[:, :, None]
    kd = jnp.exp((c - 1.0 - n)[None, :] * log_gamma[:, None])[:, :, None]
    cd = jnp.broadcast_to(jnp.exp(c * log_gamma)[:, None, None], (RET_HEADS, 1, RET_DV))
    const = lambda *shape: pl.BlockSpec(shape, lambda b, i: (0,) * len(shape))
    tok = lambda w, j: pl.BlockSpec((1, c, w), lambda b, i: (b, i, j))
    nc = seq // c
    return pl.pallas_call(
        _ret_core_kernel,
        out_shape=(jax.ShapeDtypeStruct((bsz, seq, d), f32),
                   jax.ShapeDtypeStruct((bsz, seq * ROW_TILES, LANE), f32),
                   jax.ShapeDtypeStruct((ROUTER_ROWS, bsz * seq), f32)),
        grid=(bsz, nc),
        in_specs=[
            tok(RET_QK, 0), tok(RET_QK, 1), tok(RET_V, 1), tok(RET_V, 2),
            tok(d, 0),
            pl.BlockSpec((1, 6, d), lambda b, i: (b, 0, 0)),
            const(RET_HEADS, c, c), const(RET_HEADS, c, 1), const(RET_HEADS, c, 1),
            const(RET_HEADS, 1, RET_DV),
            const(1, RET_V), const(RET_V, d), const(1, d),
        ] + _logit_specs(d),
        out_specs=(tok(d, 0), pl.BlockSpec((1, c * ROW_TILES, LANE), lambda b, i: (b, i, 0)),
                   pl.BlockSpec((ROUTER_ROWS, c), lambda b, i: (0, b * nc + i))),
        scratch_shapes=[pltpu.VMEM((RET_HEADS, RET_DK, RET_DV), f32),
                        pltpu.VMEM((c, RET_V), BF16)],
        compiler_params=_params("arbitrary", "arbitrary"),
        name="ret_core",
    )(proj, proj, proj, proj, x, ada, intra, qd, kd, cd, gn, w_out, ngain, *router)


def _moe_kernel(blk_ref, exp_ref, nitem_ref, start_ref, cnt_ref, src_prv, src_cur, src_nxt, h_hbm,
                w1_ref, w3_ref, w2_ref, y_hbm, xbuf, obuf, w1b, w3b, w2b, gsem, ssem):
    w = pl.program_id(0)
    bm = MOE_BM
    n_item = nitem_ref[0]
    n_tok = h_hbm.shape[0] // ROW_TILES
    d = w1b.shape[0]
    e = exp_ref[w]
    blk = blk_ref[w]
    slot = blk % 2
    prev = jnp.maximum(w - 1, 0)
    first_visit = jnp.logical_or(w == 0, blk_ref[prev] != blk)
    live = w < n_item

    def gather_row(src_ref, s, r):
        tok = src_ref[0, 0, r] >> TOP_K_SHIFT
        pltpu.make_async_copy(_row_tile(h_hbm, (), tok), _row_tile(xbuf, (s,), r), gsem.at[s]).start()

    def scatter_row(src_ref, s, r):
        a = src_ref[0, 0, r]
        dst = (a & (TOP_K - 1)) * n_tok + (a >> TOP_K_SHIFT)
        pltpu.make_async_copy(_row_tile(obuf, (s,), r), _row_tile(y_hbm, (), dst), ssem.at[s]).start()

    def rolled(row_copy, src_ref, s):
        def body(r, carry):
            row_copy(src_ref, s, r)
            return carry
        lax.fori_loop(0, bm, body, 0, unroll=8)

    def wait_gather(s):
        pltpu.make_async_copy(h_hbm.at[pl.ds(0, bm * ROW_TILES), :], xbuf.at[s], gsem.at[s]).wait()

    def wait_scatter(s):
        pltpu.make_async_copy(obuf.at[s], y_hbm.at[pl.ds(0, bm * ROW_TILES), :], ssem.at[s]).wait()

    def expert(row_copies):
        pieces = 2 * (d // MOE_TN)
        per_piece = -(-len(row_copies) // pieces)
        groups = [row_copies[g * per_piece:(g + 1) * per_piece] for g in range(pieces)]
        x = _load_row_tiles(xbuf, (slot,), bm).astype(BF16)
        hid = []
        for c in range(d // MOE_TN):
            for issue in groups[c]:
                issue()
            a = _dot(x, w1b[:, c * MOE_TN:(c + 1) * MOE_TN])
            b = _dot(x, w3b[:, c * MOE_TN:(c + 1) * MOE_TN])
            hid.append((_silu(a) * b).astype(BF16))
        hid = jnp.concatenate(hid, axis=1)
        y = []
        for c in range(d // MOE_TN):
            for issue in groups[d // MOE_TN + c]:
                issue()
            y.append(_dot(hid, w2b[:, c * MOE_TN:(c + 1) * MOE_TN]))
        y = jnp.concatenate(y, axis=1)
        row = blk * bm + lax.broadcasted_iota(jnp.int32, (bm, 1), 0)
        mine = jnp.logical_and(row >= start_ref[e], row < start_ref[e] + cnt_ref[e])
        return y, mine

    @pl.when(jnp.logical_and(live, jnp.logical_or(w == 0, exp_ref[prev] != e)))
    def _():
        w1b[...] = w1_ref[0, 0].astype(BF16)
        w3b[...] = w3_ref[0, 0].astype(BF16)
        w2b[...] = w2_ref[0, 0].astype(BF16)

    @pl.when(w == 0)
    def _():
        rolled(gather_row, src_cur, 0)

    @pl.when(jnp.logical_and(live, first_visit))
    def _():
        wait_gather(slot)

        @pl.when(blk >= 2)
        def _():
            wait_scatter(slot)

    gather_next = [functools.partial(gather_row, src_nxt, 1 - slot, r) for r in range(bm)]
    scatter_prev = [functools.partial(scatter_row, src_prv, 1 - slot, r) for r in range(bm)]

    @pl.when(jnp.logical_and(live, jnp.logical_and(first_visit, blk == 0)))
    def _():
        y, mine = expert(gather_next)
        _store_row_tiles(obuf, (slot,), jnp.where(mine, y, 0.0))

    @pl.when(jnp.logical_and(live, jnp.logical_and(first_visit, blk > 0)))
    def _():
        y, mine = expert([f for pair in zip(gather_next, scatter_prev) for f in pair])
        _store_row_tiles(obuf, (slot,), jnp.where(mine, y, 0.0))

    @pl.when(jnp.logical_and(live, jnp.logical_not(first_visit)))
    def _():
        y, mine = expert([])
        _store_row_tiles(obuf, (slot,), jnp.where(mine, y, _load_row_tiles(obuf, (slot,), bm)))

    @pl.when(w == n_item - 1)
    def _():
        rolled(scatter_row, src_cur, slot)
        wait_gather(1 - slot)

        @pl.when(blk >= 1)
        def _():
            wait_scatter(1 - slot)
        wait_scatter(slot)


def _moe(h, order, counts, w1, w3, w2, layer):
    m = order.shape[0]
    d = w1.shape[-1]
    bm = MOE_BM
    n_blocks = m // bm
    n_items = n_blocks + N_EXPERTS - 1
    ends = jnp.cumsum(counts)
    starts = ends - counts
    first_blk = starts // bm
    last_blk = jnp.maximum(ends - 1, starts) // bm
    visits = jnp.where(counts > 0, last_blk - first_blk + 1, 0)
    item_end = jnp.cumsum(visits)
    item_start = item_end - visits
    n_item = item_end[-1]
    w = jnp.minimum(jnp.arange(n_items, dtype=jnp.int32), n_item - 1)
    item_e = jnp.sum(w[:, None] >= item_end[None, :], axis=1).astype(jnp.int32)
    item_blk = (first_blk[item_e] + w - item_start[item_e]).astype(jnp.int32)
    wspec = lambda: pl.BlockSpec((1, 1, d, d), lambda i, blk, e, *_: (layer, e[i], 0, 0))
    idx = lambda f: pl.BlockSpec((1, 1, bm), f, memory_space=pltpu.SMEM)
    order3 = order.reshape(n_blocks, 1, bm)
    return pl.pallas_call(
        _moe_kernel,
        out_shape=jax.ShapeDtypeStruct((m * ROW_TILES, LANE), F32),
        grid_spec=pltpu.PrefetchScalarGridSpec(
            num_scalar_prefetch=5,
            grid=(n_items,),
            in_specs=[idx(lambda i, blk, *_: (jnp.maximum(blk[i] - 1, 0), 0, 0)),
                      idx(lambda i, blk, *_: (blk[i], 0, 0)),
                      idx(lambda i, blk, *_: (jnp.minimum(blk[i] + 1, n_blocks - 1), 0, 0)),
                      pl.BlockSpec(memory_space=pl.ANY), wspec(), wspec(), wspec()],
            out_specs=pl.BlockSpec(memory_space=pl.ANY),
            scratch_shapes=[pltpu.VMEM((2, bm * ROW_TILES, LANE), F32), pltpu.VMEM((2, bm * ROW_TILES, LANE), F32),
                            pltpu.VMEM((d, d), BF16), pltpu.VMEM((d, d), BF16), pltpu.VMEM((d, d), BF16),
                            pltpu.SemaphoreType.DMA((2,)), pltpu.SemaphoreType.DMA((2,))],
        ),
        compiler_params=_params("arbitrary"),
        name="moe",
    )(item_blk, item_e, n_item.astype(jnp.int32).reshape(1), starts.astype(jnp.int32), counts.astype(jnp.int32),
      order3, order3, order3, h, w1, w3, w2)


def _moe_combined(y_ref, w1_ref, w2_ref, tm):
    return w1_ref[...] * _load_row_tiles(y_ref, (0,), tm) + w2_ref[...] * _load_row_tiles(y_ref, (1,), tm)


def _combine_specs(tm, step_of):
    wcol = lambda: pl.BlockSpec((tm, 1), lambda *g: (step_of(*g), 0))
    return [wcol(), wcol(), pl.BlockSpec((TOP_K, tm * ROW_TILES, LANE), lambda *g: (0, step_of(*g), 0))]


def _group_norm64(t, ind_ref, indt_ref):
    ss = _dot((t * t).astype(BF16), ind_ref[...])
    r = lax.rsqrt(ss * (1.0 / DIFF_HEAD_DIM) + EPS)
    r_hi = r.astype(BF16)
    r_lo = (r - r_hi.astype(F32)).astype(BF16)
    rb = _dot(jnp.concatenate([r_hi, r_lo], axis=1), indt_ref[...])
    return t * rb


def _kvq_kernel(w1_ref, w2_ref, y_ref, x_ref, ada0_ref, ada1_ref, kvada_ref, kvgain_ref,
                qgain_ref, knorm_ref, qnorm_ref, wk_ref, wvt_ref, wq_ref, ind_ref, indt_ref,
                xo_ref, q_ref, k_ref, vt_ref):
    tm = x_ref.shape[1]
    xn = x_ref[0] + ada0_ref[0, 5:6, :] * _moe_combined(y_ref, w1_ref, w2_ref, tm)
    xo_ref[0] = xn
    r = _rms(xn)
    hk = ((r * kvgain_ref[...]) * (1.0 + kvada_ref[0, 1:2, :]) + kvada_ref[0, 0:1, :]).astype(BF16)
    hq = ((r * qgain_ref[...]) * (1.0 + ada1_ref[0, 1:2, :]) + ada1_ref[0, 0:1, :]).astype(BF16)
    hd2 = 2 * DIFF_HEAD_DIM
    kk = _dot(hk, wk_ref[...])
    kn = (_group_norm64(kk, ind_ref, indt_ref) * knorm_ref[...]).astype(BF16)
    qq = _dot(hq, wq_ref[...])
    qn = (_group_norm64(qq, ind_ref, indt_ref) * qnorm_ref[...]).astype(BF16)
    for hd in range(DIFF_HEADS):
        k_ref[0, hd] = kn[:, hd * hd2:(hd + 1) * hd2]
        q_ref[0, hd] = qn[:, hd * hd2:(hd + 1) * hd2]
    vt = _dot_nt(wvt_ref[...], hk).astype(BF16)
    for j in range(tm // ATT_TK):
        vt_ref[0, j] = vt[:, j * ATT_TK:(j + 1) * ATT_TK].reshape(DIFF_HEADS, hd2, ATT_TK)


def _kvq(x, y, w1, w2, ada0, ada1, kvada, kv_gain, q_gain, k_norm, q_norm, w_k, w_vt, w_q):
    bsz, seq, d = x.shape
    tm = ROW_TILE
    nt = seq // tm
    lane_group = jnp.arange(d, dtype=jnp.int32) // DIFF_HEAD_DIM
    ind = (lane_group[:, None] == jnp.arange(LANE, dtype=jnp.int32)[None, :]).astype(BF16)
    indt = jnp.concatenate([ind.T, ind.T], axis=0)
    reps = d // DIFF_HEAD_DIM
    q_scale = jnp.tile(q_norm, reps)[None, :] * (DIFF_HEAD_DIM ** -0.5 * math.log2(math.e))
    const = lambda *shape: pl.BlockSpec(shape, lambda b, i: (0,) * len(shape))
    tok = lambda w: pl.BlockSpec((1, tm, w), lambda b, i: (b, i, 0))
    hd2 = 2 * DIFF_HEAD_DIM
    heads = pl.BlockSpec((1, DIFF_HEADS, tm, hd2), lambda b, i: (b, 0, i, 0))
    ada_spec = lambda rows: pl.BlockSpec((1, rows, d), lambda b, i: (b, 0, 0))
    col = lambda a: a.reshape(-1, 1)
    return pl.pallas_call(
        _kvq_kernel,
        out_shape=(jax.ShapeDtypeStruct((bsz, seq, d), F32),
                   jax.ShapeDtypeStruct((bsz, DIFF_HEADS, seq, hd2), BF16),
                   jax.ShapeDtypeStruct((bsz, DIFF_HEADS, seq, hd2), BF16),
                   jax.ShapeDtypeStruct((bsz, seq // ATT_TK, DIFF_HEADS, hd2, ATT_TK), BF16)),
        grid=(bsz, nt),
        in_specs=_combine_specs(tm, lambda b, i: b * nt + i) + [
            tok(d),
            ada_spec(6), ada_spec(6), ada_spec(2),
            const(1, d), const(1, d), const(1, d), const(1, d),
            const(d, d), const(d, d), const(d, d), const(d, LANE), const(2 * LANE, d),
        ],
        out_specs=(tok(d), heads, heads,
                   pl.BlockSpec((1, tm // ATT_TK, DIFF_HEADS, hd2, ATT_TK), lambda b, i: (b, i, 0, 0, 0))),
        compiler_params=_params("arbitrary", "arbitrary"),
        name="kvq",
    )(col(w1), col(w2), y.reshape(TOP_K, -1, LANE), x, ada0, ada1, kvada,
      kv_gain, q_gain, jnp.tile(k_norm, reps)[None, :], q_scale, w_k, w_vt, w_q, ind, indt)


def _attn_kernel(lambda_init, q_ref, k_ref, vt_ref, x_ref, ada_ref, lam_ref, subln_ref, wo_ref, ngain_ref,
                 wr_hi_ref, wr_lo_ref, br_ref, xo_ref, h_ref, lt_ref,
                 s_a, s_b, p_a, p_b, acc, obuf):
    i = pl.program_id(1)
    tq, tk = ATT_TQ, ATT_TK
    hd2 = 2 * DIFF_HEAD_DIM
    n_kblk = k_ref.shape[2] // tk
    lam = lam_ref[...]
    lam_full = (jnp.exp(jnp.sum(lam[0:1] * lam[1:2], axis=-1, keepdims=True))
                - jnp.exp(jnp.sum(lam[2:3] * lam[3:4], axis=-1, keepdims=True)) + lambda_init)
    lane = lax.broadcasted_iota(jnp.int32, (1, hd2), 1)
    keep = (jnp.where(lane < DIFF_HEAD_DIM, 1.0, 0.0).astype(BF16),
            jnp.where(lane >= DIFF_HEAD_DIM, 1.0, 0.0).astype(BF16))
    key = lax.broadcasted_iota(jnp.int32, (tk, tq), 0)
    qry = lax.broadcasted_iota(jnp.int32, (1, tq), 1)
    key_limit = ((i * tq + qry) // MASK_CHUNK + 1) * MASK_CHUNK
    neg_inf = jnp.float32(-jnp.inf)
    n_pairs = (i + 2) // 2
    n_plain = jnp.maximum((i - 1) // 2, 0)

    streams = [(u, t) for u in range(ATT_HEADS) for t in range(2)]

    def head_group(hg, _):
        hds = [hg * ATT_HEADS + u for u in range(ATT_HEADS)]
        qs = []
        for hd in hds:
            qh = q_ref[0, hd]
            qs.append((qh * keep[0], qh * keep[1]))

        def produce_scores(jb, s_out, masked):
            start = pl.multiple_of(jnp.minimum(jb, n_kblk - 1) * tk, tk)
            visible = key < key_limit - jb * tk
            maxima = []
            for n, (u, t) in enumerate(streams):
                st = _dot_nt(k_ref[0, hds[u], pl.ds(start, tk), :], qs[u][t])
                if masked:
                    st = jnp.where(visible, st, neg_inf)
                s_out[n] = st
                maxima.append(jnp.max(st, axis=0, keepdims=True))
            return tuple(maxima)

        def step(j, s_in, s_out, p_in, p_out, state, masked):
            stats, blk_max = state
            jv = jnp.clip(j - 1, 0, n_kblk - 1)
            next_max = produce_scores(j + 1, s_out, masked)
            new_stats = []
            for n, (u, t) in enumerate(streams):
                m, l = stats[n]
                m_new = jnp.maximum(m, blk_max[n])
                alpha = jnp.exp2(m - m_new)
                p = jnp.exp2(s_in[n] - m_new)
                p_out[n] = p.astype(BF16)
                new_stats.append((m_new, alpha * l + jnp.sum(p, axis=0, keepdims=True)))
                acc[n] = alpha * (acc[n] + _dot(vt_ref[0, jv, hds[u]], p_in[n]))
            return tuple(new_stats), next_max

        p_b[...] = jnp.zeros_like(p_b)
        acc[...] = jnp.zeros_like(acc)
        stat0 = (jnp.full((1, tq), neg_inf, F32), jnp.zeros((1, tq), F32))
        state = ((stat0,) * len(streams), produce_scores(0, s_a, True))

        def pair(masked, jj, state):
            state = step(2 * jj, s_a, s_b, p_b, p_a, state, masked)
            return step(2 * jj + 1, s_b, s_a, p_a, p_b, state, masked)

        state = lax.fori_loop(0, n_plain, functools.partial(pair, False), state)
        (stats, _) = lax.fori_loop(n_plain, n_pairs, functools.partial(pair, True), state)
        jv = jnp.minimum(2 * n_pairs - 1, n_kblk - 1)
        for u, hd in enumerate(hds):
            vtb = vt_ref[0, jv, hd]
            a1 = acc[2 * u] + _dot(vtb, p_b[2 * u])
            a2 = acc[2 * u + 1] + _dot(vtb, p_b[2 * u + 1])
            o = a1 * (1.0 / stats[2 * u][1]) - lam_full * (a2 * (1.0 / stats[2 * u + 1][1]))
            o = o * lax.rsqrt(jnp.mean(o * o, axis=0, keepdims=True) + EPS)
            obuf[hd] = (o * (subln_ref[...] * (1.0 - lambda_init))).astype(BF16)
        return 0

    lax.fori_loop(0, DIFF_HEADS // ATT_HEADS, head_group, 0)

    y = _dot_tn(obuf[...].reshape(DIFF_HEADS * hd2, tq), wo_ref[...])
    xn = x_ref[0] + ada_ref[0, 2:3, :] * y
    xo_ref[0] = xn
    h = _ffn_prenorm_logits(xn, ngain_ref[...], ada_ref[0, 3:4, :], ada_ref[0, 4:5, :],
                            wr_hi_ref, wr_lo_ref, br_ref, lt_ref)
    _store_row_tiles(h_ref, (0,), h)


def _attn(q, k, vt, x, ada, lam, subln, w_out, ngain, router, lambda_init):
    bsz, seq, d = x.shape
    tq, tk = ATT_TQ, ATT_TK
    hd2 = 2 * DIFF_HEAD_DIM
    const = lambda *shape: pl.BlockSpec(shape, lambda b, i: (0,) * len(shape))
    tok = lambda w: pl.BlockSpec((1, tq, w), lambda b, i: (b, i, 0))
    nq = seq // tq
    ns = 2 * ATT_HEADS
    return pl.pallas_call(
        functools.partial(_attn_kernel, lambda_init),
        out_shape=(jax.ShapeDtypeStruct((bsz, seq, d), F32),
                   jax.ShapeDtypeStruct((bsz, seq * ROW_TILES, LANE), F32),
                   jax.ShapeDtypeStruct((ROUTER_ROWS, bsz * seq), F32)),
        grid=(bsz, nq),
        in_specs=[
            pl.BlockSpec((1, DIFF_HEADS, tq, hd2), lambda b, i: (b, 0, i, 0)),
            pl.BlockSpec((1, DIFF_HEADS, seq, hd2), lambda b, i: (b, 0, 0, 0)),
            pl.BlockSpec((1, seq // tk, DIFF_HEADS, hd2, tk), lambda b, i: (b, 0, 0, 0, 0)),
            tok(d),
            pl.BlockSpec((1, 6, d), lambda b, i: (b, 0, 0)),
            const(4, DIFF_HEAD_DIM), const(hd2, 1), const(d, d), const(1, d),
        ] + _logit_specs(d),
        out_specs=(tok(d), pl.BlockSpec((1, tq * ROW_TILES, LANE), lambda b, i: (b, i, 0)),
                   pl.BlockSpec((ROUTER_ROWS, tq), lambda b, i: (0, b * nq + i))),
        scratch_shapes=[pltpu.VMEM((ns, tk, tq), F32), pltpu.VMEM((ns, tk, tq), F32),
                        pltpu.VMEM((ns, tk, tq), BF16), pltpu.VMEM((ns, tk, tq), BF16),
                        pltpu.VMEM((ns, hd2, tq), F32),
                        pltpu.VMEM((DIFF_HEADS, hd2, tq), BF16)],
        compiler_params=_params("arbitrary", "arbitrary"),
        name="attn",
    )(q, k, vt, x, ada, lam, subln, w_out, ngain, *router)


def _combine_kernel(w1_ref, w2_ref, y_ref, x_ref, ada_ref, o_ref):
    tm = x_ref.shape[1]
    o_ref[0] = x_ref[0] + ada_ref[0, 5:6, :] * _moe_combined(y_ref, w1_ref, w2_ref, tm)


def _combine(x, y, w1, w2, ada):
    bsz, seq, d = x.shape
    tm = ROW_TILE
    nt = seq // tm
    col = lambda a: a.reshape(-1, 1)
    return pl.pallas_call(
        _combine_kernel,
        out_shape=jax.ShapeDtypeStruct((bsz, seq, d), F32),
        grid=(bsz, nt),
        in_specs=_combine_specs(tm, lambda b, i: b * nt + i) + [
            pl.BlockSpec((1, tm, d), lambda b, i: (b, i, 0)),
            pl.BlockSpec((1, 6, d), lambda b, i: (b, 0, 0)),
        ],
        out_specs=pl.BlockSpec((1, tm, d), lambda b, i: (b, i, 0)),
        compiler_params=_params("arbitrary", "arbitrary"),
        name="combine",
    )(col(w1), col(w2), y.reshape(TOP_K, -1, LANE), x, ada)


def _router_weights(w_group, b_group, w_expert, b_expert):
    d = w_group.shape[0]
    pad = ROUTER_ROWS - N_EXPERTS - N_GROUPS
    w = jnp.concatenate([w_expert.T, w_group.T, jnp.zeros((pad, d), F32)], axis=0)
    b = jnp.concatenate([b_expert, b_group, jnp.zeros((pad,), F32)])[:, None]
    w_hi = w.astype(BF16)
    w_lo = (w - w_hi.astype(F32)).astype(BF16)
    return w_hi, w_lo, b


def _routing_tables(route, cnt):
    rec = lambda r: route[r]
    counts = cnt[:, 0].astype(jnp.int32)
    starts = jnp.cumsum(counts) - counts
    dest = jnp.stack([starts[rec(0).astype(jnp.int32)] + rec(2).astype(jnp.int32),
                      starts[rec(1).astype(jnp.int32)] + rec(3).astype(jnp.int32)], axis=1).reshape(-1)
    _, order = lax.sort((dest, jnp.arange(dest.shape[0], dtype=jnp.int32)), num_keys=1)
    return order, rec(4), rec(5), counts


def kernel(x, c, ada_w, ada_b, norm_mix, norm_ffn, ret_w_in, ret_gn, ret_w_out, kv_ada_w, kv_ada_b, kv_norm, kv_w, k_norm, diff_w_q, q_norm, diff_lam, diff_subln, diff_w_out, moe_w_group, moe_b_group, moe_w_expert, moe_b_expert, moe_w1, moe_w3, moe_w2):
    bsz, seq, d = x.shape
    n_tok = bsz * seq

    ada = _ada(c, ada_w, ada_b).reshape(2, bsz, 6, d)
    kvada = _ada(c, kv_ada_w[None], kv_ada_b[None]).reshape(bsz, 2, d)

    half = RET_DK // 2
    inv_freq = 1.0 / (ROPE_BASE ** (jnp.arange(half, dtype=F32) / half))
    ang = jnp.arange(seq, dtype=F32)[:, None] * inv_freq[None, :]
    cos, sin = jnp.cos(ang), jnp.sin(ang)

    routers = [_router_weights(moe_w_group[l], moe_b_group[l], moe_w_expert[l], moe_b_expert[l])
               for l in range(2)]

    def moe_layer(h, logits_t, layer):
        order, w1, w2, counts = _routing_tables(*_route(logits_t))
        y = _moe(h.reshape(n_tok * ROW_TILES, LANE), order, counts, moe_w1, moe_w3, moe_w2, layer)
        return y, w1, w2

    proj = _ret_in(x, ada[0], norm_mix[0][None, :], cos, sin, ret_w_in[0].astype(BF16))
    x1, h, logits_t = _ret_core(proj, x, ada[0], ret_gn[0][None, :], ret_w_out[0].astype(BF16),
                                norm_ffn[0][None, :], routers[0])
    moe0 = moe_layer(h, logits_t, 0)

    x2, q, k, vt = _kvq(x1, *moe0, ada[0], ada[1], kvada, kv_norm[None, :], norm_mix[1][None, :], k_norm, q_norm[0],
                        kv_w[:, :d].astype(BF16), kv_w[:, d:].T.astype(BF16), diff_w_q[0].astype(BF16))
    lambda_init = 0.8 - 0.6 * math.exp(-0.3 * 1)
    x3, h, logits_t = _attn(q, k, vt, x2, ada[1], diff_lam[0], diff_subln[0][:, None], diff_w_out[0].astype(BF16),
                            norm_ffn[1][None, :], routers[1], lambda_init)
    moe1 = moe_layer(h, logits_t, 1)
    return _combine(x3, *moe1, ada[1])
```

```python
import functools
import math

import jax
import jax.numpy as jnp
from jax import lax
from jax.experimental import pallas as pl
from jax.experimental.pallas import tpu as pltpu

F32 = jnp.float32
BF16 = jnp.bfloat16

D_MODEL = 1024
EPS = 1e-6

RET_HEADS = 4
RET_DK = D_MODEL // RET_HEADS
RET_DV = 2 * RET_DK
RET_QK = RET_HEADS * RET_DK
RET_V = RET_HEADS * RET_DV
RET_IN = 2 * RET_QK + 2 * RET_V
ROPE_BASE = 10000.0
RET_CHUNK = 256

DIFF_HEAD_DIM = 64
DIFF_HEADS = D_MODEL // (2 * DIFF_HEAD_DIM)
MASK_CHUNK = 64
ATT_TQ = 256
ATT_TK = 256
ATT_HEADS = 1

N_GROUPS = 4
EXPERTS_PER_GROUP = 4
N_EXPERTS = N_GROUPS * EXPERTS_PER_GROUP
TOP_K = 2
TOP_K_SHIFT = 1
MOE_BM = 256
MOE_TN = 256
ROUTER_ROWS = 32
ROUTE_ROWS = 8
ROW_TILE = 512
ROUTE_TILE = 2048
ROUTE_SEG = 256

LANE = 128
SUBLANE = 8
VMEM_LIMIT = 56 * 1024 * 1024


def _dot(a, b):
    return jnp.dot(a, b, preferred_element_type=F32)


def _dot_nt(a, b):
    return lax.dot_general(a, b, (((1,), (1,)), ((), ())), preferred_element_type=F32)


def _dot_tn(a, b):
    return lax.dot_general(a, b, (((0,), (0,)), ((), ())), preferred_element_type=F32)


def _silu(x):
    return x * (1.0 / (1.0 + jnp.exp(-x)))


def _rms(x):
    return x * lax.rsqrt(jnp.mean(x * x, axis=-1, keepdims=True) + EPS)


def _params(*sem, flags=None):
    return pltpu.CompilerParams(dimension_semantics=sem, vmem_limit_bytes=VMEM_LIMIT, flags=flags)


ROW_TILES = D_MODEL // LANE


def _load_row_tiles(ref, lead, rows):
    return jnp.concatenate(
        [ref[lead + (pl.ds(c, rows, stride=ROW_TILES), slice(None))] for c in range(ROW_TILES)], axis=1)


def _store_row_tiles(ref, lead, val):
    rows = val.shape[0]
    for c in range(ROW_TILES):
        ref[lead + (pl.ds(c, rows, stride=ROW_TILES), slice(None))] = val[:, c * LANE:(c + 1) * LANE]


def _row_tile(ref, lead, r):
    return ref.at[lead + (pl.ds(pl.multiple_of(r * ROW_TILES, ROW_TILES), ROW_TILES), slice(None))]


def _ada_kernel(c_ref, w_ref, b_ref, o_ref):
    ca = _silu(c_ref[...])
    o_ref[0] = jnp.dot(ca, w_ref[0], preferred_element_type=F32,
                       precision=lax.Precision.HIGHEST) + b_ref[0]


def _ada(c, w, b):
    n_l, d, n = w.shape
    bsz = c.shape[0]
    tn = 1024
    return pl.pallas_call(
        _ada_kernel,
        out_shape=jax.ShapeDtypeStruct((n_l, bsz, n), F32),
        grid=(n_l, n // tn),
        in_specs=[
            pl.BlockSpec((bsz, d), lambda l, j: (0, 0)),
            pl.BlockSpec((1, d, tn), lambda l, j: (l, 0, j)),
            pl.BlockSpec((1, 1, tn), lambda l, j: (l, 0, j)),
        ],
        out_specs=pl.BlockSpec((1, bsz, tn), lambda l, j: (l, 0, j)),
        compiler_params=_params("arbitrary", "arbitrary"),
        name="ada",
    )(c, w, b.reshape(n_l, 1, n))


def _ffn_prenorm_logits(xn, gain, shift, scale, wr_hi_ref, wr_lo_ref, br_ref, lt_ref):
    h = _rms(xn) * gain
    h = h * (1.0 + scale) + shift
    h_hi = h.astype(BF16)
    h_lo = (h - h_hi.astype(F32)).astype(BF16)
    w_hi = wr_hi_ref[...]
    lt_ref[...] = _dot_nt(w_hi, h_hi) + _dot_nt(wr_lo_ref[...], h_hi) + _dot_nt(w_hi, h_lo) + br_ref[...]
    return h


def _route_kernel(lt_ref, tri_ref, route_ref, cnt_ref, carry):
    lt = lt_ref[...]
    c = lt.shape[1]
    seg = tri_ref.shape[0]
    neg_inf = jnp.float32(-jnp.inf)

    grow = lax.broadcasted_iota(jnp.int32, (SUBLANE, c), 0).astype(F32)
    g = jnp.where(grow < N_GROUPS, lt[N_EXPERTS:N_EXPERTS + SUBLANE], neg_inf)
    gmax = jnp.max(g, axis=0, keepdims=True)
    gidx = jnp.min(jnp.where(g == gmax, grow, float(SUBLANE)), axis=0, keepdims=True)
    gate = 1.0 / jnp.sum(jnp.exp(g - gmax), axis=0, keepdims=True)

    erow_i = lax.broadcasted_iota(jnp.int32, (N_EXPERTS, c), 0)
    erow = erow_i.astype(F32)
    egroup = (erow_i // EXPERTS_PER_GROUP).astype(F32)
    el = jnp.where(egroup == gidx, lt[0:N_EXPERTS], neg_inf)
    m1 = jnp.max(el, axis=0, keepdims=True)
    i1 = jnp.min(jnp.where(el == m1, erow, float(N_EXPERTS)), axis=0, keepdims=True)
    el2 = jnp.where(erow == i1, neg_inf, el)
    m2 = jnp.max(el2, axis=0, keepdims=True)
    i2 = jnp.min(jnp.where(el2 == m2, erow, float(N_EXPERTS)), axis=0, keepdims=True)
    t = jnp.exp(m2 - m1)
    den = 1.0 / (1.0 + t)
    w1 = gate * den
    w2 = gate * t * den

    @pl.when(pl.program_id(0) == 0)
    def _():
        carry[...] = jnp.zeros_like(carry)

    oh1 = erow == i1
    oh2 = erow == i2
    oh = jnp.where(jnp.logical_or(oh1, oh2), 1.0, 0.0)
    offset = carry[...]
    before = []
    for s in range(c // seg):
        oh_s = oh[:, s * seg:(s + 1) * seg]
        before.append(offset + _dot(oh_s.astype(BF16), tri_ref[...]))
        offset = offset + jnp.sum(oh_s, axis=1, keepdims=True)
    before = jnp.concatenate(before, axis=1)
    rank1 = jnp.sum(jnp.where(oh1, before, 0.0), axis=0, keepdims=True)
    rank2 = jnp.sum(jnp.where(oh2, before, 0.0), axis=0, keepdims=True)
    carry[...] = offset
    cnt_ref[...] = jnp.broadcast_to(offset, cnt_ref.shape)

    rrow = lax.broadcasted_iota(jnp.int32, (ROUTE_ROWS, c), 0)
    rec = jnp.zeros((ROUTE_ROWS, c), F32)
    for idx, val in enumerate((i1, i2, rank1, rank2, w1, w2)):
        rec = jnp.where(rrow == idx, val, rec)
    route_ref[...] = rec


def _route(lt):
    n_tok = lt.shape[1]
    tile = min(ROUTE_TILE, n_tok)
    t = jnp.arange(ROUTE_SEG, dtype=jnp.int32)
    earlier = (t[:, None] < t[None, :]).astype(BF16)
    return pl.pallas_call(
        _route_kernel,
        out_shape=(jax.ShapeDtypeStruct((ROUTE_ROWS, n_tok), F32),
                   jax.ShapeDtypeStruct((N_EXPERTS, LANE), F32)),
        grid=(n_tok // tile,),
        in_specs=[pl.BlockSpec((ROUTER_ROWS, tile), lambda i: (0, i)),
                  pl.BlockSpec((ROUTE_SEG, ROUTE_SEG), lambda i: (0, 0))],
        out_specs=(pl.BlockSpec((ROUTE_ROWS, tile), lambda i: (0, i)),
                   pl.BlockSpec((N_EXPERTS, LANE), lambda i: (0, 0))),
        scratch_shapes=[pltpu.VMEM((N_EXPERTS, 1), F32)],
        compiler_params=_params("arbitrary"),
        name="route",
    )(lt, earlier)


def _logit_specs(d):
    const = lambda *shape: pl.BlockSpec(shape, lambda b, i: (0,) * len(shape))
    return [const(ROUTER_ROWS, d), const(ROUTER_ROWS, d), const(ROUTER_ROWS, 1)]


def _ret_in_kernel(x_ref, ada_ref, gain_ref, cos_ref, sin_ref, w_ref, o_ref):
    x = x_ref[0]
    shift = ada_ref[0, 0:1, :]
    scale = ada_ref[0, 1:2, :]
    h = (_rms(x) * gain_ref[...]) * (1.0 + scale) + shift
    hb = h.astype(BF16)
    cos = cos_ref[...]
    sin = sin_ref[...]
    half = RET_DK // 2
    for j in range(RET_IN // D_MODEL):
        p = _dot(hb, w_ref[:, j * D_MODEL:(j + 1) * D_MODEL])
        if j < 2:
            post = 1.0 if j == 0 else RET_DK ** -0.5
            for hd in range(RET_HEADS):
                lo = hd * RET_DK
                x1 = p[:, lo:lo + half]
                x2 = p[:, lo + half:lo + RET_DK]
                o_ref[0, :, j * D_MODEL + lo:j * D_MODEL + lo + half] = (
                    (x1 * cos - x2 * sin) * post).astype(BF16)
                o_ref[0, :, j * D_MODEL + lo + half:j * D_MODEL + lo + RET_DK] = (
                    (x2 * cos + x1 * sin) * post).astype(BF16)
        else:
            o_ref[0, :, j * D_MODEL:(j + 1) * D_MODEL] = p.astype(BF16)


def _ret_in(x, ada, gain, cos, sin, w_in):
    bsz, seq, d = x.shape
    tm = ROW_TILE
    return pl.pallas_call(
        _ret_in_kernel,
        out_shape=jax.ShapeDtypeStruct((bsz, seq, RET_IN), BF16),
        grid=(bsz, seq // tm),
        in_specs=[
            pl.BlockSpec((1, tm, d), lambda b, i: (b, i, 0)),
            pl.BlockSpec((1, 6, d), lambda b, i: (b, 0, 0)),
            pl.BlockSpec((1, d), lambda b, i: (0, 0)),
            pl.BlockSpec((tm, RET_DK // 2), lambda b, i: (i, 0)),
            pl.BlockSpec((tm, RET_DK // 2), lambda b, i: (i, 0)),
            pl.BlockSpec((d, RET_IN), lambda b, i: (0, 0), pipeline_mode=pl.Buffered(1)),
        ],
        out_specs=pl.BlockSpec((1, tm, RET_IN), lambda b, i: (b, i, 0)),
        compiler_params=_params("arbitrary", "arbitrary"),
        name="ret_in",
    )(x, ada, gain, cos, sin, w_in)


def _ret_core_kernel(q_ref, k_ref, v_ref, g_ref, x_ref, ada_ref, intra_ref, qd_ref, kd_ref, cd_ref,
                     gn_ref, wo_ref, ngain_ref, wr_hi_ref, wr_lo_ref, br_ref,
                     xo_ref, h_ref, lt_ref, state, ybuf):
    @pl.when(pl.program_id(1) == 0)
    def _():
        state[...] = jnp.zeros_like(state)

    for hd in range(RET_HEADS):
        q = q_ref[0, :, hd * RET_DK:(hd + 1) * RET_DK]
        k = k_ref[0, :, hd * RET_DK:(hd + 1) * RET_DK]
        v = v_ref[0, :, hd * RET_DV:(hd + 1) * RET_DV]
        st = state[hd]
        s = _dot_nt(q, k) * intra_ref[hd]
        o = _dot(s.astype(BF16), v) + qd_ref[hd] * _dot(q, st.astype(BF16))
        kdec = (k.astype(F32) * kd_ref[hd]).astype(BF16)
        state[hd] = st * cd_ref[hd] + _dot_tn(kdec, v)
        o = _rms(o) * gn_ref[:, hd * RET_DV:(hd + 1) * RET_DV]
        g = g_ref[0, :, hd * RET_DV:(hd + 1) * RET_DV].astype(F32)
        ybuf[:, hd * RET_DV:(hd + 1) * RET_DV] = (_silu(g) * o).astype(BF16)

    y = _dot(ybuf[...], wo_ref[...])
    xn = x_ref[0] + ada_ref[0, 2:3, :] * y
    xo_ref[0] = xn
    h = _ffn_prenorm_logits(xn, ngain_ref[...], ada_ref[0, 3:4, :], ada_ref[0, 4:5, :],
                            wr_hi_ref, wr_lo_ref, br_ref, lt_ref)
    _store_row_tiles(h_ref, (0,), h)


def _ret_core(proj, x, ada, gn, w_out, ngain, router):
    bsz, seq, d = x.shape
    c = RET_CHUNK
    f32 = F32
    log_gamma = jnp.log1p(-jnp.exp2(-5.0 - jnp.arange(RET_HEADS, dtype=f32)))
    n = jnp.arange(c, dtype=f32)
    rel = n[:, None] - n[None, :]
    intra = jnp.where(rel >= 0, jnp.exp(jnp.maximum(rel, 0.0)[None] * log_gamma[:, None, None]), 0.0)
    qd = jnp.exp((n + 1.0)[None, :] * log_gamma[:, None])[:, :, None]
    kd = jnp.exp((c - 1.0 - n)[None, :] * log_gamma[:, None])[:, :, None]
    cd = jnp.broadcast_to(jnp.exp(c * log_gamma)[:, None, None], (RET_HEADS, 1, RET_DV))
    const = lambda *shape: pl.BlockSpec(shape, lambda b, i: (0,) * len(shape))
    tok = lambda w, j: pl.BlockSpec((1, c, w), lambda b, i: (b, i, j))
    nc = seq // c
    return pl.pallas_call(
        _ret_core_kernel,
        out_shape=(jax.ShapeDtypeStruct((bsz, seq, d), f32),
                   jax.ShapeDtypeStruct((bsz, seq * ROW_TILES, LANE), f32),
                   jax.ShapeDtypeStruct((ROUTER_ROWS, bsz * seq), f32)),
        grid=(bsz, nc),
        in_specs=[
            tok(RET_QK, 0), tok(RET_QK, 1), tok(RET_V, 1), tok(RET_V, 2),
            tok(d, 0),
            pl.BlockSpec((1, 6, d), lambda b, i: (b, 0, 0)),
            const(RET_HEADS, c, c), const(RET_HEADS, c, 1), const(RET_HEADS, c, 1),
            const(RET_HEADS, 1, RET_DV),
            const(1, RET_V), const(RET_V, d), const(1, d),
        ] + _logit_specs(d),
        out_specs=(tok(d, 0), pl.BlockSpec((1, c * ROW_TILES, LANE), lambda b, i: (b, i, 0)),
                   pl.BlockSpec((ROUTER_ROWS, c), lambda b, i: (0, b * nc + i))),
        scratch_shapes=[pltpu.VMEM((RET_HEADS, RET_DK, RET_DV), f32),
                        pltpu.VMEM((c, RET_V), BF16)],
        compiler_params=_params("arbitrary", "arbitrary"),
        name="ret_core",
    )(proj, proj, proj, proj, x, ada, intra, qd, kd, cd, gn, w_out, ngain, *router)


def _moe_kernel(blk_ref, exp_ref, nitem_ref, start_ref, cnt_ref, src_prv, src_cur, src_nxt, h_hbm,
                w1_ref, w3_ref, w2_ref, y_hbm, xbuf, obuf, w1b, w3b, w2b, gsem, ssem):
    w = pl.program_id(0)
    bm = MOE_BM
    n_item = nitem_ref[0]
    n_tok = h_hbm.shape[0] // ROW_TILES
    d = w1b.shape[0]
    e = exp_ref[w]
    blk = blk_ref[w]
    slot = blk % 2
    prev = jnp.maximum(w - 1, 0)
    first_visit = jnp.logical_or(w == 0, blk_ref[prev] != blk)
    live = w < n_item

    def gather_row(src_ref, s, r):
        tok = src_ref[0, 0, r] >> TOP_K_SHIFT
        pltpu.make_async_copy(_row_tile(h_hbm, (), tok), _row_tile(xbuf, (s,), r), gsem.at[s]).start()

    def scatter_row(src_ref, s, r):
        a = src_ref[0, 0, r]
        dst = (a & (TOP_K - 1)) * n_tok + (a >> TOP_K_SHIFT)
        pltpu.make_async_copy(_row_tile(obuf, (s,), r), _row_tile(y_hbm, (), dst), ssem.at[s]).start(priority=1)

    def rolled(row_copy, src_ref, s):
        def body(r, carry):
            row_copy(src_ref, s, r)
            return carry
        lax.fori_loop(0, bm, body, 0, unroll=8)

    def wait_gather(s):
        pltpu.make_async_copy(h_hbm.at[pl.ds(0, bm * ROW_TILES), :], xbuf.at[s], gsem.at[s]).wait()

    def wait_scatter(s):
        pltpu.make_async_copy(obuf.at[s], y_hbm.at[pl.ds(0, bm * ROW_TILES), :], ssem.at[s]).wait()

    def expert(row_copies):
        pieces = 2 * (d // MOE_TN)
        per_piece = -(-len(row_copies) // pieces)
        groups = [row_copies[g * per_piece:(g + 1) * per_piece] for g in range(pieces)]
        x = _load_row_tiles(xbuf, (slot,), bm).astype(BF16)
        hid = []
        for c in range(d // MOE_TN):
            for issue in groups[c]:
                issue()
            a = _dot(x, w1b[:, c * MOE_TN:(c + 1) * MOE_TN])
            b = _dot(x, w3b[:, c * MOE_TN:(c + 1) * MOE_TN])
            hid.append((_silu(a) * b).astype(BF16))
        hid = jnp.concatenate(hid, axis=1)
        y = []
        for c in range(d // MOE_TN):
            for issue in groups[d // MOE_TN + c]:
                issue()
            y.append(_dot(hid, w2b[:, c * MOE_TN:(c + 1) * MOE_TN]))
        y = jnp.concatenate(y, axis=1)
        row = blk * bm + lax.broadcasted_iota(jnp.int32, (bm, 1), 0)
        mine = jnp.logical_and(row >= start_ref[e], row < start_ref[e] + cnt_ref[e])
        return y, mine

    @pl.when(jnp.logical_and(live, jnp.logical_or(w == 0, exp_ref[prev] != e)))
    def _():
        w1b[...] = w1_ref[0, 0].astype(BF16)
        w3b[...] = w3_ref[0, 0].astype(BF16)
        w2b[...] = w2_ref[0, 0].astype(BF16)

    @pl.when(w == 0)
    def _():
        rolled(gather_row, src_cur, 0)

    @pl.when(jnp.logical_and(live, first_visit))
    def _():
        wait_gather(slot)

        @pl.when(blk >= 2)
        def _():
            wait_scatter(slot)

    gather_next = [functools.partial(gather_row, src_nxt, 1 - slot, r) for r in range(bm)]
    scatter_prev = [functools.partial(scatter_row, src_prv, 1 - slot, r) for r in range(bm)]

    @pl.when(jnp.logical_and(live, jnp.logical_and(first_visit, blk == 0)))
    def _():
        y, mine = expert(gather_next)
        _store_row_tiles(obuf, (slot,), jnp.where(mine, y, 0.0))

    @pl.when(jnp.logical_and(live, jnp.logical_and(first_visit, blk > 0)))
    def _():
        y, mine = expert([f for pair in zip(gather_next, scatter_prev) for f in pair])
        _store_row_tiles(obuf, (slot,), jnp.where(mine, y, 0.0))

    @pl.when(jnp.logical_and(live, jnp.logical_not(first_visit)))
    def _():
        y, mine = expert([])
        _store_row_tiles(obuf, (slot,), jnp.where(mine, y, _load_row_tiles(obuf, (slot,), bm)))

    @pl.when(w == n_item - 1)
    def _():
        rolled(scatter_row, src_cur, slot)
        wait_gather(1 - slot)

        @pl.when(blk >= 1)
        def _():
            wait_scatter(1 - slot)
        wait_scatter(slot)


def _moe(h, order, counts, w1, w3, w2, layer):
    m = order.shape[0]
    d = w1.shape[-1]
    bm = MOE_BM
    n_blocks = m // bm
    n_items = n_blocks + N_EXPERTS - 1
    ends = jnp.cumsum(counts)
    starts = ends - counts
    first_blk = starts // bm
    last_blk = jnp.maximum(ends - 1, starts) // bm
    visits = jnp.where(counts > 0, last_blk - first_blk + 1, 0)
    item_end = jnp.cumsum(visits)
    item_start = item_end - visits
    n_item = item_end[-1]
    w = jnp.minimum(jnp.arange(n_items, dtype=jnp.int32), n_item - 1)
    item_e = jnp.sum(w[:, None] >= item_end[None, :], axis=1).astype(jnp.int32)
    item_blk = (first_blk[item_e] + w - item_start[item_e]).astype(jnp.int32)
    wspec = lambda: pl.BlockSpec((1, 1, d, d), lambda i, blk, e, *_: (layer, e[i], 0, 0))
    idx = lambda f: pl.BlockSpec((1, 1, bm), f, memory_space=pltpu.SMEM)
    order3 = order.reshape(n_blocks, 1, bm)
    return pl.pallas_call(
        _moe_kernel,
        out_shape=jax.ShapeDtypeStruct((m * ROW_TILES, LANE), F32),
        grid_spec=pltpu.PrefetchScalarGridSpec(
            num_scalar_prefetch=5,
            grid=(n_items,),
            in_specs=[idx(lambda i, blk, *_: (jnp.maximum(blk[i] - 1, 0), 0, 0)),
                      idx(lambda i, blk, *_: (blk[i], 0, 0)),
                      idx(lambda i, blk, *_: (jnp.minimum(blk[i] + 1, n_blocks - 1), 0, 0)),
                      pl.BlockSpec(memory_space=pl.ANY), wspec(), wspec(), wspec()],
            out_specs=pl.BlockSpec(memory_space=pl.ANY),
            scratch_shapes=[pltpu.VMEM((2, bm * ROW_TILES, LANE), F32), pltpu.VMEM((2, bm * ROW_TILES, LANE), F32),
                            pltpu.VMEM((d, d), BF16), pltpu.VMEM((d, d), BF16), pltpu.VMEM((d, d), BF16),
                            pltpu.SemaphoreType.DMA((2,)), pltpu.SemaphoreType.DMA((2,))],
        ),
        compiler_params=_params("arbitrary"),
        name="moe",
    )(item_blk, item_e, n_item.astype(jnp.int32).reshape(1), starts.astype(jnp.int32), counts.astype(jnp.int32),
      order3, order3, order3, h, w1, w3, w2)


def _moe_combined(y_ref, w1_ref, w2_ref, tm):
    return w1_ref[...] * _load_row_tiles(y_ref, (0,), tm) + w2_ref[...] * _load_row_tiles(y_ref, (1,), tm)


def _combine_specs(tm, step_of):
    wcol = lambda: pl.BlockSpec((tm, 1), lambda *g: (step_of(*g), 0))
    return [wcol(), wcol(), pl.BlockSpec((TOP_K, tm * ROW_TILES, LANE), lambda *g: (0, step_of(*g), 0))]


def _group_norm64(t, ind_ref, indt_ref):
    ss = _dot((t * t).astype(BF16), ind_ref[...])
    r = lax.rsqrt(ss * (1.0 / DIFF_HEAD_DIM) + EPS)
    r_hi = r.astype(BF16)
    r_lo = (r - r_hi.astype(F32)).astype(BF16)
    rb = _dot(jnp.concatenate([r_hi, r_lo], axis=1), indt_ref[...])
    return t * rb


def _kvq_kernel(w1_ref, w2_ref, y_ref, x_ref, ada0_ref, ada1_ref, kvada_ref, kvgain_ref,
                qgain_ref, knorm_ref, qnorm_ref, wk_ref, wvt_ref, wq_ref, ind_ref, indt_ref,
                xo_ref, q_ref, k_ref, vt_ref):
    tm = x_ref.shape[1]
    xn = x_ref[0] + ada0_ref[0, 5:6, :] * _moe_combined(y_ref, w1_ref, w2_ref, tm)
    xo_ref[0] = xn
    r = _rms(xn)
    hk = ((r * kvgain_ref[...]) * (1.0 + kvada_ref[0, 1:2, :]) + kvada_ref[0, 0:1, :]).astype(BF16)
    hq = ((r * qgain_ref[...]) * (1.0 + ada1_ref[0, 1:2, :]) + ada1_ref[0, 0:1, :]).astype(BF16)
    hd2 = 2 * DIFF_HEAD_DIM
    kk = _dot(hk, wk_ref[...])
    kn = (_group_norm64(kk, ind_ref, indt_ref) * knorm_ref[...]).astype(BF16)
    qq = _dot(hq, wq_ref[...])
    qn = (_group_norm64(qq, ind_ref, indt_ref) * qnorm_ref[...]).astype(BF16)
    for hd in range(DIFF_HEADS):
        k_ref[0, hd] = kn[:, hd * hd2:(hd + 1) * hd2]
        q_ref[0, hd] = qn[:, hd * hd2:(hd + 1) * hd2]
    vt = _dot_nt(wvt_ref[...], hk).astype(BF16)
    for j in range(tm // ATT_TK):
        vt_ref[0, j] = vt[:, j * ATT_TK:(j + 1) * ATT_TK].reshape(DIFF_HEADS, hd2, ATT_TK)


def _kvq(x, y, w1, w2, ada0, ada1, kvada, kv_gain, q_gain, k_norm, q_norm, w_k, w_vt, w_q):
    bsz, seq, d = x.shape
    tm = ROW_TILE
    nt = seq // tm
    lane_group = jnp.arange(d, dtype=jnp.int32) // DIFF_HEAD_DIM
    ind = (lane_group[:, None] == jnp.arange(LANE, dtype=jnp.int32)[None, :]).astype(BF16)
    indt = jnp.concatenate([ind.T, ind.T], axis=0)
    reps = d // DIFF_HEAD_DIM
    q_scale = jnp.tile(q_norm, reps)[None, :] * (DIFF_HEAD_DIM ** -0.5 * math.log2(math.e))
    const = lambda *shape: pl.BlockSpec(shape, lambda b, i: (0,) * len(shape))
    tok = lambda w: pl.BlockSpec((1, tm, w), lambda b, i: (b, i, 0))
    hd2 = 2 * DIFF_HEAD_DIM
    heads = pl.BlockSpec((1, DIFF_HEADS, tm, hd2), lambda b, i: (b, 0, i, 0))
    ada_spec = lambda rows: pl.BlockSpec((1, rows, d), lambda b, i: (b, 0, 0))
    col = lambda a: a.reshape(-1, 1)
    return pl.pallas_call(
        _kvq_kernel,
        out_shape=(jax.ShapeDtypeStruct((bsz, seq, d), F32),
                   jax.ShapeDtypeStruct((bsz, DIFF_HEADS, seq, hd2), BF16),
                   jax.ShapeDtypeStruct((bsz, DIFF_HEADS, seq, hd2), BF16),
                   jax.ShapeDtypeStruct((bsz, seq // ATT_TK, DIFF_HEADS, hd2, ATT_TK), BF16)),
        grid=(bsz, nt),
        in_specs=_combine_specs(tm, lambda b, i: b * nt + i) + [
            tok(d),
            ada_spec(6), ada_spec(6), ada_spec(2),
            const(1, d), const(1, d), const(1, d), const(1, d),
            const(d, d), const(d, d), const(d, d), const(d, LANE), const(2 * LANE, d),
        ],
        out_specs=(tok(d), heads, heads,
                   pl.BlockSpec((1, tm // ATT_TK, DIFF_HEADS, hd2, ATT_TK), lambda b, i: (b, i, 0, 0, 0))),
        compiler_params=_params("arbitrary", "arbitrary"),
        name="kvq",
    )(col(w1), col(w2), y.reshape(TOP_K, -1, LANE), x, ada0, ada1, kvada,
      kv_gain, q_gain, jnp.tile(k_norm, reps)[None, :], q_scale, w_k, w_vt, w_q, ind, indt)


def _attn_kernel(lambda_init, q_ref, k_ref, vt_ref, x_ref, ada_ref, lam_ref, subln_ref, wo_ref, ngain_ref,
                 wr_hi_ref, wr_lo_ref, br_ref, xo_ref, h_ref, lt_ref,
                 s_a, s_b, p_a, p_b, acc, obuf):
    i = pl.program_id(1)
    tq, tk = ATT_TQ, ATT_TK
    hd2 = 2 * DIFF_HEAD_DIM
    n_kblk = k_ref.shape[2] // tk
    lam = lam_ref[...]
    lam_full = (jnp.exp(jnp.sum(lam[0:1] * lam[1:2], axis=-1, keepdims=True))
                - jnp.exp(jnp.sum(lam[2:3] * lam[3:4], axis=-1, keepdims=True)) + lambda_init)
    lane = lax.broadcasted_iota(jnp.int32, (1, hd2), 1)
    keep = (jnp.where(lane < DIFF_HEAD_DIM, 1.0, 0.0).astype(BF16),
            jnp.where(lane >= DIFF_HEAD_DIM, 1.0, 0.0).astype(BF16))
    key = lax.broadcasted_iota(jnp.int32, (tk, tq), 0)
    qry = lax.broadcasted_iota(jnp.int32, (1, tq), 1)
    key_limit = ((i * tq + qry) // MASK_CHUNK + 1) * MASK_CHUNK
    neg_inf = jnp.float32(-jnp.inf)
    n_pairs = (i + 2) // 2
    n_plain = jnp.maximum((i - 1) // 2, 0)

    streams = [(u, t) for u in range(ATT_HEADS) for t in range(2)]

    def head_group(hg, _):
        hds = [hg * ATT_HEADS + u for u in range(ATT_HEADS)]
        qs = []
        for hd in hds:
            qh = q_ref[0, hd]
            qs.append((qh * keep[0], qh * keep[1]))

        def produce_scores(jb, s_out, masked):
            start = pl.multiple_of(jnp.minimum(jb, n_kblk - 1) * tk, tk)
            visible = key < key_limit - jb * tk
            maxima = []
            for n, (u, t) in enumerate(streams):
                st = _dot_nt(k_ref[0, hds[u], pl.ds(start, tk), :], qs[u][t])
                if masked:
                    st = jnp.where(visible, st, neg_inf)
                s_out[n] = st
                maxima.append(jnp.max(st, axis=0, keepdims=True))
            return tuple(maxima)

        def step(j, s_in, s_out, p_in, p_out, state, masked):
            stats, blk_max = state
            jv = jnp.clip(j - 1, 0, n_kblk - 1)
            next_max = produce_scores(j + 1, s_out, masked)
            new_stats = []
            for n, (u, t) in enumerate(streams):
                m, l = stats[n]
                m_new = jnp.maximum(m, blk_max[n])
                alpha = jnp.exp2(m - m_new)
                p = jnp.exp2(s_in[n] - m_new)
                p_out[n] = p.astype(BF16)
                new_stats.append((m_new, alpha * l + jnp.sum(p, axis=0, keepdims=True)))
                acc[n] = alpha * (acc[n] + _dot(vt_ref[0, jv, hds[u]], p_in[n]))
            return tuple(new_stats), next_max

        p_b[...] = jnp.zeros_like(p_b)
        acc[...] = jnp.zeros_like(acc)
        stat0 = (jnp.full((1, tq), neg_inf, F32), jnp.zeros((1, tq), F32))
        state = ((stat0,) * len(streams), produce_scores(0, s_a, True))

        def pair(masked, jj, state):
            state = step(2 * jj, s_a, s_b, p_b, p_a, state, masked)
            return step(2 * jj + 1, s_b, s_a, p_a, p_b, state, masked)

        state = lax.fori_loop(0, n_plain, functools.partial(pair, False), state)
        (stats, _) = lax.fori_loop(n_plain, n_pairs, functools.partial(pair, True), state)
        jv = jnp.minimum(2 * n_pairs - 1, n_kblk - 1)
        for u, hd in enumerate(hds):
            vtb = vt_ref[0, jv, hd]
            a1 = acc[2 * u] + _dot(vtb, p_b[2 * u])
            a2 = acc[2 * u + 1] + _dot(vtb, p_b[2 * u + 1])
            o = a1 * (1.0 / stats[2 * u][1]) - lam_full * (a2 * (1.0 / stats[2 * u + 1][1]))
            o = o * lax.rsqrt(jnp.mean(o * o, axis=0, keepdims=True) + EPS)
            obuf[hd] = (o * (subln_ref[...] * (1.0 - lambda_init))).astype(BF16)
        return 0

    lax.fori_loop(0, DIFF_HEADS // ATT_HEADS, head_group, 0)

    y = _dot_tn(obuf[...].reshape(DIFF_HEADS * hd2, tq), wo_ref[...])
    xn = x_ref[0] + ada_ref[0, 2:3, :] * y
    xo_ref[0] = xn
    h = _ffn_prenorm_logits(xn, ngain_ref[...], ada_ref[0, 3:4, :], ada_ref[0, 4:5, :],
                            wr_hi_ref, wr_lo_ref, br_ref, lt_ref)
    _store_row_tiles(h_ref, (0,), h)


def _attn(q, k, vt, x, ada, lam, subln, w_out, ngain, router, lambda_init):
    bsz, seq, d = x.shape
    tq, tk = ATT_TQ, ATT_TK
    hd2 = 2 * DIFF_HEAD_DIM
    const = lambda *shape: pl.BlockSpec(shape, lambda b, i: (0,) * len(shape))
    tok = lambda w: pl.BlockSpec((1, tq, w), lambda b, i: (b, i, 0))
    nq = seq // tq
    ns = 2 * ATT_HEADS
    return pl.pallas_call(
        functools.partial(_attn_kernel, lambda_init),
        out_shape=(jax.ShapeDtypeStruct((bsz, seq, d), F32),
                   jax.ShapeDtypeStruct((bsz, seq * ROW_TILES, LANE), F32),
                   jax.ShapeDtypeStruct((ROUTER_ROWS, bsz * seq), F32)),
        grid=(bsz, nq),
        in_specs=[
            pl.BlockSpec((1, DIFF_HEADS, tq, hd2), lambda b, i: (b, 0, i, 0)),
            pl.BlockSpec((1, DIFF_HEADS, seq, hd2), lambda b, i: (b, 0, 0, 0)),
            pl.BlockSpec((1, seq // tk, DIFF_HEADS, hd2, tk), lambda b, i: (b, 0, 0, 0, 0)),
            tok(d),
            pl.BlockSpec((1, 6, d), lambda b, i: (b, 0, 0)),
            const(4, DIFF_HEAD_DIM), const(hd2, 1), const(d, d), const(1, d),
        ] + _logit_specs(d),
        out_specs=(tok(d), pl.BlockSpec((1, tq * ROW_TILES, LANE), lambda b, i: (b, i, 0)),
                   pl.BlockSpec((ROUTER_ROWS, tq), lambda b, i: (0, b * nq + i))),
        scratch_shapes=[pltpu.VMEM((ns, tk, tq), F32), pltpu.VMEM((ns, tk, tq), F32),
                        pltpu.VMEM((ns, tk, tq), BF16), pltpu.VMEM((ns, tk, tq), BF16),
                        pltpu.VMEM((ns, hd2, tq), F32),
                        pltpu.VMEM((DIFF_HEADS, hd2, tq), BF16)],
        compiler_params=_params("arbitrary", "arbitrary"),
        name="attn",
    )(q, k, vt, x, ada, lam, subln, w_out, ngain, *router)


def _combine_kernel(w1_ref, w2_ref, y_ref, x_ref, ada_ref, o_ref):
    tm = x_ref.shape[1]
    o_ref[0] = x_ref[0] + ada_ref[0, 5:6, :] * _moe_combined(y_ref, w1_ref, w2_ref, tm)


def _combine(x, y, w1, w2, ada):
    bsz, seq, d = x.shape
    tm = ROW_TILE
    nt = seq // tm
    col = lambda a: a.reshape(-1, 1)
    return pl.pallas_call(
        _combine_kernel,
        out_shape=jax.ShapeDtypeStruct((bsz, seq, d), F32),
        grid=(bsz, nt),
        in_specs=_combine_specs(tm, lambda b, i: b * nt + i) + [
            pl.BlockSpec((1, tm, d), lambda b, i: (b, i, 0)),
            pl.BlockSpec((1, 6, d), lambda b, i: (b, 0, 0)),
        ],
        out_specs=pl.BlockSpec((1, tm, d), lambda b, i: (b, i, 0)),
        compiler_params=_params("arbitrary", "arbitrary"),
        name="combine",
    )(col(w1), col(w2), y.reshape(TOP_K, -1, LANE), x, ada)


def _router_weights(w_group, b_group, w_expert, b_expert):
    d = w_group.shape[0]
    pad = ROUTER_ROWS - N_EXPERTS - N_GROUPS
    w = jnp.concatenate([w_expert.T, w_group.T, jnp.zeros((pad, d), F32)], axis=0)
    b = jnp.concatenate([b_expert, b_group, jnp.zeros((pad,), F32)])[:, None]
    w_hi = w.astype(BF16)
    w_lo = (w - w_hi.astype(F32)).astype(BF16)
    return w_hi, w_lo, b


def _routing_tables(route, cnt):
    rec = lambda r: route[r]
    counts = cnt[:, 0].astype(jnp.int32)
    starts = jnp.cumsum(counts) - counts
    dest = jnp.stack([starts[rec(0).astype(jnp.int32)] + rec(2).astype(jnp.int32),
                      starts[rec(1).astype(jnp.int32)] + rec(3).astype(jnp.int32)], axis=1).reshape(-1)
    _, order = lax.sort((dest, jnp.arange(dest.shape[0], dtype=jnp.int32)), num_keys=1)
    return order, rec(4), rec(5), counts


def kernel(x, c, ada_w, ada_b, norm_mix, norm_ffn, ret_w_in, ret_gn, ret_w_out, kv_ada_w, kv_ada_b, kv_norm, kv_w, k_norm, diff_w_q, q_norm, diff_lam, diff_subln, diff_w_out, moe_w_group, moe_b_group, moe_w_expert, moe_b_expert, moe_w1, moe_w3, moe_w2):
    bsz, seq, d = x.shape
    n_tok = bsz * seq

    ada = _ada(c, ada_w, ada_b).reshape(2, bsz, 6, d)
    kvada = _ada(c, kv_ada_w[None], kv_ada_b[None]).reshape(bsz, 2, d)

    half = RET_DK // 2
    inv_freq = 1.0 / (ROPE_BASE ** (jnp.arange(half, dtype=F32) / half))
    ang = jnp.arange(seq, dtype=F32)[:, None] * inv_freq[None, :]
    cos, sin = jnp.cos(ang), jnp.sin(ang)

    routers = [_router_weights(moe_w_group[l], moe_b_group[l], moe_w_expert[l], moe_b_expert[l])
               for l in range(2)]

    def moe_layer(h, logits_t, layer):
        order, w1, w2, counts = _routing_tables(*_route(logits_t))
        y = _moe(h.reshape(n_tok * ROW_TILES, LANE), order, counts, moe_w1, moe_w3, moe_w2, layer)
        return y, w1, w2

    proj = _ret_in(x, ada[0], norm_mix[0][None, :], cos, sin, ret_w_in[0].astype(BF16))
    x1, h, logits_t = _ret_core(proj, x, ada[0], ret_gn[0][None, :], ret_w_out[0].astype(BF16),
                                norm_ffn[0][None, :], routers[0])
    moe0 = moe_layer(h, logits_t, 0)

    x2, q, k, vt = _kvq(x1, *moe0, ada[0], ada[1], kvada, kv_norm[None, :], norm_mix[1][None, :], k_norm, q_norm[0],
                        kv_w[:, :d].astype(BF16), kv_w[:, d:].T.astype(BF16), diff_w_q[0].astype(BF16))
    lambda_init = 0.8 - 0.6 * math.exp(-0.3 * 1)
    x3, h, logits_t = _attn(q, k, vt, x2, ada[1], diff_lam[0], diff_subln[0][:, None], diff_w_out[0].astype(BF16),
                            norm_ffn[1][None, :], routers[1], lambda_init)
    moe1 = moe_layer(h, logits_t, 1)
    return _combine(x3, *moe1, ada[1])
```

```python
import functools
import math

import jax
import jax.numpy as jnp
from jax import lax
from jax.experimental import pallas as pl
from jax.experimental.pallas import tpu as pltpu

F32 = jnp.float32
BF16 = jnp.bfloat16

D_MODEL = 1024
EPS = 1e-6

RET_HEADS = 4
RET_DK = D_MODEL // RET_HEADS
RET_DV = 2 * RET_DK
RET_QK = RET_HEADS * RET_DK
RET_V = RET_HEADS * RET_DV
RET_IN = 2 * RET_QK + 2 * RET_V
ROPE_BASE = 10000.0
RET_CHUNK = 256

DIFF_HEAD_DIM = 64
DIFF_HEADS = D_MODEL // (2 * DIFF_HEAD_DIM)
MASK_CHUNK = 64
ATT_TQ = 256
ATT_TK = 256
ATT_HEADS = 1

N_GROUPS = 4
EXPERTS_PER_GROUP = 4
N_EXPERTS = N_GROUPS * EXPERTS_PER_GROUP
TOP_K = 2
TOP_K_SHIFT = 1
MOE_BM = 256
MOE_TN = 256
KVQ_TN = 256
ROUTER_ROWS = 32
ROUTE_ROWS = 8
ROW_TILE = 512
ROUTE_TILE = 2048
ROUTE_SEG = 256

LANE = 128
SUBLANE = 8
VMEM_LIMIT = 56 * 1024 * 1024


def _dot(a, b):
    return jnp.dot(a, b, preferred_element_type=F32)


def _dot_nt(a, b):
    return lax.dot_general(a, b, (((1,), (1,)), ((), ())), preferred_element_type=F32)


def _dot_tn(a, b):
    return lax.dot_general(a, b, (((0,), (0,)), ((), ())), preferred_element_type=F32)


def _silu(x):
    return x * (1.0 / (1.0 + jnp.exp(-x)))


def _rms(x):
    return x * lax.rsqrt(jnp.mean(x * x, axis=-1, keepdims=True) + EPS)


def _params(*sem, flags=None):
    return pltpu.CompilerParams(dimension_semantics=sem, vmem_limit_bytes=VMEM_LIMIT, flags=flags)


ROW_TILES = D_MODEL // LANE


def _load_row_tiles(ref, lead, rows):
    return jnp.concatenate(
        [ref[lead + (pl.ds(c, rows, stride=ROW_TILES), slice(None))] for c in range(ROW_TILES)], axis=1)


def _store_row_tiles(ref, lead, val):
    rows = val.shape[0]
    for c in range(ROW_TILES):
        ref[lead + (pl.ds(c, rows, stride=ROW_TILES), slice(None))] = val[:, c * LANE:(c + 1) * LANE]


def _row_tile(ref, lead, r):
    return ref.at[lead + (pl.ds(pl.multiple_of(r * ROW_TILES, ROW_TILES), ROW_TILES), slice(None))]


def _ada_kernel(c_ref, w_ref, b_ref, o_ref):
    ca = _silu(c_ref[...])
    o_ref[0] = jnp.dot(ca, w_ref[0], preferred_element_type=F32,
                       precision=lax.Precision.HIGHEST) + b_ref[0]


def _ada(c, w, b):
    n_l, d, n = w.shape
    bsz = c.shape[0]
    tn = 1024
    return pl.pallas_call(
        _ada_kernel,
        out_shape=jax.ShapeDtypeStruct((n_l, bsz, n), F32),
        grid=(n_l, n // tn),
        in_specs=[
            pl.BlockSpec((bsz, d), lambda l, j: (0, 0)),
            pl.BlockSpec((1, d, tn), lambda l, j: (l, 0, j)),
            pl.BlockSpec((1, 1, tn), lambda l, j: (l, 0, j)),
        ],
        out_specs=pl.BlockSpec((1, bsz, tn), lambda l, j: (l, 0, j)),
        compiler_params=_params("arbitrary", "arbitrary"),
        name="ada",
    )(c, w, b.reshape(n_l, 1, n))


def _ffn_prenorm_logits(xn, gain, shift, scale, wr_hi_ref, wr_lo_ref, br_ref, lt_ref):
    h = _rms(xn) * gain
    h = h * (1.0 + scale) + shift
    h_hi = h.astype(BF16)
    h_lo = (h - h_hi.astype(F32)).astype(BF16)
    w_hi = wr_hi_ref[...]
    lt_ref[...] = _dot_nt(w_hi, h_hi) + _dot_nt(wr_lo_ref[...], h_hi) + _dot_nt(w_hi, h_lo) + br_ref[...]
    return h


def _route_kernel(lt_ref, tri_ref, route_ref, cnt_ref, carry):
    lt = lt_ref[...]
    c = lt.shape[1]
    seg = tri_ref.shape[0]
    neg_inf = jnp.float32(-jnp.inf)

    grow = lax.broadcasted_iota(jnp.int32, (SUBLANE, c), 0).astype(F32)
    g = jnp.where(grow < N_GROUPS, lt[N_EXPERTS:N_EXPERTS + SUBLANE], neg_inf)
    gmax = jnp.max(g, axis=0, keepdims=True)
    gidx = jnp.min(jnp.where(g == gmax, grow, float(SUBLANE)), axis=0, keepdims=True)
    gate = 1.0 / jnp.sum(jnp.exp(g - gmax), axis=0, keepdims=True)

    erow_i = lax.broadcasted_iota(jnp.int32, (N_EXPERTS, c), 0)
    erow = erow_i.astype(F32)
    egroup = (erow_i // EXPERTS_PER_GROUP).astype(F32)
    el = jnp.where(egroup == gidx, lt[0:N_EXPERTS], neg_inf)
    m1 = jnp.max(el, axis=0, keepdims=True)
    i1 = jnp.min(jnp.where(el == m1, erow, float(N_EXPERTS)), axis=0, keepdims=True)
    el2 = jnp.where(erow == i1, neg_inf, el)
    m2 = jnp.max(el2, axis=0, keepdims=True)
    i2 = jnp.min(jnp.where(el2 == m2, erow, float(N_EXPERTS)), axis=0, keepdims=True)
    t = jnp.exp(m2 - m1)
    den = 1.0 / (1.0 + t)
    w1 = gate * den
    w2 = gate * t * den

    @pl.when(pl.program_id(0) == 0)
    def _():
        carry[...] = jnp.zeros_like(carry)

    oh1 = erow == i1
    oh2 = erow == i2
    oh = jnp.where(jnp.logical_or(oh1, oh2), 1.0, 0.0)
    offset = carry[...]
    before = []
    for s in range(c // seg):
        oh_s = oh[:, s * seg:(s + 1) * seg]
        before.append(offset + _dot(oh_s.astype(BF16), tri_ref[...]))
        offset = offset + jnp.sum(oh_s, axis=1, keepdims=True)
    before = jnp.concatenate(before, axis=1)
    rank1 = jnp.sum(jnp.where(oh1, before, 0.0), axis=0, keepdims=True)
    rank2 = jnp.sum(jnp.where(oh2, before, 0.0), axis=0, keepdims=True)
    carry[...] = offset
    cnt_ref[...] = jnp.broadcast_to(offset, cnt_ref.shape)

    rrow = lax.broadcasted_iota(jnp.int32, (ROUTE_ROWS, c), 0)
    rec = jnp.zeros((ROUTE_ROWS, c), F32)
    for idx, val in enumerate((i1, i2, rank1, rank2, w1, w2)):
        rec = jnp.where(rrow == idx, val, rec)
    route_ref[...] = rec


def _route(lt):
    n_tok = lt.shape[1]
    tile = min(ROUTE_TILE, n_tok)
    t = jnp.arange(ROUTE_SEG, dtype=jnp.int32)
    earlier = (t[:, None] < t[None, :]).astype(BF16)
    return pl.pallas_call(
        _route_kernel,
        out_shape=(jax.ShapeDtypeStruct((ROUTE_ROWS, n_tok), F32),
                   jax.ShapeDtypeStruct((N_EXPERTS, LANE), F32)),
        grid=(n_tok // tile,),
        in_specs=[pl.BlockSpec((ROUTER_ROWS, tile), lambda i: (0, i)),
                  pl.BlockSpec((ROUTE_SEG, ROUTE_SEG), lambda i: (0, 0))],
        out_specs=(pl.BlockSpec((ROUTE_ROWS, tile), lambda i: (0, i)),
                   pl.BlockSpec((N_EXPERTS, LANE), lambda i: (0, 0))),
        scratch_shapes=[pltpu.VMEM((N_EXPERTS, 1), F32)],
        compiler_params=_params("arbitrary"),
        name="route",
    )(lt, earlier)


def _logit_specs(d):
    const = lambda *shape: pl.BlockSpec(shape, lambda b, i: (0,) * len(shape))
    return [const(ROUTER_ROWS, d), const(ROUTER_ROWS, d), const(ROUTER_ROWS, 1)]


def _ret_in_kernel(x_ref, ada_ref, gain_ref, cos_ref, sin_ref, w_ref, o_ref):
    x = x_ref[0]
    shift = ada_ref[0, 0:1, :]
    scale = ada_ref[0, 1:2, :]
    h = (_rms(x) * gain_ref[...]) * (1.0 + scale) + shift
    hb = h.astype(BF16)
    cos = cos_ref[...]
    sin = sin_ref[...]
    half = RET_DK // 2
    for j in range(RET_IN // D_MODEL):
        p = _dot(hb, w_ref[:, j * D_MODEL:(j + 1) * D_MODEL])
        if j < 2:
            post = 1.0 if j == 0 else RET_DK ** -0.5
            for hd in range(RET_HEADS):
                lo = hd * RET_DK
                x1 = p[:, lo:lo + half]
                x2 = p[:, lo + half:lo + RET_DK]
                o_ref[0, :, j * D_MODEL + lo:j * D_MODEL + lo + half] = (
                    (x1 * cos - x2 * sin) * post).astype(BF16)
                o_ref[0, :, j * D_MODEL + lo + half:j * D_MODEL + lo + RET_DK] = (
                    (x2 * cos + x1 * sin) * post).astype(BF16)
        else:
            o_ref[0, :, j * D_MODEL:(j + 1) * D_MODEL] = p.astype(BF16)


def _ret_in(x, ada, gain, cos, sin, w_in):
    bsz, seq, d = x.shape
    tm = ROW_TILE
    return pl.pallas_call(
        _ret_in_kernel,
        out_shape=jax.ShapeDtypeStruct((bsz, seq, RET_IN), BF16),
        grid=(bsz, seq // tm),
        in_specs=[
            pl.BlockSpec((1, tm, d), lambda b, i: (b, i, 0)),
            pl.BlockSpec((1, 6, d), lambda b, i: (b, 0, 0)),
            pl.BlockSpec((1, d), lambda b, i: (0, 0)),
            pl.BlockSpec((tm, RET_DK // 2), lambda b, i: (i, 0)),
            pl.BlockSpec((tm, RET_DK // 2), lambda b, i: (i, 0)),
            pl.BlockSpec((d, RET_IN), lambda b, i: (0, 0), pipeline_mode=pl.Buffered(1)),
        ],
        out_specs=pl.BlockSpec((1, tm, RET_IN), lambda b, i: (b, i, 0)),
        compiler_params=_params("arbitrary", "arbitrary"),
        name="ret_in",
    )(x, ada, gain, cos, sin, w_in)


def _ret_core_kernel(q_ref, k_ref, v_ref, g_ref, x_ref, ada_ref, intra_ref, qd_ref, kd_ref, cd_ref,
                     gn_ref, wo_ref, ngain_ref, wr_hi_ref, wr_lo_ref, br_ref,
                     xo_ref, h_ref, lt_ref, state, ybuf):
    @pl.when(pl.program_id(1) == 0)
    def _():
        state[...] = jnp.zeros_like(state)

    for hd in range(RET_HEADS):
        q = q_ref[0, :, hd * RET_DK:(hd + 1) * RET_DK]
        k = k_ref[0, :, hd * RET_DK:(hd + 1) * RET_DK]
        v = v_ref[0, :, hd * RET_DV:(hd + 1) * RET_DV]
        st = state[hd]
        s = _dot_nt(q, k) * intra_ref[hd]
        o = _dot(s.astype(BF16), v) + qd_ref[hd] * _dot(q, st.astype(BF16))
        kdec = (k.astype(F32) * kd_ref[hd]).astype(BF16)
        state[hd] = st * cd_ref[hd] + _dot_tn(kdec, v)
        o = _rms(o) * gn_ref[:, hd * RET_DV:(hd + 1) * RET_DV]
        g = g_ref[0, :, hd * RET_DV:(hd + 1) * RET_DV].astype(F32)
        ybuf[:, hd * RET_DV:(hd + 1) * RET_DV] = (_silu(g) * o).astype(BF16)

    y = _dot(ybuf[...], wo_ref[...])
    xn = x_ref[0] + ada_ref[0, 2:3, :] * y
    xo_ref[0] = xn
    h = _ffn_prenorm_logits(xn, ngain_ref[...], ada_ref[0, 3:4, :], ada_ref[0, 4:5, :],
                            wr_hi_ref, wr_lo_ref, br_ref, lt_ref)
    _store_row_tiles(h_ref, (0,), h)


def _ret_core(proj, x, ada, gn, w_out, ngain, router):
    bsz, seq, d = x.shape
    c = RET_CHUNK
    f32 = F32
    log_gamma = jnp.log1p(-jnp.exp2(-5.0 - jnp.arange(RET_HEADS, dtype=f32)))
    n = jnp.arange(c, dtype=f32)
    rel = n[:, None] - n[None, :]
    intra = jnp.where(rel >= 0, jnp.exp(jnp.maximum(rel, 0.0)[None] * log_gamma[:, None, None]), 0.0)
    qd = jnp.exp((n + 1.0)[None, :] * log_gamma[:, None])[:, :, None]
    kd = jnp.exp((c - 1.0 - n)[None, :] * log_gamma[:, None])[:, :, None]
    cd = jnp.broadcast_to(jnp.exp(c * log_gamma)[:, None, None], (RET_HEADS, 1, RET_DV))
    const = lambda *shape: pl.BlockSpec(shape, lambda b, i: (0,) * len(shape))
    tok = lambda w, j: pl.BlockSpec((1, c, w), lambda b, i: (b, i, j))
    nc = seq // c
    return pl.pallas_call(
        _ret_core_kernel,
        out_shape=(jax.ShapeDtypeStruct((bsz, seq, d), f32),
                   jax.ShapeDtypeStruct((bsz, seq * ROW_TILES, LANE), f32),
                   jax.ShapeDtypeStruct((ROUTER_ROWS, bsz * seq), f32)),
        grid=(bsz, nc),
        in_specs=[
            tok(RET_QK, 0), tok(RET_QK, 1), tok(RET_V, 1), tok(RET_V, 2),
            tok(d, 0),
            pl.BlockSpec((1, 6, d), lambda b, i: (b, 0, 0)),
            const(RET_HEADS, c, c), const(RET_HEADS, c, 1), const(RET_HEADS, c, 1),
            const(RET_HEADS, 1, RET_DV),
            const(1, RET_V), const(RET_V, d), const(1, d),
        ] + _logit_specs(d),
        out_specs=(tok(d, 0), pl.BlockSpec((1, c * ROW_TILES, LANE), lambda b, i: (b, i, 0)),
                   pl.BlockSpec((ROUTER_ROWS, c), lambda b, i: (0, b * nc + i))),
        scratch_shapes=[pltpu.VMEM((RET_HEADS, RET_DK, RET_DV), f32),
                        pltpu.VMEM((c, RET_V), BF16)],
        compiler_params=_params("arbitrary", "arbitrary"),
        name="ret_core",
    )(proj, proj, proj, proj, x, ada, intra, qd, kd, cd, gn, w_out, ngain, *router)


def _moe_kernel(blk_ref, exp_ref, nitem_ref, start_ref, cnt_ref, tok_cur, tok_nxt, h_hbm,
                w1_ref, w3_ref, w2_ref, o_ref, xbuf, w1b, w3b, w2b, gsem):
    w = pl.program_id(0)
    bm = MOE_BM
    n_item = nitem_ref[0]
    d = w1b.shape[0]
    e = exp_ref[w]
    blk = blk_ref[w]
    slot = blk % 2
    prev = jnp.maximum(w - 1, 0)
    first_visit = jnp.logical_or(w == 0, blk_ref[prev] != blk)
    live = w < n_item

    def gather_row(tok_ref, s, r):
        pltpu.make_async_copy(_row_tile(h_hbm, (), tok_ref[0, 0, r]), _row_tile(xbuf, (s,), r), gsem.at[s]).start()

    def wait_gather(s):
        pltpu.make_async_copy(h_hbm.at[pl.ds(0, bm * ROW_TILES), :], xbuf.at[s], gsem.at[s]).wait()

    def expert(row_copies):
        n_dots = 2 * (d // MOE_TN)
        per_dot = -(-len(row_copies) // n_dots)
        groups = iter([row_copies[g * per_dot:(g + 1) * per_dot] for g in range(n_dots)])

        def piece(lhs, w_ref, c):
            for issue in next(groups, ()):
                issue()
            return _dot(lhs, w_ref[:, c * MOE_TN:(c + 1) * MOE_TN])

        x = _load_row_tiles(xbuf, (slot,), bm).astype(BF16)
        hid = []
        for c in range(d // MOE_TN):
            a = piece(x, w1b, c)
            b = piece(x, w3b, c)
            hid.append((_silu(a) * b).astype(BF16))
        hid = jnp.concatenate(hid, axis=1)
        y = jnp.concatenate([piece(hid, w2b, c) for c in range(d // MOE_TN)], axis=1)
        row = blk * bm + lax.broadcasted_iota(jnp.int32, (bm, 1), 0)
        mine = jnp.logical_and(row >= start_ref[e], row < start_ref[e] + cnt_ref[e])
        return y, mine

    @pl.when(jnp.logical_and(live, jnp.logical_or(w == 0, exp_ref[prev] != e)))
    def _():
        w1b[...] = w1_ref[0, 0].astype(BF16)
        w3b[...] = w3_ref[0, 0].astype(BF16)
        w2b[...] = w2_ref[0, 0].astype(BF16)

    @pl.when(w == 0)
    def _():
        def body(r, carry):
            gather_row(tok_cur, 0, r)
            return carry
        lax.fori_loop(0, bm, body, 0, unroll=8)

    @pl.when(jnp.logical_and(live, first_visit))
    def _():
        wait_gather(slot)
        y, mine = expert([functools.partial(gather_row, tok_nxt, 1 - slot, r) for r in range(bm)])
        _store_row_tiles(o_ref, (), jnp.where(mine, y, 0.0))

    @pl.when(jnp.logical_and(live, jnp.logical_not(first_visit)))
    def _():
        y, mine = expert([])
        _store_row_tiles(o_ref, (), jnp.where(mine, y, _load_row_tiles(o_ref, (), bm)))

    @pl.when(w == n_item - 1)
    def _():
        wait_gather(1 - slot)


def _moe(h, order, counts, w1, w3, w2, layer):
    m = order.shape[0]
    d = w1.shape[-1]
    bm = MOE_BM
    n_blocks = m // bm
    n_items = n_blocks + N_EXPERTS - 1
    ends = jnp.cumsum(counts)
    starts = ends - counts
    first_blk = starts // bm
    last_blk = jnp.maximum(ends - 1, starts) // bm
    visits = jnp.where(counts > 0, last_blk - first_blk + 1, 0)
    item_end = jnp.cumsum(visits)
    item_start = item_end - visits
    n_item = item_end[-1]
    w = jnp.minimum(jnp.arange(n_items, dtype=jnp.int32), n_item - 1)
    item_e = jnp.sum(w[:, None] >= item_end[None, :], axis=1).astype(jnp.int32)
    item_blk = (first_blk[item_e] + w - item_start[item_e]).astype(jnp.int32)
    wspec = lambda: pl.BlockSpec((1, 1, d, d), lambda i, blk, e, *_: (layer, e[i], 0, 0))
    idx = lambda f: pl.BlockSpec((1, 1, bm), f, memory_space=pltpu.SMEM)
    tok3 = (order >> TOP_K_SHIFT).reshape(n_blocks, 1, bm)
    here = lambda i, blk, *_: (blk[i], 0, 0)
    after = lambda i, blk, *_: (jnp.minimum(blk[i] + 1, n_blocks - 1), 0, 0)
    return pl.pallas_call(
        _moe_kernel,
        out_shape=jax.ShapeDtypeStruct((m * ROW_TILES, LANE), F32),
        grid_spec=pltpu.PrefetchScalarGridSpec(
            num_scalar_prefetch=5,
            grid=(n_items,),
            in_specs=[idx(here), idx(after), pl.BlockSpec(memory_space=pl.ANY), wspec(), wspec(), wspec()],
            out_specs=pl.BlockSpec((bm * ROW_TILES, LANE), lambda i, blk, *_: (blk[i], 0)),
            scratch_shapes=[pltpu.VMEM((2, bm * ROW_TILES, LANE), F32),
                            pltpu.VMEM((d, d), BF16), pltpu.VMEM((d, d), BF16), pltpu.VMEM((d, d), BF16),
                            pltpu.SemaphoreType.DMA((2,))],
        ),
        compiler_params=_params("arbitrary"),
        name="moe",
    )(item_blk, item_e, n_item.astype(jnp.int32).reshape(1), starts.astype(jnp.int32), counts.astype(jnp.int32),
      tok3, tok3, h, w1, w3, w2)


class _RowGather:
    def __init__(self, ys_hbm, buf, sem, tm):
        self.ys, self.buf, self.sem, self.tm = ys_hbm, buf, sem, tm

    def row_copy(self, d_ref, slot, k, t):
        pltpu.make_async_copy(_row_tile(self.ys, (), d_ref[0, 0, t]), _row_tile(self.buf, (slot, k), t),
                              self.sem.at[slot]).start()

    def start(self, d_refs, slot):
        def body(t, carry):
            for k, d_ref in enumerate(d_refs):
                self.row_copy(d_ref, slot, k, t)
            return carry
        lax.fori_loop(0, self.tm, body, 0, unroll=8)

    def unrolled(self, d_refs, slot):
        return [functools.partial(self.row_copy, d_ref, slot, k, t)
                for t in range(self.tm) for k, d_ref in enumerate(d_refs)]

    def wait(self, slot):
        for k in range(TOP_K):
            pltpu.make_async_copy(self.ys.at[pl.ds(0, self.tm * ROW_TILES), :], self.buf.at[slot, k],
                                  self.sem.at[slot]).wait()

    def combined(self, slot, w1_ref, w2_ref):
        return (w1_ref[...] * _load_row_tiles(self.buf, (slot, 0), self.tm)
                + w2_ref[...] * _load_row_tiles(self.buf, (slot, 1), self.tm))


def _gather_specs(tm, nt_total, step_of):
    cur = lambda: pl.BlockSpec((1, 1, tm), lambda *g: (step_of(*g), 0, 0), memory_space=pltpu.SMEM)
    nxt = lambda: pl.BlockSpec((1, 1, tm), lambda *g: (jnp.minimum(step_of(*g) + 1, nt_total - 1), 0, 0),
                               memory_space=pltpu.SMEM)
    wcol = lambda: pl.BlockSpec((tm, 1), lambda *g: (step_of(*g), 0))
    return [cur(), cur(), nxt(), nxt(), wcol(), wcol()]


def _gather_scratch(tm):
    return [pltpu.VMEM((2, TOP_K, tm * ROW_TILES, LANE), F32), pltpu.SemaphoreType.DMA((2,))]


def _group_norm64(t, ind_ref, indt_ref):
    ss = _dot((t * t).astype(BF16), ind_ref[...])
    r = lax.rsqrt(ss * (1.0 / DIFF_HEAD_DIM) + EPS)
    r_hi = r.astype(BF16)
    r_lo = (r - r_hi.astype(F32)).astype(BF16)
    rb = _dot(jnp.concatenate([r_hi, r_lo], axis=1), indt_ref[...])
    return t * rb


def _kvq_kernel(d1c, d2c, d1n, d2n, w1_ref, w2_ref, ys_hbm, x_ref, ada0_ref, ada1_ref, kvada_ref, kvgain_ref,
                qgain_ref, knorm_ref, qnorm_ref, wk_ref, wvt_ref, wq_ref, ind_ref, indt_ref,
                xo_ref, q_ref, k_ref, vt_ref, gbuf, gsem):
    tm = x_ref.shape[1]
    d = x_ref.shape[2]
    step = pl.program_id(0) * pl.num_programs(1) + pl.program_id(1)
    slot = step % 2
    gather = _RowGather(ys_hbm, gbuf, gsem, tm)

    @pl.when(step == 0)
    def _():
        gather.start((d1c, d2c), 0)

    gather.wait(slot)
    xn = x_ref[0] + ada0_ref[0, 5:6, :] * gather.combined(slot, w1_ref, w2_ref)
    xo_ref[0] = xn
    copies = gather.unrolled((d1n, d2n), 1 - slot)
    n_pieces = 3 * (d // KVQ_TN)
    per_piece = -(-len(copies) // n_pieces)
    groups = iter([copies[g * per_piece:(g + 1) * per_piece] for g in range(n_pieces)])

    def issue_group():
        for issue in next(groups):
            issue()

    def project(lhs, w_ref):
        out = []
        for c in range(d // KVQ_TN):
            issue_group()
            out.append(_dot(lhs, w_ref[:, c * KVQ_TN:(c + 1) * KVQ_TN]))
        return jnp.concatenate(out, axis=1)

    r = _rms(xn)
    hk = ((r * kvgain_ref[...]) * (1.0 + kvada_ref[0, 1:2, :]) + kvada_ref[0, 0:1, :]).astype(BF16)
    hq = ((r * qgain_ref[...]) * (1.0 + ada1_ref[0, 1:2, :]) + ada1_ref[0, 0:1, :]).astype(BF16)
    hd2 = 2 * DIFF_HEAD_DIM
    kk = project(hk, wk_ref)
    kn = (_group_norm64(kk, ind_ref, indt_ref) * knorm_ref[...]).astype(BF16)
    qq = project(hq, wq_ref)
    qn = (_group_norm64(qq, ind_ref, indt_ref) * qnorm_ref[...]).astype(BF16)
    for hd in range(DIFF_HEADS):
        k_ref[0, hd] = kn[:, hd * hd2:(hd + 1) * hd2]
        q_ref[0, hd] = qn[:, hd * hd2:(hd + 1) * hd2]
    vt = []
    for c in range(d // KVQ_TN):
        issue_group()
        vt.append(_dot_nt(wvt_ref[c * KVQ_TN:(c + 1) * KVQ_TN, :], hk))
    vt = jnp.concatenate(vt, axis=0).astype(BF16)
    for j in range(tm // ATT_TK):
        vt_ref[0, j] = vt[:, j * ATT_TK:(j + 1) * ATT_TK].reshape(DIFF_HEADS, hd2, ATT_TK)

    @pl.when(step == pl.num_programs(0) * pl.num_programs(1) - 1)
    def _():
        gather.wait(1 - slot)


def _kvq(x, ys, dest1, dest2, w1, w2, ada0, ada1, kvada, kv_gain, q_gain, k_norm, q_norm, w_k, w_vt, w_q):
    bsz, seq, d = x.shape
    tm = ROW_TILE
    nt = seq // tm
    nt_total = bsz * nt
    lane_group = jnp.arange(d, dtype=jnp.int32) // DIFF_HEAD_DIM
    ind = (lane_group[:, None] == jnp.arange(LANE, dtype=jnp.int32)[None, :]).astype(BF16)
    indt = jnp.concatenate([ind.T, ind.T], axis=0)
    reps = d // DIFF_HEAD_DIM
    q_scale = jnp.tile(q_norm, reps)[None, :] * (DIFF_HEAD_DIM ** -0.5 * math.log2(math.e))
    const = lambda *shape: pl.BlockSpec(shape, lambda b, i: (0,) * len(shape))
    tok = lambda w: pl.BlockSpec((1, tm, w), lambda b, i: (b, i, 0))
    hd2 = 2 * DIFF_HEAD_DIM
    heads = pl.BlockSpec((1, DIFF_HEADS, tm, hd2), lambda b, i: (b, 0, i, 0))
    ada_spec = lambda rows: pl.BlockSpec((1, rows, d), lambda b, i: (b, 0, 0))
    col = lambda a: a.reshape(-1, 1)
    idx = lambda a: a.reshape(nt_total, 1, tm)
    return pl.pallas_call(
        _kvq_kernel,
        out_shape=(jax.ShapeDtypeStruct((bsz, seq, d), F32),
                   jax.ShapeDtypeStruct((bsz, DIFF_HEADS, seq, hd2), BF16),
                   jax.ShapeDtypeStruct((bsz, DIFF_HEADS, seq, hd2), BF16),
                   jax.ShapeDtypeStruct((bsz, seq // ATT_TK, DIFF_HEADS, hd2, ATT_TK), BF16)),
        grid=(bsz, nt),
        in_specs=_gather_specs(tm, nt_total, lambda b, i: b * nt + i) + [
            pl.BlockSpec(memory_space=pl.ANY),
            tok(d),
            ada_spec(6), ada_spec(6), ada_spec(2),
            const(1, d), const(1, d), const(1, d), const(1, d),
            const(d, d), const(d, d), const(d, d), const(d, LANE), const(2 * LANE, d),
        ],
        out_specs=(tok(d), heads, heads,
                   pl.BlockSpec((1, tm // ATT_TK, DIFF_HEADS, hd2, ATT_TK), lambda b, i: (b, i, 0, 0, 0))),
        scratch_shapes=_gather_scratch(tm),
        compiler_params=_params("arbitrary", "arbitrary"),
        name="kvq",
    )(idx(dest1), idx(dest2), idx(dest1), idx(dest2), col(w1), col(w2), ys, x, ada0, ada1, kvada,
      kv_gain, q_gain, jnp.tile(k_norm, reps)[None, :], q_scale, w_k, w_vt, w_q, ind, indt)


def _attn_kernel(lambda_init, q_ref, k_ref, vt_ref, x_ref, ada_ref, lam_ref, subln_ref, wo_ref, ngain_ref,
                 wr_hi_ref, wr_lo_ref, br_ref, xo_ref, h_ref, lt_ref,
                 s_a, s_b, p_a, p_b, acc, obuf):
    i = pl.program_id(1)
    tq, tk = ATT_TQ, ATT_TK
    hd2 = 2 * DIFF_HEAD_DIM
    n_kblk = k_ref.shape[2] // tk
    lam = lam_ref[...]
    lam_full = (jnp.exp(jnp.sum(lam[0:1] * lam[1:2], axis=-1, keepdims=True))
                - jnp.exp(jnp.sum(lam[2:3] * lam[3:4], axis=-1, keepdims=True)) + lambda_init)
    lane = lax.broadcasted_iota(jnp.int32, (1, hd2), 1)
    keep = (jnp.where(lane < DIFF_HEAD_DIM, 1.0, 0.0).astype(BF16),
            jnp.where(lane >= DIFF_HEAD_DIM, 1.0, 0.0).astype(BF16))
    key = lax.broadcasted_iota(jnp.int32, (tk, tq), 0)
    qry = lax.broadcasted_iota(jnp.int32, (1, tq), 1)
    key_limit = ((i * tq + qry) // MASK_CHUNK + 1) * MASK_CHUNK
    neg_inf = jnp.float32(-jnp.inf)
    n_pairs = (i + 2) // 2
    n_plain = jnp.maximum((i - 1) // 2, 0)

    streams = [(u, t) for u in range(ATT_HEADS) for t in range(2)]

    def head_group(hg, _):
        hds = [hg * ATT_HEADS + u for u in range(ATT_HEADS)]
        qs = []
        for hd in hds:
            qh = q_ref[0, hd]
            qs.append((qh * keep[0], qh * keep[1]))

        def produce_scores(jb, s_out, masked):
            start = pl.multiple_of(jnp.minimum(jb, n_kblk - 1) * tk, tk)
            visible = key < key_limit - jb * tk
            maxima = []
            for n, (u, t) in enumerate(streams):
                st = _dot_nt(k_ref[0, hds[u], pl.ds(start, tk), :], qs[u][t])
                if masked:
                    st = jnp.where(visible, st, neg_inf)
                s_out[n] = st
                maxima.append(jnp.max(st, axis=0, keepdims=True))
            return tuple(maxima)

        def step(j, s_in, s_out, p_in, p_out, state, masked):
            stats, blk_max = state
            jv = jnp.clip(j - 1, 0, n_kblk - 1)
            next_max = produce_scores(j + 1, s_out, masked)
            new_stats = []
            for n, (u, t) in enumerate(streams):
                m, l = stats[n]
                m_new = jnp.maximum(m, blk_max[n])
                alpha = jnp.exp2(m - m_new)
                p = jnp.exp2(s_in[n] - m_new)
                p_out[n] = p.astype(BF16)
                new_stats.append((m_new, alpha * l + jnp.sum(p, axis=0, keepdims=True)))
                acc[n] = alpha * (acc[n] + _dot(vt_ref[0, jv, hds[u]], p_in[n]))
            return tuple(new_stats), next_max

        p_b[...] = jnp.zeros_like(p_b)
        acc[...] = jnp.zeros_like(acc)
        stat0 = (jnp.full((1, tq), neg_inf, F32), jnp.zeros((1, tq), F32))
        state = ((stat0,) * len(streams), produce_scores(0, s_a, True))

        def pair(masked, jj, state):
            state = step(2 * jj, s_a, s_b, p_b, p_a, state, masked)
            return step(2 * jj + 1, s_b, s_a, p_a, p_b, state, masked)

        state = lax.fori_loop(0, n_plain, functools.partial(pair, False), state)
        (stats, _) = lax.fori_loop(n_plain, n_pairs, functools.partial(pair, True), state)
        jv = jnp.minimum(2 * n_pairs - 1, n_kblk - 1)
        for u, hd in enumerate(hds):
            vtb = vt_ref[0, jv, hd]
            a1 = acc[2 * u] + _dot(vtb, p_b[2 * u])
            a2 = acc[2 * u + 1] + _dot(vtb, p_b[2 * u + 1])
            o = a1 * (1.0 / stats[2 * u][1]) - lam_full * (a2 * (1.0 / stats[2 * u + 1][1]))
            o = o * lax.rsqrt(jnp.mean(o * o, axis=0, keepdims=True) + EPS)
            obuf[hd] = (o * (subln_ref[...] * (1.0 - lambda_init))).astype(BF16)
        return 0

    lax.fori_loop(0, DIFF_HEADS // ATT_HEADS, head_group, 0)

    y = _dot_tn(obuf[...].reshape(DIFF_HEADS * hd2, tq), wo_ref[...])
    xn = x_ref[0] + ada_ref[0, 2:3, :] * y
    xo_ref[0] = xn
    h = _ffn_prenorm_logits(xn, ngain_ref[...], ada_ref[0, 3:4, :], ada_ref[0, 4:5, :],
                            wr_hi_ref, wr_lo_ref, br_ref, lt_ref)
    _store_row_tiles(h_ref, (0,), h)


def _attn(q, k, vt, x, ada, lam, subln, w_out, ngain, router, lambda_init):
    bsz, seq, d = x.shape
    tq, tk = ATT_TQ, ATT_TK
    hd2 = 2 * DIFF_HEAD_DIM
    const = lambda *shape: pl.BlockSpec(shape, lambda b, i: (0,) * len(shape))
    tok = lambda w: pl.BlockSpec((1, tq, w), lambda b, i: (b, i, 0))
    nq = seq // tq
    ns = 2 * ATT_HEADS
    return pl.pallas_call(
        functools.partial(_attn_kernel, lambda_init),
        out_shape=(jax.ShapeDtypeStruct((bsz, seq, d), F32),
                   jax.ShapeDtypeStruct((bsz, seq * ROW_TILES, LANE), F32),
                   jax.ShapeDtypeStruct((ROUTER_ROWS, bsz * seq), F32)),
        grid=(bsz, nq),
        in_specs=[
            pl.BlockSpec((1, DIFF_HEADS, tq, hd2), lambda b, i: (b, 0, i, 0)),
            pl.BlockSpec((1, DIFF_HEADS, seq, hd2), lambda b, i: (b, 0, 0, 0)),
            pl.BlockSpec((1, seq // tk, DIFF_HEADS, hd2, tk), lambda b, i: (b, 0, 0, 0, 0)),
            tok(d),
            pl.BlockSpec((1, 6, d), lambda b, i: (b, 0, 0)),
            const(4, DIFF_HEAD_DIM), const(hd2, 1), const(d, d), const(1, d),
        ] + _logit_specs(d),
        out_specs=(tok(d), pl.BlockSpec((1, tq * ROW_TILES, LANE), lambda b, i: (b, i, 0)),
                   pl.BlockSpec((ROUTER_ROWS, tq), lambda b, i: (0, b * nq + i))),
        scratch_shapes=[pltpu.VMEM((ns, tk, tq), F32), pltpu.VMEM((ns, tk, tq), F32),
                        pltpu.VMEM((ns, tk, tq), BF16), pltpu.VMEM((ns, tk, tq), BF16),
                        pltpu.VMEM((ns, hd2, tq), F32),
                        pltpu.VMEM((DIFF_HEADS, hd2, tq), BF16)],
        compiler_params=_params("arbitrary", "arbitrary"),
        name="attn",
    )(q, k, vt, x, ada, lam, subln, w_out, ngain, *router)


def _combine_kernel(d1c, d2c, d1n, d2n, w1_ref, w2_ref, ys_hbm, x_ref, ada_ref, o_ref, gbuf, gsem):
    tm = x_ref.shape[1]
    step = pl.program_id(0) * pl.num_programs(1) + pl.program_id(1)
    slot = step % 2
    gather = _RowGather(ys_hbm, gbuf, gsem, tm)

    @pl.when(step == 0)
    def _():
        gather.start((d1c, d2c), 0)

    @pl.when(step + 1 < pl.num_programs(0) * pl.num_programs(1))
    def _():
        gather.start((d1n, d2n), 1 - slot)

    gather.wait(slot)
    o_ref[0] = x_ref[0] + ada_ref[0, 5:6, :] * gather.combined(slot, w1_ref, w2_ref)


def _combine(x, ys, dest1, dest2, w1, w2, ada):
    bsz, seq, d = x.shape
    tm = ROW_TILE
    nt = seq // tm
    nt_total = bsz * nt
    col = lambda a: a.reshape(-1, 1)
    idx = lambda a: a.reshape(nt_total, 1, tm)
    return pl.pallas_call(
        _combine_kernel,
        out_shape=jax.ShapeDtypeStruct((bsz, seq, d), F32),
        grid=(bsz, nt),
        in_specs=_gather_specs(tm, nt_total, lambda b, i: b * nt + i) + [
            pl.BlockSpec(memory_space=pl.ANY),
            pl.BlockSpec((1, tm, d), lambda b, i: (b, i, 0)),
            pl.BlockSpec((1, 6, d), lambda b, i: (b, 0, 0)),
        ],
        out_specs=pl.BlockSpec((1, tm, d), lambda b, i: (b, i, 0)),
        scratch_shapes=_gather_scratch(tm),
        compiler_params=_params("arbitrary", "arbitrary"),
        name="combine",
    )(idx(dest1), idx(dest2), idx(dest1), idx(dest2), col(w1), col(w2), ys, x, ada)


def _router_weights(w_group, b_group, w_expert, b_expert):
    d = w_group.shape[0]
    pad = ROUTER_ROWS - N_EXPERTS - N_GROUPS
    w = jnp.concatenate([w_expert.T, w_group.T, jnp.zeros((pad, d), F32)], axis=0)
    b = jnp.concatenate([b_expert, b_group, jnp.zeros((pad,), F32)])[:, None]
    w_hi = w.astype(BF16)
    w_lo = (w - w_hi.astype(F32)).astype(BF16)
    return w_hi, w_lo, b


def _routing_tables(route, cnt):
    rec = lambda r: route[r]
    counts = cnt[:, 0].astype(jnp.int32)
    starts = jnp.cumsum(counts) - counts
    dest1 = starts[rec(0).astype(jnp.int32)] + rec(2).astype(jnp.int32)
    dest2 = starts[rec(1).astype(jnp.int32)] + rec(3).astype(jnp.int32)
    dest = jnp.stack([dest1, dest2], axis=1).reshape(-1)
    _, order = lax.sort((dest, jnp.arange(dest.shape[0], dtype=jnp.int32)), num_keys=1)
    return order, dest1, dest2, rec(4), rec(5), counts


def kernel(x, c, ada_w, ada_b, norm_mix, norm_ffn, ret_w_in, ret_gn, ret_w_out, kv_ada_w, kv_ada_b, kv_norm, kv_w, k_norm, diff_w_q, q_norm, diff_lam, diff_subln, diff_w_out, moe_w_group, moe_b_group, moe_w_expert, moe_b_expert, moe_w1, moe_w3, moe_w2):
    bsz, seq, d = x.shape
    n_tok = bsz * seq

    ada = _ada(c, ada_w, ada_b).reshape(2, bsz, 6, d)
    kvada = _ada(c, kv_ada_w[None], kv_ada_b[None]).reshape(bsz, 2, d)

    half = RET_DK // 2
    inv_freq = 1.0 / (ROPE_BASE ** (jnp.arange(half, dtype=F32) / half))
    ang = jnp.arange(seq, dtype=F32)[:, None] * inv_freq[None, :]
    cos, sin = jnp.cos(ang), jnp.sin(ang)

    routers = [_router_weights(moe_w_group[l], moe_b_group[l], moe_w_expert[l], moe_b_expert[l])
               for l in range(2)]

    def moe_layer(h, logits_t, layer):
        order, dest1, dest2, w1, w2, counts = _routing_tables(*_route(logits_t))
        ys = _moe(h.reshape(n_tok * ROW_TILES, LANE), order, counts, moe_w1, moe_w3, moe_w2, layer)
        return ys, dest1, dest2, w1, w2

    proj = _ret_in(x, ada[0], norm_mix[0][None, :], cos, sin, ret_w_in[0].astype(BF16))
    x1, h, logits_t = _ret_core(proj, x, ada[0], ret_gn[0][None, :], ret_w_out[0].astype(BF16),
                                norm_ffn[0][None, :], routers[0])
    moe0 = moe_layer(h, logits_t, 0)

    x2, q, k, vt = _kvq(x1, *moe0, ada[0], ada[1], kvada, kv_norm[None, :], norm_mix[1][None, :], k_norm, q_norm[0],
                        kv_w[:, :d].astype(BF16), kv_w[:, d:].T.astype(BF16), diff_w_q[0].astype(BF16))
    lambda_init = 0.8 - 0.6 * math.exp(-0.3 * 1)
    x3, h, logits_t = _attn(q, k, vt, x2, ada[1], diff_lam[0], diff_subln[0][:, None], diff_w_out[0].astype(BF16),
                            norm_ffn[1][None, :], routers[1], lambda_init)
    moe1 = moe_layer(h, logits_t, 1)
    return _combine(x3, *moe1, ada[1])
```

```python
import functools
import math

import jax
import jax.numpy as jnp
from jax import lax
from jax.experimental import pallas as pl
from jax.experimental.pallas import tpu as pltpu

F32 = jnp.float32
BF16 = jnp.bfloat16

D_MODEL = 1024
EPS = 1e-6

RET_HEADS = 4
RET_DK = D_MODEL // RET_HEADS
RET_DV = 2 * RET_DK
RET_QK = RET_HEADS * RET_DK
RET_V = RET_HEADS * RET_DV
RET_IN = 2 * RET_QK + 2 * RET_V
ROPE_BASE = 10000.0
RET_CHUNK = 256

DIFF_HEAD_DIM = 64
DIFF_HEADS = D_MODEL // (2 * DIFF_HEAD_DIM)
MASK_CHUNK = 64
ATT_TQ = 512
ATT_TK = 256
ATT_HEADS = 1

N_GROUPS = 4
EXPERTS_PER_GROUP = 4
N_EXPERTS = N_GROUPS * EXPERTS_PER_GROUP
TOP_K = 2
TOP_K_SHIFT = 1
MOE_BM = 256
MOE_TN = 256
KVQ_TN = 256
MOE_AHEAD = 2
ROUTER_ROWS = 32
ROUTE_ROWS = 8
ROW_TILE = 512
ROUTE_TILE = 2048
ROUTE_SEG = 256

LANE = 128
SUBLANE = 8
VMEM_LIMIT = 56 * 1024 * 1024


def _dot(a, b):
    return jnp.dot(a, b, preferred_element_type=F32)


def _dot_nt(a, b):
    return lax.dot_general(a, b, (((1,), (1,)), ((), ())), preferred_element_type=F32)


def _dot_tn(a, b):
    return lax.dot_general(a, b, (((0,), (0,)), ((), ())), preferred_element_type=F32)


def _silu(x):
    return x * (1.0 / (1.0 + jnp.exp(-x)))


def _rms(x):
    return x * lax.rsqrt(jnp.mean(x * x, axis=-1, keepdims=True) + EPS)


def _params(*sem, flags=None):
    return pltpu.CompilerParams(dimension_semantics=sem, vmem_limit_bytes=VMEM_LIMIT, flags=flags)


ROW_TILES = D_MODEL // LANE


def _load_row_tiles(ref, lead, rows):
    return jnp.concatenate(
        [ref[lead + (pl.ds(c, rows, stride=ROW_TILES), slice(None))] for c in range(ROW_TILES)], axis=1)


def _store_row_tiles(ref, lead, val):
    rows = val.shape[0]
    for c in range(ROW_TILES):
        ref[lead + (pl.ds(c, rows, stride=ROW_TILES), slice(None))] = val[:, c * LANE:(c + 1) * LANE]


def _row_tile(ref, lead, r):
    return ref.at[lead + (pl.ds(pl.multiple_of(r * ROW_TILES, ROW_TILES), ROW_TILES), slice(None))]


def _ada_kernel(c_ref, w_ref, b_ref, o_ref):
    ca = _silu(c_ref[...])
    o_ref[0] = jnp.dot(ca, w_ref[0], preferred_element_type=F32,
                       precision=lax.Precision.HIGHEST) + b_ref[0]


def _ada(c, w, b):
    n_l, d, n = w.shape
    bsz = c.shape[0]
    tn = 1024
    return pl.pallas_call(
        _ada_kernel,
        out_shape=jax.ShapeDtypeStruct((n_l, bsz, n), F32),
        grid=(n_l, n // tn),
        in_specs=[
            pl.BlockSpec((bsz, d), lambda l, j: (0, 0)),
            pl.BlockSpec((1, d, tn), lambda l, j: (l, 0, j)),
            pl.BlockSpec((1, 1, tn), lambda l, j: (l, 0, j)),
        ],
        out_specs=pl.BlockSpec((1, bsz, tn), lambda l, j: (l, 0, j)),
        compiler_params=_params("arbitrary", "arbitrary"),
        name="ada",
    )(c, w, b.reshape(n_l, 1, n))


def _ffn_prenorm_logits(xn, gain, shift, scale, wr_hi_ref, wr_lo_ref, br_ref, lt_ref):
    h = _rms(xn) * gain
    h = h * (1.0 + scale) + shift
    h_hi = h.astype(BF16)
    h_lo = (h - h_hi.astype(F32)).astype(BF16)
    w_hi = wr_hi_ref[...]
    lt_ref[...] = _dot_nt(w_hi, h_hi) + _dot_nt(wr_lo_ref[...], h_hi) + _dot_nt(w_hi, h_lo) + br_ref[...]
    return h


def _route_kernel(lt_ref, tri_ref, route_ref, cnt_ref, carry):
    lt = lt_ref[...]
    c = lt.shape[1]
    seg = tri_ref.shape[0]
    neg_inf = jnp.float32(-jnp.inf)

    grow = lax.broadcasted_iota(jnp.int32, (SUBLANE, c), 0).astype(F32)
    g = jnp.where(grow < N_GROUPS, lt[N_EXPERTS:N_EXPERTS + SUBLANE], neg_inf)
    gmax = jnp.max(g, axis=0, keepdims=True)
    gidx = jnp.min(jnp.where(g == gmax, grow, float(SUBLANE)), axis=0, keepdims=True)
    gate = 1.0 / jnp.sum(jnp.exp(g - gmax), axis=0, keepdims=True)

    erow_i = lax.broadcasted_iota(jnp.int32, (N_EXPERTS, c), 0)
    erow = erow_i.astype(F32)
    egroup = (erow_i // EXPERTS_PER_GROUP).astype(F32)
    el = jnp.where(egroup == gidx, lt[0:N_EXPERTS], neg_inf)
    m1 = jnp.max(el, axis=0, keepdims=True)
    i1 = jnp.min(jnp.where(el == m1, erow, float(N_EXPERTS)), axis=0, keepdims=True)
    el2 = jnp.where(erow == i1, neg_inf, el)
    m2 = jnp.max(el2, axis=0, keepdims=True)
    i2 = jnp.min(jnp.where(el2 == m2, erow, float(N_EXPERTS)), axis=0, keepdims=True)
    t = jnp.exp(m2 - m1)
    den = 1.0 / (1.0 + t)
    w1 = gate * den
    w2 = gate * t * den

    @pl.when(pl.program_id(0) == 0)
    def _():
        carry[...] = jnp.zeros_like(carry)

    oh1 = erow == i1
    oh2 = erow == i2
    oh = jnp.where(jnp.logical_or(oh1, oh2), 1.0, 0.0)
    offset = carry[...]
    before = []
    for s in range(c // seg):
        oh_s = oh[:, s * seg:(s + 1) * seg]
        before.append(offset + _dot(oh_s.astype(BF16), tri_ref[...]))
        offset = offset + jnp.sum(oh_s, axis=1, keepdims=True)
    before = jnp.concatenate(before, axis=1)
    rank1 = jnp.sum(jnp.where(oh1, before, 0.0), axis=0, keepdims=True)
    rank2 = jnp.sum(jnp.where(oh2, before, 0.0), axis=0, keepdims=True)
    carry[...] = offset
    cnt_ref[...] = jnp.broadcast_to(offset, cnt_ref.shape)

    rrow = lax.broadcasted_iota(jnp.int32, (ROUTE_ROWS, c), 0)
    rec = jnp.zeros((ROUTE_ROWS, c), F32)
    for idx, val in enumerate((i1, i2, rank1, rank2, w1, w2)):
        rec = jnp.where(rrow == idx, val, rec)
    route_ref[...] = rec


def _route(lt):
    n_tok = lt.shape[1]
    tile = min(ROUTE_TILE, n_tok)
    t = jnp.arange(ROUTE_SEG, dtype=jnp.int32)
    earlier = (t[:, None] < t[None, :]).astype(BF16)
    return pl.pallas_call(
        _route_kernel,
        out_shape=(jax.ShapeDtypeStruct((ROUTE_ROWS, n_tok), F32),
                   jax.ShapeDtypeStruct((N_EXPERTS, LANE), F32)),
        grid=(n_tok // tile,),
        in_specs=[pl.BlockSpec((ROUTER_ROWS, tile), lambda i: (0, i)),
                  pl.BlockSpec((ROUTE_SEG, ROUTE_SEG), lambda i: (0, 0))],
        out_specs=(pl.BlockSpec((ROUTE_ROWS, tile), lambda i: (0, i)),
                   pl.BlockSpec((N_EXPERTS, LANE), lambda i: (0, 0))),
        scratch_shapes=[pltpu.VMEM((N_EXPERTS, 1), F32)],
        compiler_params=_params("arbitrary"),
        name="route",
    )(lt, earlier)


def _logit_specs(d):
    const = lambda *shape: pl.BlockSpec(shape, lambda b, i: (0,) * len(shape))
    return [const(ROUTER_ROWS, d), const(ROUTER_ROWS, d), const(ROUTER_ROWS, 1)]


def _ret_in_kernel(x_ref, ada_ref, gain_ref, cos_ref, sin_ref, w_ref, o_ref):
    x = x_ref[0]
    shift = ada_ref[0, 0:1, :]
    scale = ada_ref[0, 1:2, :]
    h = (_rms(x) * gain_ref[...]) * (1.0 + scale) + shift
    hb = h.astype(BF16)
    cos = cos_ref[...]
    sin = sin_ref[...]
    half = RET_DK // 2
    for j in range(RET_IN // D_MODEL):
        p = _dot(hb, w_ref[:, j * D_MODEL:(j + 1) * D_MODEL])
        if j < 2:
            post = 1.0 if j == 0 else RET_DK ** -0.5
            for hd in range(RET_HEADS):
                lo = hd * RET_DK
                x1 = p[:, lo:lo + half]
                x2 = p[:, lo + half:lo + RET_DK]
                o_ref[0, :, j * D_MODEL + lo:j * D_MODEL + lo + half] = (
                    (x1 * cos - x2 * sin) * post).astype(BF16)
                o_ref[0, :, j * D_MODEL + lo + half:j * D_MODEL + lo + RET_DK] = (
                    (x2 * cos + x1 * sin) * post).astype(BF16)
        else:
            o_ref[0, :, j * D_MODEL:(j + 1) * D_MODEL] = p.astype(BF16)


def _ret_in(x, ada, gain, cos, sin, w_in):
    bsz, seq, d = x.shape
    tm = ROW_TILE
    return pl.pallas_call(
        _ret_in_kernel,
        out_shape=jax.ShapeDtypeStruct((bsz, seq, RET_IN), BF16),
        grid=(bsz, seq // tm),
        in_specs=[
            pl.BlockSpec((1, tm, d), lambda b, i: (b, i, 0)),
            pl.BlockSpec((1, 6, d), lambda b, i: (b, 0, 0)),
            pl.BlockSpec((1, d), lambda b, i: (0, 0)),
            pl.BlockSpec((tm, RET_DK // 2), lambda b, i: (i, 0)),
            pl.BlockSpec((tm, RET_DK // 2), lambda b, i: (i, 0)),
            pl.BlockSpec((d, RET_IN), lambda b, i: (0, 0), pipeline_mode=pl.Buffered(1)),
        ],
        out_specs=pl.BlockSpec((1, tm, RET_IN), lambda b, i: (b, i, 0)),
        compiler_params=_params("arbitrary", "arbitrary"),
        name="ret_in",
    )(x, ada, gain, cos, sin, w_in)


def _ret_core_kernel(q_ref, k_ref, v_ref, g_ref, x_ref, ada_ref, intra_ref, qd_ref, kd_ref, cd_ref,
                     gn_ref, wo_ref, ngain_ref, wr_hi_ref, wr_lo_ref, br_ref,
                     xo_ref, h_ref, lt_ref, state, ybuf):
    @pl.when(pl.program_id(1) == 0)
    def _():
        state[...] = jnp.zeros_like(state)

    for hd in range(RET_HEADS):
        q = q_ref[0, :, hd * RET_DK:(hd + 1) * RET_DK]
        k = k_ref[0, :, hd * RET_DK:(hd + 1) * RET_DK]
        v = v_ref[0, :, hd * RET_DV:(hd + 1) * RET_DV]
        st = state[hd]
        s = _dot_nt(q, k) * intra_ref[hd]
        o = _dot(s.astype(BF16), v) + qd_ref[hd] * _dot(q, st.astype(BF16))
        kdec = (k.astype(F32) * kd_ref[hd]).astype(BF16)
        state[hd] = st * cd_ref[hd] + _dot_tn(kdec, v)
        o = _rms(o) * gn_ref[:, hd * RET_DV:(hd + 1) * RET_DV]
        g = g_ref[0, :, hd * RET_DV:(hd + 1) * RET_DV].astype(F32)
        ybuf[:, hd * RET_DV:(hd + 1) * RET_DV] = (_silu(g) * o).astype(BF16)

    y = _dot(ybuf[...], wo_ref[...])
    xn = x_ref[0] + ada_ref[0, 2:3, :] * y
    xo_ref[0] = xn
    h = _ffn_prenorm_logits(xn, ngain_ref[...], ada_ref[0, 3:4, :], ada_ref[0, 4:5, :],
                            wr_hi_ref, wr_lo_ref, br_ref, lt_ref)
    _store_row_tiles(h_ref, (0,), h)


def _ret_core(proj, x, ada, gn, w_out, ngain, router):
    bsz, seq, d = x.shape
    c = RET_CHUNK
    f32 = F32
    log_gamma = jnp.log1p(-jnp.exp2(-5.0 - jnp.arange(RET_HEADS, dtype=f32)))
    n = jnp.arange(c, dtype=f32)
    rel = n[:, None] - n[None, :]
    intra = jnp.where(rel >= 0, jnp.exp(jnp.maximum(rel, 0.0)[None] * log_gamma[:, None, None]), 0.0)
    qd = jnp.exp((n + 1.0)[None, :] * log_gamma[:, None])[:, :, None]
    kd = jnp.exp((c - 1.0 - n)[None, :] * log_gamma[:, None])[:, :, None]
    cd = jnp.broadcast_to(jnp.exp(c * log_gamma)[:, None, None], (RET_HEADS, 1, RET_DV))
    const = lambda *shape: pl.BlockSpec(shape, lambda b, i: (0,) * len(shape))
    tok = lambda w, j: pl.BlockSpec((1, c, w), lambda b, i: (b, i, j))
    nc = seq // c
    return pl.pallas_call(
        _ret_core_kernel,
        out_shape=(jax.ShapeDtypeStruct((bsz, seq, d), f32),
                   jax.ShapeDtypeStruct((bsz, seq * ROW_TILES, LANE), f32),
                   jax.ShapeDtypeStruct((ROUTER_ROWS, bsz * seq), f32)),
        grid=(bsz, nc),
        in_specs=[
            tok(RET_QK, 0), tok(RET_QK, 1), tok(RET_V, 1), tok(RET_V, 2),
            tok(d, 0),
            pl.BlockSpec((1, 6, d), lambda b, i: (b, 0, 0)),
            const(RET_HEADS, c, c), const(RET_HEADS, c, 1), const(RET_HEADS, c, 1),
            const(RET_HEADS, 1, RET_DV),
            const(1, RET_V), const(RET_V, d), const(1, d),
        ] + _logit_specs(d),
        out_specs=(tok(d, 0), pl.BlockSpec((1, c * ROW_TILES, LANE), lambda b, i: (b, i, 0)),
                   pl.BlockSpec((ROUTER_ROWS, c), lambda b, i: (0, b * nc + i))),
        scratch_shapes=[pltpu.VMEM((RET_HEADS, RET_DK, RET_DV), f32),
                        pltpu.VMEM((c, RET_V), BF16)],
        compiler_params=_params("arbitrary", "arbitrary"),
        name="ret_core",
    )(proj, proj, proj, proj, x, ada, intra, qd, kd, cd, gn, w_out, ngain, *router)


def _moe_kernel(blk_ref, exp_ref, nitem_ref, start_ref, cnt_ref, *refs):
    tok_first, refs = refs[:MOE_AHEAD], refs[MOE_AHEAD:]
    tok_ahead, h_hbm, w1_ref, w3_ref, w2_ref, o_ref, xbuf, w1b, w3b, w2b, gsem = refs
    n_buf = MOE_AHEAD + 1
    w = pl.program_id(0)
    bm = MOE_BM
    n_item = nitem_ref[0]
    d = w1b.shape[0]
    e = exp_ref[w]
    blk = blk_ref[w]
    slot = blk % n_buf
    ahead_slot = (blk + MOE_AHEAD) % n_buf
    prev = jnp.maximum(w - 1, 0)
    first_visit = jnp.logical_or(w == 0, blk_ref[prev] != blk)
    live = w < n_item

    def gather_row(tok_ref, s, r):
        pltpu.make_async_copy(_row_tile(h_hbm, (), tok_ref[0, 0, r]), _row_tile(xbuf, (s,), r), gsem.at[s]).start()

    def wait_gather(s):
        pltpu.make_async_copy(h_hbm.at[pl.ds(0, bm * ROW_TILES), :], xbuf.at[s], gsem.at[s]).wait()

    def expert(row_copies):
        n_dots = 2 * (d // MOE_TN)
        per_dot = -(-len(row_copies) // n_dots)
        groups = iter([row_copies[g * per_dot:(g + 1) * per_dot] for g in range(n_dots)])

        def piece(lhs, w_ref, c):
            for issue in next(groups, ()):
                issue()
            return _dot(lhs, w_ref[:, c * MOE_TN:(c + 1) * MOE_TN])

        x = _load_row_tiles(xbuf, (slot,), bm).astype(BF16)
        hid = []
        for c in range(d // MOE_TN):
            a = piece(x, w1b, c)
            b = piece(x, w3b, c)
            hid.append((_silu(a) * b).astype(BF16))
        hid = jnp.concatenate(hid, axis=1)
        y = jnp.concatenate([piece(hid, w2b, c) for c in range(d // MOE_TN)], axis=1)
        row = blk * bm + lax.broadcasted_iota(jnp.int32, (bm, 1), 0)
        mine = jnp.logical_and(row >= start_ref[e], row < start_ref[e] + cnt_ref[e])
        return y, mine

    @pl.when(jnp.logical_and(live, jnp.logical_or(w == 0, exp_ref[prev] != e)))
    def _():
        w1b[...] = w1_ref[0, 0].astype(BF16)
        w3b[...] = w3_ref[0, 0].astype(BF16)
        w2b[...] = w2_ref[0, 0].astype(BF16)

    @pl.when(w == 0)
    def _():
        for k in range(MOE_AHEAD):
            def body(r, carry, k=k):
                gather_row(tok_first[k], k, r)
                return carry
            lax.fori_loop(0, bm, body, 0, unroll=8)

    @pl.when(jnp.logical_and(live, first_visit))
    def _():
        wait_gather(slot)
        y, mine = expert([functools.partial(gather_row, tok_ahead, ahead_slot, r) for r in range(bm)])
        _store_row_tiles(o_ref, (), jnp.where(mine, y, 0.0))

    @pl.when(jnp.logical_and(live, jnp.logical_not(first_visit)))
    def _():
        y, mine = expert([])
        _store_row_tiles(o_ref, (), jnp.where(mine, y, _load_row_tiles(o_ref, (), bm)))

    @pl.when(w == n_item - 1)
    def _():
        for k in range(1, n_buf):
            wait_gather((blk + k) % n_buf)


def _moe(h, order, counts, w1, w3, w2, layer):
    m = order.shape[0]
    d = w1.shape[-1]
    bm = MOE_BM
    n_blocks = m // bm
    n_items = n_blocks + N_EXPERTS - 1
    ends = jnp.cumsum(counts)
    starts = ends - counts
    first_blk = starts // bm
    last_blk = jnp.maximum(ends - 1, starts) // bm
    visits = jnp.where(counts > 0, last_blk - first_blk + 1, 0)
    item_end = jnp.cumsum(visits)
    item_start = item_end - visits
    n_item = item_end[-1]
    w = jnp.minimum(jnp.arange(n_items, dtype=jnp.int32), n_item - 1)
    item_e = jnp.sum(w[:, None] >= item_end[None, :], axis=1).astype(jnp.int32)
    item_blk = (first_blk[item_e] + w - item_start[item_e]).astype(jnp.int32)
    wspec = lambda: pl.BlockSpec((1, 1, d, d), lambda i, blk, e, *_: (layer, e[i], 0, 0))
    idx = lambda f: pl.BlockSpec((1, 1, bm), f, memory_space=pltpu.SMEM)
    tok3 = (order >> TOP_K_SHIFT).reshape(n_blocks, 1, bm)
    first = [idx(lambda i, blk, *_, k=k: (k, 0, 0)) for k in range(MOE_AHEAD)]
    ahead = idx(lambda i, blk, *_: (jnp.minimum(blk[i] + MOE_AHEAD, n_blocks - 1), 0, 0))
    return pl.pallas_call(
        _moe_kernel,
        out_shape=jax.ShapeDtypeStruct((m * ROW_TILES, LANE), F32),
        grid_spec=pltpu.PrefetchScalarGridSpec(
            num_scalar_prefetch=5,
            grid=(n_items,),
            in_specs=first + [ahead, pl.BlockSpec(memory_space=pl.ANY), wspec(), wspec(), wspec()],
            out_specs=pl.BlockSpec((bm * ROW_TILES, LANE), lambda i, blk, *_: (blk[i], 0)),
            scratch_shapes=[pltpu.VMEM((MOE_AHEAD + 1, bm * ROW_TILES, LANE), F32),
                            pltpu.VMEM((d, d), BF16), pltpu.VMEM((d, d), BF16), pltpu.VMEM((d, d), BF16),
                            pltpu.SemaphoreType.DMA((MOE_AHEAD + 1,))],
        ),
        compiler_params=_params("arbitrary"),
        name="moe",
    )(item_blk, item_e, n_item.astype(jnp.int32).reshape(1), starts.astype(jnp.int32), counts.astype(jnp.int32),
      *([tok3] * (MOE_AHEAD + 1)), h, w1, w3, w2)


class _RowGather:
    def __init__(self, ys_hbm, buf, sem, tm):
        self.ys, self.buf, self.sem, self.tm = ys_hbm, buf, sem, tm

    def row_copy(self, d_ref, slot, k, t):
        pltpu.make_async_copy(_row_tile(self.ys, (), d_ref[0, 0, t]), _row_tile(self.buf, (slot, k), t),
                              self.sem.at[slot]).start()

    def start(self, d_refs, slot):
        def body(t, carry):
            for k, d_ref in enumerate(d_refs):
                self.row_copy(d_ref, slot, k, t)
            return carry
        lax.fori_loop(0, self.tm, body, 0, unroll=8)

    def unrolled(self, d_refs, slot):
        return [functools.partial(self.row_copy, d_ref, slot, k, t)
                for t in range(self.tm) for k, d_ref in enumerate(d_refs)]

    def wait(self, slot):
        for k in range(TOP_K):
            pltpu.make_async_copy(self.ys.at[pl.ds(0, self.tm * ROW_TILES), :], self.buf.at[slot, k],
                                  self.sem.at[slot]).wait()

    def combined(self, slot, w1_ref, w2_ref):
        return (w1_ref[...] * _load_row_tiles(self.buf, (slot, 0), self.tm)
                + w2_ref[...] * _load_row_tiles(self.buf, (slot, 1), self.tm))


def _gather_specs(tm, nt_total, step_of):
    cur = lambda: pl.BlockSpec((1, 1, tm), lambda *g: (step_of(*g), 0, 0), memory_space=pltpu.SMEM)
    nxt = lambda: pl.BlockSpec((1, 1, tm), lambda *g: (jnp.minimum(step_of(*g) + 1, nt_total - 1), 0, 0),
                               memory_space=pltpu.SMEM)
    wcol = lambda: pl.BlockSpec((tm, 1), lambda *g: (step_of(*g), 0))
    return [cur(), cur(), nxt(), nxt(), wcol(), wcol()]


def _gather_scratch(tm):
    return [pltpu.VMEM((2, TOP_K, tm * ROW_TILES, LANE), F32), pltpu.SemaphoreType.DMA((2,))]


def _group_norm64(t, ind_ref, indt_ref):
    ss = _dot((t * t).astype(BF16), ind_ref[...])
    r = lax.rsqrt(ss * (1.0 / DIFF_HEAD_DIM) + EPS)
    r_hi = r.astype(BF16)
    r_lo = (r - r_hi.astype(F32)).astype(BF16)
    rb = _dot(jnp.concatenate([r_hi, r_lo], axis=1), indt_ref[...])
    return t * rb


def _kvq_kernel(d1c, d2c, d1n, d2n, w1_ref, w2_ref, ys_hbm, x_ref, ada0_ref, ada1_ref, kvada_ref, kvgain_ref,
                qgain_ref, knorm_ref, qnorm_ref, wk_ref, wvt_ref, wq_ref, ind_ref, indt_ref,
                xo_ref, q_ref, k_ref, vt_ref, gbuf, gsem):
    tm = x_ref.shape[1]
    d = x_ref.shape[2]
    step = pl.program_id(0) * pl.num_programs(1) + pl.program_id(1)
    slot = step % 2
    gather = _RowGather(ys_hbm, gbuf, gsem, tm)

    @pl.when(step == 0)
    def _():
        gather.start((d1c, d2c), 0)

    gather.wait(slot)
    xn = x_ref[0] + ada0_ref[0, 5:6, :] * gather.combined(slot, w1_ref, w2_ref)
    xo_ref[0] = xn
    copies = gather.unrolled((d1n, d2n), 1 - slot)
    n_pieces = 3 * (d // KVQ_TN)
    per_piece = -(-len(copies) // n_pieces)
    groups = iter([copies[g * per_piece:(g + 1) * per_piece] for g in range(n_pieces)])

    def issue_group():
        for issue in next(groups):
            issue()

    def project(lhs, w_ref):
        out = []
        for c in range(d // KVQ_TN):
            issue_group()
            out.append(_dot(lhs, w_ref[:, c * KVQ_TN:(c + 1) * KVQ_TN]))
        return jnp.concatenate(out, axis=1)

    r = _rms(xn)
    hk = ((r * kvgain_ref[...]) * (1.0 + kvada_ref[0, 1:2, :]) + kvada_ref[0, 0:1, :]).astype(BF16)
    hq = ((r * qgain_ref[...]) * (1.0 + ada1_ref[0, 1:2, :]) + ada1_ref[0, 0:1, :]).astype(BF16)
    hd2 = 2 * DIFF_HEAD_DIM
    kk = project(hk, wk_ref)
    kn = (_group_norm64(kk, ind_ref, indt_ref) * knorm_ref[...]).astype(BF16)
    qq = project(hq, wq_ref)
    qn = (_group_norm64(qq, ind_ref, indt_ref) * qnorm_ref[...]).astype(BF16)
    for hd in range(DIFF_HEADS):
        k_ref[0, hd] = kn[:, hd * hd2:(hd + 1) * hd2]
        q_ref[0, hd] = qn[:, hd * hd2:(hd + 1) * hd2]
    vt = []
    for c in range(d // KVQ_TN):
        issue_group()
        vt.append(_dot_nt(wvt_ref[c * KVQ_TN:(c + 1) * KVQ_TN, :], hk))
    vt = jnp.concatenate(vt, axis=0).astype(BF16)
    for j in range(tm // ATT_TK):
        vt_ref[0, j] = vt[:, j * ATT_TK:(j + 1) * ATT_TK].reshape(DIFF_HEADS, hd2, ATT_TK)

    @pl.when(step == pl.num_programs(0) * pl.num_programs(1) - 1)
    def _():
        gather.wait(1 - slot)


def _kvq(x, ys, dest1, dest2, w1, w2, ada0, ada1, kvada, kv_gain, q_gain, k_norm, q_norm, w_k, w_vt, w_q):
    bsz, seq, d = x.shape
    tm = ROW_TILE
    nt = seq // tm
    nt_total = bsz * nt
    lane_group = jnp.arange(d, dtype=jnp.int32) // DIFF_HEAD_DIM
    ind = (lane_group[:, None] == jnp.arange(LANE, dtype=jnp.int32)[None, :]).astype(BF16)
    indt = jnp.concatenate([ind.T, ind.T], axis=0)
    reps = d // DIFF_HEAD_DIM
    q_scale = jnp.tile(q_norm, reps)[None, :] * (DIFF_HEAD_DIM ** -0.5 * math.log2(math.e))
    const = lambda *shape: pl.BlockSpec(shape, lambda b, i: (0,) * len(shape))
    tok = lambda w: pl.BlockSpec((1, tm, w), lambda b, i: (b, i, 0))
    hd2 = 2 * DIFF_HEAD_DIM
    heads = pl.BlockSpec((1, DIFF_HEADS, tm, hd2), lambda b, i: (b, 0, i, 0))
    ada_spec = lambda rows: pl.BlockSpec((1, rows, d), lambda b, i: (b, 0, 0))
    col = lambda a: a.reshape(-1, 1)
    idx = lambda a: a.reshape(nt_total, 1, tm)
    return pl.pallas_call(
        _kvq_kernel,
        out_shape=(jax.ShapeDtypeStruct((bsz, seq, d), F32),
                   jax.ShapeDtypeStruct((bsz, DIFF_HEADS, seq, hd2), BF16),
                   jax.ShapeDtypeStruct((bsz, DIFF_HEADS, seq, hd2), BF16),
                   jax.ShapeDtypeStruct((bsz, seq // ATT_TK, DIFF_HEADS, hd2, ATT_TK), BF16)),
        grid=(bsz, nt),
        in_specs=_gather_specs(tm, nt_total, lambda b, i: b * nt + i) + [
            pl.BlockSpec(memory_space=pl.ANY),
            tok(d),
            ada_spec(6), ada_spec(6), ada_spec(2),
            const(1, d), const(1, d), const(1, d), const(1, d),
            const(d, d), const(d, d), const(d, d), const(d, LANE), const(2 * LANE, d),
        ],
        out_specs=(tok(d), heads, heads,
                   pl.BlockSpec((1, tm // ATT_TK, DIFF_HEADS, hd2, ATT_TK), lambda b, i: (b, i, 0, 0, 0))),
        scratch_shapes=_gather_scratch(tm),
        compiler_params=_params("arbitrary", "arbitrary"),
        name="kvq",
    )(idx(dest1), idx(dest2), idx(dest1), idx(dest2), col(w1), col(w2), ys, x, ada0, ada1, kvada,
      kv_gain, q_gain, jnp.tile(k_norm, reps)[None, :], q_scale, w_k, w_vt, w_q, ind, indt)


def _attn_kernel(lambda_init, q_ref, k_ref, vt_ref, x_ref, ada_ref, lam_ref, subln_ref, wo_ref, ngain_ref,
                 wr_hi_ref, wr_lo_ref, br_ref, xo_ref, h_ref, lt_ref,
                 s_a, s_b, p_a, p_b, acc, obuf):
    i = pl.program_id(1)
    tq, tk = ATT_TQ, ATT_TK
    hd2 = 2 * DIFF_HEAD_DIM
    n_kblk = k_ref.shape[2] // tk
    lam = lam_ref[...]
    lam_full = (jnp.exp(jnp.sum(lam[0:1] * lam[1:2], axis=-1, keepdims=True))
                - jnp.exp(jnp.sum(lam[2:3] * lam[3:4], axis=-1, keepdims=True)) + lambda_init)
    lane = lax.broadcasted_iota(jnp.int32, (1, hd2), 1)
    keep = (jnp.where(lane < DIFF_HEAD_DIM, 1.0, 0.0).astype(BF16),
            jnp.where(lane >= DIFF_HEAD_DIM, 1.0, 0.0).astype(BF16))
    key = lax.broadcasted_iota(jnp.int32, (tk, tq), 0)
    qry = lax.broadcasted_iota(jnp.int32, (1, tq), 1)
    key_limit = ((i * tq + qry) // MASK_CHUNK + 1) * MASK_CHUNK
    neg_inf = jnp.float32(-jnp.inf)
    first_masked = i * (tq // tk)
    n_blocks = first_masked + tq // tk
    n_pairs = (n_blocks + 1) // 2
    n_plain = jnp.maximum((first_masked - 1) // 2, 0)

    streams = [(u, t) for u in range(ATT_HEADS) for t in range(2)]

    def head_group(hg, _):
        hds = [hg * ATT_HEADS + u for u in range(ATT_HEADS)]
        qs = []
        for hd in hds:
            qh = q_ref[0, hd]
            qs.append((qh * keep[0], qh * keep[1]))

        def produce_scores(jb, s_out, masked):
            start = pl.multiple_of(jnp.minimum(jb, n_kblk - 1) * tk, tk)
            visible = key < key_limit - jb * tk
            maxima = []
            for n, (u, t) in enumerate(streams):
                st = _dot_nt(k_ref[0, hds[u], pl.ds(start, tk), :], qs[u][t])
                if masked:
                    st = jnp.where(visible, st, neg_inf)
                s_out[n] = st
                maxima.append(jnp.max(st, axis=0, keepdims=True))
            return tuple(maxima)

        def step(j, s_in, s_out, p_in, p_out, state, masked):
            stats, blk_max = state
            jv = jnp.clip(j - 1, 0, n_kblk - 1)
            next_max = produce_scores(j + 1, s_out, masked)
            new_stats = []
            for n, (u, t) in enumerate(streams):
                m, l = stats[n]
                m_new = jnp.maximum(m, blk_max[n])
                alpha = jnp.exp2(m - m_new)
                p = jnp.exp2(s_in[n] - m_new)
                p_out[n] = p.astype(BF16)
                new_stats.append((m_new, alpha * l + jnp.sum(p, axis=0, keepdims=True)))
                acc[n] = alpha * (acc[n] + _dot(vt_ref[0, jv, hds[u]], p_in[n]))
            return tuple(new_stats), next_max

        p_b[...] = jnp.zeros_like(p_b)
        acc[...] = jnp.zeros_like(acc)
        stat0 = (jnp.full((1, tq), neg_inf, F32), jnp.zeros((1, tq), F32))
        state = ((stat0,) * len(streams), produce_scores(0, s_a, True))

        def pair(masked, jj, state):
            state = step(2 * jj, s_a, s_b, p_b, p_a, state, masked)
            return step(2 * jj + 1, s_b, s_a, p_a, p_b, state, masked)

        state = lax.fori_loop(0, n_plain, functools.partial(pair, False), state)
        (stats, _) = lax.fori_loop(n_plain, n_pairs, functools.partial(pair, True), state)
        jv = jnp.minimum(2 * n_pairs - 1, n_kblk - 1)
        for u, hd in enumerate(hds):
            vtb = vt_ref[0, jv, hd]
            a1 = acc[2 * u] + _dot(vtb, p_b[2 * u])
            a2 = acc[2 * u + 1] + _dot(vtb, p_b[2 * u + 1])
            o = a1 * (1.0 / stats[2 * u][1]) - lam_full * (a2 * (1.0 / stats[2 * u + 1][1]))
            o = o * lax.rsqrt(jnp.mean(o * o, axis=0, keepdims=True) + EPS)
            obuf[hd] = (o * (subln_ref[...] * (1.0 - lambda_init))).astype(BF16)
        return 0

    lax.fori_loop(0, DIFF_HEADS // ATT_HEADS, head_group, 0)

    y = _dot_tn(obuf[...].reshape(DIFF_HEADS * hd2, tq), wo_ref[...])
    xn = x_ref[0] + ada_ref[0, 2:3, :] * y
    xo_ref[0] = xn
    h = _ffn_prenorm_logits(xn, ngain_ref[...], ada_ref[0, 3:4, :], ada_ref[0, 4:5, :],
                            wr_hi_ref, wr_lo_ref, br_ref, lt_ref)
    _store_row_tiles(h_ref, (0,), h)


def _attn(q, k, vt, x, ada, lam, subln, w_out, ngain, router, lambda_init):
    bsz, seq, d = x.shape
    tq, tk = ATT_TQ, ATT_TK
    hd2 = 2 * DIFF_HEAD_DIM
    const = lambda *shape: pl.BlockSpec(shape, lambda b, i: (0,) * len(shape))
    tok = lambda w: pl.BlockSpec((1, tq, w), lambda b, i: (b, i, 0))
    nq = seq // tq
    ns = 2 * ATT_HEADS
    return pl.pallas_call(
        functools.partial(_attn_kernel, lambda_init),
        out_shape=(jax.ShapeDtypeStruct((bsz, seq, d), F32),
                   jax.ShapeDtypeStruct((bsz, seq * ROW_TILES, LANE), F32),
                   jax.ShapeDtypeStruct((ROUTER_ROWS, bsz * seq), F32)),
        grid=(bsz, nq),
        in_specs=[
            pl.BlockSpec((1, DIFF_HEADS, tq, hd2), lambda b, i: (b, 0, i, 0)),
            pl.BlockSpec((1, DIFF_HEADS, seq, hd2), lambda b, i: (b, 0, 0, 0)),
            pl.BlockSpec((1, seq // tk, DIFF_HEADS, hd2, tk), lambda b, i: (b, 0, 0, 0, 0)),
            tok(d),
            pl.BlockSpec((1, 6, d), lambda b, i: (b, 0, 0)),
            const(4, DIFF_HEAD_DIM), const(hd2, 1), const(d, d), const(1, d),
        ] + _logit_specs(d),
        out_specs=(tok(d), pl.BlockSpec((1, tq * ROW_TILES, LANE), lambda b, i: (b, i, 0)),
                   pl.BlockSpec((ROUTER_ROWS, tq), lambda b, i: (0, b * nq + i))),
        scratch_shapes=[pltpu.VMEM((ns, tk, tq), F32), pltpu.VMEM((ns, tk, tq), F32),
                        pltpu.VMEM((ns, tk, tq), BF16), pltpu.VMEM((ns, tk, tq), BF16),
                        pltpu.VMEM((ns, hd2, tq), F32),
                        pltpu.VMEM((DIFF_HEADS, hd2, tq), BF16)],
        compiler_params=_params("arbitrary", "arbitrary"),
        name="attn",
    )(q, k, vt, x, ada, lam, subln, w_out, ngain, *router)


def _combine_kernel(d1c, d2c, d1n, d2n, w1_ref, w2_ref, ys_hbm, x_ref, ada_ref, o_ref, gbuf, gsem):
    tm = x_ref.shape[1]
    step = pl.program_id(0) * pl.num_programs(1) + pl.program_id(1)
    slot = step % 2
    gather = _RowGather(ys_hbm, gbuf, gsem, tm)

    @pl.when(step == 0)
    def _():
        gather.start((d1c, d2c), 0)

    @pl.when(step + 1 < pl.num_programs(0) * pl.num_programs(1))
    def _():
        gather.start((d1n, d2n), 1 - slot)

    gather.wait(slot)
    o_ref[0] = x_ref[0] + ada_ref[0, 5:6, :] * gather.combined(slot, w1_ref, w2_ref)


def _combine(x, ys, dest1, dest2, w1, w2, ada):
    bsz, seq, d = x.shape
    tm = ROW_TILE
    nt = seq // tm
    nt_total = bsz * nt
    col = lambda a: a.reshape(-1, 1)
    idx = lambda a: a.reshape(nt_total, 1, tm)
    return pl.pallas_call(
        _combine_kernel,
        out_shape=jax.ShapeDtypeStruct((bsz, seq, d), F32),
        grid=(bsz, nt),
        in_specs=_gather_specs(tm, nt_total, lambda b, i: b * nt + i) + [
            pl.BlockSpec(memory_space=pl.ANY),
            pl.BlockSpec((1, tm, d), lambda b, i: (b, i, 0)),
            pl.BlockSpec((1, 6, d), lambda b, i: (b, 0, 0)),
        ],
        out_specs=pl.BlockSpec((1, tm, d), lambda b, i: (b, i, 0)),
        scratch_shapes=_gather_scratch(tm),
        compiler_params=_params("arbitrary", "arbitrary"),
        name="combine",
    )(idx(dest1), idx(dest2), idx(dest1), idx(dest2), col(w1), col(w2), ys, x, ada)


def _router_weights(w_group, b_group, w_expert, b_expert):
    d = w_group.shape[0]
    pad = ROUTER_ROWS - N_EXPERTS - N_GROUPS
    w = jnp.concatenate([w_expert.T, w_group.T, jnp.zeros((pad, d), F32)], axis=0)
    b = jnp.concatenate([b_expert, b_group, jnp.zeros((pad,), F32)])[:, None]
    w_hi = w.astype(BF16)
    w_lo = (w - w_hi.astype(F32)).astype(BF16)
    return w_hi, w_lo, b


def _routing_tables(route, cnt):
    rec = lambda r: route[r]
    counts = cnt[:, 0].astype(jnp.int32)
    starts = jnp.cumsum(counts) - counts
    dest1 = starts[rec(0).astype(jnp.int32)] + rec(2).astype(jnp.int32)
    dest2 = starts[rec(1).astype(jnp.int32)] + rec(3).astype(jnp.int32)
    dest = jnp.stack([dest1, dest2], axis=1).reshape(-1)
    _, order = lax.sort((dest, jnp.arange(dest.shape[0], dtype=jnp.int32)), num_keys=1)
    return order, dest1, dest2, rec(4), rec(5), counts


def kernel(x, c, ada_w, ada_b, norm_mix, norm_ffn, ret_w_in, ret_gn, ret_w_out, kv_ada_w, kv_ada_b, kv_norm, kv_w, k_norm, diff_w_q, q_norm, diff_lam, diff_subln, diff_w_out, moe_w_group, moe_b_group, moe_w_expert, moe_b_expert, moe_w1, moe_w3, moe_w2):
    bsz, seq, d = x.shape
    n_tok = bsz * seq

    ada = _ada(c, ada_w, ada_b).reshape(2, bsz, 6, d)
    kvada = _ada(c, kv_ada_w[None], kv_ada_b[None]).reshape(bsz, 2, d)

    half = RET_DK // 2
    inv_freq = 1.0 / (ROPE_BASE ** (jnp.arange(half, dtype=F32) / half))
    ang = jnp.arange(seq, dtype=F32)[:, None] * inv_freq[None, :]
    cos, sin = jnp.cos(ang), jnp.sin(ang)

    routers = [_router_weights(moe_w_group[l], moe_b_group[l], moe_w_expert[l], moe_b_expert[l])
               for l in range(2)]

    def moe_layer(h, logits_t, layer):
        order, dest1, dest2, w1, w2, counts = _routing_tables(*_route(logits_t))
        ys = _moe(h.reshape(n_tok * ROW_TILES, LANE), order, counts, moe_w1, moe_w3, moe_w2, layer)
        return ys, dest1, dest2, w1, w2

    proj = _ret_in(x, ada[0], norm_mix[0][None, :], cos, sin, ret_w_in[0].astype(BF16))
    x1, h, logits_t = _ret_core(proj, x, ada[0], ret_gn[0][None, :], ret_w_out[0].astype(BF16),
                                norm_ffn[0][None, :], routers[0])
    moe0 = moe_layer(h, logits_t, 0)

    x2, q, k, vt = _kvq(x1, *moe0, ada[0], ada[1], kvada, kv_norm[None, :], norm_mix[1][None, :], k_norm, q_norm[0],
                        kv_w[:, :d].astype(BF16), kv_w[:, d:].T.astype(BF16), diff_w_q[0].astype(BF16))
    lambda_init = 0.8 - 0.6 * math.exp(-0.3 * 1)
    x3, h, logits_t = _attn(q, k, vt, x2, ada[1], diff_lam[0], diff_subln[0][:, None], diff_w_out[0].astype(BF16),
                            norm_ffn[1][None, :], routers[1], lambda_init)
    moe1 = moe_layer(h, logits_t, 1)
    return _combine(x3, *moe1, ada[1])
```

```python
import functools
import math

import jax
import jax.numpy as jnp
from jax import lax
from jax.experimental import pallas as pl
from jax.experimental.pallas import tpu as pltpu

F32 = jnp.float32
BF16 = jnp.bfloat16

D_MODEL = 1024
EPS = 1e-6

RET_HEADS = 4
RET_DK = D_MODEL // RET_HEADS
RET_DV = 2 * RET_DK
RET_QK = RET_HEADS * RET_DK
RET_V = RET_HEADS * RET_DV
RET_IN = 2 * RET_QK + 2 * RET_V
ROPE_BASE = 10000.0
RET_CHUNK = 256

DIFF_HEAD_DIM = 64
DIFF_HEADS = D_MODEL // (2 * DIFF_HEAD_DIM)
MASK_CHUNK = 64
ATT_TQ = 512
ATT_TK = 256
ATT_HEADS = 1

N_GROUPS = 4
EXPERTS_PER_GROUP = 4
N_EXPERTS = N_GROUPS * EXPERTS_PER_GROUP
TOP_K = 2
TOP_K_SHIFT = 1
MOE_BM = 512
MOE_TN = 256
KVQ_TN = 256
MOE_AHEAD = 2
ROUTER_ROWS = 32
ROUTE_ROWS = 8
ROW_TILE = 512
ROUTE_TILE = 2048
ROUTE_SEG = 256

LANE = 128
SUBLANE = 8
VMEM_LIMIT = 56 * 1024 * 1024


def _dot(a, b):
    return jnp.dot(a, b, preferred_element_type=F32)


def _dot_nt(a, b):
    return lax.dot_general(a, b, (((1,), (1,)), ((), ())), preferred_element_type=F32)


def _dot_tn(a, b):
    return lax.dot_general(a, b, (((0,), (0,)), ((), ())), preferred_element_type=F32)


def _silu(x):
    return x * (1.0 / (1.0 + jnp.exp(-x)))


def _rms(x):
    return x * lax.rsqrt(jnp.mean(x * x, axis=-1, keepdims=True) + EPS)


def _params(*sem, flags=None):
    return pltpu.CompilerParams(dimension_semantics=sem, vmem_limit_bytes=VMEM_LIMIT, flags=flags)


ROW_TILES = D_MODEL // LANE


def _load_row_tiles(ref, lead, rows):
    return jnp.concatenate(
        [ref[lead + (pl.ds(c, rows, stride=ROW_TILES), slice(None))] for c in range(ROW_TILES)], axis=1)


def _store_row_tiles(ref, lead, val):
    rows = val.shape[0]
    for c in range(ROW_TILES):
        ref[lead + (pl.ds(c, rows, stride=ROW_TILES), slice(None))] = val[:, c * LANE:(c + 1) * LANE]


def _row_tile(ref, lead, r):
    return ref.at[lead + (pl.ds(pl.multiple_of(r * ROW_TILES, ROW_TILES), ROW_TILES), slice(None))]


def _ada_kernel(c_ref, w_ref, b_ref, o_ref):
    ca = _silu(c_ref[...])
    o_ref[0] = jnp.dot(ca, w_ref[0], preferred_element_type=F32,
                       precision=lax.Precision.HIGHEST) + b_ref[0]


def _ada(c, w, b):
    n_l, d, n = w.shape
    bsz = c.shape[0]
    tn = 1024
    return pl.pallas_call(
        _ada_kernel,
        out_shape=jax.ShapeDtypeStruct((n_l, bsz, n), F32),
        grid=(n_l, n // tn),
        in_specs=[
            pl.BlockSpec((bsz, d), lambda l, j: (0, 0)),
            pl.BlockSpec((1, d, tn), lambda l, j: (l, 0, j)),
            pl.BlockSpec((1, 1, tn), lambda l, j: (l, 0, j)),
        ],
        out_specs=pl.BlockSpec((1, bsz, tn), lambda l, j: (l, 0, j)),
        compiler_params=_params("arbitrary", "arbitrary"),
        name="ada",
    )(c, w, b.reshape(n_l, 1, n))


def _ffn_prenorm_logits(xn, gain, shift, scale, wr_hi_ref, wr_lo_ref, br_ref, lt_ref):
    h = _rms(xn) * gain
    h = h * (1.0 + scale) + shift
    h_hi = h.astype(BF16)
    h_lo = (h - h_hi.astype(F32)).astype(BF16)
    w_hi = wr_hi_ref[...]
    lt_ref[...] = _dot_nt(w_hi, h_hi) + _dot_nt(wr_lo_ref[...], h_hi) + _dot_nt(w_hi, h_lo) + br_ref[...]
    return h


def _route_kernel(lt_ref, tri_ref, route_ref, cnt_ref, carry):
    lt = lt_ref[...]
    c = lt.shape[1]
    seg = tri_ref.shape[0]
    neg_inf = jnp.float32(-jnp.inf)

    grow = lax.broadcasted_iota(jnp.int32, (SUBLANE, c), 0).astype(F32)
    g = jnp.where(grow < N_GROUPS, lt[N_EXPERTS:N_EXPERTS + SUBLANE], neg_inf)
    gmax = jnp.max(g, axis=0, keepdims=True)
    gidx = jnp.min(jnp.where(g == gmax, grow, float(SUBLANE)), axis=0, keepdims=True)
    gate = 1.0 / jnp.sum(jnp.exp(g - gmax), axis=0, keepdims=True)

    erow_i = lax.broadcasted_iota(jnp.int32, (N_EXPERTS, c), 0)
    erow = erow_i.astype(F32)
    egroup = (erow_i // EXPERTS_PER_GROUP).astype(F32)
    el = jnp.where(egroup == gidx, lt[0:N_EXPERTS], neg_inf)
    m1 = jnp.max(el, axis=0, keepdims=True)
    i1 = jnp.min(jnp.where(el == m1, erow, float(N_EXPERTS)), axis=0, keepdims=True)
    el2 = jnp.where(erow == i1, neg_inf, el)
    m2 = jnp.max(el2, axis=0, keepdims=True)
    i2 = jnp.min(jnp.where(el2 == m2, erow, float(N_EXPERTS)), axis=0, keepdims=True)
    t = jnp.exp(m2 - m1)
    den = 1.0 / (1.0 + t)
    w1 = gate * den
    w2 = gate * t * den

    @pl.when(pl.program_id(0) == 0)
    def _():
        carry[...] = jnp.zeros_like(carry)

    oh1 = erow == i1
    oh2 = erow == i2
    oh = jnp.where(jnp.logical_or(oh1, oh2), 1.0, 0.0)
    offset = carry[...]
    before = []
    for s in range(c // seg):
        oh_s = oh[:, s * seg:(s + 1) * seg]
        before.append(offset + _dot(oh_s.astype(BF16), tri_ref[...]))
        offset = offset + jnp.sum(oh_s, axis=1, keepdims=True)
    before = jnp.concatenate(before, axis=1)
    rank1 = jnp.sum(jnp.where(oh1, before, 0.0), axis=0, keepdims=True)
    rank2 = jnp.sum(jnp.where(oh2, before, 0.0), axis=0, keepdims=True)
    carry[...] = offset
    cnt_ref[...] = jnp.broadcast_to(offset, cnt_ref.shape)

    rrow = lax.broadcasted_iota(jnp.int32, (ROUTE_ROWS, c), 0)
    rec = jnp.zeros((ROUTE_ROWS, c), F32)
    for idx, val in enumerate((i1, i2, rank1, rank2, w1, w2)):
        rec = jnp.where(rrow == idx, val, rec)
    route_ref[...] = rec


def _route(lt):
    n_tok = lt.shape[1]
    tile = min(ROUTE_TILE, n_tok)
    t = jnp.arange(ROUTE_SEG, dtype=jnp.int32)
    earlier = (t[:, None] < t[None, :]).astype(BF16)
    return pl.pallas_call(
        _route_kernel,
        out_shape=(jax.ShapeDtypeStruct((ROUTE_ROWS, n_tok), F32),
                   jax.ShapeDtypeStruct((N_EXPERTS, LANE), F32)),
        grid=(n_tok // tile,),
        in_specs=[pl.BlockSpec((ROUTER_ROWS, tile), lambda i: (0, i)),
                  pl.BlockSpec((ROUTE_SEG, ROUTE_SEG), lambda i: (0, 0))],
        out_specs=(pl.BlockSpec((ROUTE_ROWS, tile), lambda i: (0, i)),
                   pl.BlockSpec((N_EXPERTS, LANE), lambda i: (0, 0))),
        scratch_shapes=[pltpu.VMEM((N_EXPERTS, 1), F32)],
        compiler_params=_params("arbitrary"),
        name="route",
    )(lt, earlier)


def _logit_specs(d):
    const = lambda *shape: pl.BlockSpec(shape, lambda b, i: (0,) * len(shape))
    return [const(ROUTER_ROWS, d), const(ROUTER_ROWS, d), const(ROUTER_ROWS, 1)]


def _ret_in_kernel(x_ref, ada_ref, gain_ref, cos_ref, sin_ref, w_ref, o_ref):
    x = x_ref[0]
    shift = ada_ref[0, 0:1, :]
    scale = ada_ref[0, 1:2, :]
    h = (_rms(x) * gain_ref[...]) * (1.0 + scale) + shift
    hb = h.astype(BF16)
    cos = cos_ref[...]
    sin = sin_ref[...]
    half = RET_DK // 2
    for j in range(RET_IN // D_MODEL):
        p = _dot(hb, w_ref[:, j * D_MODEL:(j + 1) * D_MODEL])
        if j < 2:
            post = 1.0 if j == 0 else RET_DK ** -0.5
            for hd in range(RET_HEADS):
                lo = hd * RET_DK
                x1 = p[:, lo:lo + half]
                x2 = p[:, lo + half:lo + RET_DK]
                o_ref[0, :, j * D_MODEL + lo:j * D_MODEL + lo + half] = (
                    (x1 * cos - x2 * sin) * post).astype(BF16)
                o_ref[0, :, j * D_MODEL + lo + half:j * D_MODEL + lo + RET_DK] = (
                    (x2 * cos + x1 * sin) * post).astype(BF16)
        else:
            o_ref[0, :, j * D_MODEL:(j + 1) * D_MODEL] = p.astype(BF16)


def _ret_in(x, ada, gain, cos, sin, w_in):
    bsz, seq, d = x.shape
    tm = ROW_TILE
    return pl.pallas_call(
        _ret_in_kernel,
        out_shape=jax.ShapeDtypeStruct((bsz, seq, RET_IN), BF16),
        grid=(bsz, seq // tm),
        in_specs=[
            pl.BlockSpec((1, tm, d), lambda b, i: (b, i, 0)),
            pl.BlockSpec((1, 6, d), lambda b, i: (b, 0, 0)),
            pl.BlockSpec((1, d), lambda b, i: (0, 0)),
            pl.BlockSpec((tm, RET_DK // 2), lambda b, i: (i, 0)),
            pl.BlockSpec((tm, RET_DK // 2), lambda b, i: (i, 0)),
            pl.BlockSpec((d, RET_IN), lambda b, i: (0, 0), pipeline_mode=pl.Buffered(1)),
        ],
        out_specs=pl.BlockSpec((1, tm, RET_IN), lambda b, i: (b, i, 0)),
        compiler_params=_params("arbitrary", "arbitrary"),
        name="ret_in",
    )(x, ada, gain, cos, sin, w_in)


def _ret_core_kernel(q_ref, k_ref, v_ref, g_ref, x_ref, ada_ref, intra_ref, qd_ref, kd_ref, cd_ref,
                     gn_ref, wo_ref, ngain_ref, wr_hi_ref, wr_lo_ref, br_ref,
                     xo_ref, h_ref, lt_ref, state, ybuf):
    @pl.when(pl.program_id(1) == 0)
    def _():
        state[...] = jnp.zeros_like(state)

    for hd in range(RET_HEADS):
        q = q_ref[0, :, hd * RET_DK:(hd + 1) * RET_DK]
        k = k_ref[0, :, hd * RET_DK:(hd + 1) * RET_DK]
        v = v_ref[0, :, hd * RET_DV:(hd + 1) * RET_DV]
        st = state[hd]
        s = _dot_nt(q, k) * intra_ref[hd]
        o = _dot(s.astype(BF16), v) + qd_ref[hd] * _dot(q, st.astype(BF16))
        kdec = (k.astype(F32) * kd_ref[hd]).astype(BF16)
        state[hd] = st * cd_ref[hd] + _dot_tn(kdec, v)
        o = _rms(o) * gn_ref[:, hd * RET_DV:(hd + 1) * RET_DV]
        g = g_ref[0, :, hd * RET_DV:(hd + 1) * RET_DV].astype(F32)
        ybuf[:, hd * RET_DV:(hd + 1) * RET_DV] = (_silu(g) * o).astype(BF16)

    y = _dot(ybuf[...], wo_ref[...])
    xn = x_ref[0] + ada_ref[0, 2:3, :] * y
    xo_ref[0] = xn
    h = _ffn_prenorm_logits(xn, ngain_ref[...], ada_ref[0, 3:4, :], ada_ref[0, 4:5, :],
                            wr_hi_ref, wr_lo_ref, br_ref, lt_ref)
    _store_row_tiles(h_ref, (0,), h)


def _ret_core(proj, x, ada, gn, w_out, ngain, router):
    bsz, seq, d = x.shape
    c = RET_CHUNK
    f32 = F32
    log_gamma = jnp.log1p(-jnp.exp2(-5.0 - jnp.arange(RET_HEADS, dtype=f32)))
    n = jnp.arange(c, dtype=f32)
    rel = n[:, None] - n[None, :]
    intra = jnp.where(rel >= 0, jnp.exp(jnp.maximum(rel, 0.0)[None] * log_gamma[:, None, None]), 0.0)
    qd = jnp.exp((n + 1.0)[None, :] * log_gamma[:, None])[:, :, None]
    kd = jnp.exp((c - 1.0 - n)[None, :] * log_gamma[:, None])[:, :, None]
    cd = jnp.broadcast_to(jnp.exp(c * log_gamma)[:, None, None], (RET_HEADS, 1, RET_DV))
    const = lambda *shape: pl.BlockSpec(shape, lambda b, i: (0,) * len(shape))
    tok = lambda w, j: pl.BlockSpec((1, c, w), lambda b, i: (b, i, j))
    nc = seq // c
    return pl.pallas_call(
        _ret_core_kernel,
        out_shape=(jax.ShapeDtypeStruct((bsz, seq, d), f32),
                   jax.ShapeDtypeStruct((bsz, seq * ROW_TILES, LANE), f32),
                   jax.ShapeDtypeStruct((ROUTER_ROWS, bsz * seq), f32)),
        grid=(bsz, nc),
        in_specs=[
            tok(RET_QK, 0), tok(RET_QK, 1), tok(RET_V, 1), tok(RET_V, 2),
            tok(d, 0),
            pl.BlockSpec((1, 6, d), lambda b, i: (b, 0, 0)),
            const(RET_HEADS, c, c), const(RET_HEADS, c, 1), const(RET_HEADS, c, 1),
            const(RET_HEADS, 1, RET_DV),
            const(1, RET_V), const(RET_V, d), const(1, d),
        ] + _logit_specs(d),
        out_specs=(tok(d, 0), pl.BlockSpec((1, c * ROW_TILES, LANE), lambda b, i: (b, i, 0)),
                   pl.BlockSpec((ROUTER_ROWS, c), lambda b, i: (0, b * nc + i))),
        scratch_shapes=[pltpu.VMEM((RET_HEADS, RET_DK, RET_DV), f32),
                        pltpu.VMEM((c, RET_V), BF16)],
        compiler_params=_params("arbitrary", "arbitrary"),
        name="ret_core",
    )(proj, proj, proj, proj, x, ada, intra, qd, kd, cd, gn, w_out, ngain, *router)


def _moe_kernel(blk_ref, exp_ref, nitem_ref, start_ref, cnt_ref, *refs):
    tok_first, refs = refs[:MOE_AHEAD], refs[MOE_AHEAD:]
    tok_ahead, h_hbm, w1_ref, w3_ref, w2_ref, o_ref, xbuf, w1b, w3b, w2b, gsem = refs
    n_buf = MOE_AHEAD + 1
    w = pl.program_id(0)
    bm = MOE_BM
    n_item = nitem_ref[0]
    d = w1b.shape[0]
    e = exp_ref[w]
    blk = blk_ref[w]
    slot = blk % n_buf
    ahead_slot = (blk + MOE_AHEAD) % n_buf
    prev = jnp.maximum(w - 1, 0)
    first_visit = jnp.logical_or(w == 0, blk_ref[prev] != blk)
    live = w < n_item

    def gather_row(tok_ref, s, r):
        pltpu.make_async_copy(_row_tile(h_hbm, (), tok_ref[0, 0, r]), _row_tile(xbuf, (s,), r), gsem.at[s]).start()

    def wait_gather(s):
        pltpu.make_async_copy(h_hbm.at[pl.ds(0, bm * ROW_TILES), :], xbuf.at[s], gsem.at[s]).wait()

    def expert(row_copies):
        n_dots = 2 * (d // MOE_TN)
        per_dot = -(-len(row_copies) // n_dots)
        groups = iter([row_copies[g * per_dot:(g + 1) * per_dot] for g in range(n_dots)])

        def piece(lhs, w_ref, c):
            for issue in next(groups, ()):
                issue()
            return _dot(lhs, w_ref[:, c * MOE_TN:(c + 1) * MOE_TN])

        x = _load_row_tiles(xbuf, (slot,), bm).astype(BF16)
        hid = []
        for c in range(d // MOE_TN):
            a = piece(x, w1b, c)
            b = piece(x, w3b, c)
            hid.append((_silu(a) * b).astype(BF16))
        hid = jnp.concatenate(hid, axis=1)
        y = jnp.concatenate([piece(hid, w2b, c) for c in range(d // MOE_TN)], axis=1)
        row = blk * bm + lax.broadcasted_iota(jnp.int32, (bm, 1), 0)
        mine = jnp.logical_and(row >= start_ref[e], row < start_ref[e] + cnt_ref[e])
        return y, mine

    @pl.when(jnp.logical_and(live, jnp.logical_or(w == 0, exp_ref[prev] != e)))
    def _():
        w1b[...] = w1_ref[0, 0].astype(BF16)
        w3b[...] = w3_ref[0, 0].astype(BF16)
        w2b[...] = w2_ref[0, 0].astype(BF16)

    @pl.when(w == 0)
    def _():
        for k in range(MOE_AHEAD):
            def body(r, carry, k=k):
                gather_row(tok_first[k], k, r)
                return carry
            lax.fori_loop(0, bm, body, 0, unroll=8)

    @pl.when(jnp.logical_and(live, first_visit))
    def _():
        wait_gather(slot)
        y, mine = expert([functools.partial(gather_row, tok_ahead, ahead_slot, r) for r in range(bm)])
        _store_row_tiles(o_ref, (), jnp.where(mine, y, 0.0))

    @pl.when(jnp.logical_and(live, jnp.logical_not(first_visit)))
    def _():
        y, mine = expert([])
        _store_row_tiles(o_ref, (), jnp.where(mine, y, _load_row_tiles(o_ref, (), bm)))

    @pl.when(w == n_item - 1)
    def _():
        for k in range(1, n_buf):
            wait_gather((blk + k) % n_buf)


def _moe(h, order, counts, w1, w3, w2, layer):
    m = order.shape[0]
    d = w1.shape[-1]
    bm = MOE_BM
    n_blocks = m // bm
    n_items = n_blocks + N_EXPERTS - 1
    ends = jnp.cumsum(counts)
    starts = ends - counts
    first_blk = starts // bm
    last_blk = jnp.maximum(ends - 1, starts) // bm
    visits = jnp.where(counts > 0, last_blk - first_blk + 1, 0)
    item_end = jnp.cumsum(visits)
    item_start = item_end - visits
    n_item = item_end[-1]
    w = jnp.minimum(jnp.arange(n_items, dtype=jnp.int32), n_item - 1)
    item_e = jnp.sum(w[:, None] >= item_end[None, :], axis=1).astype(jnp.int32)
    item_blk = (first_blk[item_e] + w - item_start[item_e]).astype(jnp.int32)
    wspec = lambda: pl.BlockSpec((1, 1, d, d), lambda i, blk, e, *_: (layer, e[i], 0, 0))
    idx = lambda f: pl.BlockSpec((1, 1, bm), f, memory_space=pltpu.SMEM)
    tok3 = (order >> TOP_K_SHIFT).reshape(n_blocks, 1, bm)
    first = [idx(lambda i, blk, *_, k=k: (k, 0, 0)) for k in range(MOE_AHEAD)]
    ahead = idx(lambda i, blk, *_: (jnp.minimum(blk[i] + MOE_AHEAD, n_blocks - 1), 0, 0))
    return pl.pallas_call(
        _moe_kernel,
        out_shape=jax.ShapeDtypeStruct((m * ROW_TILES, LANE), F32),
        grid_spec=pltpu.PrefetchScalarGridSpec(
            num_scalar_prefetch=5,
            grid=(n_items,),
            in_specs=first + [ahead, pl.BlockSpec(memory_space=pl.ANY), wspec(), wspec(), wspec()],
            out_specs=pl.BlockSpec((bm * ROW_TILES, LANE), lambda i, blk, *_: (blk[i], 0)),
            scratch_shapes=[pltpu.VMEM((MOE_AHEAD + 1, bm * ROW_TILES, LANE), F32),
                            pltpu.VMEM((d, d), BF16), pltpu.VMEM((d, d), BF16), pltpu.VMEM((d, d), BF16),
                            pltpu.SemaphoreType.DMA((MOE_AHEAD + 1,))],
        ),
        compiler_params=_params("arbitrary"),
        name="moe",
    )(item_blk, item_e, n_item.astype(jnp.int32).reshape(1), starts.astype(jnp.int32), counts.astype(jnp.int32),
      *([tok3] * (MOE_AHEAD + 1)), h, w1, w3, w2)


class _RowGather:
    def __init__(self, ys_hbm, buf, sem, tm):
        self.ys, self.buf, self.sem, self.tm = ys_hbm, buf, sem, tm

    def row_copy(self, d_ref, slot, k, t):
        pltpu.make_async_copy(_row_tile(self.ys, (), d_ref[0, 0, t]), _row_tile(self.buf, (slot, k), t),
                              self.sem.at[slot]).start(priority=k)

    def start(self, d_refs, slot):
        def body(t, carry):
            for k, d_ref in enumerate(d_refs):
                self.row_copy(d_ref, slot, k, t)
            return carry
        lax.fori_loop(0, self.tm, body, 0, unroll=8)

    def unrolled(self, d_refs, slot):
        return [functools.partial(self.row_copy, d_ref, slot, k, t)
                for t in range(self.tm) for k, d_ref in enumerate(d_refs)]

    def wait(self, slot):
        for k in range(TOP_K):
            pltpu.make_async_copy(self.ys.at[pl.ds(0, self.tm * ROW_TILES), :], self.buf.at[slot, k],
                                  self.sem.at[slot]).wait()

    def combined(self, slot, w1_ref, w2_ref):
        return (w1_ref[...] * _load_row_tiles(self.buf, (slot, 0), self.tm)
                + w2_ref[...] * _load_row_tiles(self.buf, (slot, 1), self.tm))


def _gather_specs(tm, nt_total, step_of):
    cur = lambda: pl.BlockSpec((1, 1, tm), lambda *g: (step_of(*g), 0, 0), memory_space=pltpu.SMEM)
    nxt = lambda: pl.BlockSpec((1, 1, tm), lambda *g: (jnp.minimum(step_of(*g) + 1, nt_total - 1), 0, 0),
                               memory_space=pltpu.SMEM)
    wcol = lambda: pl.BlockSpec((tm, 1), lambda *g: (step_of(*g), 0))
    return [cur(), cur(), nxt(), nxt(), wcol(), wcol()]


def _gather_scratch(tm):
    return [pltpu.VMEM((2, TOP_K, tm * ROW_TILES, LANE), F32), pltpu.SemaphoreType.DMA((2,))]


def _group_norm64(t, ind_ref, indt_ref):
    ss = _dot((t * t).astype(BF16), ind_ref[...])
    r = lax.rsqrt(ss * (1.0 / DIFF_HEAD_DIM) + EPS)
    r_hi = r.astype(BF16)
    r_lo = (r - r_hi.astype(F32)).astype(BF16)
    rb = _dot(jnp.concatenate([r_hi, r_lo], axis=1), indt_ref[...])
    return t * rb


def _kvq_kernel(d1c, d2c, d1n, d2n, w1_ref, w2_ref, ys_hbm, x_ref, ada0_ref, ada1_ref, kvada_ref, kvgain_ref,
                qgain_ref, knorm_ref, qnorm_ref, wk_ref, wvt_ref, wq_ref, ind_ref, indt_ref,
                xo_ref, q_ref, k_ref, vt_ref, gbuf, gsem):
    tm = x_ref.shape[1]
    d = x_ref.shape[2]
    step = pl.program_id(0) * pl.num_programs(1) + pl.program_id(1)
    slot = step % 2
    gather = _RowGather(ys_hbm, gbuf, gsem, tm)

    @pl.when(step == 0)
    def _():
        gather.start((d1c, d2c), 0)

    gather.wait(slot)
    xn = x_ref[0] + ada0_ref[0, 5:6, :] * gather.combined(slot, w1_ref, w2_ref)
    xo_ref[0] = xn
    copies = gather.unrolled((d1n, d2n), 1 - slot)
    n_pieces = 3 * (d // KVQ_TN)
    per_piece = -(-len(copies) // n_pieces)
    groups = iter([copies[g * per_piece:(g + 1) * per_piece] for g in range(n_pieces)])

    def issue_group():
        for issue in next(groups):
            issue()

    def project(lhs, w_ref):
        out = []
        for c in range(d // KVQ_TN):
            issue_group()
            out.append(_dot(lhs, w_ref[:, c * KVQ_TN:(c + 1) * KVQ_TN]))
        return jnp.concatenate(out, axis=1)

    r = _rms(xn)
    hk = ((r * kvgain_ref[...]) * (1.0 + kvada_ref[0, 1:2, :]) + kvada_ref[0, 0:1, :]).astype(BF16)
    hq = ((r * qgain_ref[...]) * (1.0 + ada1_ref[0, 1:2, :]) + ada1_ref[0, 0:1, :]).astype(BF16)
    hd2 = 2 * DIFF_HEAD_DIM
    kk = project(hk, wk_ref)
    kn = (_group_norm64(kk, ind_ref, indt_ref) * knorm_ref[...]).astype(BF16)
    qq = project(hq, wq_ref)
    qn = (_group_norm64(qq, ind_ref, indt_ref) * qnorm_ref[...]).astype(BF16)
    for hd in range(DIFF_HEADS):
        k_ref[0, hd] = kn[:, hd * hd2:(hd + 1) * hd2]
        q_ref[0, hd] = qn[:, hd * hd2:(hd + 1) * hd2]
    vt = []
    for c in range(d // KVQ_TN):
        issue_group()
        vt.append(_dot_nt(wvt_ref[c * KVQ_TN:(c + 1) * KVQ_TN, :], hk))
    vt = jnp.concatenate(vt, axis=0).astype(BF16)
    for j in range(tm // ATT_TK):
        vt_ref[0, j] = vt[:, j * ATT_TK:(j + 1) * ATT_TK].reshape(DIFF_HEADS, hd2, ATT_TK)

    @pl.when(step == pl.num_programs(0) * pl.num_programs(1) - 1)
    def _():
        gather.wait(1 - slot)


def _kvq(x, ys, dest1, dest2, w1, w2, ada0, ada1, kvada, kv_gain, q_gain, k_norm, q_norm, w_k, w_vt, w_q):
    bsz, seq, d = x.shape
    tm = ROW_TILE
    nt = seq // tm
    nt_total = bsz * nt
    lane_group = jnp.arange(d, dtype=jnp.int32) // DIFF_HEAD_DIM
    ind = (lane_group[:, None] == jnp.arange(LANE, dtype=jnp.int32)[None, :]).astype(BF16)
    indt = jnp.concatenate([ind.T, ind.T], axis=0)
    reps = d // DIFF_HEAD_DIM
    q_scale = jnp.tile(q_norm, reps)[None, :] * (DIFF_HEAD_DIM ** -0.5 * math.log2(math.e))
    const = lambda *shape: pl.BlockSpec(shape, lambda b, i: (0,) * len(shape))
    tok = lambda w: pl.BlockSpec((1, tm, w), lambda b, i: (b, i, 0))
    hd2 = 2 * DIFF_HEAD_DIM
    heads = pl.BlockSpec((1, DIFF_HEADS, tm, hd2), lambda b, i: (b, 0, i, 0))
    ada_spec = lambda rows: pl.BlockSpec((1, rows, d), lambda b, i: (b, 0, 0))
    col = lambda a: a.reshape(-1, 1)
    idx = lambda a: a.reshape(nt_total, 1, tm)
    return pl.pallas_call(
        _kvq_kernel,
        out_shape=(jax.ShapeDtypeStruct((bsz, seq, d), F32),
                   jax.ShapeDtypeStruct((bsz, DIFF_HEADS, seq, hd2), BF16),
                   jax.ShapeDtypeStruct((bsz, DIFF_HEADS, seq, hd2), BF16),
                   jax.ShapeDtypeStruct((bsz, seq // ATT_TK, DIFF_HEADS, hd2, ATT_TK), BF16)),
        grid=(bsz, nt),
        in_specs=_gather_specs(tm, nt_total, lambda b, i: b * nt + i) + [
            pl.BlockSpec(memory_space=pl.ANY),
            tok(d),
            ada_spec(6), ada_spec(6), ada_spec(2),
            const(1, d), const(1, d), const(1, d), const(1, d),
            const(d, d), const(d, d), const(d, d), const(d, LANE), const(2 * LANE, d),
        ],
        out_specs=(tok(d), heads, heads,
                   pl.BlockSpec((1, tm // ATT_TK, DIFF_HEADS, hd2, ATT_TK), lambda b, i: (b, i, 0, 0, 0))),
        scratch_shapes=_gather_scratch(tm),
        compiler_params=_params("arbitrary", "arbitrary"),
        name="kvq",
    )(idx(dest1), idx(dest2), idx(dest1), idx(dest2), col(w1), col(w2), ys, x, ada0, ada1, kvada,
      kv_gain, q_gain, jnp.tile(k_norm, reps)[None, :], q_scale, w_k, w_vt, w_q, ind, indt)


def _attn_kernel(lambda_init, q_ref, k_ref, vt_ref, x_ref, ada_ref, lam_ref, subln_ref, wo_ref, ngain_ref,
                 wr_hi_ref, wr_lo_ref, br_ref, xo_ref, h_ref, lt_ref,
                 s_a, s_b, p_a, p_b, acc, obuf):
    i = pl.program_id(1)
    tq, tk = ATT_TQ, ATT_TK
    hd2 = 2 * DIFF_HEAD_DIM
    n_kblk = k_ref.shape[2] // tk
    lam = lam_ref[...]
    lam_full = (jnp.exp(jnp.sum(lam[0:1] * lam[1:2], axis=-1, keepdims=True))
                - jnp.exp(jnp.sum(lam[2:3] * lam[3:4], axis=-1, keepdims=True)) + lambda_init)
    lane = lax.broadcasted_iota(jnp.int32, (1, hd2), 1)
    keep = (jnp.where(lane < DIFF_HEAD_DIM, 1.0, 0.0).astype(BF16),
            jnp.where(lane >= DIFF_HEAD_DIM, 1.0, 0.0).astype(BF16))
    key = lax.broadcasted_iota(jnp.int32, (tk, tq), 0)
    qry = lax.broadcasted_iota(jnp.int32, (1, tq), 1)
    key_limit = ((i * tq + qry) // MASK_CHUNK + 1) * MASK_CHUNK
    neg_inf = jnp.float32(-jnp.inf)
    first_masked = i * (tq // tk)
    n_blocks = first_masked + tq // tk
    n_pairs = (n_blocks + 1) // 2
    n_plain = jnp.maximum((first_masked - 1) // 2, 0)

    streams = [(u, t) for u in range(ATT_HEADS) for t in range(2)]

    def head_group(hg, _):
        hds = [hg * ATT_HEADS + u for u in range(ATT_HEADS)]
        qs = []
        for hd in hds:
            qh = q_ref[0, hd]
            qs.append((qh * keep[0], qh * keep[1]))

        def produce_scores(jb, s_out, masked):
            start = pl.multiple_of(jnp.minimum(jb, n_kblk - 1) * tk, tk)
            visible = key < key_limit - jb * tk
            maxima = []
            for n, (u, t) in enumerate(streams):
                st = _dot_nt(k_ref[0, hds[u], pl.ds(start, tk), :], qs[u][t])
                if masked:
                    st = jnp.where(visible, st, neg_inf)
                s_out[n] = st
                maxima.append(jnp.max(st, axis=0, keepdims=True))
            return tuple(maxima)

        def step(j, s_in, s_out, p_in, p_out, state, masked):
            stats, blk_max = state
            jv = jnp.clip(j - 1, 0, n_kblk - 1)
            next_max = produce_scores(j + 1, s_out, masked)
            new_stats = []
            for n, (u, t) in enumerate(streams):
                m, l = stats[n]
                m_new = jnp.maximum(m, blk_max[n])
                alpha = jnp.exp2(m - m_new)
                p = jnp.exp2(s_in[n] - m_new)
                p_out[n] = p.astype(BF16)
                new_stats.append((m_new, alpha * l + jnp.sum(p, axis=0, keepdims=True)))
                acc[n] = alpha * (acc[n] + _dot(vt_ref[0, jv, hds[u]], p_in[n]))
            return tuple(new_stats), next_max

        p_b[...] = jnp.zeros_like(p_b)
        acc[...] = jnp.zeros_like(acc)
        stat0 = (jnp.full((1, tq), neg_inf, F32), jnp.zeros((1, tq), F32))
        state = ((stat0,) * len(streams), produce_scores(0, s_a, True))

        def pair(masked, jj, state):
            state = step(2 * jj, s_a, s_b, p_b, p_a, state, masked)
            return step(2 * jj + 1, s_b, s_a, p_a, p_b, state, masked)

        state = lax.fori_loop(0, n_plain, functools.partial(pair, False), state)
        (stats, _) = lax.fori_loop(n_plain, n_pairs, functools.partial(pair, True), state)
        jv = jnp.minimum(2 * n_pairs - 1, n_kblk - 1)
        for u, hd in enumerate(hds):
            vtb = vt_ref[0, jv, hd]
            a1 = acc[2 * u] + _dot(vtb, p_b[2 * u])
            a2 = acc[2 * u + 1] + _dot(vtb, p_b[2 * u + 1])
            o = a1 * (1.0 / stats[2 * u][1]) - lam_full * (a2 * (1.0 / stats[2 * u + 1][1]))
            o = o * lax.rsqrt(jnp.mean(o * o, axis=0, keepdims=True) + EPS)
            obuf[hd] = (o * (subln_ref[...] * (1.0 - lambda_init))).astype(BF16)
        return 0

    lax.fori_loop(0, DIFF_HEADS // ATT_HEADS, head_group, 0)

    y = _dot_tn(obuf[...].reshape(DIFF_HEADS * hd2, tq), wo_ref[...])
    xn = x_ref[0] + ada_ref[0, 2:3, :] * y
    xo_ref[0] = xn
    h = _ffn_prenorm_logits(xn, ngain_ref[...], ada_ref[0, 3:4, :], ada_ref[0, 4:5, :],
                            wr_hi_ref, wr_lo_ref, br_ref, lt_ref)
    _store_row_tiles(h_ref, (0,), h)


def _attn(q, k, vt, x, ada, lam, subln, w_out, ngain, router, lambda_init):
    bsz, seq, d = x.shape
    tq, tk = ATT_TQ, ATT_TK
    hd2 = 2 * DIFF_HEAD_DIM
    const = lambda *shape: pl.BlockSpec(shape, lambda b, i: (0,) * len(shape))
    tok = lambda w: pl.BlockSpec((1, tq, w), lambda b, i: (b, i, 0))
    nq = seq // tq
    ns = 2 * ATT_HEADS
    return pl.pallas_call(
        functools.partial(_attn_kernel, lambda_init),
        out_shape=(jax.ShapeDtypeStruct((bsz, seq, d), F32),
                   jax.ShapeDtypeStruct((bsz, seq * ROW_TILES, LANE), F32),
                   jax.ShapeDtypeStruct((ROUTER_ROWS, bsz * seq), F32)),
        grid=(bsz, nq),
        in_specs=[
            pl.BlockSpec((1, DIFF_HEADS, tq, hd2), lambda b, i: (b, 0, i, 0)),
            pl.BlockSpec((1, DIFF_HEADS, seq, hd2), lambda b, i: (b, 0, 0, 0)),
            pl.BlockSpec((1, seq // tk, DIFF_HEADS, hd2, tk), lambda b, i: (b, 0, 0, 0, 0)),
            tok(d),
            pl.BlockSpec((1, 6, d), lambda b, i: (b, 0, 0)),
            const(4, DIFF_HEAD_DIM), const(hd2, 1), const(d, d), const(1, d),
        ] + _logit_specs(d),
        out_specs=(tok(d), pl.BlockSpec((1, tq * ROW_TILES, LANE), lambda b, i: (b, i, 0)),
                   pl.BlockSpec((ROUTER_ROWS, tq), lambda b, i: (0, b * nq + i))),
        scratch_shapes=[pltpu.VMEM((ns, tk, tq), F32), pltpu.VMEM((ns, tk, tq), F32),
                        pltpu.VMEM((ns, tk, tq), BF16), pltpu.VMEM((ns, tk, tq), BF16),
                        pltpu.VMEM((ns, hd2, tq), F32),
                        pltpu.VMEM((DIFF_HEADS, hd2, tq), BF16)],
        compiler_params=_params("arbitrary", "arbitrary"),
        name="attn",
    )(q, k, vt, x, ada, lam, subln, w_out, ngain, *router)


def _combine_kernel(d1c, d2c, d1n, d2n, w1_ref, w2_ref, ys_hbm, x_ref, ada_ref, o_ref, gbuf, gsem):
    tm = x_ref.shape[1]
    step = pl.program_id(0) * pl.num_programs(1) + pl.program_id(1)
    slot = step % 2
    gather = _RowGather(ys_hbm, gbuf, gsem, tm)

    @pl.when(step == 0)
    def _():
        gather.start((d1c, d2c), 0)

    @pl.when(step + 1 < pl.num_programs(0) * pl.num_programs(1))
    def _():
        gather.start((d1n, d2n), 1 - slot)

    gather.wait(slot)
    o_ref[0] = x_ref[0] + ada_ref[0, 5:6, :] * gather.combined(slot, w1_ref, w2_ref)


def _combine(x, ys, dest1, dest2, w1, w2, ada):
    bsz, seq, d = x.shape
    tm = ROW_TILE
    nt = seq // tm
    nt_total = bsz * nt
    col = lambda a: a.reshape(-1, 1)
    idx = lambda a: a.reshape(nt_total, 1, tm)
    return pl.pallas_call(
        _combine_kernel,
        out_shape=jax.ShapeDtypeStruct((bsz, seq, d), F32),
        grid=(bsz, nt),
        in_specs=_gather_specs(tm, nt_total, lambda b, i: b * nt + i) + [
            pl.BlockSpec(memory_space=pl.ANY),
            pl.BlockSpec((1, tm, d), lambda b, i: (b, i, 0)),
            pl.BlockSpec((1, 6, d), lambda b, i: (b, 0, 0)),
        ],
        out_specs=pl.BlockSpec((1, tm, d), lambda b, i: (b, i, 0)),
        scratch_shapes=_gather_scratch(tm),
        compiler_params=_params("arbitrary", "arbitrary"),
        name="combine",
    )(idx(dest1), idx(dest2), idx(dest1), idx(dest2), col(w1), col(w2), ys, x, ada)


def _router_weights(w_group, b_group, w_expert, b_expert):
    d = w_group.shape[0]
    pad = ROUTER_ROWS - N_EXPERTS - N_GROUPS
    w = jnp.concatenate([w_expert.T, w_group.T, jnp.zeros((pad, d), F32)], axis=0)
    b = jnp.concatenate([b_expert, b_group, jnp.zeros((pad,), F32)])[:, None]
    w_hi = w.astype(BF16)
    w_lo = (w - w_hi.astype(F32)).astype(BF16)
    return w_hi, w_lo, b


def _routing_tables(route, cnt):
    rec = lambda r: route[r]
    counts = cnt[:, 0].astype(jnp.int32)
    starts = jnp.cumsum(counts) - counts
    dest1 = starts[rec(0).astype(jnp.int32)] + rec(2).astype(jnp.int32)
    dest2 = starts[rec(1).astype(jnp.int32)] + rec(3).astype(jnp.int32)
    dest = jnp.stack([dest1, dest2], axis=1).reshape(-1)
    _, order = lax.sort((dest, jnp.arange(dest.shape[0], dtype=jnp.int32)), num_keys=1)
    return order, dest1, dest2, rec(4), rec(5), counts


def kernel(x, c, ada_w, ada_b, norm_mix, norm_ffn, ret_w_in, ret_gn, ret_w_out, kv_ada_w, kv_ada_b, kv_norm, kv_w, k_norm, diff_w_q, q_norm, diff_lam, diff_subln, diff_w_out, moe_w_group, moe_b_group, moe_w_expert, moe_b_expert, moe_w1, moe_w3, moe_w2):
    bsz, seq, d = x.shape
    n_tok = bsz * seq

    ada = _ada(c, ada_w, ada_b).reshape(2, bsz, 6, d)
    kvada = _ada(c, kv_ada_w[None], kv_ada_b[None]).reshape(bsz, 2, d)

    half = RET_DK // 2
    inv_freq = 1.0 / (ROPE_BASE ** (jnp.arange(half, dtype=F32) / half))
    ang = jnp.arange(seq, dtype=F32)[:, None] * inv_freq[None, :]
    cos, sin = jnp.cos(ang), jnp.sin(ang)

    routers = [_router_weights(moe_w_group[l], moe_b_group[l], moe_w_expert[l], moe_b_expert[l])
               for l in range(2)]

    def moe_layer(h, logits_t, layer):
        order, dest1, dest2, w1, w2, counts = _routing_tables(*_route(logits_t))
        ys = _moe(h.reshape(n_tok * ROW_TILES, LANE), order, counts, moe_w1, moe_w3, moe_w2, layer)
        return ys, dest1, dest2, w1, w2

    proj = _ret_in(x, ada[0], norm_mix[0][None, :], cos, sin, ret_w_in[0].astype(BF16))
    x1, h, logits_t = _ret_core(proj, x, ada[0], ret_gn[0][None, :], ret_w_out[0].astype(BF16),
                                norm_ffn[0][None, :], routers[0])
    moe0 = moe_layer(h, logits_t, 0)

    x2, q, k, vt = _kvq(x1, *moe0, ada[0], ada[1], kvada, kv_norm[None, :], norm_mix[1][None, :], k_norm, q_norm[0],
                        kv_w[:, :d].astype(BF16), kv_w[:, d:].T.astype(BF16), diff_w_q[0].astype(BF16))
    lambda_init = 0.8 - 0.6 * math.exp(-0.3 * 1)
    x3, h, logits_t = _attn(q, k, vt, x2, ada[1], diff_lam[0], diff_subln[0][:, None], diff_w_out[0].astype(BF16),
                            norm_ffn[1][None, :], routers[1], lambda_init)
    moe1 = moe_layer(h, logits_t, 1)
    return _combine(x3, *moe1, ada[1])
```

```python
import functools
import math

import jax
import jax.numpy as jnp
from jax import lax
from jax.experimental import pallas as pl
from jax.experimental.pallas import tpu as pltpu

F32 = jnp.float32
BF16 = jnp.bfloat16

D_MODEL = 1024
EPS = 1e-6

RET_HEADS = 4
RET_DK = D_MODEL // RET_HEADS
RET_DV = 2 * RET_DK
RET_QK = RET_HEADS * RET_DK
RET_V = RET_HEADS * RET_DV
RET_IN = 2 * RET_QK + 2 * RET_V
ROPE_BASE = 10000.0
RET_CHUNK = 256

DIFF_HEAD_DIM = 64
DIFF_HEADS = D_MODEL // (2 * DIFF_HEAD_DIM)
MASK_CHUNK = 64
ATT_TQ = 512
ATT_TK = 256
ATT_HEADS = 1

N_GROUPS = 4
EXPERTS_PER_GROUP = 4
N_EXPERTS = N_GROUPS * EXPERTS_PER_GROUP
TOP_K = 2
TOP_K_SHIFT = 1
MOE_BM = 512
MOE_TN = 256
KVQ_TN = 256
MOE_AHEAD = 2
ROUTER_ROWS = 32
ROUTE_ROWS = 8
ROW_TILE = 512
ROUTE_TILE = 2048
ROUTE_SEG = 256

LANE = 128
SUBLANE = 8
VMEM_LIMIT = 56 * 1024 * 1024


def _dot(a, b):
    return jnp.dot(a, b, preferred_element_type=F32)


def _dot_nt(a, b):
    return lax.dot_general(a, b, (((1,), (1,)), ((), ())), preferred_element_type=F32)


def _dot_tn(a, b):
    return lax.dot_general(a, b, (((0,), (0,)), ((), ())), preferred_element_type=F32)


def _silu(x):
    return x * (1.0 / (1.0 + jnp.exp(-x)))


def _rms(x):
    return x * lax.rsqrt(jnp.mean(x * x, axis=-1, keepdims=True) + EPS)


def _params(*sem, flags=None):
    return pltpu.CompilerParams(dimension_semantics=sem, vmem_limit_bytes=VMEM_LIMIT, flags=flags)


ROW_TILES = D_MODEL // LANE


def _load_row_tiles(ref, lead, rows):
    return jnp.concatenate(
        [ref[lead + (pl.ds(c, rows, stride=ROW_TILES), slice(None))] for c in range(ROW_TILES)], axis=1)


def _store_row_tiles(ref, lead, val):
    rows = val.shape[0]
    for c in range(ROW_TILES):
        ref[lead + (pl.ds(c, rows, stride=ROW_TILES), slice(None))] = val[:, c * LANE:(c + 1) * LANE]


def _row_tile(ref, lead, r):
    return ref.at[lead + (pl.ds(pl.multiple_of(r * ROW_TILES, ROW_TILES), ROW_TILES), slice(None))]


def _ada_kernel(c_ref, w_ref, b_ref, o_ref):
    ca = _silu(c_ref[...])
    o_ref[0] = jnp.dot(ca, w_ref[0], preferred_element_type=F32,
                       precision=lax.Precision.HIGHEST) + b_ref[0]


def _ada(c, w, b):
    n_l, d, n = w.shape
    bsz = c.shape[0]
    tn = 1024
    return pl.pallas_call(
        _ada_kernel,
        out_shape=jax.ShapeDtypeStruct((n_l, bsz, n), F32),
        grid=(n_l, n // tn),
        in_specs=[
            pl.BlockSpec((bsz, d), lambda l, j: (0, 0)),
            pl.BlockSpec((1, d, tn), lambda l, j: (l, 0, j)),
            pl.BlockSpec((1, 1, tn), lambda l, j: (l, 0, j)),
        ],
        out_specs=pl.BlockSpec((1, bsz, tn), lambda l, j: (l, 0, j)),
        compiler_params=_params("arbitrary", "arbitrary"),
        name="ada",
    )(c, w, b.reshape(n_l, 1, n))


def _ffn_prenorm_logits(xn, gain, shift, scale, wr_hi_ref, wr_lo_ref, br_ref, lt_ref):
    h = _rms(xn) * gain
    h = h * (1.0 + scale) + shift
    h_hi = h.astype(BF16)
    h_lo = (h - h_hi.astype(F32)).astype(BF16)
    w_hi = wr_hi_ref[...]
    lt_ref[...] = _dot_nt(w_hi, h_hi) + _dot_nt(wr_lo_ref[...], h_hi) + _dot_nt(w_hi, h_lo) + br_ref[...]
    return h


def _route_kernel(lt_ref, tri_ref, route_ref, cnt_ref, carry):
    lt = lt_ref[...]
    c = lt.shape[1]
    seg = tri_ref.shape[0]
    neg_inf = jnp.float32(-jnp.inf)

    grow = lax.broadcasted_iota(jnp.int32, (SUBLANE, c), 0).astype(F32)
    g = jnp.where(grow < N_GROUPS, lt[N_EXPERTS:N_EXPERTS + SUBLANE], neg_inf)
    gmax = jnp.max(g, axis=0, keepdims=True)
    gidx = jnp.min(jnp.where(g == gmax, grow, float(SUBLANE)), axis=0, keepdims=True)
    gate = 1.0 / jnp.sum(jnp.exp(g - gmax), axis=0, keepdims=True)

    erow_i = lax.broadcasted_iota(jnp.int32, (N_EXPERTS, c), 0)
    erow = erow_i.astype(F32)
    egroup = (erow_i // EXPERTS_PER_GROUP).astype(F32)
    el = jnp.where(egroup == gidx, lt[0:N_EXPERTS], neg_inf)
    m1 = jnp.max(el, axis=0, keepdims=True)
    i1 = jnp.min(jnp.where(el == m1, erow, float(N_EXPERTS)), axis=0, keepdims=True)
    el2 = jnp.where(erow == i1, neg_inf, el)
    m2 = jnp.max(el2, axis=0, keepdims=True)
    i2 = jnp.min(jnp.where(el2 == m2, erow, float(N_EXPERTS)), axis=0, keepdims=True)
    t = jnp.exp(m2 - m1)
    den = 1.0 / (1.0 + t)
    w1 = gate * den
    w2 = gate * t * den

    @pl.when(pl.program_id(0) == 0)
    def _():
        carry[...] = jnp.zeros_like(carry)

    oh1 = erow == i1
    oh2 = erow == i2
    oh = jnp.where(jnp.logical_or(oh1, oh2), 1.0, 0.0)
    offset = carry[...]
    before = []
    for s in range(c // seg):
        oh_s = oh[:, s * seg:(s + 1) * seg]
        before.append(offset + _dot(oh_s.astype(BF16), tri_ref[...]))
        offset = offset + jnp.sum(oh_s, axis=1, keepdims=True)
    before = jnp.concatenate(before, axis=1)
    rank1 = jnp.sum(jnp.where(oh1, before, 0.0), axis=0, keepdims=True)
    rank2 = jnp.sum(jnp.where(oh2, before, 0.0), axis=0, keepdims=True)
    carry[...] = offset
    cnt_ref[...] = jnp.broadcast_to(offset, cnt_ref.shape)

    rrow = lax.broadcasted_iota(jnp.int32, (ROUTE_ROWS, c), 0)
    rec = jnp.zeros((ROUTE_ROWS, c), F32)
    for idx, val in enumerate((i1, i2, rank1, rank2, w1, w2)):
        rec = jnp.where(rrow == idx, val, rec)
    route_ref[...] = rec


def _route(lt):
    n_tok = lt.shape[1]
    tile = min(ROUTE_TILE, n_tok)
    t = jnp.arange(ROUTE_SEG, dtype=jnp.int32)
    earlier = (t[:, None] < t[None, :]).astype(BF16)
    return pl.pallas_call(
        _route_kernel,
        out_shape=(jax.ShapeDtypeStruct((ROUTE_ROWS, n_tok), F32),
                   jax.ShapeDtypeStruct((N_EXPERTS, LANE), F32)),
        grid=(n_tok // tile,),
        in_specs=[pl.BlockSpec((ROUTER_ROWS, tile), lambda i: (0, i)),
                  pl.BlockSpec((ROUTE_SEG, ROUTE_SEG), lambda i: (0, 0))],
        out_specs=(pl.BlockSpec((ROUTE_ROWS, tile), lambda i: (0, i)),
                   pl.BlockSpec((N_EXPERTS, LANE), lambda i: (0, 0))),
        scratch_shapes=[pltpu.VMEM((N_EXPERTS, 1), F32)],
        compiler_params=_params("arbitrary"),
        name="route",
    )(lt, earlier)


def _logit_specs(d):
    const = lambda *shape: pl.BlockSpec(shape, lambda b, i: (0,) * len(shape))
    return [const(ROUTER_ROWS, d), const(ROUTER_ROWS, d), const(ROUTER_ROWS, 1)]


def _ret_in_kernel(x_ref, ada_ref, gain_ref, cos_ref, sin_ref, w_ref, o_ref):
    x = x_ref[0]
    shift = ada_ref[0, 0:1, :]
    scale = ada_ref[0, 1:2, :]
    h = (_rms(x) * gain_ref[...]) * (1.0 + scale) + shift
    hb = h.astype(BF16)
    cos = cos_ref[...]
    sin = sin_ref[...]
    half = RET_DK // 2
    for j in range(RET_IN // D_MODEL):
        p = _dot(hb, w_ref[:, j * D_MODEL:(j + 1) * D_MODEL])
        if j < 2:
            post = 1.0 if j == 0 else RET_DK ** -0.5
            for hd in range(RET_HEADS):
                lo = hd * RET_DK
                x1 = p[:, lo:lo + half]
                x2 = p[:, lo + half:lo + RET_DK]
                o_ref[0, :, j * D_MODEL + lo:j * D_MODEL + lo + half] = (
                    (x1 * cos - x2 * sin) * post).astype(BF16)
                o_ref[0, :, j * D_MODEL + lo + half:j * D_MODEL + lo + RET_DK] = (
                    (x2 * cos + x1 * sin) * post).astype(BF16)
        else:
            o_ref[0, :, j * D_MODEL:(j + 1) * D_MODEL] = p.astype(BF16)


def _ret_in(x, ada, gain, cos, sin, w_in):
    bsz, seq, d = x.shape
    tm = ROW_TILE
    return pl.pallas_call(
        _ret_in_kernel,
        out_shape=jax.ShapeDtypeStruct((bsz, seq, RET_IN), BF16),
        grid=(bsz, seq // tm),
        in_specs=[
            pl.BlockSpec((1, tm, d), lambda b, i: (b, i, 0)),
            pl.BlockSpec((1, 6, d), lambda b, i: (b, 0, 0)),
            pl.BlockSpec((1, d), lambda b, i: (0, 0)),
            pl.BlockSpec((tm, RET_DK // 2), lambda b, i: (i, 0)),
            pl.BlockSpec((tm, RET_DK // 2), lambda b, i: (i, 0)),
            pl.BlockSpec((d, RET_IN), lambda b, i: (0, 0), pipeline_mode=pl.Buffered(1)),
        ],
        out_specs=pl.BlockSpec((1, tm, RET_IN), lambda b, i: (b, i, 0)),
        compiler_params=_params("arbitrary", "arbitrary"),
        name="ret_in",
    )(x, ada, gain, cos, sin, w_in)


def _ret_core_kernel(q_ref, k_ref, v_ref, g_ref, x_ref, ada_ref, intra_ref, qd_ref, kd_ref, cd_ref,
                     gn_ref, wo_ref, ngain_ref, wr_hi_ref, wr_lo_ref, br_ref,
                     xo_ref, h_ref, lt_ref, state, ybuf):
    @pl.when(pl.program_id(1) == 0)
    def _():
        state[...] = jnp.zeros_like(state)

    for hd in range(RET_HEADS):
        q = q_ref[0, :, hd * RET_DK:(hd + 1) * RET_DK]
        k = k_ref[0, :, hd * RET_DK:(hd + 1) * RET_DK]
        v = v_ref[0, :, hd * RET_DV:(hd + 1) * RET_DV]
        st = state[hd]
        s = _dot_nt(q, k) * intra_ref[hd]
        o = _dot(s.astype(BF16), v) + qd_ref[hd] * _dot(q, st.astype(BF16))
        kdec = (k.astype(F32) * kd_ref[hd]).astype(BF16)
        state[hd] = st * cd_ref[hd] + _dot_tn(kdec, v)
        o = _rms(o) * gn_ref[:, hd * RET_DV:(hd + 1) * RET_DV]
        g = g_ref[0, :, hd * RET_DV:(hd + 1) * RET_DV].astype(F32)
        ybuf[:, hd * RET_DV:(hd + 1) * RET_DV] = (_silu(g) * o).astype(BF16)

    y = _dot(ybuf[...], wo_ref[...])
    xn = x_ref[0] + ada_ref[0, 2:3, :] * y
    xo_ref[0] = xn
    h = _ffn_prenorm_logits(xn, ngain_ref[...], ada_ref[0, 3:4, :], ada_ref[0, 4:5, :],
                            wr_hi_ref, wr_lo_ref, br_ref, lt_ref)
    h_ref[0] = h


def _ret_core(proj, x, ada, gn, w_out, ngain, router):
    bsz, seq, d = x.shape
    c = RET_CHUNK
    f32 = F32
    log_gamma = jnp.log1p(-jnp.exp2(-5.0 - jnp.arange(RET_HEADS, dtype=f32)))
    n = jnp.arange(c, dtype=f32)
    rel = n[:, None] - n[None, :]
    intra = jnp.where(rel >= 0, jnp.exp(jnp.maximum(rel, 0.0)[None] * log_gamma[:, None, None]), 0.0)
    qd = jnp.exp((n + 1.0)[None, :] * log_gamma[:, None])[:, :, None]
    kd = jnp.exp((c - 1.0 - n)[None, :] * log_gamma[:, None])[:, :, None]
    cd = jnp.broadcast_to(jnp.exp(c * log_gamma)[:, None, None], (RET_HEADS, 1, RET_DV))
    const = lambda *shape: pl.BlockSpec(shape, lambda b, i: (0,) * len(shape))
    tok = lambda w, j: pl.BlockSpec((1, c, w), lambda b, i: (b, i, j))
    nc = seq // c
    return pl.pallas_call(
        _ret_core_kernel,
        out_shape=(jax.ShapeDtypeStruct((bsz, seq, d), f32),
                   jax.ShapeDtypeStruct((bsz, seq, d), f32),
                   jax.ShapeDtypeStruct((ROUTER_ROWS, bsz * seq), f32)),
        grid=(bsz, nc),
        in_specs=[
            tok(RET_QK, 0), tok(RET_QK, 1), tok(RET_V, 1), tok(RET_V, 2),
            tok(d, 0),
            pl.BlockSpec((1, 6, d), lambda b, i: (b, 0, 0)),
            const(RET_HEADS, c, c), const(RET_HEADS, c, 1), const(RET_HEADS, c, 1),
            const(RET_HEADS, 1, RET_DV),
            const(1, RET_V), const(RET_V, d), const(1, d),
        ] + _logit_specs(d),
        out_specs=(tok(d, 0), tok(d, 0),
                   pl.BlockSpec((ROUTER_ROWS, c), lambda b, i: (0, b * nc + i))),
        scratch_shapes=[pltpu.VMEM((RET_HEADS, RET_DK, RET_DV), f32),
                        pltpu.VMEM((c, RET_V), BF16)],
        compiler_params=_params("arbitrary", "arbitrary"),
        name="ret_core",
    )(proj, proj, proj, proj, x, ada, intra, qd, kd, cd, gn, w_out, ngain, *router)


def _moe_kernel(blk_ref, exp_ref, nitem_ref, start_ref, cnt_ref, *refs):
    tok_first, refs = refs[:MOE_AHEAD], refs[MOE_AHEAD:]
    tok_ahead, h_hbm, w1_ref, w3_ref, w2_ref, o_ref, xbuf, w1b, w3b, w2b, gsem = refs
    n_buf = MOE_AHEAD + 1
    w = pl.program_id(0)
    bm = MOE_BM
    n_item = nitem_ref[0]
    d = w1b.shape[0]
    e = exp_ref[w]
    blk = blk_ref[w]
    slot = blk % n_buf
    ahead_slot = (blk + MOE_AHEAD) % n_buf
    prev = jnp.maximum(w - 1, 0)
    first_visit = jnp.logical_or(w == 0, blk_ref[prev] != blk)
    live = w < n_item

    def gather_row(tok_ref, s, r):
        pltpu.make_async_copy(h_hbm.at[pl.ds(tok_ref[0, 0, r], 1), :], xbuf.at[s, pl.ds(r, 1), :], gsem.at[s]).start()

    def wait_gather(s):
        pltpu.make_async_copy(h_hbm.at[pl.ds(0, bm), :], xbuf.at[s], gsem.at[s]).wait()

    def expert(row_copies):
        n_dots = 2 * (d // MOE_TN)
        per_dot = -(-len(row_copies) // n_dots)
        groups = iter([row_copies[g * per_dot:(g + 1) * per_dot] for g in range(n_dots)])

        def piece(lhs, w_ref, c):
            for issue in next(groups, ()):
                issue()
            return _dot(lhs, w_ref[:, c * MOE_TN:(c + 1) * MOE_TN])

        x = xbuf[slot].astype(BF16)
        hid = []
        for c in range(d // MOE_TN):
            a = piece(x, w1b, c)
            b = piece(x, w3b, c)
            hid.append((_silu(a) * b).astype(BF16))
        hid = jnp.concatenate(hid, axis=1)
        y = jnp.concatenate([piece(hid, w2b, c) for c in range(d // MOE_TN)], axis=1)
        row = blk * bm + lax.broadcasted_iota(jnp.int32, (bm, 1), 0)
        mine = jnp.logical_and(row >= start_ref[e], row < start_ref[e] + cnt_ref[e])
        return y, mine

    @pl.when(jnp.logical_and(live, jnp.logical_or(w == 0, exp_ref[prev] != e)))
    def _():
        w1b[...] = w1_ref[0, 0].astype(BF16)
        w3b[...] = w3_ref[0, 0].astype(BF16)
        w2b[...] = w2_ref[0, 0].astype(BF16)

    @pl.when(w == 0)
    def _():
        for k in range(MOE_AHEAD):
            def body(r, carry, k=k):
                gather_row(tok_first[k], k, r)
                return carry
            lax.fori_loop(0, bm, body, 0, unroll=8)

    @pl.when(jnp.logical_and(live, first_visit))
    def _():
        wait_gather(slot)
        y, mine = expert([functools.partial(gather_row, tok_ahead, ahead_slot, r) for r in range(bm)])
        _store_row_tiles(o_ref, (), jnp.where(mine, y, 0.0))

    @pl.when(jnp.logical_and(live, jnp.logical_not(first_visit)))
    def _():
        y, mine = expert([])
        _store_row_tiles(o_ref, (), jnp.where(mine, y, _load_row_tiles(o_ref, (), bm)))

    @pl.when(w == n_item - 1)
    def _():
        for k in range(1, n_buf):
            wait_gather((blk + k) % n_buf)


def _moe(h, order, counts, w1, w3, w2, layer):
    m = order.shape[0]
    d = w1.shape[-1]
    bm = MOE_BM
    n_blocks = m // bm
    n_items = n_blocks + N_EXPERTS - 1
    ends = jnp.cumsum(counts)
    starts = ends - counts
    first_blk = starts // bm
    last_blk = jnp.maximum(ends - 1, starts) // bm
    visits = jnp.where(counts > 0, last_blk - first_blk + 1, 0)
    item_end = jnp.cumsum(visits)
    item_start = item_end - visits
    n_item = item_end[-1]
    w = jnp.minimum(jnp.arange(n_items, dtype=jnp.int32), n_item - 1)
    item_e = jnp.sum(w[:, None] >= item_end[None, :], axis=1).astype(jnp.int32)
    item_blk = (first_blk[item_e] + w - item_start[item_e]).astype(jnp.int32)
    wspec = lambda: pl.BlockSpec((1, 1, d, d), lambda i, blk, e, *_: (layer, e[i], 0, 0))
    idx = lambda f: pl.BlockSpec((1, 1, bm), f, memory_space=pltpu.SMEM)
    tok3 = (order >> TOP_K_SHIFT).reshape(n_blocks, 1, bm)
    first = [idx(lambda i, blk, *_, k=k: (k, 0, 0)) for k in range(MOE_AHEAD)]
    ahead = idx(lambda i, blk, *_: (jnp.minimum(blk[i] + MOE_AHEAD, n_blocks - 1), 0, 0))
    return pl.pallas_call(
        _moe_kernel,
        out_shape=jax.ShapeDtypeStruct((m * ROW_TILES, LANE), F32),
        grid_spec=pltpu.PrefetchScalarGridSpec(
            num_scalar_prefetch=5,
            grid=(n_items,),
            in_specs=first + [ahead, pl.BlockSpec(memory_space=pl.ANY), wspec(), wspec(), wspec()],
            out_specs=pl.BlockSpec((bm * ROW_TILES, LANE), lambda i, blk, *_: (blk[i], 0)),
            scratch_shapes=[pltpu.VMEM((MOE_AHEAD + 1, bm, d), F32),
                            pltpu.VMEM((d, d), BF16), pltpu.VMEM((d, d), BF16), pltpu.VMEM((d, d), BF16),
                            pltpu.SemaphoreType.DMA((MOE_AHEAD + 1,))],
        ),
        compiler_params=_params("arbitrary"),
        name="moe",
    )(item_blk, item_e, n_item.astype(jnp.int32).reshape(1), starts.astype(jnp.int32), counts.astype(jnp.int32),
      *([tok3] * (MOE_AHEAD + 1)), h, w1, w3, w2)


class _RowGather:
    def __init__(self, ys_hbm, buf, sem, tm):
        self.ys, self.buf, self.sem, self.tm = ys_hbm, buf, sem, tm

    def row_copy(self, d_ref, slot, k, t):
        pltpu.make_async_copy(_row_tile(self.ys, (), d_ref[0, 0, t]), _row_tile(self.buf, (slot, k), t),
                              self.sem.at[slot]).start(priority=k)

    def start(self, d_refs, slot):
        def body(t, carry):
            for k, d_ref in enumerate(d_refs):
                self.row_copy(d_ref, slot, k, t)
            return carry
        lax.fori_loop(0, self.tm, body, 0, unroll=8)

    def unrolled(self, d_refs, slot):
        return [functools.partial(self.row_copy, d_ref, slot, k, t)
                for t in range(self.tm) for k, d_ref in enumerate(d_refs)]

    def wait(self, slot):
        for k in range(TOP_K):
            pltpu.make_async_copy(self.ys.at[pl.ds(0, self.tm * ROW_TILES), :], self.buf.at[slot, k],
                                  self.sem.at[slot]).wait()

    def combined(self, slot, w1_ref, w2_ref):
        return (w1_ref[...] * _load_row_tiles(self.buf, (slot, 0), self.tm)
                + w2_ref[...] * _load_row_tiles(self.buf, (slot, 1), self.tm))


def _gather_specs(tm, nt_total, step_of):
    cur = lambda: pl.BlockSpec((1, 1, tm), lambda *g: (step_of(*g), 0, 0), memory_space=pltpu.SMEM)
    nxt = lambda: pl.BlockSpec((1, 1, tm), lambda *g: (jnp.minimum(step_of(*g) + 1, nt_total - 1), 0, 0),
                               memory_space=pltpu.SMEM)
    wcol = lambda: pl.BlockSpec((tm, 1), lambda *g: (step_of(*g), 0))
    return [cur(), cur(), nxt(), nxt(), wcol(), wcol()]


def _gather_scratch(tm):
    return [pltpu.VMEM((2, TOP_K, tm * ROW_TILES, LANE), F32), pltpu.SemaphoreType.DMA((2,))]


def _group_norm64(t, ind_ref, indt_ref):
    ss = _dot((t * t).astype(BF16), ind_ref[...])
    r = lax.rsqrt(ss * (1.0 / DIFF_HEAD_DIM) + EPS)
    r_hi = r.astype(BF16)
    r_lo = (r - r_hi.astype(F32)).astype(BF16)
    rb = _dot(jnp.concatenate([r_hi, r_lo], axis=1), indt_ref[...])
    return t * rb


def _kvq_kernel(d1c, d2c, d1n, d2n, w1_ref, w2_ref, ys_hbm, x_ref, ada0_ref, ada1_ref, kvada_ref, kvgain_ref,
                qgain_ref, knorm_ref, qnorm_ref, wk_ref, wvt_ref, wq_ref, ind_ref, indt_ref,
                xo_ref, q_ref, k_ref, vt_ref, gbuf, gsem):
    tm = x_ref.shape[1]
    d = x_ref.shape[2]
    step = pl.program_id(0) * pl.num_programs(1) + pl.program_id(1)
    slot = step % 2
    gather = _RowGather(ys_hbm, gbuf, gsem, tm)

    @pl.when(step == 0)
    def _():
        gather.start((d1c, d2c), 0)

    gather.wait(slot)
    xn = x_ref[0] + ada0_ref[0, 5:6, :] * gather.combined(slot, w1_ref, w2_ref)
    xo_ref[0] = xn
    copies = gather.unrolled((d1n, d2n), 1 - slot)
    n_pieces = 3 * (d // KVQ_TN)
    per_piece = -(-len(copies) // n_pieces)
    groups = iter([copies[g * per_piece:(g + 1) * per_piece] for g in range(n_pieces)])

    def issue_group():
        for issue in next(groups):
            issue()

    def project(lhs, w_ref):
        out = []
        for c in range(d // KVQ_TN):
            issue_group()
            out.append(_dot(lhs, w_ref[:, c * KVQ_TN:(c + 1) * KVQ_TN]))
        return jnp.concatenate(out, axis=1)

    r = _rms(xn)
    hk = ((r * kvgain_ref[...]) * (1.0 + kvada_ref[0, 1:2, :]) + kvada_ref[0, 0:1, :]).astype(BF16)
    hq = ((r * qgain_ref[...]) * (1.0 + ada1_ref[0, 1:2, :]) + ada1_ref[0, 0:1, :]).astype(BF16)
    hd2 = 2 * DIFF_HEAD_DIM
    kk = project(hk, wk_ref)
    kn = (_group_norm64(kk, ind_ref, indt_ref) * knorm_ref[...]).astype(BF16)
    qq = project(hq, wq_ref)
    qn = (_group_norm64(qq, ind_ref, indt_ref) * qnorm_ref[...]).astype(BF16)
    for hd in range(DIFF_HEADS):
        k_ref[0, hd] = kn[:, hd * hd2:(hd + 1) * hd2]
        q_ref[0, hd] = qn[:, hd * hd2:(hd + 1) * hd2]
    vt = []
    for c in range(d // KVQ_TN):
        issue_group()
        vt.append(_dot_nt(wvt_ref[c * KVQ_TN:(c + 1) * KVQ_TN, :], hk))
    vt = jnp.concatenate(vt, axis=0).astype(BF16)
    for j in range(tm // ATT_TK):
        vt_ref[0, j] = vt[:, j * ATT_TK:(j + 1) * ATT_TK].reshape(DIFF_HEADS, hd2, ATT_TK)

    @pl.when(step == pl.num_programs(0) * pl.num_programs(1) - 1)
    def _():
        gather.wait(1 - slot)


def _kvq(x, ys, dest1, dest2, w1, w2, ada0, ada1, kvada, kv_gain, q_gain, k_norm, q_norm, w_k, w_vt, w_q):
    bsz, seq, d = x.shape
    tm = ROW_TILE
    nt = seq // tm
    nt_total = bsz * nt
    lane_group = jnp.arange(d, dtype=jnp.int32) // DIFF_HEAD_DIM
    ind = (lane_group[:, None] == jnp.arange(LANE, dtype=jnp.int32)[None, :]).astype(BF16)
    indt = jnp.concatenate([ind.T, ind.T], axis=0)
    reps = d // DIFF_HEAD_DIM
    q_scale = jnp.tile(q_norm, reps)[None, :] * (DIFF_HEAD_DIM ** -0.5 * math.log2(math.e))
    const = lambda *shape: pl.BlockSpec(shape, lambda b, i: (0,) * len(shape))
    tok = lambda w: pl.BlockSpec((1, tm, w), lambda b, i: (b, i, 0))
    hd2 = 2 * DIFF_HEAD_DIM
    heads = pl.BlockSpec((1, DIFF_HEADS, tm, hd2), lambda b, i: (b, 0, i, 0))
    ada_spec = lambda rows: pl.BlockSpec((1, rows, d), lambda b, i: (b, 0, 0))
    col = lambda a: a.reshape(-1, 1)
    idx = lambda a: a.reshape(nt_total, 1, tm)
    return pl.pallas_call(
        _kvq_kernel,
        out_shape=(jax.ShapeDtypeStruct((bsz, seq, d), F32),
                   jax.ShapeDtypeStruct((bsz, DIFF_HEADS, seq, hd2), BF16),
                   jax.ShapeDtypeStruct((bsz, DIFF_HEADS, seq, hd2), BF16),
                   jax.ShapeDtypeStruct((bsz, seq // ATT_TK, DIFF_HEADS, hd2, ATT_TK), BF16)),
        grid=(bsz, nt),
        in_specs=_gather_specs(tm, nt_total, lambda b, i: b * nt + i) + [
            pl.BlockSpec(memory_space=pl.ANY),
            tok(d),
            ada_spec(6), ada_spec(6), ada_spec(2),
            const(1, d), const(1, d), const(1, d), const(1, d),
            const(d, d), const(d, d), const(d, d), const(d, LANE), const(2 * LANE, d),
        ],
        out_specs=(tok(d), heads, heads,
                   pl.BlockSpec((1, tm // ATT_TK, DIFF_HEADS, hd2, ATT_TK), lambda b, i: (b, i, 0, 0, 0))),
        scratch_shapes=_gather_scratch(tm),
        compiler_params=_params("arbitrary", "arbitrary"),
        name="kvq",
    )(idx(dest1), idx(dest2), idx(dest1), idx(dest2), col(w1), col(w2), ys, x, ada0, ada1, kvada,
      kv_gain, q_gain, jnp.tile(k_norm, reps)[None, :], q_scale, w_k, w_vt, w_q, ind, indt)


def _attn_kernel(lambda_init, q_ref, k_ref, vt_ref, x_ref, ada_ref, lam_ref, subln_ref, wo_ref, ngain_ref,
                 wr_hi_ref, wr_lo_ref, br_ref, xo_ref, h_ref, lt_ref,
                 s_a, s_b, p_a, p_b, acc, obuf):
    i = pl.program_id(1)
    tq, tk = ATT_TQ, ATT_TK
    hd2 = 2 * DIFF_HEAD_DIM
    n_kblk = k_ref.shape[2] // tk
    lam = lam_ref[...]
    lam_full = (jnp.exp(jnp.sum(lam[0:1] * lam[1:2], axis=-1, keepdims=True))
                - jnp.exp(jnp.sum(lam[2:3] * lam[3:4], axis=-1, keepdims=True)) + lambda_init)
    lane = lax.broadcasted_iota(jnp.int32, (1, hd2), 1)
    keep = (jnp.where(lane < DIFF_HEAD_DIM, 1.0, 0.0).astype(BF16),
            jnp.where(lane >= DIFF_HEAD_DIM, 1.0, 0.0).astype(BF16))
    key = lax.broadcasted_iota(jnp.int32, (tk, tq), 0)
    qry = lax.broadcasted_iota(jnp.int32, (1, tq), 1)
    key_limit = ((i * tq + qry) // MASK_CHUNK + 1) * MASK_CHUNK
    neg_inf = jnp.float32(-jnp.inf)
    first_masked = i * (tq // tk)
    n_blocks = first_masked + tq // tk
    n_pairs = (n_blocks + 1) // 2
    n_plain = jnp.maximum((first_masked - 1) // 2, 0)

    streams = [(u, t) for u in range(ATT_HEADS) for t in range(2)]

    def head_group(hg, _):
        hds = [hg * ATT_HEADS + u for u in range(ATT_HEADS)]
        qs = []
        for hd in hds:
            qh = q_ref[0, hd]
            qs.append((qh * keep[0], qh * keep[1]))

        def produce_scores(jb, s_out, masked):
            start = pl.multiple_of(jnp.minimum(jb, n_kblk - 1) * tk, tk)
            visible = key < key_limit - jb * tk
            maxima = []
            for n, (u, t) in enumerate(streams):
                st = _dot_nt(k_ref[0, hds[u], pl.ds(start, tk), :], qs[u][t])
                if masked:
                    st = jnp.where(visible, st, neg_inf)
                s_out[n] = st
                maxima.append(jnp.max(st, axis=0, keepdims=True))
            return tuple(maxima)

        def step(j, s_in, s_out, p_in, p_out, state, masked):
            stats, blk_max = state
            jv = jnp.clip(j - 1, 0, n_kblk - 1)
            next_max = produce_scores(j + 1, s_out, masked)
            new_stats = []
            for n, (u, t) in enumerate(streams):
                m, l = stats[n]
                m_new = jnp.maximum(m, blk_max[n])
                alpha = jnp.exp2(m - m_new)
                p = jnp.exp2(s_in[n] - m_new)
                p_out[n] = p.astype(BF16)
                new_stats.append((m_new, alpha * l + jnp.sum(p, axis=0, keepdims=True)))
                acc[n] = alpha * (acc[n] + _dot(vt_ref[0, jv, hds[u]], p_in[n]))
            return tuple(new_stats), next_max

        p_b[...] = jnp.zeros_like(p_b)
        acc[...] = jnp.zeros_like(acc)
        stat0 = (jnp.full((1, tq), neg_inf, F32), jnp.zeros((1, tq), F32))
        state = ((stat0,) * len(streams), produce_scores(0, s_a, True))

        def pair(masked, jj, state):
            state = step(2 * jj, s_a, s_b, p_b, p_a, state, masked)
            return step(2 * jj + 1, s_b, s_a, p_a, p_b, state, masked)

        state = lax.fori_loop(0, n_plain, functools.partial(pair, False), state)
        (stats, _) = lax.fori_loop(n_plain, n_pairs, functools.partial(pair, True), state)
        jv = jnp.minimum(2 * n_pairs - 1, n_kblk - 1)
        for u, hd in enumerate(hds):
            vtb = vt_ref[0, jv, hd]
            a1 = acc[2 * u] + _dot(vtb, p_b[2 * u])
            a2 = acc[2 * u + 1] + _dot(vtb, p_b[2 * u + 1])
            o = a1 * (1.0 / stats[2 * u][1]) - lam_full * (a2 * (1.0 / stats[2 * u + 1][1]))
            o = o * lax.rsqrt(jnp.mean(o * o, axis=0, keepdims=True) + EPS)
            obuf[hd] = (o * (subln_ref[...] * (1.0 - lambda_init))).astype(BF16)
        return 0

    lax.fori_loop(0, DIFF_HEADS // ATT_HEADS, head_group, 0)

    y = _dot_tn(obuf[...].reshape(DIFF_HEADS * hd2, tq), wo_ref[...])
    xn = x_ref[0] + ada_ref[0, 2:3, :] * y
    xo_ref[0] = xn
    h = _ffn_prenorm_logits(xn, ngain_ref[...], ada_ref[0, 3:4, :], ada_ref[0, 4:5, :],
                            wr_hi_ref, wr_lo_ref, br_ref, lt_ref)
    h_ref[0] = h


def _attn(q, k, vt, x, ada, lam, subln, w_out, ngain, router, lambda_init):
    bsz, seq, d = x.shape
    tq, tk = ATT_TQ, ATT_TK
    hd2 = 2 * DIFF_HEAD_DIM
    const = lambda *shape: pl.BlockSpec(shape, lambda b, i: (0,) * len(shape))
    tok = lambda w: pl.BlockSpec((1, tq, w), lambda b, i: (b, i, 0))
    nq = seq // tq
    ns = 2 * ATT_HEADS
    return pl.pallas_call(
        functools.partial(_attn_kernel, lambda_init),
        out_shape=(jax.ShapeDtypeStruct((bsz, seq, d), F32),
                   jax.ShapeDtypeStruct((bsz, seq, d), F32),
                   jax.ShapeDtypeStruct((ROUTER_ROWS, bsz * seq), F32)),
        grid=(bsz, nq),
        in_specs=[
            pl.BlockSpec((1, DIFF_HEADS, tq, hd2), lambda b, i: (b, 0, i, 0)),
            pl.BlockSpec((1, DIFF_HEADS, seq, hd2), lambda b, i: (b, 0, 0, 0)),
            pl.BlockSpec((1, seq // tk, DIFF_HEADS, hd2, tk), lambda b, i: (b, 0, 0, 0, 0)),
            tok(d),
            pl.BlockSpec((1, 6, d), lambda b, i: (b, 0, 0)),
            const(4, DIFF_HEAD_DIM), const(hd2, 1), const(d, d), const(1, d),
        ] + _logit_specs(d),
        out_specs=(tok(d), tok(d),
                   pl.BlockSpec((ROUTER_ROWS, tq), lambda b, i: (0, b * nq + i))),
        scratch_shapes=[pltpu.VMEM((ns, tk, tq), F32), pltpu.VMEM((ns, tk, tq), F32),
                        pltpu.VMEM((ns, tk, tq), BF16), pltpu.VMEM((ns, tk, tq), BF16),
                        pltpu.VMEM((ns, hd2, tq), F32),
                        pltpu.VMEM((DIFF_HEADS, hd2, tq), BF16)],
        compiler_params=_params("arbitrary", "arbitrary"),
        name="attn",
    )(q, k, vt, x, ada, lam, subln, w_out, ngain, *router)


def _combine_kernel(d1c, d2c, d1n, d2n, w1_ref, w2_ref, ys_hbm, x_ref, ada_ref, o_ref, gbuf, gsem):
    tm = x_ref.shape[1]
    step = pl.program_id(0) * pl.num_programs(1) + pl.program_id(1)
    slot = step % 2
    gather = _RowGather(ys_hbm, gbuf, gsem, tm)

    @pl.when(step == 0)
    def _():
        gather.start((d1c, d2c), 0)

    @pl.when(step + 1 < pl.num_programs(0) * pl.num_programs(1))
    def _():
        gather.start((d1n, d2n), 1 - slot)

    gather.wait(slot)
    o_ref[0] = x_ref[0] + ada_ref[0, 5:6, :] * gather.combined(slot, w1_ref, w2_ref)


def _combine(x, ys, dest1, dest2, w1, w2, ada):
    bsz, seq, d = x.shape
    tm = ROW_TILE
    nt = seq // tm
    nt_total = bsz * nt
    col = lambda a: a.reshape(-1, 1)
    idx = lambda a: a.reshape(nt_total, 1, tm)
    return pl.pallas_call(
        _combine_kernel,
        out_shape=jax.ShapeDtypeStruct((bsz, seq, d), F32),
        grid=(bsz, nt),
        in_specs=_gather_specs(tm, nt_total, lambda b, i: b * nt + i) + [
            pl.BlockSpec(memory_space=pl.ANY),
            pl.BlockSpec((1, tm, d), lambda b, i: (b, i, 0)),
            pl.BlockSpec((1, 6, d), lambda b, i: (b, 0, 0)),
        ],
        out_specs=pl.BlockSpec((1, tm, d), lambda b, i: (b, i, 0)),
        scratch_shapes=_gather_scratch(tm),
        compiler_params=_params("arbitrary", "arbitrary"),
        name="combine",
    )(idx(dest1), idx(dest2), idx(dest1), idx(dest2), col(w1), col(w2), ys, x, ada)


def _router_weights(w_group, b_group, w_expert, b_expert):
    d = w_group.shape[0]
    pad = ROUTER_ROWS - N_EXPERTS - N_GROUPS
    w = jnp.concatenate([w_expert.T, w_group.T, jnp.zeros((pad, d), F32)], axis=0)
    b = jnp.concatenate([b_expert, b_group, jnp.zeros((pad,), F32)])[:, None]
    w_hi = w.astype(BF16)
    w_lo = (w - w_hi.astype(F32)).astype(BF16)
    return w_hi, w_lo, b


def _routing_tables(route, cnt):
    rec = lambda r: route[r]
    counts = cnt[:, 0].astype(jnp.int32)
    starts = jnp.cumsum(counts) - counts
    dest1 = starts[rec(0).astype(jnp.int32)] + rec(2).astype(jnp.int32)
    dest2 = starts[rec(1).astype(jnp.int32)] + rec(3).astype(jnp.int32)
    dest = jnp.stack([dest1, dest2], axis=1).reshape(-1)
    _, order = lax.sort((dest, jnp.arange(dest.shape[0], dtype=jnp.int32)), num_keys=1)
    return order, dest1, dest2, rec(4), rec(5), counts


def kernel(x, c, ada_w, ada_b, norm_mix, norm_ffn, ret_w_in, ret_gn, ret_w_out, kv_ada_w, kv_ada_b, kv_norm, kv_w, k_norm, diff_w_q, q_norm, diff_lam, diff_subln, diff_w_out, moe_w_group, moe_b_group, moe_w_expert, moe_b_expert, moe_w1, moe_w3, moe_w2):
    bsz, seq, d = x.shape
    n_tok = bsz * seq

    ada = _ada(c, ada_w, ada_b).reshape(2, bsz, 6, d)
    kvada = _ada(c, kv_ada_w[None], kv_ada_b[None]).reshape(bsz, 2, d)

    half = RET_DK // 2
    inv_freq = 1.0 / (ROPE_BASE ** (jnp.arange(half, dtype=F32) / half))
    ang = jnp.arange(seq, dtype=F32)[:, None] * inv_freq[None, :]
    cos, sin = jnp.cos(ang), jnp.sin(ang)

    routers = [_router_weights(moe_w_group[l], moe_b_group[l], moe_w_expert[l], moe_b_expert[l])
               for l in range(2)]

    def moe_layer(h, logits_t, layer):
        order, dest1, dest2, w1, w2, counts = _routing_tables(*_route(logits_t))
        ys = _moe(h.reshape(n_tok, d), order, counts, moe_w1, moe_w3, moe_w2, layer)
        return ys, dest1, dest2, w1, w2

    proj = _ret_in(x, ada[0], norm_mix[0][None, :], cos, sin, ret_w_in[0].astype(BF16))
    x1, h, logits_t = _ret_core(proj, x, ada[0], ret_gn[0][None, :], ret_w_out[0].astype(BF16),
                                norm_ffn[0][None, :], routers[0])
    moe0 = moe_layer(h, logits_t, 0)

    x2, q, k, vt = _kvq(x1, *moe0, ada[0], ada[1], kvada, kv_norm[None, :], norm_mix[1][None, :], k_norm, q_norm[0],
                        kv_w[:, :d].astype(BF16), kv_w[:, d:].T.astype(BF16), diff_w_q[0].astype(BF16))
    lambda_init = 0.8 - 0.6 * math.exp(-0.3 * 1)
    x3, h, logits_t = _attn(q, k, vt, x2, ada[1], diff_lam[0], diff_subln[0][:, None], diff_w_out[0].astype(BF16),
                            norm_ffn[1][None, :], routers[1], lambda_init)
    moe1 = moe_layer(h, logits_t, 1)
    return _combine(x3, *moe1, ada[1])
```

```python
import functools
import math

import jax
import jax.numpy as jnp
from jax import lax
from jax.experimental import pallas as pl
from jax.experimental.pallas import tpu as pltpu

F32 = jnp.float32
BF16 = jnp.bfloat16

D_MODEL = 1024
EPS = 1e-6

RET_HEADS = 4
RET_DK = D_MODEL // RET_HEADS
RET_DV = 2 * RET_DK
RET_QK = RET_HEADS * RET_DK
RET_V = RET_HEADS * RET_DV
RET_IN = 2 * RET_QK + 2 * RET_V
ROPE_BASE = 10000.0
RET_CHUNK = 256

DIFF_HEAD_DIM = 64
DIFF_HEADS = D_MODEL // (2 * DIFF_HEAD_DIM)
MASK_CHUNK = 64
ATT_TQ = 512
ATT_TK = 256
ATT_HEADS = 1

N_GROUPS = 4
EXPERTS_PER_GROUP = 4
N_EXPERTS = N_GROUPS * EXPERTS_PER_GROUP
TOP_K = 2
TOP_K_SHIFT = 1
MOE_BM = 512
MOE_TN = 256
KVQ_TN = 256
MOE_AHEAD = 2
ROUTER_ROWS = 32
ROUTE_ROWS = 8
ROW_TILE = 512
ROUTE_TILE = 2048
ROUTE_SEG = 256

LANE = 128
SUBLANE = 8
VMEM_LIMIT = 56 * 1024 * 1024


def _dot(a, b):
    return jnp.dot(a, b, preferred_element_type=F32)


def _dot_nt(a, b):
    return lax.dot_general(a, b, (((1,), (1,)), ((), ())), preferred_element_type=F32)


def _dot_tn(a, b):
    return lax.dot_general(a, b, (((0,), (0,)), ((), ())), preferred_element_type=F32)


def _silu(x):
    return x * (1.0 / (1.0 + jnp.exp(-x)))


def _rms(x):
    return x * lax.rsqrt(jnp.mean(x * x, axis=-1, keepdims=True) + EPS)


def _params(*sem, flags=None):
    return pltpu.CompilerParams(dimension_semantics=sem, vmem_limit_bytes=VMEM_LIMIT, flags=flags)


ROW_TILES = D_MODEL // LANE


def _load_row_tiles(ref, lead, rows):
    return jnp.concatenate(
        [ref[lead + (pl.ds(c, rows, stride=ROW_TILES), slice(None))] for c in range(ROW_TILES)], axis=1)


def _store_row_tiles(ref, lead, val):
    rows = val.shape[0]
    for c in range(ROW_TILES):
        ref[lead + (pl.ds(c, rows, stride=ROW_TILES), slice(None))] = val[:, c * LANE:(c + 1) * LANE]


def _row_tile(ref, lead, r):
    return ref.at[lead + (pl.ds(pl.multiple_of(r * ROW_TILES, ROW_TILES), ROW_TILES), slice(None))]


def _ada_kernel(c_ref, w_ref, b_ref, o_ref):
    ca = _silu(c_ref[...])
    o_ref[0] = jnp.dot(ca, w_ref[0], preferred_element_type=F32,
                       precision=lax.Precision.HIGHEST) + b_ref[0]


def _ada(c, w, b):
    n_l, d, n = w.shape
    bsz = c.shape[0]
    tn = 1024
    return pl.pallas_call(
        _ada_kernel,
        out_shape=jax.ShapeDtypeStruct((n_l, bsz, n), F32),
        grid=(n_l, n // tn),
        in_specs=[
            pl.BlockSpec((bsz, d), lambda l, j: (0, 0)),
            pl.BlockSpec((1, d, tn), lambda l, j: (l, 0, j)),
            pl.BlockSpec((1, 1, tn), lambda l, j: (l, 0, j)),
        ],
        out_specs=pl.BlockSpec((1, bsz, tn), lambda l, j: (l, 0, j)),
        compiler_params=_params("arbitrary", "arbitrary"),
        name="ada",
    )(c, w, b.reshape(n_l, 1, n))


def _ffn_prenorm_logits(xn, gain, shift, scale, wr_hi_ref, wr_lo_ref, br_ref, lt_ref):
    h = _rms(xn) * gain
    h = h * (1.0 + scale) + shift
    h_hi = h.astype(BF16)
    h_lo = (h - h_hi.astype(F32)).astype(BF16)
    w_hi = wr_hi_ref[...]
    lt_ref[...] = _dot_nt(w_hi, h_hi) + _dot_nt(wr_lo_ref[...], h_hi) + _dot_nt(w_hi, h_lo) + br_ref[...]
    return h


def _route_kernel(lt_ref, tri_ref, route_ref, cnt_ref, carry):
    lt = lt_ref[...]
    c = lt.shape[1]
    seg = tri_ref.shape[0]
    neg_inf = jnp.float32(-jnp.inf)

    grow = lax.broadcasted_iota(jnp.int32, (SUBLANE, c), 0).astype(F32)
    g = jnp.where(grow < N_GROUPS, lt[N_EXPERTS:N_EXPERTS + SUBLANE], neg_inf)
    gmax = jnp.max(g, axis=0, keepdims=True)
    gidx = jnp.min(jnp.where(g == gmax, grow, float(SUBLANE)), axis=0, keepdims=True)
    gate = 1.0 / jnp.sum(jnp.exp(g - gmax), axis=0, keepdims=True)

    erow_i = lax.broadcasted_iota(jnp.int32, (N_EXPERTS, c), 0)
    erow = erow_i.astype(F32)
    egroup = (erow_i // EXPERTS_PER_GROUP).astype(F32)
    el = jnp.where(egroup == gidx, lt[0:N_EXPERTS], neg_inf)
    m1 = jnp.max(el, axis=0, keepdims=True)
    i1 = jnp.min(jnp.where(el == m1, erow, float(N_EXPERTS)), axis=0, keepdims=True)
    el2 = jnp.where(erow == i1, neg_inf, el)
    m2 = jnp.max(el2, axis=0, keepdims=True)
    i2 = jnp.min(jnp.where(el2 == m2, erow, float(N_EXPERTS)), axis=0, keepdims=True)
    t = jnp.exp(m2 - m1)
    den = 1.0 / (1.0 + t)
    w1 = gate * den
    w2 = gate * t * den

    @pl.when(pl.program_id(0) == 0)
    def _():
        carry[...] = jnp.zeros_like(carry)

    oh1 = erow == i1
    oh2 = erow == i2
    oh = jnp.where(jnp.logical_or(oh1, oh2), 1.0, 0.0)
    offset = carry[...]
    before = []
    for s in range(c // seg):
        oh_s = oh[:, s * seg:(s + 1) * seg]
        before.append(offset + _dot(oh_s.astype(BF16), tri_ref[...]))
        offset = offset + jnp.sum(oh_s, axis=1, keepdims=True)
    before = jnp.concatenate(before, axis=1)
    rank1 = jnp.sum(jnp.where(oh1, before, 0.0), axis=0, keepdims=True)
    rank2 = jnp.sum(jnp.where(oh2, before, 0.0), axis=0, keepdims=True)
    carry[...] = offset
    cnt_ref[...] = jnp.broadcast_to(offset, cnt_ref.shape)

    rrow = lax.broadcasted_iota(jnp.int32, (ROUTE_ROWS, c), 0)
    rec = jnp.zeros((ROUTE_ROWS, c), F32)
    for idx, val in enumerate((i1, i2, rank1, rank2, w1, w2)):
        rec = jnp.where(rrow == idx, val, rec)
    route_ref[...] = rec


def _route(lt):
    n_tok = lt.shape[1]
    tile = min(ROUTE_TILE, n_tok)
    t = jnp.arange(ROUTE_SEG, dtype=jnp.int32)
    earlier = (t[:, None] < t[None, :]).astype(BF16)
    return pl.pallas_call(
        _route_kernel,
        out_shape=(jax.ShapeDtypeStruct((ROUTE_ROWS, n_tok), F32),
                   jax.ShapeDtypeStruct((N_EXPERTS, LANE), F32)),
        grid=(n_tok // tile,),
        in_specs=[pl.BlockSpec((ROUTER_ROWS, tile), lambda i: (0, i)),
                  pl.BlockSpec((ROUTE_SEG, ROUTE_SEG), lambda i: (0, 0))],
        out_specs=(pl.BlockSpec((ROUTE_ROWS, tile), lambda i: (0, i)),
                   pl.BlockSpec((N_EXPERTS, LANE), lambda i: (0, 0))),
        scratch_shapes=[pltpu.VMEM((N_EXPERTS, 1), F32)],
        compiler_params=_params("arbitrary"),
        name="route",
    )(lt, earlier)


def _logit_specs(d):
    const = lambda *shape: pl.BlockSpec(shape, lambda b, i: (0,) * len(shape))
    return [const(ROUTER_ROWS, d), const(ROUTER_ROWS, d), const(ROUTER_ROWS, 1)]


def _ret_in_kernel(x_ref, ada_ref, gain_ref, cos_ref, sin_ref, w_ref, o_ref):
    x = x_ref[0]
    shift = ada_ref[0, 0:1, :]
    scale = ada_ref[0, 1:2, :]
    h = (_rms(x) * gain_ref[...]) * (1.0 + scale) + shift
    hb = h.astype(BF16)
    cos = cos_ref[...]
    sin = sin_ref[...]
    half = RET_DK // 2
    for j in range(RET_IN // D_MODEL):
        p = _dot(hb, w_ref[:, j * D_MODEL:(j + 1) * D_MODEL])
        if j < 2:
            post = 1.0 if j == 0 else RET_DK ** -0.5
            for hd in range(RET_HEADS):
                lo = hd * RET_DK
                x1 = p[:, lo:lo + half]
                x2 = p[:, lo + half:lo + RET_DK]
                o_ref[0, :, j * D_MODEL + lo:j * D_MODEL + lo + half] = (
                    (x1 * cos - x2 * sin) * post).astype(BF16)
                o_ref[0, :, j * D_MODEL + lo + half:j * D_MODEL + lo + RET_DK] = (
                    (x2 * cos + x1 * sin) * post).astype(BF16)
        else:
            o_ref[0, :, j * D_MODEL:(j + 1) * D_MODEL] = p.astype(BF16)


def _ret_in(x, ada, gain, cos, sin, w_in):
    bsz, seq, d = x.shape
    tm = ROW_TILE
    return pl.pallas_call(
        _ret_in_kernel,
        out_shape=jax.ShapeDtypeStruct((bsz, seq, RET_IN), BF16),
        grid=(bsz, seq // tm),
        in_specs=[
            pl.BlockSpec((1, tm, d), lambda b, i: (b, i, 0)),
            pl.BlockSpec((1, 6, d), lambda b, i: (b, 0, 0)),
            pl.BlockSpec((1, d), lambda b, i: (0, 0)),
            pl.BlockSpec((tm, RET_DK // 2), lambda b, i: (i, 0)),
            pl.BlockSpec((tm, RET_DK // 2), lambda b, i: (i, 0)),
            pl.BlockSpec((d, RET_IN), lambda b, i: (0, 0), pipeline_mode=pl.Buffered(1)),
        ],
        out_specs=pl.BlockSpec((1, tm, RET_IN), lambda b, i: (b, i, 0)),
        compiler_params=_params("arbitrary", "arbitrary"),
        name="ret_in",
    )(x, ada, gain, cos, sin, w_in)


def _ret_core_kernel(q_ref, k_ref, v_ref, g_ref, x_ref, ada_ref, intra_ref, qd_ref, kd_ref, cd_ref,
                     gn_ref, wo_ref, ngain_ref, wr_hi_ref, wr_lo_ref, br_ref,
                     xo_ref, h_ref, lt_ref, state, ybuf):
    @pl.when(pl.program_id(1) == 0)
    def _():
        state[...] = jnp.zeros_like(state)

    for hd in range(RET_HEADS):
        q = q_ref[0, :, hd * RET_DK:(hd + 1) * RET_DK]
        k = k_ref[0, :, hd * RET_DK:(hd + 1) * RET_DK]
        v = v_ref[0, :, hd * RET_DV:(hd + 1) * RET_DV]
        st = state[hd]
        s = _dot_nt(q, k) * intra_ref[hd]
        o = _dot(s.astype(BF16), v) + qd_ref[hd] * _dot(q, st.astype(BF16))
        kdec = (k.astype(F32) * kd_ref[hd]).astype(BF16)
        state[hd] = st * cd_ref[hd] + _dot_tn(kdec, v)
        o = _rms(o) * gn_ref[:, hd * RET_DV:(hd + 1) * RET_DV]
        g = g_ref[0, :, hd * RET_DV:(hd + 1) * RET_DV].astype(F32)
        ybuf[:, hd * RET_DV:(hd + 1) * RET_DV] = (_silu(g) * o).astype(BF16)

    y = _dot(ybuf[...], wo_ref[...])
    xn = x_ref[0] + ada_ref[0, 2:3, :] * y
    xo_ref[0] = xn
    h = _ffn_prenorm_logits(xn, ngain_ref[...], ada_ref[0, 3:4, :], ada_ref[0, 4:5, :],
                            wr_hi_ref, wr_lo_ref, br_ref, lt_ref)
    _store_row_tiles(h_ref, (0,), h)


def _ret_core(proj, x, ada, gn, w_out, ngain, router):
    bsz, seq, d = x.shape
    c = RET_CHUNK
    f32 = F32
    log_gamma = jnp.log1p(-jnp.exp2(-5.0 - jnp.arange(RET_HEADS, dtype=f32)))
    n = jnp.arange(c, dtype=f32)
    rel = n[:, None] - n[None, :]
    intra = jnp.where(rel >= 0, jnp.exp(jnp.maximum(rel, 0.0)[None] * log_gamma[:, None, None]), 0.0)
    qd = jnp.exp((n + 1.0)[None, :] * log_gamma[:, None])[:, :, None]
    kd = jnp.exp((c - 1.0 - n)[None, :] * log_gamma[:, None])[:, :, None]
    cd = jnp.broadcast_to(jnp.exp(c * log_gamma)[:, None, None], (RET_HEADS, 1, RET_DV))
    const = lambda *shape: pl.BlockSpec(shape, lambda b, i: (0,) * len(shape))
    tok = lambda w, j: pl.BlockSpec((1, c, w), lambda b, i: (b, i, j))
    nc = seq // c
    return pl.pallas_call(
        _ret_core_kernel,
        out_shape=(jax.ShapeDtypeStruct((bsz, seq, d), f32),
                   jax.ShapeDtypeStruct((bsz, seq * ROW_TILES, LANE), f32),
                   jax.ShapeDtypeStruct((ROUTER_ROWS, bsz * seq), f32)),
        grid=(bsz, nc),
        in_specs=[
            tok(RET_QK, 0), tok(RET_QK, 1), tok(RET_V, 1), tok(RET_V, 2),
            tok(d, 0),
            pl.BlockSpec((1, 6, d), lambda b, i: (b, 0, 0)),
            const(RET_HEADS, c, c), const(RET_HEADS, c, 1), const(RET_HEADS, c, 1),
            const(RET_HEADS, 1, RET_DV),
            const(1, RET_V), const(RET_V, d), const(1, d),
        ] + _logit_specs(d),
        out_specs=(tok(d, 0), pl.BlockSpec((1, c * ROW_TILES, LANE), lambda b, i: (b, i, 0)),
                   pl.BlockSpec((ROUTER_ROWS, c), lambda b, i: (0, b * nc + i))),
        scratch_shapes=[pltpu.VMEM((RET_HEADS, RET_DK, RET_DV), f32),
                        pltpu.VMEM((c, RET_V), BF16)],
        compiler_params=_params("arbitrary", "arbitrary"),
        name="ret_core",
    )(proj, proj, proj, proj, x, ada, intra, qd, kd, cd, gn, w_out, ngain, *router)


def _moe_kernel(blk_ref, exp_ref, nitem_ref, start_ref, cnt_ref, *refs):
    tok_first, refs = refs[:MOE_AHEAD], refs[MOE_AHEAD:]
    tok_ahead, h_hbm, w1_ref, w3_ref, w2_ref, o_ref, xbuf, w1b, w3b, w2b, gsem = refs
    n_buf = MOE_AHEAD + 1
    w = pl.program_id(0)
    bm = MOE_BM
    n_item = nitem_ref[0]
    d = w1b.shape[0]
    e = exp_ref[w]
    blk = blk_ref[w]
    slot = blk % n_buf
    ahead_slot = (blk + MOE_AHEAD) % n_buf
    prev = jnp.maximum(w - 1, 0)
    first_visit = jnp.logical_or(w == 0, blk_ref[prev] != blk)
    live = w < n_item

    def gather_row(tok_ref, s, r):
        priority = r % 2 if isinstance(r, int) else 0
        pltpu.make_async_copy(_row_tile(h_hbm, (), tok_ref[0, 0, r]), _row_tile(xbuf, (s,), r),
                              gsem.at[s]).start(priority=priority)

    def wait_gather(s):
        pltpu.make_async_copy(h_hbm.at[pl.ds(0, bm * ROW_TILES), :], xbuf.at[s], gsem.at[s]).wait()

    def expert(row_copies):
        n_dots = 2 * (d // MOE_TN)
        per_dot = -(-len(row_copies) // n_dots)
        groups = iter([row_copies[g * per_dot:(g + 1) * per_dot] for g in range(n_dots)])

        def piece(lhs, w_ref, c):
            for issue in next(groups, ()):
                issue()
            return _dot(lhs, w_ref[:, c * MOE_TN:(c + 1) * MOE_TN])

        x = _load_row_tiles(xbuf, (slot,), bm).astype(BF16)
        hid = []
        for c in range(d // MOE_TN):
            a = piece(x, w1b, c)
            b = piece(x, w3b, c)
            hid.append((_silu(a) * b).astype(BF16))
        hid = jnp.concatenate(hid, axis=1)
        y = jnp.concatenate([piece(hid, w2b, c) for c in range(d // MOE_TN)], axis=1)
        row = blk * bm + lax.broadcasted_iota(jnp.int32, (bm, 1), 0)
        mine = jnp.logical_and(row >= start_ref[e], row < start_ref[e] + cnt_ref[e])
        return y, mine

    @pl.when(jnp.logical_and(live, jnp.logical_or(w == 0, exp_ref[prev] != e)))
    def _():
        w1b[...] = w1_ref[0, 0].astype(BF16)
        w3b[...] = w3_ref[0, 0].astype(BF16)
        w2b[...] = w2_ref[0, 0].astype(BF16)

    @pl.when(w == 0)
    def _():
        for k in range(MOE_AHEAD):
            def body(r, carry, k=k):
                gather_row(tok_first[k], k, r)
                return carry
            lax.fori_loop(0, bm, body, 0, unroll=8)

    @pl.when(jnp.logical_and(live, first_visit))
    def _():
        wait_gather(slot)
        y, mine = expert([functools.partial(gather_row, tok_ahead, ahead_slot, r) for r in range(bm)])
        _store_row_tiles(o_ref, (), jnp.where(mine, y, 0.0))

    @pl.when(jnp.logical_and(live, jnp.logical_not(first_visit)))
    def _():
        y, mine = expert([])
        _store_row_tiles(o_ref, (), jnp.where(mine, y, _load_row_tiles(o_ref, (), bm)))

    @pl.when(w == n_item - 1)
    def _():
        for k in range(1, n_buf):
            wait_gather((blk + k) % n_buf)


def _moe(h, order, counts, w1, w3, w2, layer):
    m = order.shape[0]
    d = w1.shape[-1]
    bm = MOE_BM
    n_blocks = m // bm
    n_items = n_blocks + N_EXPERTS - 1
    ends = jnp.cumsum(counts)
    starts = ends - counts
    first_blk = starts // bm
    last_blk = jnp.maximum(ends - 1, starts) // bm
    visits = jnp.where(counts > 0, last_blk - first_blk + 1, 0)
    item_end = jnp.cumsum(visits)
    item_start = item_end - visits
    n_item = item_end[-1]
    w = jnp.minimum(jnp.arange(n_items, dtype=jnp.int32), n_item - 1)
    item_e = jnp.sum(w[:, None] >= item_end[None, :], axis=1).astype(jnp.int32)
    item_blk = (first_blk[item_e] + w - item_start[item_e]).astype(jnp.int32)
    wspec = lambda: pl.BlockSpec((1, 1, d, d), lambda i, blk, e, *_: (layer, e[i], 0, 0))
    idx = lambda f: pl.BlockSpec((1, 1, bm), f, memory_space=pltpu.SMEM)
    tok3 = (order >> TOP_K_SHIFT).reshape(n_blocks, 1, bm)
    first = [idx(lambda i, blk, *_, k=k: (k, 0, 0)) for k in range(MOE_AHEAD)]
    ahead = idx(lambda i, blk, *_: (jnp.minimum(blk[i] + MOE_AHEAD, n_blocks - 1), 0, 0))
    return pl.pallas_call(
        _moe_kernel,
        out_shape=jax.ShapeDtypeStruct((m * ROW_TILES, LANE), F32),
        grid_spec=pltpu.PrefetchScalarGridSpec(
            num_scalar_prefetch=5,
            grid=(n_items,),
            in_specs=first + [ahead, pl.BlockSpec(memory_space=pl.ANY), wspec(), wspec(), wspec()],
            out_specs=pl.BlockSpec((bm * ROW_TILES, LANE), lambda i, blk, *_: (blk[i], 0)),
            scratch_shapes=[pltpu.VMEM((MOE_AHEAD + 1, bm * ROW_TILES, LANE), F32),
                            pltpu.VMEM((d, d), BF16), pltpu.VMEM((d, d), BF16), pltpu.VMEM((d, d), BF16),
                            pltpu.SemaphoreType.DMA((MOE_AHEAD + 1,))],
        ),
        compiler_params=_params("arbitrary"),
        name="moe",
    )(item_blk, item_e, n_item.astype(jnp.int32).reshape(1), starts.astype(jnp.int32), counts.astype(jnp.int32),
      *([tok3] * (MOE_AHEAD + 1)), h, w1, w3, w2)


class _RowGather:
    def __init__(self, ys_hbm, buf, sem, tm):
        self.ys, self.buf, self.sem, self.tm = ys_hbm, buf, sem, tm

    def row_copy(self, d_ref, slot, k, t):
        pltpu.make_async_copy(_row_tile(self.ys, (), d_ref[0, 0, t]), _row_tile(self.buf, (slot, k), t),
                              self.sem.at[slot]).start(priority=k)

    def start(self, d_refs, slot):
        def body(t, carry):
            for k, d_ref in enumerate(d_refs):
                self.row_copy(d_ref, slot, k, t)
            return carry
        lax.fori_loop(0, self.tm, body, 0, unroll=8)

    def unrolled(self, d_refs, slot):
        return [functools.partial(self.row_copy, d_ref, slot, k, t)
                for t in range(self.tm) for k, d_ref in enumerate(d_refs)]

    def wait(self, slot):
        for k in range(TOP_K):
            pltpu.make_async_copy(self.ys.at[pl.ds(0, self.tm * ROW_TILES), :], self.buf.at[slot, k],
                                  self.sem.at[slot]).wait()

    def combined(self, slot, w1_ref, w2_ref):
        return (w1_ref[...] * _load_row_tiles(self.buf, (slot, 0), self.tm)
                + w2_ref[...] * _load_row_tiles(self.buf, (slot, 1), self.tm))


def _gather_specs(tm, nt_total, step_of):
    cur = lambda: pl.BlockSpec((1, 1, tm), lambda *g: (step_of(*g), 0, 0), memory_space=pltpu.SMEM)
    nxt = lambda: pl.BlockSpec((1, 1, tm), lambda *g: (jnp.minimum(step_of(*g) + 1, nt_total - 1), 0, 0),
                               memory_space=pltpu.SMEM)
    wcol = lambda: pl.BlockSpec((tm, 1), lambda *g: (step_of(*g), 0))
    return [cur(), cur(), nxt(), nxt(), wcol(), wcol()]


def _gather_scratch(tm):
    return [pltpu.VMEM((2, TOP_K, tm * ROW_TILES, LANE), F32), pltpu.SemaphoreType.DMA((2,))]


def _group_norm64(t, ind_ref, indt_ref):
    ss = _dot((t * t).astype(BF16), ind_ref[...])
    r = lax.rsqrt(ss * (1.0 / DIFF_HEAD_DIM) + EPS)
    r_hi = r.astype(BF16)
    r_lo = (r - r_hi.astype(F32)).astype(BF16)
    rb = _dot(jnp.concatenate([r_hi, r_lo], axis=1), indt_ref[...])
    return t * rb


def _kvq_kernel(d1c, d2c, d1n, d2n, w1_ref, w2_ref, ys_hbm, x_ref, ada0_ref, ada1_ref, kvada_ref, kvgain_ref,
                qgain_ref, knorm_ref, qnorm_ref, wk_ref, wvt_ref, wq_ref, ind_ref, indt_ref,
                xo_ref, q_ref, k_ref, vt_ref, gbuf, gsem):
    tm = x_ref.shape[1]
    d = x_ref.shape[2]
    step = pl.program_id(0) * pl.num_programs(1) + pl.program_id(1)
    slot = step % 2
    gather = _RowGather(ys_hbm, gbuf, gsem, tm)

    @pl.when(step == 0)
    def _():
        gather.start((d1c, d2c), 0)

    gather.wait(slot)
    xn = x_ref[0] + ada0_ref[0, 5:6, :] * gather.combined(slot, w1_ref, w2_ref)
    xo_ref[0] = xn
    copies = gather.unrolled((d1n, d2n), 1 - slot)
    n_pieces = 3 * (d // KVQ_TN)
    per_piece = -(-len(copies) // n_pieces)
    groups = iter([copies[g * per_piece:(g + 1) * per_piece] for g in range(n_pieces)])

    def issue_group():
        for issue in next(groups):
            issue()

    def project(lhs, w_ref):
        out = []
        for c in range(d // KVQ_TN):
            issue_group()
            out.append(_dot(lhs, w_ref[:, c * KVQ_TN:(c + 1) * KVQ_TN]))
        return jnp.concatenate(out, axis=1)

    r = _rms(xn)
    hk = ((r * kvgain_ref[...]) * (1.0 + kvada_ref[0, 1:2, :]) + kvada_ref[0, 0:1, :]).astype(BF16)
    hq = ((r * qgain_ref[...]) * (1.0 + ada1_ref[0, 1:2, :]) + ada1_ref[0, 0:1, :]).astype(BF16)
    hd2 = 2 * DIFF_HEAD_DIM
    kk = project(hk, wk_ref)
    kn = (_group_norm64(kk, ind_ref, indt_ref) * knorm_ref[...]).astype(BF16)
    qq = project(hq, wq_ref)
    qn = (_group_norm64(qq, ind_ref, indt_ref) * qnorm_ref[...]).astype(BF16)
    for hd in range(DIFF_HEADS):
        k_ref[0, hd] = kn[:, hd * hd2:(hd + 1) * hd2]
        q_ref[0, hd] = qn[:, hd * hd2:(hd + 1) * hd2]
    vt = []
    for c in range(d // KVQ_TN):
        issue_group()
        vt.append(_dot_nt(wvt_ref[c * KVQ_TN:(c + 1) * KVQ_TN, :], hk))
    vt = jnp.concatenate(vt, axis=0).astype(BF16)
    for j in range(tm // ATT_TK):
        vt_ref[0, j] = vt[:, j * ATT_TK:(j + 1) * ATT_TK].reshape(DIFF_HEADS, hd2, ATT_TK)

    @pl.when(step == pl.num_programs(0) * pl.num_programs(1) - 1)
    def _():
        gather.wait(1 - slot)


def _kvq(x, ys, dest1, dest2, w1, w2, ada0, ada1, kvada, kv_gain, q_gain, k_norm, q_norm, w_k, w_vt, w_q):
    bsz, seq, d = x.shape
    tm = ROW_TILE
    nt = seq // tm
    nt_total = bsz * nt
    lane_group = jnp.arange(d, dtype=jnp.int32) // DIFF_HEAD_DIM
    ind = (lane_group[:, None] == jnp.arange(LANE, dtype=jnp.int32)[None, :]).astype(BF16)
    indt = jnp.concatenate([ind.T, ind.T], axis=0)
    reps = d // DIFF_HEAD_DIM
    q_scale = jnp.tile(q_norm, reps)[None, :] * (DIFF_HEAD_DIM ** -0.5 * math.log2(math.e))
    const = lambda *shape: pl.BlockSpec(shape, lambda b, i: (0,) * len(shape))
    tok = lambda w: pl.BlockSpec((1, tm, w), lambda b, i: (b, i, 0))
    hd2 = 2 * DIFF_HEAD_DIM
    heads = pl.BlockSpec((1, DIFF_HEADS, tm, hd2), lambda b, i: (b, 0, i, 0))
    ada_spec = lambda rows: pl.BlockSpec((1, rows, d), lambda b, i: (b, 0, 0))
    col = lambda a: a.reshape(-1, 1)
    idx = lambda a: a.reshape(nt_total, 1, tm)
    return pl.pallas_call(
        _kvq_kernel,
        out_shape=(jax.ShapeDtypeStruct((bsz, seq, d), F32),
                   jax.ShapeDtypeStruct((bsz, DIFF_HEADS, seq, hd2), BF16),
                   jax.ShapeDtypeStruct((bsz, DIFF_HEADS, seq, hd2), BF16),
                   jax.ShapeDtypeStruct((bsz, seq // ATT_TK, DIFF_HEADS, hd2, ATT_TK), BF16)),
        grid=(bsz, nt),
        in_specs=_gather_specs(tm, nt_total, lambda b, i: b * nt + i) + [
            pl.BlockSpec(memory_space=pl.ANY),
            tok(d),
            ada_spec(6), ada_spec(6), ada_spec(2),
            const(1, d), const(1, d), const(1, d), const(1, d),
            const(d, d), const(d, d), const(d, d), const(d, LANE), const(2 * LANE, d),
        ],
        out_specs=(tok(d), heads, heads,
                   pl.BlockSpec((1, tm // ATT_TK, DIFF_HEADS, hd2, ATT_TK), lambda b, i: (b, i, 0, 0, 0))),
        scratch_shapes=_gather_scratch(tm),
        compiler_params=_params("arbitrary", "arbitrary"),
        name="kvq",
    )(idx(dest1), idx(dest2), idx(dest1), idx(dest2), col(w1), col(w2), ys, x, ada0, ada1, kvada,
      kv_gain, q_gain, jnp.tile(k_norm, reps)[None, :], q_scale, w_k, w_vt, w_q, ind, indt)


def _attn_kernel(lambda_init, q_ref, k_ref, vt_ref, x_ref, ada_ref, lam_ref, subln_ref, wo_ref, ngain_ref,
                 wr_hi_ref, wr_lo_ref, br_ref, xo_ref, h_ref, lt_ref,
                 s_a, s_b, p_a, p_b, acc, obuf):
    i = pl.program_id(1)
    tq, tk = ATT_TQ, ATT_TK
    hd2 = 2 * DIFF_HEAD_DIM
    n_kblk = k_ref.shape[2] // tk
    lam = lam_ref[...]
    lam_full = (jnp.exp(jnp.sum(lam[0:1] * lam[1:2], axis=-1, keepdims=True))
                - jnp.exp(jnp.sum(lam[2:3] * lam[3:4], axis=-1, keepdims=True)) + lambda_init)
    lane = lax.broadcasted_iota(jnp.int32, (1, hd2), 1)
    keep = (jnp.where(lane < DIFF_HEAD_DIM, 1.0, 0.0).astype(BF16),
            jnp.where(lane >= DIFF_HEAD_DIM, 1.0, 0.0).astype(BF16))
    key = lax.broadcasted_iota(jnp.int32, (tk, tq), 0)
    qry = lax.broadcasted_iota(jnp.int32, (1, tq), 1)
    key_limit = ((i * tq + qry) // MASK_CHUNK + 1) * MASK_CHUNK
    neg_inf = jnp.float32(-jnp.inf)
    first_masked = i * (tq // tk)
    n_blocks = first_masked + tq // tk
    n_pairs = (n_blocks + 1) // 2
    n_plain = jnp.maximum((first_masked - 1) // 2, 0)

    streams = [(u, t) for u in range(ATT_HEADS) for t in range(2)]

    def head_group(hg, _):
        hds = [hg * ATT_HEADS + u for u in range(ATT_HEADS)]
        qs = []
        for hd in hds:
            qh = q_ref[0, hd]
            qs.append((qh * keep[0], qh * keep[1]))

        def produce_scores(jb, s_out, masked):
            start = pl.multiple_of(jnp.minimum(jb, n_kblk - 1) * tk, tk)
            visible = key < key_limit - jb * tk
            maxima = []
            for n, (u, t) in enumerate(streams):
                st = _dot_nt(k_ref[0, hds[u], pl.ds(start, tk), :], qs[u][t])
                if masked:
                    st = jnp.where(visible, st, neg_inf)
                s_out[n] = st
                maxima.append(jnp.max(st, axis=0, keepdims=True))
            return tuple(maxima)

        def step(j, s_in, s_out, p_in, p_out, state, masked):
            stats, blk_max = state
            jv = jnp.clip(j - 1, 0, n_kblk - 1)
            next_max = produce_scores(j + 1, s_out, masked)
            new_stats = []
            for n, (u, t) in enumerate(streams):
                m, l = stats[n]
                m_new = jnp.maximum(m, blk_max[n])
                alpha = jnp.exp2(m - m_new)
                p = jnp.exp2(s_in[n] - m_new)
                p_out[n] = p.astype(BF16)
                new_stats.append((m_new, alpha * l + jnp.sum(p, axis=0, keepdims=True)))
                acc[n] = alpha * (acc[n] + _dot(vt_ref[0, jv, hds[u]], p_in[n]))
            return tuple(new_stats), next_max

        p_b[...] = jnp.zeros_like(p_b)
        acc[...] = jnp.zeros_like(acc)
        stat0 = (jnp.full((1, tq), neg_inf, F32), jnp.zeros((1, tq), F32))
        state = ((stat0,) * len(streams), produce_scores(0, s_a, True))

        def pair(masked, jj, state):
            state = step(2 * jj, s_a, s_b, p_b, p_a, state, masked)
            return step(2 * jj + 1, s_b, s_a, p_a, p_b, state, masked)

        state = lax.fori_loop(0, n_plain, functools.partial(pair, False), state)
        (stats, _) = lax.fori_loop(n_plain, n_pairs, functools.partial(pair, True), state)
        jv = jnp.minimum(2 * n_pairs - 1, n_kblk - 1)
        for u, hd in enumerate(hds):
            vtb = vt_ref[0, jv, hd]
            a1 = acc[2 * u] + _dot(vtb, p_b[2 * u])
            a2 = acc[2 * u + 1] + _dot(vtb, p_b[2 * u + 1])
            o = a1 * (1.0 / stats[2 * u][1]) - lam_full * (a2 * (1.0 / stats[2 * u + 1][1]))
            o = o * lax.rsqrt(jnp.mean(o * o, axis=0, keepdims=True) + EPS)
            obuf[hd] = (o * (subln_ref[...] * (1.0 - lambda_init))).astype(BF16)
        return 0

    lax.fori_loop(0, DIFF_HEADS // ATT_HEADS, head_group, 0)

    y = _dot_tn(obuf[...].reshape(DIFF_HEADS * hd2, tq), wo_ref[...])
    xn = x_ref[0] + ada_ref[0, 2:3, :] * y
    xo_ref[0] = xn
    h = _ffn_prenorm_logits(xn, ngain_ref[...], ada_ref[0, 3:4, :], ada_ref[0, 4:5, :],
                            wr_hi_ref, wr_lo_ref, br_ref, lt_ref)
    _store_row_tiles(h_ref, (0,), h)


def _attn(q, k, vt, x, ada, lam, subln, w_out, ngain, router, lambda_init):
    bsz, seq, d = x.shape
    tq, tk = ATT_TQ, ATT_TK
    hd2 = 2 * DIFF_HEAD_DIM
    const = lambda *shape: pl.BlockSpec(shape, lambda b, i: (0,) * len(shape))
    tok = lambda w: pl.BlockSpec((1, tq, w), lambda b, i: (b, i, 0))
    nq = seq // tq
    ns = 2 * ATT_HEADS
    return pl.pallas_call(
        functools.partial(_attn_kernel, lambda_init),
        out_shape=(jax.ShapeDtypeStruct((bsz, seq, d), F32),
                   jax.ShapeDtypeStruct((bsz, seq * ROW_TILES, LANE), F32),
                   jax.ShapeDtypeStruct((ROUTER_ROWS, bsz * seq), F32)),
        grid=(bsz, nq),
        in_specs=[
            pl.BlockSpec((1, DIFF_HEADS, tq, hd2), lambda b, i: (b, 0, i, 0)),
            pl.BlockSpec((1, DIFF_HEADS, seq, hd2), lambda b, i: (b, 0, 0, 0)),
            pl.BlockSpec((1, seq // tk, DIFF_HEADS, hd2, tk), lambda b, i: (b, 0, 0, 0, 0)),
            tok(d),
            pl.BlockSpec((1, 6, d), lambda b, i: (b, 0, 0)),
            const(4, DIFF_HEAD_DIM), const(hd2, 1), const(d, d), const(1, d),
        ] + _logit_specs(d),
        out_specs=(tok(d), pl.BlockSpec((1, tq * ROW_TILES, LANE), lambda b, i: (b, i, 0)),
                   pl.BlockSpec((ROUTER_ROWS, tq), lambda b, i: (0, b * nq + i))),
        scratch_shapes=[pltpu.VMEM((ns, tk, tq), F32), pltpu.VMEM((ns, tk, tq), F32),
                        pltpu.VMEM((ns, tk, tq), BF16), pltpu.VMEM((ns, tk, tq), BF16),
                        pltpu.VMEM((ns, hd2, tq), F32),
                        pltpu.VMEM((DIFF_HEADS, hd2, tq), BF16)],
        compiler_params=_params("arbitrary", "arbitrary"),
        name="attn",
    )(q, k, vt, x, ada, lam, subln, w_out, ngain, *router)


def _combine_kernel(d1c, d2c, d1n, d2n, w1_ref, w2_ref, ys_hbm, x_ref, ada_ref, o_ref, gbuf, gsem):
    tm = x_ref.shape[1]
    step = pl.program_id(0) * pl.num_programs(1) + pl.program_id(1)
    slot = step % 2
    gather = _RowGather(ys_hbm, gbuf, gsem, tm)

    @pl.when(step == 0)
    def _():
        gather.start((d1c, d2c), 0)

    @pl.when(step + 1 < pl.num_programs(0) * pl.num_programs(1))
    def _():
        gather.start((d1n, d2n), 1 - slot)

    gather.wait(slot)
    o_ref[0] = x_ref[0] + ada_ref[0, 5:6, :] * gather.combined(slot, w1_ref, w2_ref)


def _combine(x, ys, dest1, dest2, w1, w2, ada):
    bsz, seq, d = x.shape
    tm = ROW_TILE
    nt = seq // tm
    nt_total = bsz * nt
    col = lambda a: a.reshape(-1, 1)
    idx = lambda a: a.reshape(nt_total, 1, tm)
    return pl.pallas_call(
        _combine_kernel,
        out_shape=jax.ShapeDtypeStruct((bsz, seq, d), F32),
        grid=(bsz, nt),
        in_specs=_gather_specs(tm, nt_total, lambda b, i: b * nt + i) + [
            pl.BlockSpec(memory_space=pl.ANY),
            pl.BlockSpec((1, tm, d), lambda b, i: (b, i, 0)),
            pl.BlockSpec((1, 6, d), lambda b, i: (b, 0, 0)),
        ],
        out_specs=pl.BlockSpec((1, tm, d), lambda b, i: (b, i, 0)),
        scratch_shapes=_gather_scratch(tm),
        compiler_params=_params("arbitrary", "arbitrary"),
        name="combine",
    )(idx(dest1), idx(dest2), idx(dest1), idx(dest2), col(w1), col(w2), ys, x, ada)


def _router_weights(w_group, b_group, w_expert, b_expert):
    d = w_group.shape[0]
    pad = ROUTER_ROWS - N_EXPERTS - N_GROUPS
    w = jnp.concatenate([w_expert.T, w_group.T, jnp.zeros((pad, d), F32)], axis=0)
    b = jnp.concatenate([b_expert, b_group, jnp.zeros((pad,), F32)])[:, None]
    w_hi = w.astype(BF16)
    w_lo = (w - w_hi.astype(F32)).astype(BF16)
    return w_hi, w_lo, b


def _routing_tables(route, cnt):
    rec = lambda r: route[r]
    counts = cnt[:, 0].astype(jnp.int32)
    starts = jnp.cumsum(counts) - counts
    dest1 = starts[rec(0).astype(jnp.int32)] + rec(2).astype(jnp.int32)
    dest2 = starts[rec(1).astype(jnp.int32)] + rec(3).astype(jnp.int32)
    dest = jnp.stack([dest1, dest2], axis=1).reshape(-1)
    _, order = lax.sort((dest, jnp.arange(dest.shape[0], dtype=jnp.int32)), num_keys=1)
    return order, dest1, dest2, rec(4), rec(5), counts


def kernel(x, c, ada_w, ada_b, norm_mix, norm_ffn, ret_w_in, ret_gn, ret_w_out, kv_ada_w, kv_ada_b, kv_norm, kv_w, k_norm, diff_w_q, q_norm, diff_lam, diff_subln, diff_w_out, moe_w_group, moe_b_group, moe_w_expert, moe_b_expert, moe_w1, moe_w3, moe_w2):
    bsz, seq, d = x.shape
    n_tok = bsz * seq

    ada = _ada(c, ada_w, ada_b).reshape(2, bsz, 6, d)
    kvada = _ada(c, kv_ada_w[None], kv_ada_b[None]).reshape(bsz, 2, d)

    half = RET_DK // 2
    inv_freq = 1.0 / (ROPE_BASE ** (jnp.arange(half, dtype=F32) / half))
    ang = jnp.arange(seq, dtype=F32)[:, None] * inv_freq[None, :]
    cos, sin = jnp.cos(ang), jnp.sin(ang)

    routers = [_router_weights(moe_w_group[l], moe_b_group[l], moe_w_expert[l], moe_b_expert[l])
               for l in range(2)]

    def moe_layer(h, logits_t, layer):
        order, dest1, dest2, w1, w2, counts = _routing_tables(*_route(logits_t))
        ys = _moe(h.reshape(n_tok * ROW_TILES, LANE), order, counts, moe_w1, moe_w3, moe_w2, layer)
        return ys, dest1, dest2, w1, w2

    proj = _ret_in(x, ada[0], norm_mix[0][None, :], cos, sin, ret_w_in[0].astype(BF16))
    x1, h, logits_t = _ret_core(proj, x, ada[0], ret_gn[0][None, :], ret_w_out[0].astype(BF16),
                                norm_ffn[0][None, :], routers[0])
    moe0 = moe_layer(h, logits_t, 0)

    x2, q, k, vt = _kvq(x1, *moe0, ada[0], ada[1], kvada, kv_norm[None, :], norm_mix[1][None, :], k_norm, q_norm[0],
                        kv_w[:, :d].astype(BF16), kv_w[:, d:].T.astype(BF16), diff_w_q[0].astype(BF16))
    lambda_init = 0.8 - 0.6 * math.exp(-0.3 * 1)
    x3, h, logits_t = _attn(q, k, vt, x2, ada[1], diff_lam[0], diff_subln[0][:, None], diff_w_out[0].astype(BF16),
                            norm_ffn[1][None, :], routers[1], lambda_init)
    moe1 = moe_layer(h, logits_t, 1)
    return _combine(x3, *moe1, ada[1])
```

```python
import functools
import math

import jax
import jax.numpy as jnp
from jax import lax
from jax.experimental import pallas as pl
from jax.experimental.pallas import tpu as pltpu

F32 = jnp.float32
BF16 = jnp.bfloat16

D_MODEL = 1024
EPS = 1e-6

RET_HEADS = 4
RET_DK = D_MODEL // RET_HEADS
RET_DV = 2 * RET_DK
RET_QK = RET_HEADS * RET_DK
RET_V = RET_HEADS * RET_DV
RET_IN = 2 * RET_QK + 2 * RET_V
ROPE_BASE = 10000.0
RET_CHUNK = 256

DIFF_HEAD_DIM = 64
DIFF_HEADS = D_MODEL // (2 * DIFF_HEAD_DIM)
MASK_CHUNK = 64
ATT_TQ = 512
ATT_TK = 256
ATT_HEADS = 1

N_GROUPS = 4
EXPERTS_PER_GROUP = 4
N_EXPERTS = N_GROUPS * EXPERTS_PER_GROUP
TOP_K = 2
TOP_K_SHIFT = 1
MOE_BM = 512
MOE_TN = 256
KVQ_TN = 256
MOE_AHEAD = 2
ROUTER_ROWS = 32
ROUTE_ROWS = 8
ROW_TILE = 512
ROUTE_TILE = 2048
ROUTE_SEG = 256

LANE = 128
SUBLANE = 8
VMEM_LIMIT = 56 * 1024 * 1024


def _dot(a, b):
    return jnp.dot(a, b, preferred_element_type=F32)


def _dot_nt(a, b):
    return lax.dot_general(a, b, (((1,), (1,)), ((), ())), preferred_element_type=F32)


def _dot_tn(a, b):
    return lax.dot_general(a, b, (((0,), (0,)), ((), ())), preferred_element_type=F32)


def _silu(x):
    return x * (1.0 / (1.0 + jnp.exp(-x)))


def _rms(x):
    return x * lax.rsqrt(jnp.mean(x * x, axis=-1, keepdims=True) + EPS)


def _params(*sem, flags=None):
    return pltpu.CompilerParams(dimension_semantics=sem, vmem_limit_bytes=VMEM_LIMIT, flags=flags)


ROW_TILES = D_MODEL // LANE


def _load_row_tiles(ref, lead, rows):
    return jnp.concatenate(
        [ref[lead + (pl.ds(c, rows, stride=ROW_TILES), slice(None))] for c in range(ROW_TILES)], axis=1)


def _store_row_tiles(ref, lead, val):
    rows = val.shape[0]
    for c in range(ROW_TILES):
        ref[lead + (pl.ds(c, rows, stride=ROW_TILES), slice(None))] = val[:, c * LANE:(c + 1) * LANE]


def _row_tile(ref, lead, r):
    return ref.at[lead + (pl.ds(pl.multiple_of(r * ROW_TILES, ROW_TILES), ROW_TILES), slice(None))]


def _ada_kernel(c_ref, w_ref, b_ref, o_ref):
    ca = _silu(c_ref[...])
    o_ref[0] = jnp.dot(ca, w_ref[0], preferred_element_type=F32,
                       precision=lax.Precision.HIGHEST) + b_ref[0]


def _ada(c, w, b):
    n_l, d, n = w.shape
    bsz = c.shape[0]
    tn = 1024
    return pl.pallas_call(
        _ada_kernel,
        out_shape=jax.ShapeDtypeStruct((n_l, bsz, n), F32),
        grid=(n_l, n // tn),
        in_specs=[
            pl.BlockSpec((bsz, d), lambda l, j: (0, 0)),
            pl.BlockSpec((1, d, tn), lambda l, j: (l, 0, j)),
            pl.BlockSpec((1, 1, tn), lambda l, j: (l, 0, j)),
        ],
        out_specs=pl.BlockSpec((1, bsz, tn), lambda l, j: (l, 0, j)),
        compiler_params=_params("arbitrary", "arbitrary"),
        name="ada",
    )(c, w, b.reshape(n_l, 1, n))


def _ffn_prenorm_logits(xn, gain, shift, scale, wr_hi_ref, wr_lo_ref, br_ref, lt_ref):
    h = _rms(xn) * gain
    h = h * (1.0 + scale) + shift
    h_hi = h.astype(BF16)
    h_lo = (h - h_hi.astype(F32)).astype(BF16)
    w_hi = wr_hi_ref[...]
    lt_ref[...] = _dot_nt(w_hi, h_hi) + _dot_nt(wr_lo_ref[...], h_hi) + _dot_nt(w_hi, h_lo) + br_ref[...]
    return h


def _route_kernel(lt_ref, tri_ref, route_ref, cnt_ref, carry):
    lt = lt_ref[...]
    c = lt.shape[1]
    seg = tri_ref.shape[0]
    neg_inf = jnp.float32(-jnp.inf)

    grow = lax.broadcasted_iota(jnp.int32, (SUBLANE, c), 0).astype(F32)
    g = jnp.where(grow < N_GROUPS, lt[N_EXPERTS:N_EXPERTS + SUBLANE], neg_inf)
    gmax = jnp.max(g, axis=0, keepdims=True)
    gidx = jnp.min(jnp.where(g == gmax, grow, float(SUBLANE)), axis=0, keepdims=True)
    gate = 1.0 / jnp.sum(jnp.exp(g - gmax), axis=0, keepdims=True)

    erow_i = lax.broadcasted_iota(jnp.int32, (N_EXPERTS, c), 0)
    erow = erow_i.astype(F32)
    egroup = (erow_i // EXPERTS_PER_GROUP).astype(F32)
    el = jnp.where(egroup == gidx, lt[0:N_EXPERTS], neg_inf)
    m1 = jnp.max(el, axis=0, keepdims=True)
    i1 = jnp.min(jnp.where(el == m1, erow, float(N_EXPERTS)), axis=0, keepdims=True)
    el2 = jnp.where(erow == i1, neg_inf, el)
    m2 = jnp.max(el2, axis=0, keepdims=True)
    i2 = jnp.min(jnp.where(el2 == m2, erow, float(N_EXPERTS)), axis=0, keepdims=True)
    t = jnp.exp(m2 - m1)
    den = 1.0 / (1.0 + t)
    w1 = gate * den
    w2 = gate * t * den

    @pl.when(pl.program_id(0) == 0)
    def _():
        carry[...] = jnp.zeros_like(carry)

    oh1 = erow == i1
    oh2 = erow == i2
    oh = jnp.where(jnp.logical_or(oh1, oh2), 1.0, 0.0)
    offset = carry[...]
    before = []
    for s in range(c // seg):
        oh_s = oh[:, s * seg:(s + 1) * seg]
        before.append(offset + _dot(oh_s.astype(BF16), tri_ref[...]))
        offset = offset + jnp.sum(oh_s, axis=1, keepdims=True)
    before = jnp.concatenate(before, axis=1)
    rank1 = jnp.sum(jnp.where(oh1, before, 0.0), axis=0, keepdims=True)
    rank2 = jnp.sum(jnp.where(oh2, before, 0.0), axis=0, keepdims=True)
    carry[...] = offset
    cnt_ref[...] = jnp.broadcast_to(offset, cnt_ref.shape)

    rrow = lax.broadcasted_iota(jnp.int32, (ROUTE_ROWS, c), 0)
    rec = jnp.zeros((ROUTE_ROWS, c), F32)
    for idx, val in enumerate((i1, i2, rank1, rank2, w1, w2)):
        rec = jnp.where(rrow == idx, val, rec)
    route_ref[...] = rec


def _route(lt):
    n_tok = lt.shape[1]
    tile = min(ROUTE_TILE, n_tok)
    t = jnp.arange(ROUTE_SEG, dtype=jnp.int32)
    earlier = (t[:, None] < t[None, :]).astype(BF16)
    return pl.pallas_call(
        _route_kernel,
        out_shape=(jax.ShapeDtypeStruct((ROUTE_ROWS, n_tok), F32),
                   jax.ShapeDtypeStruct((N_EXPERTS, LANE), F32)),
        grid=(n_tok // tile,),
        in_specs=[pl.BlockSpec((ROUTER_ROWS, tile), lambda i: (0, i)),
                  pl.BlockSpec((ROUTE_SEG, ROUTE_SEG), lambda i: (0, 0))],
        out_specs=(pl.BlockSpec((ROUTE_ROWS, tile), lambda i: (0, i)),
                   pl.BlockSpec((N_EXPERTS, LANE), lambda i: (0, 0))),
        scratch_shapes=[pltpu.VMEM((N_EXPERTS, 1), F32)],
        compiler_params=_params("arbitrary"),
        name="route",
    )(lt, earlier)


def _logit_specs(d):
    const = lambda *shape: pl.BlockSpec(shape, lambda b, i: (0,) * len(shape))
    return [const(ROUTER_ROWS, d), const(ROUTER_ROWS, d), const(ROUTER_ROWS, 1)]


def _ret_in_kernel(x_ref, ada_ref, gain_ref, cos_ref, sin_ref, w_ref, o_ref):
    x = x_ref[0]
    shift = ada_ref[0, 0:1, :]
    scale = ada_ref[0, 1:2, :]
    h = (_rms(x) * gain_ref[...]) * (1.0 + scale) + shift
    hb = h.astype(BF16)
    cos = cos_ref[...]
    sin = sin_ref[...]
    half = RET_DK // 2
    for j in range(RET_IN // D_MODEL):
        p = _dot(hb, w_ref[:, j * D_MODEL:(j + 1) * D_MODEL])
        if j < 2:
            post = 1.0 if j == 0 else RET_DK ** -0.5
            for hd in range(RET_HEADS):
                lo = hd * RET_DK
                x1 = p[:, lo:lo + half]
                x2 = p[:, lo + half:lo + RET_DK]
                o_ref[0, :, j * D_MODEL + lo:j * D_MODEL + lo + half] = (
                    (x1 * cos - x2 * sin) * post).astype(BF16)
                o_ref[0, :, j * D_MODEL + lo + half:j * D_MODEL + lo + RET_DK] = (
                    (x2 * cos + x1 * sin) * post).astype(BF16)
        else:
            o_ref[0, :, j * D_MODEL:(j + 1) * D_MODEL] = p.astype(BF16)


def _ret_in(x, ada, gain, cos, sin, w_in):
    bsz, seq, d = x.shape
    tm = ROW_TILE
    return pl.pallas_call(
        _ret_in_kernel,
        out_shape=jax.ShapeDtypeStruct((bsz, seq, RET_IN), BF16),
        grid=(bsz, seq // tm),
        in_specs=[
            pl.BlockSpec((1, tm, d), lambda b, i: (b, i, 0)),
            pl.BlockSpec((1, 6, d), lambda b, i: (b, 0, 0)),
            pl.BlockSpec((1, d), lambda b, i: (0, 0)),
            pl.BlockSpec((tm, RET_DK // 2), lambda b, i: (i, 0)),
            pl.BlockSpec((tm, RET_DK // 2), lambda b, i: (i, 0)),
            pl.BlockSpec((d, RET_IN), lambda b, i: (0, 0), pipeline_mode=pl.Buffered(1)),
        ],
        out_specs=pl.BlockSpec((1, tm, RET_IN), lambda b, i: (b, i, 0)),
        compiler_params=_params("arbitrary", "arbitrary"),
        name="ret_in",
    )(x, ada, gain, cos, sin, w_in)


def _ret_core_kernel(q_ref, k_ref, v_ref, g_ref, x_ref, ada_ref, intra_ref, qd_ref, kd_ref, cd_ref,
                     gn_ref, wo_ref, ngain_ref, wr_hi_ref, wr_lo_ref, br_ref,
                     xo_ref, h_ref, lt_ref, state, ybuf):
    @pl.when(pl.program_id(1) == 0)
    def _():
        state[...] = jnp.zeros_like(state)

    for hd in range(RET_HEADS):
        q = q_ref[0, :, hd * RET_DK:(hd + 1) * RET_DK]
        k = k_ref[0, :, hd * RET_DK:(hd + 1) * RET_DK]
        v = v_ref[0, :, hd * RET_DV:(hd + 1) * RET_DV]
        st = state[hd]
        s = _dot_nt(q, k) * intra_ref[hd]
        o = _dot(s.astype(BF16), v) + qd_ref[hd] * _dot(q, st.astype(BF16))
        kdec = (k.astype(F32) * kd_ref[hd]).astype(BF16)
        state[hd] = st * cd_ref[hd] + _dot_tn(kdec, v)
        o = _rms(o) * gn_ref[:, hd * RET_DV:(hd + 1) * RET_DV]
        g = g_ref[0, :, hd * RET_DV:(hd + 1) * RET_DV].astype(F32)
        ybuf[:, hd * RET_DV:(hd + 1) * RET_DV] = (_silu(g) * o).astype(BF16)

    y = _dot(ybuf[...], wo_ref[...])
    xn = x_ref[0] + ada_ref[0, 2:3, :] * y
    xo_ref[0] = xn
    h = _ffn_prenorm_logits(xn, ngain_ref[...], ada_ref[0, 3:4, :], ada_ref[0, 4:5, :],
                            wr_hi_ref, wr_lo_ref, br_ref, lt_ref)
    _store_row_tiles(h_ref, (0,), h)


def _ret_core(proj, x, ada, gn, w_out, ngain, router):
    bsz, seq, d = x.shape
    c = RET_CHUNK
    f32 = F32
    log_gamma = jnp.log1p(-jnp.exp2(-5.0 - jnp.arange(RET_HEADS, dtype=f32)))
    n = jnp.arange(c, dtype=f32)
    rel = n[:, None] - n[None, :]
    intra = jnp.where(rel >= 0, jnp.exp(jnp.maximum(rel, 0.0)[None] * log_gamma[:, None, None]), 0.0)
    qd = jnp.exp((n + 1.0)[None, :] * log_gamma[:, None])[:, :, None]
    kd = jnp.exp((c - 1.0 - n)[None, :] * log_gamma[:, None])[:, :, None]
    cd = jnp.broadcast_to(jnp.exp(c * log_gamma)[:, None, None], (RET_HEADS, 1, RET_DV))
    const = lambda *shape: pl.BlockSpec(shape, lambda b, i: (0,) * len(shape))
    tok = lambda w, j: pl.BlockSpec((1, c, w), lambda b, i: (b, i, j))
    nc = seq // c
    return pl.pallas_call(
        _ret_core_kernel,
        out_shape=(jax.ShapeDtypeStruct((bsz, seq, d), f32),
                   jax.ShapeDtypeStruct((bsz, seq * ROW_TILES, LANE), f32),
                   jax.ShapeDtypeStruct((ROUTER_ROWS, bsz * seq), f32)),
        grid=(bsz, nc),
        in_specs=[
            tok(RET_QK, 0), tok(RET_QK, 1), tok(RET_V, 1), tok(RET_V, 2),
            tok(d, 0),
            pl.BlockSpec((1, 6, d), lambda b, i: (b, 0, 0)),
            const(RET_HEADS, c, c), const(RET_HEADS, c, 1), const(RET_HEADS, c, 1),
            const(RET_HEADS, 1, RET_DV),
            const(1, RET_V), const(RET_V, d), const(1, d),
        ] + _logit_specs(d),
        out_specs=(tok(d, 0), pl.BlockSpec((1, c * ROW_TILES, LANE), lambda b, i: (b, i, 0)),
                   pl.BlockSpec((ROUTER_ROWS, c), lambda b, i: (0, b * nc + i))),
        scratch_shapes=[pltpu.VMEM((RET_HEADS, RET_DK, RET_DV), f32),
                        pltpu.VMEM((c, RET_V), BF16)],
        compiler_params=_params("arbitrary", "arbitrary"),
        name="ret_core",
    )(proj, proj, proj, proj, x, ada, intra, qd, kd, cd, gn, w_out, ngain, *router)


def _moe_kernel(blk_ref, exp_ref, nitem_ref, start_ref, cnt_ref, *refs):
    tok_first, refs = refs[:MOE_AHEAD], refs[MOE_AHEAD:]
    tok_ahead, h_hbm, w1_ref, w3_ref, w2_ref, o_ref, xbuf, w1b, w3b, w2b, gsem = refs
    n_buf = MOE_AHEAD + 1
    w = pl.program_id(0)
    bm = MOE_BM
    n_item = nitem_ref[0]
    d = w1b.shape[0]
    e = exp_ref[w]
    blk = blk_ref[w]
    slot = blk % n_buf
    ahead_slot = (blk + MOE_AHEAD) % n_buf
    prev = jnp.maximum(w - 1, 0)
    first_visit = jnp.logical_or(w == 0, blk_ref[prev] != blk)
    live = w < n_item

    def gather_row(tok_ref, s, r):
        pltpu.make_async_copy(_row_tile(h_hbm, (), tok_ref[0, 0, r]), _row_tile(xbuf, (s,), r), gsem.at[s]).start()

    def wait_gather(s):
        pltpu.make_async_copy(h_hbm.at[pl.ds(0, bm * ROW_TILES), :], xbuf.at[s], gsem.at[s]).wait()

    def expert(row_copies):
        n_dots = 2 * (d // MOE_TN)
        per_dot = -(-len(row_copies) // n_dots)
        groups = iter([row_copies[g * per_dot:(g + 1) * per_dot] for g in range(n_dots)])

        def piece(lhs, w_ref, c):
            for issue in next(groups, ()):
                issue()
            return _dot(lhs, w_ref[:, c * MOE_TN:(c + 1) * MOE_TN])

        x = _load_row_tiles(xbuf, (slot,), bm).astype(BF16)
        hid = []
        for c in range(d // MOE_TN):
            a = piece(x, w1b, c)
            b = piece(x, w3b, c)
            hid.append((_silu(a) * b).astype(BF16))
        hid = jnp.concatenate(hid, axis=1)
        y = jnp.concatenate([piece(hid, w2b, c) for c in range(d // MOE_TN)], axis=1)
        row = blk * bm + lax.broadcasted_iota(jnp.int32, (bm, 1), 0)
        mine = jnp.logical_and(row >= start_ref[e], row < start_ref[e] + cnt_ref[e])
        return y, mine

    @pl.when(jnp.logical_and(live, jnp.logical_or(w == 0, exp_ref[prev] != e)))
    def _():
        w1b[...] = w1_ref[0, 0].astype(BF16)
        w3b[...] = w3_ref[0, 0].astype(BF16)
        w2b[...] = w2_ref[0, 0].astype(BF16)

    @pl.when(w == 0)
    def _():
        for k in range(MOE_AHEAD):
            def body(r, carry, k=k):
                gather_row(tok_first[k], k, r)
                return carry
            lax.fori_loop(0, bm, body, 0, unroll=8)

    @pl.when(jnp.logical_and(live, first_visit))
    def _():
        wait_gather(slot)
        y, mine = expert([functools.partial(gather_row, tok_ahead, ahead_slot, r) for r in range(bm)])
        _store_row_tiles(o_ref, (), jnp.where(mine, y, 0.0))

    @pl.when(jnp.logical_and(live, jnp.logical_not(first_visit)))
    def _():
        y, mine = expert([])
        _store_row_tiles(o_ref, (), jnp.where(mine, y, _load_row_tiles(o_ref, (), bm)))

    @pl.when(w == n_item - 1)
    def _():
        for k in range(1, n_buf):
            wait_gather((blk + k) % n_buf)


def _moe(h, order, counts, w1, w3, w2, layer):
    m = order.shape[0]
    d = w1.shape[-1]
    bm = MOE_BM
    n_blocks = m // bm
    n_items = n_blocks + N_EXPERTS - 1
    ends = jnp.cumsum(counts)
    starts = ends - counts
    first_blk = starts // bm
    last_blk = jnp.maximum(ends - 1, starts) // bm
    visits = jnp.where(counts > 0, last_blk - first_blk + 1, 0)
    item_end = jnp.cumsum(visits)
    item_start = item_end - visits
    n_item = item_end[-1]
    w = jnp.minimum(jnp.arange(n_items, dtype=jnp.int32), n_item - 1)
    item_e = jnp.sum(w[:, None] >= item_end[None, :], axis=1).astype(jnp.int32)
    item_blk = (first_blk[item_e] + w - item_start[item_e]).astype(jnp.int32)
    wspec = lambda: pl.BlockSpec((1, 1, d, d), lambda i, blk, e, *_: (layer, e[i], 0, 0))
    idx = lambda f: pl.BlockSpec((1, 1, bm), f, memory_space=pltpu.SMEM)
    tok3 = (order >> TOP_K_SHIFT).reshape(n_blocks, 1, bm)
    first = [idx(lambda i, blk, *_, k=k: (k, 0, 0)) for k in range(MOE_AHEAD)]
    ahead = idx(lambda i, blk, *_: (jnp.minimum(blk[i] + MOE_AHEAD, n_blocks - 1), 0, 0))
    return pl.pallas_call(
        _moe_kernel,
        out_shape=jax.ShapeDtypeStruct((m * ROW_TILES, LANE), F32),
        grid_spec=pltpu.PrefetchScalarGridSpec(
            num_scalar_prefetch=5,
            grid=(n_items,),
            in_specs=first + [ahead, pl.BlockSpec(memory_space=pl.ANY), wspec(), wspec(), wspec()],
            out_specs=pl.BlockSpec((bm * ROW_TILES, LANE), lambda i, blk, *_: (blk[i], 0)),
            scratch_shapes=[pltpu.VMEM((MOE_AHEAD + 1, bm * ROW_TILES, LANE), F32),
                            pltpu.VMEM((d, d), BF16), pltpu.VMEM((d, d), BF16), pltpu.VMEM((d, d), BF16),
                            pltpu.SemaphoreType.DMA((MOE_AHEAD + 1,))],
        ),
        compiler_params=_params("arbitrary"),
        name="moe",
    )(item_blk, item_e, n_item.astype(jnp.int32).reshape(1), starts.astype(jnp.int32), counts.astype(jnp.int32),
      *([tok3] * (MOE_AHEAD + 1)), h, w1, w3, w2)


class _RowGather:
    def __init__(self, ys_hbm, buf, sem, tm):
        self.ys, self.buf, self.sem, self.tm = ys_hbm, buf, sem, tm

    def row_copy(self, d_ref, slot, k, t):
        pltpu.make_async_copy(_row_tile(self.ys, (), d_ref[0, 0, t]), _row_tile(self.buf, (slot, k), t),
                              self.sem.at[slot]).start(priority=k)

    def start(self, d_refs, slot):
        def body(t, carry):
            for k, d_ref in enumerate(d_refs):
                self.row_copy(d_ref, slot, k, t)
            return carry
        lax.fori_loop(0, self.tm, body, 0, unroll=8)

    def unrolled(self, d_refs, slot):
        return [functools.partial(self.row_copy, d_ref, slot, k, t)
                for t in range(self.tm) for k, d_ref in enumerate(d_refs)]

    def wait(self, slot):
        for k in range(TOP_K):
            pltpu.make_async_copy(self.ys.at[pl.ds(0, self.tm * ROW_TILES), :], self.buf.at[slot, k],
                                  self.sem.at[slot]).wait()

    def combined(self, slot, w1_ref, w2_ref):
        return (w1_ref[...] * _load_row_tiles(self.buf, (slot, 0), self.tm)
                + w2_ref[...] * _load_row_tiles(self.buf, (slot, 1), self.tm))


def _gather_specs(tm, nt_total, step_of):
    cur = lambda: pl.BlockSpec((1, 1, tm), lambda *g: (step_of(*g), 0, 0), memory_space=pltpu.SMEM)
    nxt = lambda: pl.BlockSpec((1, 1, tm), lambda *g: (jnp.minimum(step_of(*g) + 1, nt_total - 1), 0, 0),
                               memory_space=pltpu.SMEM)
    wcol = lambda: pl.BlockSpec((tm, 1), lambda *g: (step_of(*g), 0))
    return [cur(), cur(), nxt(), nxt(), wcol(), wcol()]


def _gather_scratch(tm):
    return [pltpu.VMEM((2, TOP_K, tm * ROW_TILES, LANE), F32), pltpu.SemaphoreType.DMA((2,))]


def _group_norm64(t, ind_ref, indt_ref):
    ss = _dot((t * t).astype(BF16), ind_ref[...])
    r = lax.rsqrt(ss * (1.0 / DIFF_HEAD_DIM) + EPS)
    r_hi = r.astype(BF16)
    r_lo = (r - r_hi.astype(F32)).astype(BF16)
    rb = _dot(jnp.concatenate([r_hi, r_lo], axis=1), indt_ref[...])
    return t * rb


def _kvq_kernel(d1c, d2c, d1n, d2n, w1_ref, w2_ref, ys_hbm, x_ref, ada0_ref, ada1_ref, kvada_ref, kvgain_ref,
                qgain_ref, knorm_ref, qnorm_ref, wk_ref, wvt_ref, wq_ref, ind_ref, indt_ref,
                xo_ref, q_ref, k_ref, vt_ref, gbuf, gsem):
    tm = x_ref.shape[1]
    d = x_ref.shape[2]
    step = pl.program_id(0) * pl.num_programs(1) + pl.program_id(1)
    slot = step % 2
    gather = _RowGather(ys_hbm, gbuf, gsem, tm)

    @pl.when(step == 0)
    def _():
        gather.start((d1c, d2c), 0)

    gather.wait(slot)
    xn = x_ref[0] + ada0_ref[0, 5:6, :] * gather.combined(slot, w1_ref, w2_ref)
    xo_ref[0] = xn
    copies = gather.unrolled((d1n, d2n), 1 - slot)
    n_pieces = 3 * (d // KVQ_TN)
    per_piece = -(-len(copies) // n_pieces)
    groups = iter([copies[g * per_piece:(g + 1) * per_piece] for g in range(n_pieces)])

    def issue_group():
        for issue in next(groups):
            issue()

    def project(lhs, w_ref):
        out = []
        for c in range(d // KVQ_TN):
            issue_group()
            out.append(_dot(lhs, w_ref[:, c * KVQ_TN:(c + 1) * KVQ_TN]))
        return jnp.concatenate(out, axis=1)

    r = _rms(xn)
    hk = ((r * kvgain_ref[...]) * (1.0 + kvada_ref[0, 1:2, :]) + kvada_ref[0, 0:1, :]).astype(BF16)
    hq = ((r * qgain_ref[...]) * (1.0 + ada1_ref[0, 1:2, :]) + ada1_ref[0, 0:1, :]).astype(BF16)
    hd2 = 2 * DIFF_HEAD_DIM
    kk = project(hk, wk_ref)
    kn = (_group_norm64(kk, ind_ref, indt_ref) * knorm_ref[...]).astype(BF16)
    qq = project(hq, wq_ref)
    qn = (_group_norm64(qq, ind_ref, indt_ref) * qnorm_ref[...]).astype(BF16)
    for hd in range(DIFF_HEADS):
        k_ref[0, hd] = kn[:, hd * hd2:(hd + 1) * hd2]
        q_ref[0, hd] = qn[:, hd * hd2:(hd + 1) * hd2]
    vt = []
    for c in range(d // KVQ_TN):
        issue_group()
        vt.append(_dot_nt(wvt_ref[c * KVQ_TN:(c + 1) * KVQ_TN, :], hk))
    vt = jnp.concatenate(vt, axis=0).astype(BF16)
    for j in range(tm // ATT_TK):
        vt_ref[0, j] = vt[:, j * ATT_TK:(j + 1) * ATT_TK].reshape(DIFF_HEADS, hd2, ATT_TK)

    @pl.when(step == pl.num_programs(0) * pl.num_programs(1) - 1)
    def _():
        gather.wait(1 - slot)


def _kvq(x, ys, dest1, dest2, w1, w2, ada0, ada1, kvada, kv_gain, q_gain, k_norm, q_norm, w_k, w_vt, w_q):
    bsz, seq, d = x.shape
    tm = ROW_TILE
    nt = seq // tm
    nt_total = bsz * nt
    lane_group = jnp.arange(d, dtype=jnp.int32) // DIFF_HEAD_DIM
    ind = (lane_group[:, None] == jnp.arange(LANE, dtype=jnp.int32)[None, :]).astype(BF16)
    indt = jnp.concatenate([ind.T, ind.T], axis=0)
    reps = d // DIFF_HEAD_DIM
    q_scale = jnp.tile(q_norm, reps)[None, :] * (DIFF_HEAD_DIM ** -0.5 * math.log2(math.e))
    const = lambda *shape: pl.BlockSpec(shape, lambda b, i: (0,) * len(shape))
    tok = lambda w: pl.BlockSpec((1, tm, w), lambda b, i: (b, i, 0))
    hd2 = 2 * DIFF_HEAD_DIM
    heads = pl.BlockSpec((1, DIFF_HEADS, tm, hd2), lambda b, i: (b, 0, i, 0))
    ada_spec = lambda rows: pl.BlockSpec((1, rows, d), lambda b, i: (b, 0, 0))
    col = lambda a: a.reshape(-1, 1)
    idx = lambda a: a.reshape(nt_total, 1, tm)
    return pl.pallas_call(
        _kvq_kernel,
        out_shape=(jax.ShapeDtypeStruct((bsz, seq, d), F32),
                   jax.ShapeDtypeStruct((bsz, DIFF_HEADS, seq, hd2), BF16),
                   jax.ShapeDtypeStruct((bsz, DIFF_HEADS, seq, hd2), BF16),
                   jax.ShapeDtypeStruct((bsz, seq // ATT_TK, DIFF_HEADS, hd2, ATT_TK), BF16)),
        grid=(bsz, nt),
        in_specs=_gather_specs(tm, nt_total, lambda b, i: b * nt + i) + [
            pl.BlockSpec(memory_space=pl.ANY),
            tok(d),
            ada_spec(6), ada_spec(6), ada_spec(2),
            const(1, d), const(1, d), const(1, d), const(1, d),
            const(d, d), const(d, d), const(d, d), const(d, LANE), const(2 * LANE, d),
        ],
        out_specs=(tok(d), heads, heads,
                   pl.BlockSpec((1, tm // ATT_TK, DIFF_HEADS, hd2, ATT_TK), lambda b, i: (b, i, 0, 0, 0))),
        scratch_shapes=_gather_scratch(tm),
        compiler_params=_params("arbitrary", "arbitrary"),
        name="kvq",
    )(idx(dest1), idx(dest2), idx(dest1), idx(dest2), col(w1), col(w2), ys, x, ada0, ada1, kvada,
      kv_gain, q_gain, jnp.tile(k_norm, reps)[None, :], q_scale, w_k, w_vt, w_q, ind, indt)


def _attn_kernel(lambda_init, q_ref, k_ref, vt_ref, x_ref, ada_ref, lam_ref, subln_ref, wo_ref, ngain_ref,
                 wr_hi_ref, wr_lo_ref, br_ref, xo_ref, h_ref, lt_ref,
                 s_a, s_b, p_a, p_b, acc, obuf):
    i = pl.program_id(1)
    tq, tk = ATT_TQ, ATT_TK
    hd2 = 2 * DIFF_HEAD_DIM
    n_kblk = k_ref.shape[2] // tk
    lam = lam_ref[...]
    lam_full = (jnp.exp(jnp.sum(lam[0:1] * lam[1:2], axis=-1, keepdims=True))
                - jnp.exp(jnp.sum(lam[2:3] * lam[3:4], axis=-1, keepdims=True)) + lambda_init)
    lane = lax.broadcasted_iota(jnp.int32, (1, hd2), 1)
    keep = (jnp.where(lane < DIFF_HEAD_DIM, 1.0, 0.0).astype(BF16),
            jnp.where(lane >= DIFF_HEAD_DIM, 1.0, 0.0).astype(BF16))
    key = lax.broadcasted_iota(jnp.int32, (tk, tq), 0)
    qry = lax.broadcasted_iota(jnp.int32, (1, tq), 1)
    key_limit = ((i * tq + qry) // MASK_CHUNK + 1) * MASK_CHUNK
    neg_inf = jnp.float32(-jnp.inf)
    first_masked = i * (tq // tk)
    n_blocks = first_masked + tq // tk
    n_pairs = (n_blocks + 1) // 2
    n_plain = jnp.maximum((first_masked - 1) // 2, 0)

    streams = [(u, t) for u in range(ATT_HEADS) for t in range(2)]

    def head_group(hg, _):
        hds = [hg * ATT_HEADS + u for u in range(ATT_HEADS)]
        qs = []
        for hd in hds:
            qh = q_ref[0, hd]
            qs.append((qh * keep[0], qh * keep[1]))

        def produce_scores(jb, s_out, masked):
            start = pl.multiple_of(jnp.minimum(jb, n_kblk - 1) * tk, tk)
            visible = key < key_limit - jb * tk
            maxima = []
            for n, (u, t) in enumerate(streams):
                st = _dot_nt(k_ref[0, hds[u], pl.ds(start, tk), :], qs[u][t])
                if masked:
                    st = jnp.where(visible, st, neg_inf)
                s_out[n] = st
                maxima.append(jnp.max(st, axis=0, keepdims=True))
            return tuple(maxima)

        def step(j, s_in, s_out, p_in, p_out, state, masked):
            stats, blk_max = state
            jv = jnp.clip(j - 1, 0, n_kblk - 1)
            next_max = produce_scores(j + 1, s_out, masked)
            new_stats = []
            for n, (u, t) in enumerate(streams):
                m, l = stats[n]
                m_new = jnp.maximum(m, blk_max[n])
                alpha = jnp.exp2(m - m_new)
                p = jnp.exp2(s_in[n] - m_new)
                p_out[n] = p.astype(BF16)
                new_stats.append((m_new, alpha * l + jnp.sum(p, axis=0, keepdims=True)))
                acc[n] = alpha * (acc[n] + _dot(vt_ref[0, jv, hds[u]], p_in[n]))
            return tuple(new_stats), next_max

        p_b[...] = jnp.zeros_like(p_b)
        acc[...] = jnp.zeros_like(acc)
        stat0 = (jnp.full((1, tq), neg_inf, F32), jnp.zeros((1, tq), F32))
        state = ((stat0,) * len(streams), produce_scores(0, s_a, True))

        def pair(masked, jj, state):
            state = step(2 * jj, s_a, s_b, p_b, p_a, state, masked)
            return step(2 * jj + 1, s_b, s_a, p_a, p_b, state, masked)

        state = lax.fori_loop(0, n_plain, functools.partial(pair, False), state)
        (stats, _) = lax.fori_loop(n_plain, n_pairs, functools.partial(pair, True), state)
        jv = jnp.minimum(2 * n_pairs - 1, n_kblk - 1)
        for u, hd in enumerate(hds):
            vtb = vt_ref[0, jv, hd]
            a1 = acc[2 * u] + _dot(vtb, p_b[2 * u])
            a2 = acc[2 * u + 1] + _dot(vtb, p_b[2 * u + 1])
            o = a1 * (1.0 / stats[2 * u][1]) - lam_full * (a2 * (1.0 / stats[2 * u + 1][1]))
            o = o * lax.rsqrt(jnp.mean(o * o, axis=0, keepdims=True) + EPS)
            obuf[hd] = (o * (subln_ref[...] * (1.0 - lambda_init))).astype(BF16)
        return 0

    lax.fori_loop(0, DIFF_HEADS // ATT_HEADS, head_group, 0)

    y = _dot_tn(obuf[...].reshape(DIFF_HEADS * hd2, tq), wo_ref[...])
    xn = x_ref[0] + ada_ref[0, 2:3, :] * y
    xo_ref[0] = xn
    h = _ffn_prenorm_logits(xn, ngain_ref[...], ada_ref[0, 3:4, :], ada_ref[0, 4:5, :],
                            wr_hi_ref, wr_lo_ref, br_ref, lt_ref)
    _store_row_tiles(h_ref, (0,), h)


def _attn(q, k, vt, x, ada, lam, subln, w_out, ngain, router, lambda_init):
    bsz, seq, d = x.shape
    tq, tk = ATT_TQ, ATT_TK
    hd2 = 2 * DIFF_HEAD_DIM
    const = lambda *shape: pl.BlockSpec(shape, lambda b, i: (0,) * len(shape))
    tok = lambda w: pl.BlockSpec((1, tq, w), lambda b, i: (b, i, 0))
    nq = seq // tq
    ns = 2 * ATT_HEADS
    return pl.pallas_call(
        functools.partial(_attn_kernel, lambda_init),
        out_shape=(jax.ShapeDtypeStruct((bsz, seq, d), F32),
                   jax.ShapeDtypeStruct((bsz, seq * ROW_TILES, LANE), F32),
                   jax.ShapeDtypeStruct((ROUTER_ROWS, bsz * seq), F32)),
        grid=(bsz, nq),
        in_specs=[
            pl.BlockSpec((1, DIFF_HEADS, tq, hd2), lambda b, i: (b, 0, i, 0)),
            pl.BlockSpec((1, DIFF_HEADS, seq, hd2), lambda b, i: (b, 0, 0, 0)),
            pl.BlockSpec((1, seq // tk, DIFF_HEADS, hd2, tk), lambda b, i: (b, 0, 0, 0, 0)),
            tok(d),
            pl.BlockSpec((1, 6, d), lambda b, i: (b, 0, 0)),
            const(4, DIFF_HEAD_DIM), const(hd2, 1), const(d, d), const(1, d),
        ] + _logit_specs(d),
        out_specs=(tok(d), pl.BlockSpec((1, tq * ROW_TILES, LANE), lambda b, i: (b, i, 0)),
                   pl.BlockSpec((ROUTER_ROWS, tq), lambda b, i: (0, b * nq + i))),
        scratch_shapes=[pltpu.VMEM((ns, tk, tq), F32), pltpu.VMEM((ns, tk, tq), F32),
                        pltpu.VMEM((ns, tk, tq), BF16), pltpu.VMEM((ns, tk, tq), BF16),
                        pltpu.VMEM((ns, hd2, tq), F32),
                        pltpu.VMEM((DIFF_HEADS, hd2, tq), BF16)],
        compiler_params=_params("arbitrary", "arbitrary"),
        name="attn",
    )(q, k, vt, x, ada, lam, subln, w_out, ngain, *router)


def _combine_kernel(d1c, d2c, d1n, d2n, w1_ref, w2_ref, ys_hbm, x_ref, ada_ref, o_ref, gbuf, gsem):
    tm = x_ref.shape[1]
    step = pl.program_id(0) * pl.num_programs(1) + pl.program_id(1)
    slot = step % 2
    gather = _RowGather(ys_hbm, gbuf, gsem, tm)

    @pl.when(step == 0)
    def _():
        gather.start((d1c, d2c), 0)

    @pl.when(step + 1 < pl.num_programs(0) * pl.num_programs(1))
    def _():
        gather.start((d1n, d2n), 1 - slot)

    gather.wait(slot)
    o_ref[0] = x_ref[0] + ada_ref[0, 5:6, :] * gather.combined(slot, w1_ref, w2_ref)


def _combine(x, ys, dest1, dest2, w1, w2, ada):
    bsz, seq, d = x.shape
    tm = ROW_TILE
    nt = seq // tm
    nt_total = bsz * nt
    col = lambda a: a.reshape(-1, 1)
    idx = lambda a: a.reshape(nt_total, 1, tm)
    return pl.pallas_call(
        _combine_kernel,
        out_shape=jax.ShapeDtypeStruct((bsz, seq, d), F32),
        grid=(bsz, nt),
        in_specs=_gather_specs(tm, nt_total, lambda b, i: b * nt + i) + [
            pl.BlockSpec(memory_space=pl.ANY),
            pl.BlockSpec((1, tm, d), lambda b, i: (b, i, 0)),
            pl.BlockSpec((1, 6, d), lambda b, i: (b, 0, 0)),
        ],
        out_specs=pl.BlockSpec((1, tm, d), lambda b, i: (b, i, 0)),
        scratch_shapes=_gather_scratch(tm),
        compiler_params=_params("arbitrary", "arbitrary"),
        name="combine",
    )(idx(dest1), idx(dest2), idx(dest1), idx(dest2), col(w1), col(w2), ys, x, ada)


def _router_weights(w_group, b_group, w_expert, b_expert):
    d = w_group.shape[0]
    pad = ROUTER_ROWS - N_EXPERTS - N_GROUPS
    w = jnp.concatenate([w_expert.T, w_group.T, jnp.zeros((pad, d), F32)], axis=0)
    b = jnp.concatenate([b_expert, b_group, jnp.zeros((pad,), F32)])[:, None]
    w_hi = w.astype(BF16)
    w_lo = (w - w_hi.astype(F32)).astype(BF16)
    return w_hi, w_lo, b


def _routing_tables(route, cnt):
    rec = lambda r: route[r]
    counts = cnt[:, 0].astype(jnp.int32)
    starts = jnp.cumsum(counts) - counts
    dest1 = starts[rec(0).astype(jnp.int32)] + rec(2).astype(jnp.int32)
    dest2 = starts[rec(1).astype(jnp.int32)] + rec(3).astype(jnp.int32)
    dest = jnp.stack([dest1, dest2], axis=1).reshape(-1)
    _, order = lax.sort((dest, jnp.arange(dest.shape[0], dtype=jnp.int32)), num_keys=1)
    return order, dest1, dest2, rec(4), rec(5), counts


def kernel(x, c, ada_w, ada_b, norm_mix, norm_ffn, ret_w_in, ret_gn, ret_w_out, kv_ada_w, kv_ada_b, kv_norm, kv_w, k_norm, diff_w_q, q_norm, diff_lam, diff_subln, diff_w_out, moe_w_group, moe_b_group, moe_w_expert, moe_b_expert, moe_w1, moe_w3, moe_w2):
    bsz, seq, d = x.shape
    n_tok = bsz * seq

    ada = _ada(c, ada_w, ada_b).reshape(2, bsz, 6, d)
    kvada = _ada(c, kv_ada_w[None], kv_ada_b[None]).reshape(bsz, 2, d)

    half = RET_DK // 2
    inv_freq = 1.0 / (ROPE_BASE ** (jnp.arange(half, dtype=F32) / half))
    ang = jnp.arange(seq, dtype=F32)[:, None] * inv_freq[None, :]
    cos, sin = jnp.cos(ang), jnp.sin(ang)

    routers = [_router_weights(moe_w_group[l], moe_b_group[l], moe_w_expert[l], moe_b_expert[l])
               for l in range(2)]

    def moe_layer(h, logits_t, layer):
        order, dest1, dest2, w1, w2, counts = _routing_tables(*_route(logits_t))
        ys = _moe(h.reshape(n_tok * ROW_TILES, LANE), order, counts, moe_w1, moe_w3, moe_w2, layer)
        return ys, dest1, dest2, w1, w2

    proj = _ret_in(x, ada[0], norm_mix[0][None, :], cos, sin, ret_w_in[0].astype(BF16))
    x1, h, logits_t = _ret_core(proj, x, ada[0], ret_gn[0][None, :], ret_w_out[0].astype(BF16),
                                norm_ffn[0][None, :], routers[0])
    moe0 = moe_layer(h, logits_t, 0)

    x2, q, k, vt = _kvq(x1, *moe0, ada[0], ada[1], kvada, kv_norm[None, :], norm_mix[1][None, :], k_norm, q_norm[0],
                        kv_w[:, :d].astype(BF16), kv_w[:, d:].T.astype(BF16), diff_w_q[0].astype(BF16))
    lambda_init = 0.8 - 0.6 * math.exp(-0.3 * 1)
    x3, h, logits_t = _attn(q, k, vt, x2, ada[1], diff_lam[0], diff_subln[0][:, None], diff_w_out[0].astype(BF16),
                            norm_ffn[1][None, :], routers[1], lambda_init)
    moe1 = moe_layer(h, logits_t, 1)
    return _combine(x3, *moe1, ada[1])
```
